```python
import math
import jax, jax.numpy as jnp
from jax import lax
import numpy as np

D_MODEL = 1024
BATCH = 32
SEQ = 256
DEPTH = 1
DEC_BATCH = 8
DEC_SEQ = 2048
PAST_LEN = 512

GRID_W = 64
D_MIX = D_MODEL
SSD_WIDTH = D_MIX // 2
SSD_HEAD_DIM = 64
SSD_HEADS = SSD_WIDTH // SSD_HEAD_DIM
SSD_GROUPS = 2
SSD_STATE = 128
SSD_CONV = 3
SSD_CHUNK = 128
SSD_CONV_CH = SSD_WIDTH + 2 * SSD_GROUPS * SSD_STATE
S5_WIDTH = D_MIX - SSD_WIDTH
S5_CH_PER_GROUP = 16
S5_GROUPS = S5_WIDTH // S5_CH_PER_GROUP
S5_STATE = 64
IN_COLS = SSD_WIDTH + SSD_CONV_CH + 2 * SSD_HEADS + S5_WIDTH
N_EXPERTS = 32
TOP_K = 4
D_FF = D_MODEL
SWIGLU_LIMIT = 7.0
SWIGLU_ALPHA = 1.702
RMS_EPS = 1e-5

kernel_name = "hybrid_ssd_s5_moe_dit_step"


def rmsnorm(x, g):
    xf = x.astype(jnp.float32)
    y = xf * lax.rsqrt(jnp.mean(xf * xf, axis=-1, keepdims=True) + RMS_EPS)
    return (y * g.astype(jnp.float32)).astype(x.dtype)


def to_cols(t):
    b, L, d = t.shape
    rows = L // GRID_W
    return t.reshape(b, rows, GRID_W, d).transpose(0, 2, 1, 3).reshape(b, L, d)


def from_cols(t):
    b, L, d = t.shape
    rows = L // GRID_W
    return t.reshape(b, GRID_W, rows, d).transpose(0, 2, 1, 3).reshape(b, L, d)


def ssd_scan(x, dt, a, bm, cm, s0):
    b, L, h, p = x.shape
    nc = L // SSD_CHUNK
    chunk = lambda t: t.reshape((b, nc, SSD_CHUNK) + t.shape[2:])
    x, dt, bm, cm = chunk(x), chunk(dt), chunk(bm), chunk(cm)
    cum = jnp.cumsum(dt * a, axis=2)
    seg = cum[:, :, :, None, :] - cum[:, :, None, :, :]
    lower = jnp.tril(jnp.ones((SSD_CHUNK, SSD_CHUNK), dtype=bool))[None, None, :, :, None]
    decay = jnp.exp(jnp.where(lower, seg, -jnp.inf))
    xdt = x * dt[..., None]
    scores = jnp.einsum('bclhn,bcshn->bclsh', cm, bm) * decay
    y_diag = jnp.einsum('bclsh,bcshp->bclhp', scores, xdt)
    tail = jnp.exp(cum[:, :, -1:, :] - cum)
    states = jnp.einsum('bcsh,bcshn,bcshp->bchpn', tail, bm, xdt)
    chunk_decay = jnp.exp(cum[:, :, -1, :])

    def step(s, inp):
        st, dec = inp
        return dec[..., None, None] * s + st, s

    final, prev = lax.scan(step, s0.astype(jnp.float32),
                           (jnp.moveaxis(states, 1, 0), jnp.moveaxis(chunk_decay, 1, 0)))
    prev = jnp.moveaxis(prev, 0, 1)
    y_off = jnp.einsum('bclhn,bchpn,bclh->bclhp', cm, prev, jnp.exp(cum))
    return (y_diag + y_off).reshape(b, L, h, p), final


def ssd_mixer(z, xbc, dt_raw, s0_f, s0_b, p):
    b, L, _ = xbc.shape
    pad = SSD_CONV // 2
    xbc = lax.conv_general_dilated(xbc, p['ssd_conv_w'][:, None, :], (1,), [(pad, pad)],
                                   dimension_numbers=('NWC', 'WIO', 'NWC'),
                                   feature_group_count=SSD_CONV_CH) + p['ssd_conv_b']
    xbc = jax.nn.silu(xbc.astype(jnp.float32))
    xs, bm, cm = jnp.split(xbc, [SSD_WIDTH, SSD_WIDTH + SSD_GROUPS * SSD_STATE], axis=-1)
    x = xs.reshape(b, L, SSD_HEADS, SSD_HEAD_DIM)
    rep = SSD_HEADS // SSD_GROUPS
    bm = jnp.repeat(bm.reshape(b, L, SSD_GROUPS, SSD_STATE), rep, axis=2)
    cm = jnp.repeat(cm.reshape(b, L, SSD_GROUPS, SSD_STATE), rep, axis=2)
    dt_raw = dt_raw.astype(jnp.float32)
    dt_f = jax.nn.softplus(dt_raw[..., :SSD_HEADS] + p['ssd_dt_bias_fwd'].astype(jnp.float32))
    dt_b = jax.nn.softplus(dt_raw[..., SSD_HEADS:] + p['ssd_dt_bias_bwd'].astype(jnp.float32))
    a_f = -jnp.exp(p['ssd_a_log_fwd'].astype(jnp.float32))
    a_b = -jnp.exp(p['ssd_a_log_bwd'].astype(jnp.float32))
    flip = lambda t: t[:, ::-1]
    y_f, s_f = ssd_scan(x, dt_f, a_f, bm, cm, s0_f)
    y_b, s_b = ssd_scan(flip(x), flip(dt_b), a_b, flip(bm), flip(cm), s0_b)
    y = y_f + flip(y_b) + p['ssd_d'].astype(jnp.float32)[:, None] * x
    y = y.reshape(b, L, SSD_WIDTH) * jax.nn.silu(z.astype(jnp.float32))
    return rmsnorm(y, p['ssd_norm_g']), s_f, s_b


def s5_direction(u, a_re, a_im, log_dt, b_re, b_im, c_re, c_im, s0):
    f32 = jnp.float32
    a_re, a_im = a_re.astype(f32), a_im.astype(f32)
    b_re, b_im, c_re, c_im = b_re.astype(f32), b_im.astype(f32), c_re.astype(f32), c_im.astype(f32)
    delta = jnp.exp(log_dt.astype(f32))[:, None]
    mag = jnp.exp(a_re * delta)
    lam_re, lam_im = mag * jnp.cos(a_im * delta), mag * jnp.sin(a_im * delta)
    den = a_re * a_re + a_im * a_im
    nr = lam_re - 1.0
    f_re = (nr * a_re + lam_im * a_im) / den
    f_im = (lam_im * a_re - nr * a_im) / den
    bb_re = f_re[..., None] * b_re - f_im[..., None] * b_im
    bb_im = f_re[..., None] * b_im + f_im[..., None] * b_re
    bu_re = jnp.einsum('gpc,blgc->blgp', bb_re, u)
    bu_im = jnp.einsum('gpc,blgc->blgp', bb_im, u)
    s0r, s0i = s0[..., 0].astype(f32), s0[..., 1].astype(f32)
    bu_re = bu_re.at[:, 0].add(lam_re * s0r - lam_im * s0i)
    bu_im = bu_im.at[:, 0].add(lam_re * s0i + lam_im * s0r)
    ar = jnp.broadcast_to(lam_re, bu_re.shape)
    ai = jnp.broadcast_to(lam_im, bu_re.shape)

    def combine(e1, e2):
        a1r, a1i, b1r, b1i = e1
        a2r, a2i, b2r, b2i = e2
        return (a2r * a1r - a2i * a1i, a2r * a1i + a2i * a1r,
                a2r * b1r - a2i * b1i + b2r, a2r * b1i + a2i * b1r + b2i)

    _, _, sr, si = lax.associative_scan(combine, (ar, ai, bu_re, bu_im), axis=1)
    y = jnp.einsum('gcp,blgp->blgc', c_re, sr) - jnp.einsum('gcp,blgp->blgc', c_im, si)
    final = jnp.stack([sr[:, -1], si[:, -1]], axis=-1)
    return y, final


def s5_mixer(u, s0_f, s0_b, p):
    b, L, _ = u.shape
    uf = u.astype(jnp.float32)
    ug = uf.reshape(b, L, S5_GROUPS, S5_CH_PER_GROUP)
    y_f, s_f = s5_direction(ug, p['s5_a_re_fwd'], p['s5_a_im_fwd'], p['s5_log_dt_fwd'],
                            p['s5_b_re_fwd'], p['s5_b_im_fwd'], p['s5_c_re_fwd'], p['s5_c_im_fwd'], s0_f)
    y_b, s_b = s5_direction(ug[:, ::-1], p['s5_a_re_bwd'], p['s5_a_im_bwd'], p['s5_log_dt_bwd'],
                            p['s5_b_re_bwd'], p['s5_b_im_bwd'], p['s5_c_re_bwd'], p['s5_c_im_bwd'], s0_b)
    y = (y_f + y_b[:, ::-1]).reshape(b, L, S5_WIDTH) + p['s5_d'].astype(jnp.float32) * uf
    y = jax.nn.gelu(y).astype(u.dtype)
    val, gate = jnp.split(y @ p['w_glu'] + p['b_glu'], 2, axis=-1)
    return val * jax.nn.sigmoid(gate), s_f, s_b


def moe(h, p):
    logits = (h @ p['w_router'] + p['b_router']).astype(jnp.float32)
    top_v, top_i = lax.top_k(logits, TOP_K)
    top_w = jax.nn.softmax(top_v, axis=-1)
    comb = jnp.sum(jax.nn.one_hot(top_i, N_EXPERTS, dtype=jnp.float32) * top_w[..., None], axis=-2)
    comb = comb.astype(h.dtype)
    out = jnp.zeros(h.shape, h.dtype)
    for e in range(N_EXPERTS):
        gu = h @ p['w_gate_up'][e] + p['b_gate_up'][e]
        gate, up = gu[..., :D_FF], gu[..., D_FF:]
        gate = jnp.minimum(gate, SWIGLU_LIMIT)
        up = jnp.clip(up, -SWIGLU_LIMIT, SWIGLU_LIMIT)
        act = gate * jax.nn.sigmoid(SWIGLU_ALPHA * gate) * (up + 1.0)
        out = out + comb[..., e:e + 1] * (act @ p['w_down'][e] + p['b_down'][e])
    return out


def trunk_layer(x, cond, s_ssd_f, s_ssd_b, s_s5_f, s_s5_b, grid, p):
    mod = (jax.nn.silu(cond) @ p['w_ada'] + p['b_ada'])[:, None, :]
    sh1, sc1, g1, sh2, sc2, g2 = jnp.split(mod, 6, axis=-1)
    h = rmsnorm(x, p['norm1_g']) * (1.0 + sc1) + sh1
    proj = h @ p['w_in']
    z, xbc, dt_raw, u = jnp.split(
        proj, [SSD_WIDTH, SSD_WIDTH + SSD_CONV_CH, SSD_WIDTH + SSD_CONV_CH + 2 * SSD_HEADS], axis=-1)
    y_ssd, n_ssd_f, n_ssd_b = ssd_mixer(z, xbc, dt_raw, s_ssd_f, s_ssd_b, p)
    if grid:
        u = to_cols(u)
    y_s5, n_s5_f, n_s5_b = s5_mixer(u, s_s5_f, s_s5_b, p)
    if grid:
        y_s5 = from_cols(y_s5)
    mixed = jnp.concatenate([y_ssd.astype(x.dtype), y_s5.astype(x.dtype)], axis=-1)
    x = x + g1 * (mixed @ p['w_out'])
    h = rmsnorm(x, p['norm2_g']) * (1.0 + sc2) + sh2
    x = x + g2 * moe(h, p)
    return x, n_ssd_f, n_ssd_b, n_s5_f, n_s5_b


def setup_inputs(seed: int = 0) -> dict:
    key = jax.random.key(seed)
    ks = iter(jax.random.split(key, 64))
    f32 = jnp.float32
    nrm = lambda shape, s: s * jax.random.normal(next(ks), shape, f32)
    unif = lambda shape, lo, hi: jax.random.uniform(next(ks), shape, f32, lo, hi)
    L = DEPTH

    def dt_bias():
        dt0 = jnp.exp(unif((L, SSD_HEADS), math.log(1e-3), math.log(1e-1)))
        return dt0 + jnp.log(-jnp.expm1(-dt0))

    def s5_params():
        a_re = -0.5 + nrm((L, S5_GROUPS, S5_STATE), 0.01)
        a_im = jnp.pi * jnp.arange(S5_STATE, dtype=f32) + nrm((L, S5_GROUPS, S5_STATE), 0.01)
        log_dt = unif((L, S5_GROUPS), math.log(1e-3), math.log(1e-1))
        bs = (2 * S5_CH_PER_GROUP) ** -0.5
        cs = (2 * S5_STATE) ** -0.5
        return (a_re, a_im, log_dt,
                nrm((L, S5_GROUPS, S5_STATE, S5_CH_PER_GROUP), bs),
                nrm((L, S5_GROUPS, S5_STATE, S5_CH_PER_GROUP), bs),
                nrm((L, S5_GROUPS, S5_CH_PER_GROUP, S5_STATE), cs),
                nrm((L, S5_GROUPS, S5_CH_PER_GROUP, S5_STATE), cs))

    fa_re, fa_im, fdt, fb_re, fb_im, fc_re, fc_im = s5_params()
    ba_re, ba_im, bdt, bb_re, bb_im, bc_re, bc_im = s5_params()
    d = D_MODEL
    return {
        'x_prompt': nrm((BATCH, SEQ, d), 1.0),
        'x_sample': nrm((DEC_BATCH, DEC_SEQ, d), 1.0),
        'state_ssd_fwd': nrm((DEC_BATCH, L, SSD_HEADS, SSD_HEAD_DIM, SSD_STATE), 0.3),
        'state_ssd_bwd': nrm((DEC_BATCH, L, SSD_HEADS, SSD_HEAD_DIM, SSD_STATE), 0.3),
        'state_s5_fwd': nrm((DEC_BATCH, L, S5_GROUPS, S5_STATE, 2), 0.1),
        'state_s5_bwd': nrm((DEC_BATCH, L, S5_GROUPS, S5_STATE, 2), 0.1),
        'c': nrm((DEC_BATCH, d), 1.0),
        'c_ctx': nrm((d,), 1.0),
        'w_ada': nrm((L, d, 6 * d), 0.5 * d ** -0.5),
        'b_ada': nrm((L, 6 * d), 0.01),
        'norm1_g': 1.0 + nrm((L, d), 0.01),
        'w_in': nrm((L, d, IN_COLS), d ** -0.5),
        'ssd_conv_w': nrm((L, SSD_CONV, SSD_CONV_CH), SSD_CONV ** -0.5),
        'ssd_conv_b': nrm((L, SSD_CONV_CH), 0.01),
        'ssd_dt_bias_fwd': dt_bias(),
        'ssd_dt_bias_bwd': dt_bias(),
        'ssd_a_log_fwd': jnp.log(unif((L, SSD_HEADS), 1.0, 16.0)),
        'ssd_a_log_bwd': jnp.log(unif((L, SSD_HEADS), 1.0, 16.0)),
        'ssd_d': 1.0 + nrm((L, SSD_HEADS), 0.1),
        'ssd_norm_g': 1.0 + nrm((L, SSD_WIDTH), 0.01),
        's5_a_re_fwd': fa_re, 's5_a_im_fwd': fa_im, 's5_log_dt_fwd': fdt,
        's5_b_re_fwd': fb_re, 's5_b_im_fwd': fb_im, 's5_c_re_fwd': fc_re, 's5_c_im_fwd': fc_im,
        's5_a_re_bwd': ba_re, 's5_a_im_bwd': ba_im, 's5_log_dt_bwd': bdt,
        's5_b_re_bwd': bb_re, 's5_b_im_bwd': bb_im, 's5_c_re_bwd': bc_re, 's5_c_im_bwd': bc_im,
        's5_d': nrm((L, S5_WIDTH), 1.0),
        'w_glu': nrm((L, S5_WIDTH, 2 * S5_WIDTH), S5_WIDTH ** -0.5),
        'b_glu': nrm((L, 2 * S5_WIDTH), 0.01),
        'w_out': nrm((L, D_MIX, d), D_MIX ** -0.5),
        'norm2_g': 1.0 + nrm((L, d), 0.01),
        'w_router': nrm((L, d, N_EXPERTS), d ** -0.5),
        'b_router': nrm((L, N_EXPERTS), 0.01),
        'w_gate_up': nrm((L, N_EXPERTS, d, 2 * D_FF), d ** -0.5),
        'b_gate_up': nrm((L, N_EXPERTS, 2 * D_FF), 0.01),
        'w_down': nrm((L, N_EXPERTS, D_FF, d), D_FF ** -0.5),
        'b_down': nrm((L, N_EXPERTS, d), 0.01),
        'norm_f_g': 1.0 + nrm((d,), 0.01),
    }


def reference(x_prompt, x_sample, state_ssd_fwd, state_ssd_bwd, state_s5_fwd, state_s5_bwd,
              c, c_ctx, w_ada, b_ada, norm1_g, w_in, ssd_conv_w, ssd_conv_b,
              ssd_dt_bias_fwd, ssd_dt_bias_bwd, ssd_a_log_fwd, ssd_a_log_bwd, ssd_d, ssd_norm_g,
              s5_a_re_fwd, s5_a_im_fwd, s5_log_dt_fwd, s5_b_re_fwd, s5_b_im_fwd, s5_c_re_fwd, s5_c_im_fwd,
              s5_a_re_bwd, s5_a_im_bwd, s5_log_dt_bwd, s5_b_re_bwd, s5_b_im_bwd, s5_c_re_bwd, s5_c_im_bwd,
              s5_d, w_glu, b_glu, w_out, norm2_g, w_router, b_router, w_gate_up, b_gate_up,
              w_down, b_down, norm_f_g):
    layer_params = dict(
        w_ada=w_ada, b_ada=b_ada, norm1_g=norm1_g, w_in=w_in, ssd_conv_w=ssd_conv_w,
        ssd_conv_b=ssd_conv_b, ssd_dt_bias_fwd=ssd_dt_bias_fwd, ssd_dt_bias_bwd=ssd_dt_bias_bwd,
        ssd_a_log_fwd=ssd_a_log_fwd, ssd_a_log_bwd=ssd_a_log_bwd, ssd_d=ssd_d, ssd_norm_g=ssd_norm_g,
        s5_a_re_fwd=s5_a_re_fwd, s5_a_im_fwd=s5_a_im_fwd, s5_log_dt_fwd=s5_log_dt_fwd,
        s5_b_re_fwd=s5_b_re_fwd, s5_b_im_fwd=s5_b_im_fwd, s5_c_re_fwd=s5_c_re_fwd, s5_c_im_fwd=s5_c_im_fwd,
        s5_a_re_bwd=s5_a_re_bwd, s5_a_im_bwd=s5_a_im_bwd, s5_log_dt_bwd=s5_log_dt_bwd,
        s5_b_re_bwd=s5_b_re_bwd, s5_b_im_bwd=s5_b_im_bwd, s5_c_re_bwd=s5_c_re_bwd, s5_c_im_bwd=s5_c_im_bwd,
        s5_d=s5_d, w_glu=w_glu, b_glu=b_glu, w_out=w_out, norm2_g=norm2_g, w_router=w_router,
        b_router=b_router, w_gate_up=w_gate_up, b_gate_up=b_gate_up, w_down=w_down, b_down=b_down)

    xc = x_prompt
    nb = x_prompt.shape[0]
    zero_ssd = jnp.zeros((nb, SSD_HEADS, SSD_HEAD_DIM, SSD_STATE), jnp.float32)
    zero_s5 = jnp.zeros((nb, S5_GROUPS, S5_STATE, 2), jnp.float32)
    ssd_f_list, ssd_b_list, s5_f_list, s5_b_list = [], [], [], []
    for l in range(DEPTH):
        pl = {k: v[l] for k, v in layer_params.items()}
        xc, sf, sb, qf, qb = trunk_layer(xc, c_ctx[None, :], zero_ssd, zero_ssd, zero_s5, zero_s5, False, pl)
        ssd_f_list.append(sf); ssd_b_list.append(sb); s5_f_list.append(qf); s5_b_list.append(qb)
    y_prompt = rmsnorm(xc, norm_f_g)
    new_ssd_fwd = jnp.stack(ssd_f_list, axis=1)
    new_ssd_bwd = jnp.stack(ssd_b_list, axis=1)
    new_s5_fwd = jnp.stack(s5_f_list, axis=1)
    new_s5_bwd = jnp.stack(s5_b_list, axis=1)

    xs = x_sample
    for l in range(DEPTH):
        pl = {k: v[l] for k, v in layer_params.items()}
        xs, _, _, _, _ = trunk_layer(xs, c, state_ssd_fwd[:, l], state_ssd_bwd[:, l],
                                     state_s5_fwd[:, l], state_s5_bwd[:, l], True, pl)
    y_sample = rmsnorm(xs, norm_f_g)
    return (y_prompt, y_sample, new_ssd_fwd, new_ssd_bwd, new_s5_fwd, new_s5_bwd)
```

```python
import functools
import math

import jax
import jax.numpy as jnp
from jax import lax
from jax.experimental import pallas as pl
from jax.experimental.pallas import tpu as pltpu

F32 = jnp.float32
BF16 = jnp.bfloat16

D_MODEL = 1024
GRID_W = 64
SSD_WIDTH = 512
SSD_HEAD_DIM = 64
SSD_HEADS = 8
SSD_GROUPS = 2
SSD_HEADS_PER_GROUP = SSD_HEADS // SSD_GROUPS
SSD_STATE = 128
SSD_CHUNK = 128
SSD_CONV_CH = SSD_WIDTH + 2 * SSD_GROUPS * SSD_STATE
S5_WIDTH = 512
S5_CH = 16
S5_GROUPS = 32
S5_STATE = 64
N_EXPERTS = 32
TOP_K = 4
D_FF = 1024
SWIGLU_LIMIT = 7.0
SWIGLU_ALPHA = 1.702
RMS_EPS = 1e-5

LANES = 128
DT_PAD = 128
S5_BATCH = 8
S5_COLS = 2 * S5_GROUPS * S5_STATE
S5_BLK = 512
S5_SUB_STEPS = 32
S5_BLOCK_STEPS = 256
MOE_TM = 256
VMEM_LIMIT = 56 * 1024 * 1024


def _sigmoid(x):
    return 1.0 / (1.0 + jnp.exp(-x))


def _split3(x):
    hi = x.astype(BF16)
    r1 = x - hi.astype(F32)
    mid = r1.astype(BF16)
    lo = (r1 - mid.astype(F32)).astype(BF16)
    return hi, mid, lo


def _dot(a, b):
    return jnp.dot(a, b, preferred_element_type=F32)


def _ada_kernel(c_ref, w_ref, b_ref, o_ref):
    c = c_ref[...]
    s = c * _sigmoid(c)
    o_ref[...] = _dot(s.astype(BF16), w_ref[...].astype(BF16)) + b_ref[...]


def _ada_call(conds, w_ada, b_ada):
    n = w_ada.shape[1]
    tn = 1536
    rows = conds.shape[0]
    return pl.pallas_call(
        _ada_kernel,
        grid=(n // tn,),
        in_specs=[pl.BlockSpec((rows, D_MODEL), lambda j: (0, 0)),
                  pl.BlockSpec((D_MODEL, tn), lambda j: (0, j)),
                  pl.BlockSpec((1, tn), lambda j: (0, j))],
        out_specs=pl.BlockSpec((rows, tn), lambda j: (0, j)),
        out_shape=jax.ShapeDtypeStruct((rows, n), F32),
        compiler_params=pltpu.CompilerParams(dimension_semantics=("arbitrary",),
                                             vmem_limit_bytes=VMEM_LIMIT),
        name="ada_mod",
    )(conds, w_ada, b_ada.reshape(1, n))


def _inproj_kernel(x_ref, mod_ref, g_ref, w_ref, z_ref, xbc_ref, u_ref, dt_ref):
    x = x_ref[0]
    ms = jnp.mean(x * x, axis=-1, keepdims=True)
    y = x * lax.rsqrt(ms + RMS_EPS) * g_ref[...]
    sh = mod_ref[0, :, 0:D_MODEL]
    sc = mod_ref[0, :, D_MODEL:2 * D_MODEL]
    h = (y * (1.0 + sc) + sh).astype(BF16)
    z_ref[0] = _dot(h, w_ref[:, 0:512]).astype(BF16)
    xbc_ref[0] = _dot(h, w_ref[:, 512:1536]).astype(BF16)
    u_ref[0] = _dot(h, w_ref[:, 1536:2048])
    dt_ref[0] = _dot(h, w_ref[:, 2048:2048 + DT_PAD])


def _inproj_call(x, mod, norm_g, w_in_r):
    b, l, _ = x.shape
    tl = min(l, 512)
    per_batch = mod.shape[0] > 1
    mod_map = (lambda i, j: (i, 0, 0)) if per_batch else (lambda i, j: (0, 0, 0))
    ncol = w_in_r.shape[1]
    tok = lambda w: pl.BlockSpec((1, tl, w), lambda i, j: (i, j, 0))
    return pl.pallas_call(
        _inproj_kernel,
        grid=(b, l // tl),
        in_specs=[tok(D_MODEL),
                  pl.BlockSpec((1, 1, 6 * D_MODEL), mod_map),
                  pl.BlockSpec((1, D_MODEL), lambda i, j: (0, 0)),
                  pl.BlockSpec((D_MODEL, ncol), lambda i, j: (0, 0))],
        out_specs=[tok(SSD_WIDTH), tok(SSD_CONV_CH), tok(S5_WIDTH), tok(DT_PAD)],
        out_shape=[jax.ShapeDtypeStruct((b, l, SSD_WIDTH), BF16),
                   jax.ShapeDtypeStruct((b, l, SSD_CONV_CH), BF16),
                   jax.ShapeDtypeStruct((b, l, S5_WIDTH), F32),
                   jax.ShapeDtypeStruct((b, l, DT_PAD), F32)],
        compiler_params=pltpu.CompilerParams(dimension_semantics=("arbitrary", "arbitrary"),
                                             vmem_limit_bytes=VMEM_LIMIT),
        name="norm1_inproj",
    )(x, mod, norm_g.reshape(1, D_MODEL), w_in_r)


def _softplus(x):
    return jnp.maximum(x, 0.0) + jnp.log1p(jnp.exp(-jnp.abs(x)))


def _ssd_kernel(z_ref, xbc_ref, dt_ref, s0f_ref, s0b_ref, cw_ref, cb_ref, dtb_ref, a_ref,
                dskip_ref, ng_ref, y_ref, sf_ref, sb_ref, xc_s, dts_s, yf_s, st_s, *, seq):
    q = SSD_CHUNK
    nc = seq // q
    row_i = lax.broadcasted_iota(jnp.int32, (q, 1), 0)

    def conv_body(c, carry):
        r0 = pl.multiple_of(c * q, q)
        cur = xbc_ref[0, pl.ds(r0, q), :].astype(F32)
        p0 = pl.multiple_of(jnp.maximum(r0 - 16, 0), 16)
        n0 = pl.multiple_of(jnp.minimum(r0 + q, seq - 16), 16)
        prev_row = xbc_ref[0, pl.ds(p0, 16), :].astype(F32)[15:16, :]
        next_row = xbc_ref[0, pl.ds(n0, 16), :].astype(F32)[0:1, :]
        prev_row = jnp.where(c > 0, prev_row, 0.0)
        next_row = jnp.where(c < nc - 1, next_row, 0.0)
        down = jnp.where(row_i == 0, prev_row, pltpu.roll(cur, 1, axis=0))
        up = jnp.where(row_i == q - 1, next_row, pltpu.roll(cur, q - 1, axis=0))
        v = cw_ref[0:1, :] * down + cw_ref[1:2, :] * cur + cw_ref[2:3, :] * up + cb_ref[...]
        xc_s[pl.ds(r0, q), :] = v * _sigmoid(v)
        dts_s[pl.ds(r0, q), :] = _softplus(dt_ref[0, pl.ds(r0, q), :] + dtb_ref[...])
        return carry

    lax.fori_loop(0, nc, conv_body, 0)

    li = lax.broadcasted_iota(jnp.int32, (q, q), 0)
    si = lax.broadcasted_iota(jnp.int32, (q, q), 1)

    def chunk(c, direction):
        r0 = pl.multiple_of(c * q, q)
        causal = (li >= si) if direction == 0 else (li <= si)
        tri = jnp.where(causal, 1.0, 0.0).astype(BF16)
        dts = dts_s[pl.ds(r0, q), :]
        dta = dts * a_ref[...]
        d_hi, d_mid, d_lo = _split3(dta)
        cs = _dot(tri, d_hi) + _dot(tri, d_mid) + _dot(tri, d_lo)
        tot = cs[q - 1:q, :] if direction == 0 else cs[0:1, :]
        cs_t = cs.T
        ecs = jnp.exp(cs)
        tail = jnp.exp(tot - cs)
        etot = jnp.exp(tot)
        for g in range(SSD_GROUPS):
            b_off = SSD_WIDTH + g * SSD_STATE
            c_off = SSD_WIDTH + SSD_GROUPS * SSD_STATE + g * SSD_STATE
            bm = xc_s[pl.ds(r0, q), b_off:b_off + SSD_STATE]
            cm = xc_s[pl.ds(r0, q), c_off:c_off + SSD_STATE].astype(BF16)
            bm_t = bm.T.astype(BF16)
            cb = lax.dot_general(cm, bm.astype(BF16), (((1,), (1,)), ((), ())),
                                 preferred_element_type=F32)
            st_g = st_s[g]
            y_off = _dot(cm, st_g.astype(BF16))
            for hh in range(SSD_HEADS_PER_GROUP):
                h = g * SSD_HEADS_PER_GROUP + hh
                col = h + SSD_HEADS * direction
                lo, hi = h * SSD_HEAD_DIM, (h + 1) * SSD_HEAD_DIM
                slo, shi = hh * SSD_HEAD_DIM, (hh + 1) * SSD_HEAD_DIM
                x_h = xc_s[pl.ds(r0, q), lo:hi]
                xdt = x_h * dts[:, col:col + 1]
                seg = cs[:, col:col + 1] - cs_t[col:col + 1, :]
                dec = jnp.exp(jnp.where(causal, seg, -1e30))
                scores = (cb * dec).astype(BF16)
                y_h = _dot(scores, xdt.astype(BF16)) + y_off[:, slo:shi] * ecs[:, col:col + 1]
                xw = (xdt * tail[:, col:col + 1]).astype(BF16)
                st_s[g, :, slo:shi] = etot[:, col:col + 1] * st_g[:, slo:shi] + _dot(bm_t, xw)
                if direction == 0:
                    yf_s[pl.ds(r0, q), lo:hi] = y_h
                else:
                    yf_s[pl.ds(r0, q), lo:hi] = yf_s[pl.ds(r0, q), lo:hi] + y_h
        if direction == 1:
            xs = xc_s[pl.ds(r0, q), 0:SSD_WIDTH]
            zz = z_ref[0, pl.ds(r0, q), :].astype(F32)
            y = (yf_s[pl.ds(r0, q), :] + dskip_ref[...] * xs) * (zz * _sigmoid(zz))
            ms = jnp.mean(y * y, axis=-1, keepdims=True)
            y_ref[0, pl.ds(r0, q), :] = (y * lax.rsqrt(ms + RMS_EPS) * ng_ref[...]).astype(y_ref.dtype)

    st_s[...] = s0f_ref[0]
    lax.fori_loop(0, nc, lambda i, carry: (chunk(i, 0), carry)[1], 0)
    sf_ref[0] = st_s[...]
    st_s[...] = s0b_ref[0]
    lax.fori_loop(0, nc, lambda i, carry: (chunk(nc - 1 - i, 1), carry)[1], 0)
    sb_ref[0] = st_s[...]


def _ssd_state_in(s):
    b = s.shape[0]
    s = s.astype(F32).reshape(b, SSD_GROUPS, SSD_HEADS_PER_GROUP, SSD_HEAD_DIM, SSD_STATE)
    return s.transpose(0, 1, 4, 2, 3).reshape(b, SSD_GROUPS, SSD_STATE, SSD_HEADS_PER_GROUP * SSD_HEAD_DIM)


def _ssd_state_out(s):
    b = s.shape[0]
    s = s.reshape(b, SSD_GROUPS, SSD_STATE, SSD_HEADS_PER_GROUP, SSD_HEAD_DIM)
    return s.transpose(0, 1, 3, 4, 2).reshape(b, SSD_HEADS, SSD_HEAD_DIM, SSD_STATE)


def _ssd_call(z, xbc, dt, s0f, s0b, prm):
    b, l, _ = z.shape
    st_shape = (SSD_GROUPS, SSD_STATE, SSD_HEADS_PER_GROUP * SSD_HEAD_DIM)
    tok = lambda w: pl.BlockSpec((1, l, w), lambda i: (i, 0, 0))
    st_spec = pl.BlockSpec((1,) + st_shape, lambda i: (i, 0, 0, 0))
    par = lambda r, w: pl.BlockSpec((r, w), lambda i: (0, 0))
    y, sf, sb = pl.pallas_call(
        functools.partial(_ssd_kernel, seq=l),
        grid=(b,),
        in_specs=[tok(SSD_WIDTH), tok(SSD_CONV_CH), tok(DT_PAD), st_spec, st_spec,
                  par(8, SSD_CONV_CH), par(1, SSD_CONV_CH), par(1, DT_PAD), par(1, DT_PAD),
                  par(1, SSD_WIDTH), par(1, SSD_WIDTH)],
        out_specs=[tok(SSD_WIDTH), st_spec, st_spec],
        out_shape=[jax.ShapeDtypeStruct((b, l, SSD_WIDTH), BF16),
                   jax.ShapeDtypeStruct((b,) + st_shape, F32),
                   jax.ShapeDtypeStruct((b,) + st_shape, F32)],
        scratch_shapes=[pltpu.VMEM((l, SSD_CONV_CH), F32),
                        pltpu.VMEM((l, DT_PAD), F32),
                        pltpu.VMEM((l, SSD_WIDTH), F32),
                        pltpu.VMEM(st_shape, F32)],
        compiler_params=pltpu.CompilerParams(dimension_semantics=("arbitrary",),
                                             vmem_limit_bytes=VMEM_LIMIT),
        name="ssd_mixer",
    )(z, xbc, dt, _ssd_state_in(s0f), _ssd_state_in(s0b),
      prm["conv_w"], prm["conv_b"], prm["dt_bias"], prm["a"], prm["d_skip"], prm["norm_g"])
    return y, _ssd_state_out(sf), _ssd_state_out(sb)


def _gelu_tanh(x):
    return 0.5 * x * (1.0 + jnp.tanh(math.sqrt(2.0 / math.pi) * (x + 0.044715 * (x * x * x))))


def _s5_kernel(*refs, grid_cols, reverse, final):
    if final:
        (u_ref, s0_ref, lr_ref, li_ref, wb_ref, wc_ref, yf_ref, d_ref, wg_ref, bg_ref,
         o_ref, sfin_ref, ur_s, bu_s, yr_s, st_s) = refs
    else:
        (u_ref, s0_ref, lr_ref, li_ref, wb_ref, wc_ref,
         o_ref, sfin_ref, ur_s, bu_s, yr_s, st_s) = refs
    nb = S5_BATCH
    steps = S5_BLOCK_STEPS
    rows = steps * nb
    sub_rows = S5_SUB_STEPS * nb
    n_sub = steps // S5_SUB_STEPS
    n_rows = u_ref.shape[1] if grid_cols else None
    j = pl.program_id(1)

    @pl.when(j == 0)
    def _():
        st_s[...] = s0_ref[0]

    nlt = S5_WIDTH // LANES

    def put_strided(dst, start, count, stride, val):
        for k in range(nlt):
            dst[k, pl.ds(start, count, stride=stride), :] = val[:, k * LANES:(k + 1) * LANES]

    def get_strided(src, start, count, stride):
        return jnp.concatenate([src[k, pl.ds(start, count, stride=stride), :] for k in range(nlt)], axis=1)

    def get_rows(src, r0, n):
        return jnp.concatenate([src[k, pl.ds(r0, n), :] for k in range(nlt)], axis=1)

    if grid_cols:
        n_cols = u_ref.shape[2]
        for b in range(nb):
            for r in range(n_rows):
                put_strided(ur_s, r * nb + b, n_cols, n_rows * nb, u_ref[b, r])
    else:
        for b in range(nb):
            put_strided(ur_s, b, steps, nb, u_ref[b])

    def sub_chunk(i, carry):
        sidx = (n_sub - 1 - i) if reverse else i
        r0 = pl.multiple_of(sidx * sub_rows, sub_rows)
        for hf in range(2):
            uh = jnp.concatenate([ur_s[2 * hf, pl.ds(r0, sub_rows), :],
                                  ur_s[2 * hf + 1, pl.ds(r0, sub_rows), :]], axis=1).astype(BF16)
            bu_s[:, hf * 2048:(hf + 1) * 2048] = _dot(uh, wb_ref[hf])
        for blk in range(S5_COLS // S5_BLK):
            c0 = blk * S5_BLK
            lam_r = jnp.broadcast_to(lr_ref[:, c0:c0 + S5_BLK], (nb, S5_BLK))
            lam_i = jnp.broadcast_to(li_ref[:, c0:c0 + S5_BLK], (nb, S5_BLK))

            def step(t, s):
                tt = (S5_SUB_STEPS - 1 - t) if reverse else t
                rr = pl.multiple_of(tt * nb, nb)
                sw = jnp.concatenate([s[:, S5_BLK // 2:], s[:, :S5_BLK // 2]], axis=1)
                s_new = lam_r * s + lam_i * sw + bu_s[pl.ds(rr, nb), c0:c0 + S5_BLK]
                bu_s[pl.ds(rr, nb), c0:c0 + S5_BLK] = s_new
                return s_new

            st_s[:, c0:c0 + S5_BLK] = lax.fori_loop(0, S5_SUB_STEPS, step, st_s[:, c0:c0 + S5_BLK],
                                                    unroll=4)
            half = (blk % 2) * 64
            yr_s[blk // 2, pl.ds(r0, sub_rows), half:half + 64] = _dot(
                bu_s[:, c0:c0 + S5_BLK].astype(BF16), wc_ref[blk])
        return carry

    lax.fori_loop(0, n_sub, sub_chunk, 0)

    @pl.when(j == pl.num_programs(1) - 1)
    def _():
        sfin_ref[0] = st_s[...]

    if not final:
        for k in range(nlt):
            o_ref[0, :, k * LANES:(k + 1) * LANES] = yr_s[k]
        return

    def glu_chunk(i, carry):
        r0 = pl.multiple_of(i * sub_rows, sub_rows)
        y = (get_rows(yr_s, r0, sub_rows) + yf_ref[0, pl.ds(r0, sub_rows), :]
             + d_ref[...] * get_rows(ur_s, r0, sub_rows))
        y = _gelu_tanh(y).astype(BF16)
        val = _dot(y, wg_ref[:, 0:S5_WIDTH]) + bg_ref[:, 0:S5_WIDTH]
        gate = _dot(y, wg_ref[:, S5_WIDTH:2 * S5_WIDTH]) + bg_ref[:, S5_WIDTH:2 * S5_WIDTH]
        res = val * _sigmoid(gate)
        for k in range(nlt):
            yr_s[k, pl.ds(r0, sub_rows), :] = res[:, k * LANES:(k + 1) * LANES]
        return carry

    lax.fori_loop(0, n_sub, glu_chunk, 0)

    if grid_cols:
        for b in range(nb):
            for r in range(n_rows):
                o_ref[b, r] = get_strided(yr_s, r * nb + b, n_cols, n_rows * nb)
    else:
        for b in range(nb):
            o_ref[b] = get_strided(yr_s, b, steps, nb)


def _s5_call(u, s0, prm, grid_cols, reverse, y_first=None, glu=None):
    b, l, _ = u.shape
    nbg = b // S5_BATCH
    nblk = l // S5_BLOCK_STEPS
    rows = S5_BLOCK_STEPS * S5_BATCH
    final = y_first is not None
    blk_of = (lambda j: nblk - 1 - j) if reverse else (lambda j: j)
    if grid_cols:
        n_rows = l // GRID_W
        u_in = u.reshape(b, n_rows, GRID_W, S5_WIDTH)
        cols_per_blk = S5_BLOCK_STEPS // n_rows
        tok_spec = pl.BlockSpec((S5_BATCH, n_rows, cols_per_blk, S5_WIDTH),
                                lambda i, j: (i, 0, blk_of(j), 0))
        out_nat = jax.ShapeDtypeStruct((b, n_rows, GRID_W, S5_WIDTH), F32)
    else:
        u_in = u
        tok_spec = pl.BlockSpec((S5_BATCH, S5_BLOCK_STEPS, S5_WIDTH), lambda i, j: (i, blk_of(j), 0))
        out_nat = jax.ShapeDtypeStruct((b, l, S5_WIDTH), F32)
    scan_spec = pl.BlockSpec((1, rows, S5_WIDTH), lambda i, j: (i, blk_of(j), 0))
    st_spec = pl.BlockSpec((1, S5_BATCH, S5_COLS), lambda i, j: (i, 0, 0))
    full = lambda shape: pl.BlockSpec(shape, lambda i, j: (0,) * len(shape))
    in_specs = [tok_spec, st_spec, full((1, S5_COLS)), full((1, S5_COLS)),
                full((2, 256, 2048)), full((S5_COLS // S5_BLK, S5_BLK, 64))]
    args = [u_in, s0, prm["lam_r"], prm["lam_i"], prm["wb"], prm["wc"]]
    if final:
        in_specs += [scan_spec, full((1, S5_WIDTH)), full((S5_WIDTH, 2 * S5_WIDTH)), full((1, 2 * S5_WIDTH))]
        args += [y_first, glu["d"], glu["w"], glu["b"]]
        out_specs = [tok_spec, st_spec]
        out_shape = [out_nat, jax.ShapeDtypeStruct((nbg, S5_BATCH, S5_COLS), F32)]
    else:
        out_specs = [scan_spec, st_spec]
        out_shape = [jax.ShapeDtypeStruct((nbg, l * S5_BATCH, S5_WIDTH), F32),
                     jax.ShapeDtypeStruct((nbg, S5_BATCH, S5_COLS), F32)]
    y, sfin = pl.pallas_call(
        functools.partial(_s5_kernel, grid_cols=grid_cols, reverse=reverse, final=final),
        grid=(nbg, nblk),
        in_specs=in_specs,
        out_specs=out_specs,
        out_shape=out_shape,
        scratch_shapes=[pltpu.VMEM((S5_WIDTH // LANES, rows, LANES), F32),
                        pltpu.VMEM((S5_SUB_STEPS * S5_BATCH, S5_COLS), F32),
                        pltpu.VMEM((S5_WIDTH // LANES, rows, LANES), F32),
                        pltpu.VMEM((S5_BATCH, S5_COLS), F32)],
        compiler_params=pltpu.CompilerParams(dimension_semantics=("arbitrary", "arbitrary"),
                                             vmem_limit_bytes=VMEM_LIMIT),
        name="s5_final" if final else "s5_first",
    )(*args)
    if final:
        y = y.reshape(b, l, S5_WIDTH)
    return y, sfin


def _s5_cols(t):
    lead = t.shape[:-2]
    return t.reshape(lead + (S5_COLS // S5_BLK, S5_BLK // 2))


def _s5_state_in(s0):
    b = s0.shape[0]
    re, im = _s5_cols(s0[..., 0].astype(F32)), _s5_cols(s0[..., 1].astype(F32))
    return jnp.stack([re, im], axis=-2).reshape(b // S5_BATCH, S5_BATCH, S5_COLS)


def _s5_state_out(s):
    b = s.shape[0] * S5_BATCH
    s = s.reshape(b, S5_COLS // S5_BLK, 2, S5_BLK // 2)
    re = s[:, :, 0].reshape(b, S5_GROUPS, S5_STATE)
    im = s[:, :, 1].reshape(b, S5_GROUPS, S5_STATE)
    return jnp.stack([re, im], axis=-1)


def _s5_params(a_re, a_im, log_dt, b_re, b_im, c_re, c_im):
    a_re, a_im = a_re.astype(F32), a_im.astype(F32)
    delta = jnp.exp(log_dt.astype(F32))[:, None]
    mag = jnp.exp(a_re * delta)
    lam_re, lam_im = mag * jnp.cos(a_im * delta), mag * jnp.sin(a_im * delta)
    den = a_re * a_re + a_im * a_im
    nr = lam_re - 1.0
    f_re = (nr * a_re + lam_im * a_im) / den
    f_im = (lam_im * a_re - nr * a_im) / den
    b_re, b_im = b_re.astype(F32), b_im.astype(F32)
    bb_re = f_re[..., None] * b_re - f_im[..., None] * b_im
    bb_im = f_re[..., None] * b_im + f_im[..., None] * b_re
    nblk = S5_COLS // S5_BLK
    lr = _s5_cols(lam_re)
    li = _s5_cols(lam_im)
    lam_r = jnp.stack([lr, lr], axis=-2).reshape(1, S5_COLS)
    lam_i = jnp.stack([-li, li], axis=-2).reshape(1, S5_COLS)
    gq = S5_BLK // 2 // S5_STATE
    eye = jnp.eye(gq, dtype=F32)

    def in_block(bb):
        t = bb.reshape(nblk, gq, S5_STATE, S5_CH)
        return jnp.einsum("gh,kgpc->kgchp", eye, t).reshape(nblk, gq * S5_CH, gq * S5_STATE)

    blk_in = jnp.concatenate([in_block(bb_re), in_block(bb_im)], axis=-1)
    half = nblk // 2
    eye_h = jnp.eye(half, dtype=F32)
    wb = jnp.einsum("jk,hkcn->hjckn", eye_h, blk_in.reshape(2, half, gq * S5_CH, S5_BLK))
    wb = wb.reshape(2, half * gq * S5_CH, half * S5_BLK).astype(BF16)

    def out_block(cc):
        t = cc.astype(F32).reshape(nblk, gq, S5_CH, S5_STATE)
        return jnp.einsum("gh,kgcp->kgphc", eye, t).reshape(nblk, gq * S5_STATE, gq * S5_CH)

    wc = jnp.concatenate([out_block(c_re), -out_block(c_im)], axis=1).astype(BF16)
    return {"lam_r": lam_r, "lam_i": lam_i, "wb": wb, "wc": wc}


def _post_kernel(x_ref, ys_ref, y5_ref, mod_ref, wo_ref, g_ref, wr_ref, br_ref, cnt_in_ref,
                 x1_ref, h_ref, ti_ref, tw_ref, rk_ref, cnt_ref, cnt_s):
    first = jnp.logical_and(pl.program_id(0) == 0, pl.program_id(1) == 0)

    @pl.when(first)
    def _():
        cnt_s[...] = cnt_in_ref[...]

    tl = x_ref.shape[1]
    mixed = (_dot(ys_ref[0], wo_ref[0:SSD_WIDTH, :])
             + _dot(y5_ref[0].astype(BF16), wo_ref[SSD_WIDTH:SSD_WIDTH + S5_WIDTH, :]))
    g1 = mod_ref[0, :, 2 * D_MODEL:3 * D_MODEL]
    sh2 = mod_ref[0, :, 3 * D_MODEL:4 * D_MODEL]
    sc2 = mod_ref[0, :, 4 * D_MODEL:5 * D_MODEL]
    x1 = x_ref[0] + g1 * mixed
    x1_ref[0] = x1
    ms = jnp.mean(x1 * x1, axis=-1, keepdims=True)
    h = x1 * lax.rsqrt(ms + RMS_EPS) * g_ref[...] * (1.0 + sc2) + sh2
    h_ref[0] = h.astype(BF16)

    h_hi, h_mid, _ = _split3(h)
    wr = wr_ref[...]
    w_hi, w_mid, _ = _split3(wr)
    logits = _dot(h_hi, w_hi) + _dot(h_hi, w_mid) + _dot(h_mid, w_hi) + br_ref[...]

    lane = lax.broadcasted_iota(jnp.int32, (tl, N_EXPERTS), 1).astype(F32)
    slot = lax.broadcasted_iota(jnp.int32, (tl, TOP_K), 1)
    work = logits
    onehot = jnp.zeros((tl, N_EXPERTS), F32)
    sels, vals = [], []
    top_i = jnp.zeros((tl, TOP_K), jnp.int32)
    for k in range(TOP_K):
        m = jnp.max(work, axis=-1, keepdims=True)
        idx = jnp.min(jnp.where(work == m, lane, float(N_EXPERTS)), axis=-1, keepdims=True)
        sel = lane == idx
        sels.append(sel)
        vals.append(m)
        top_i = jnp.where(slot == k, idx.astype(jnp.int32), top_i)
        onehot = jnp.where(sel, 1.0, onehot)
        work = jnp.where(sel, -jnp.inf, work)
    es = [jnp.exp(v - vals[0]) for v in vals]
    den = es[0] + es[1] + es[2] + es[3]
    top_w = jnp.zeros((tl, TOP_K), F32)
    for k in range(TOP_K):
        top_w = jnp.where(slot == k, es[k] / den, top_w)

    ri = lax.broadcasted_iota(jnp.int32, (tl, tl), 0)
    ci = lax.broadcasted_iota(jnp.int32, (tl, tl), 1)
    strict = jnp.where(ri > ci, 1.0, 0.0).astype(BF16)
    before = _dot(strict, onehot.astype(BF16)) + cnt_s[...]
    rank = jnp.zeros((tl, TOP_K), jnp.int32)
    for k in range(TOP_K):
        rk = jnp.sum(jnp.where(sels[k], before, 0.0), axis=-1, keepdims=True)
        rank = jnp.where(slot == k, rk.astype(jnp.int32), rank)
    cnt_s[...] = cnt_s[...] + jnp.sum(onehot, axis=0, keepdims=True)
    ti_ref[0] = top_i
    tw_ref[0] = top_w
    rk_ref[0] = rank
    cnt_ref[...] = cnt_s[...]


def _post_call(x, y_ssd, y_s5, mod, w_out, norm_g, w_router, b_router, cnt_in):
    b, l, _ = x.shape
    tl = 256
    per_batch = mod.shape[0] > 1
    mod_map = (lambda i, j: (i, 0, 0)) if per_batch else (lambda i, j: (0, 0, 0))
    tok = lambda w: pl.BlockSpec((1, tl, w), lambda i, j: (i, j, 0))
    full = lambda r, w: pl.BlockSpec((r, w), lambda i, j: (0, 0))
    return pl.pallas_call(
        _post_kernel,
        grid=(b, l // tl),
        in_specs=[tok(D_MODEL), tok(SSD_WIDTH), tok(S5_WIDTH),
                  pl.BlockSpec((1, 1, 6 * D_MODEL), mod_map),
                  full(D_MODEL, D_MODEL), full(1, D_MODEL), full(D_MODEL, N_EXPERTS),
                  full(1, N_EXPERTS), full(1, N_EXPERTS)],
        out_specs=[tok(D_MODEL), tok(D_MODEL), tok(TOP_K), tok(TOP_K), tok(TOP_K), full(1, N_EXPERTS)],
        out_shape=[jax.ShapeDtypeStruct((b, l, D_MODEL), F32),
                   jax.ShapeDtypeStruct((b, l, D_MODEL), BF16),
                   jax.ShapeDtypeStruct((b, l, TOP_K), jnp.int32),
                   jax.ShapeDtypeStruct((b, l, TOP_K), F32),
                   jax.ShapeDtypeStruct((b, l, TOP_K), jnp.int32),
                   jax.ShapeDtypeStruct((1, N_EXPERTS), F32)],
        scratch_shapes=[pltpu.VMEM((1, N_EXPERTS), F32)],
        compiler_params=pltpu.CompilerParams(dimension_semantics=("arbitrary", "arbitrary"),
                                             vmem_limit_bytes=VMEM_LIMIT),
        name="outproj_norm2_router",
    )(x, y_ssd, y_s5, mod, w_out, norm_g.reshape(1, D_MODEL), w_router,
      b_router.reshape(1, N_EXPERTS), cnt_in)


def _moe_kernel(te_ref, nu_ref, x_ref, wgu_ref, bgu_ref, wd_ref, bd_ref, o_ref, gu_s, dn_s):
    i = pl.program_id(0)
    prev = te_ref[jnp.maximum(i - 1, 0)]
    new_expert = jnp.logical_or(i == 0, te_ref[i] != prev)

    @pl.when(new_expert)
    def _():
        for r in range(D_MODEL // 128):
            gu_s[r * 128:(r + 1) * 128, :] = wgu_ref[0, r * 128:(r + 1) * 128, :].astype(BF16)
            dn_s[r * 128:(r + 1) * 128, :] = wd_ref[0, r * 128:(r + 1) * 128, :].astype(BF16)

    @pl.when(i < nu_ref[0])
    def _():
        x = x_ref[...]
        gate = _dot(x, gu_s[:, 0:D_FF]) + bgu_ref[0, :, 0:D_FF]
        up = _dot(x, gu_s[:, D_FF:2 * D_FF]) + bgu_ref[0, :, D_FF:2 * D_FF]
        gate = jnp.minimum(gate, SWIGLU_LIMIT)
        up = jnp.clip(up, -SWIGLU_LIMIT, SWIGLU_LIMIT)
        act = gate * _sigmoid(SWIGLU_ALPHA * gate) * (up + 1.0)
        o_ref[...] = (_dot(act.astype(BF16), dn_s[...]) + bd_ref[0]).astype(o_ref.dtype)

    @pl.when(i >= nu_ref[0])
    def _():
        o_ref[...] = jnp.zeros(o_ref.shape, o_ref.dtype)


def _moe_call(tile_expert, n_used, xs, w_gate_up, b_gate_up, w_down, b_down):
    p = xs.shape[0]
    n_tiles = p // MOE_TM
    grid_spec = pltpu.PrefetchScalarGridSpec(
        num_scalar_prefetch=2,
        grid=(n_tiles,),
        in_specs=[pl.BlockSpec((MOE_TM, D_MODEL), lambda i, te, nu: (i, 0)),
                  pl.BlockSpec((1, D_MODEL, 2 * D_FF), lambda i, te, nu: (te[i], 0, 0)),
                  pl.BlockSpec((1, 1, 2 * D_FF), lambda i, te, nu: (te[i], 0, 0)),
                  pl.BlockSpec((1, D_FF, D_MODEL), lambda i, te, nu: (te[i], 0, 0)),
                  pl.BlockSpec((1, 1, D_MODEL), lambda i, te, nu: (te[i], 0, 0))],
        out_specs=pl.BlockSpec((MOE_TM, D_MODEL), lambda i, te, nu: (i, 0)),
        scratch_shapes=[pltpu.VMEM((D_MODEL, 2 * D_FF), BF16), pltpu.VMEM((D_FF, D_MODEL), BF16)],
    )
    return pl.pallas_call(
        _moe_kernel,
        grid_spec=grid_spec,
        out_shape=jax.ShapeDtypeStruct((p, D_MODEL), BF16),
        compiler_params=pltpu.CompilerParams(dimension_semantics=("arbitrary",),
                                             vmem_limit_bytes=VMEM_LIMIT),
        name="moe_experts",
    )(tile_expert, n_used, xs, w_gate_up, b_gate_up.reshape(N_EXPERTS, 1, 2 * D_FF),
      w_down, b_down.reshape(N_EXPERTS, 1, D_MODEL))


def _combine_kernel(x1_ref, yg_ref, tw_ref, mod_ref, g_ref, o_ref):
    tw = tw_ref[0]
    acc = tw[:, 0:1] * yg_ref[0, :, 0:D_MODEL].astype(F32)
    for k in range(1, TOP_K):
        acc = acc + tw[:, k:k + 1] * yg_ref[0, :, k * D_MODEL:(k + 1) * D_MODEL].astype(F32)
    g2 = mod_ref[0, :, 5 * D_MODEL:6 * D_MODEL]
    x2 = x1_ref[0] + g2 * acc
    ms = jnp.mean(x2 * x2, axis=-1, keepdims=True)
    o_ref[0] = x2 * lax.rsqrt(ms + RMS_EPS) * g_ref[...]


def _combine_call(x1, yg, top_w, mod, norm_g):
    b, l, _ = x1.shape
    tl = 256
    per_batch = mod.shape[0] > 1
    mod_map = (lambda i, j: (i, 0, 0)) if per_batch else (lambda i, j: (0, 0, 0))
    tok = lambda w: pl.BlockSpec((1, tl, w), lambda i, j: (i, j, 0))
    return pl.pallas_call(
        _combine_kernel,
        grid=(b, l // tl),
        in_specs=[tok(D_MODEL), tok(TOP_K * D_MODEL), tok(TOP_K),
                  pl.BlockSpec((1, 1, 6 * D_MODEL), mod_map),
                  pl.BlockSpec((1, D_MODEL), lambda i, j: (0, 0))],
        out_specs=tok(D_MODEL),
        out_shape=jax.ShapeDtypeStruct((b, l, D_MODEL), F32),
        compiler_params=pltpu.CompilerParams(dimension_semantics=("arbitrary", "arbitrary"),
                                             vmem_limit_bytes=VMEM_LIMIT),
        name="moe_combine_final_norm",
    )(x1, yg, top_w, mod, norm_g.reshape(1, D_MODEL))


def _route(top_i, rank, counts):
    t = top_i.shape[0]
    p = t * TOP_K + N_EXPERTS * MOE_TM
    n_tiles = p // MOE_TM
    padded = ((counts + MOE_TM - 1) // MOE_TM) * MOE_TM
    ends = jnp.cumsum(padded)
    starts = ends - padded
    dest = starts[top_i] + rank
    tile_start = jnp.arange(n_tiles, dtype=jnp.int32) * MOE_TM
    tile_expert = jnp.minimum(jnp.searchsorted(ends, tile_start, side="right"), N_EXPERTS - 1)
    n_used = (ends[-1] // MOE_TM).reshape(1)
    tok = jnp.broadcast_to(jnp.arange(t, dtype=jnp.int32)[:, None], (t, TOP_K))
    src_tok = jnp.zeros((p,), jnp.int32).at[dest.reshape(-1)].set(tok.reshape(-1), unique_indices=True)
    return dest, src_tok, tile_expert.astype(jnp.int32), n_used.astype(jnp.int32)


def _layer_front(x, mod, s_ssd_f, s_ssd_b, s_s5_f, s_s5_b, grid_cols, prm, cnt_in):
    z, xbc, u, dt = _inproj_call(x, mod, prm["norm1_g"], prm["w_in_r"])
    y_ssd, n_ssd_f, n_ssd_b = _ssd_call(z, xbc, dt, s_ssd_f, s_ssd_b, prm["ssd"])
    y_f, n_s5_f = _s5_call(u, _s5_state_in(s_s5_f), prm["s5_f"], grid_cols, False)
    y_s5, n_s5_b = _s5_call(u, _s5_state_in(s_s5_b), prm["s5_b"], grid_cols, True,
                            y_first=y_f, glu=prm["glu"])
    x1, h2, top_i, top_w, rank, cnt = _post_call(x, y_ssd, y_s5, mod, prm["w_out"], prm["norm2_g"],
                                                 prm["w_router"], prm["b_router"], cnt_in)
    states = (n_ssd_f, n_ssd_b, _s5_state_out(n_s5_f), _s5_state_out(n_s5_b))
    return x1, h2, top_i, top_w, rank, cnt, states


def kernel(x_prompt, x_sample, state_ssd_fwd, state_ssd_bwd, state_s5_fwd, state_s5_bwd, c, c_ctx, w_ada, b_ada, norm1_g, w_in, ssd_conv_w, ssd_conv_b, ssd_dt_bias_fwd, ssd_dt_bias_bwd, ssd_a_log_fwd, ssd_a_log_bwd, ssd_d, ssd_norm_g, s5_a_re_fwd, s5_a_im_fwd, s5_log_dt_fwd, s5_b_re_fwd, s5_b_im_fwd, s5_c_re_fwd, s5_c_im_fwd, s5_a_re_bwd, s5_a_im_bwd, s5_log_dt_bwd, s5_b_re_bwd, s5_b_im_bwd, s5_c_re_bwd, s5_c_im_bwd, s5_d, w_glu, b_glu, w_out, norm2_g, w_router, b_router, w_gate_up, b_gate_up, w_down, b_down, norm_f_g):
    depth = w_ada.shape[0]
    assert depth == 1, "single trunk layer"
    nb_ctx, l_ctx, _ = x_prompt.shape
    nb_lat, l_lat, _ = x_sample.shape

    w_in0 = w_in[0]
    dt_lo = SSD_WIDTH + SSD_CONV_CH
    dt_cols = jnp.pad(w_in0[:, dt_lo:dt_lo + 2 * SSD_HEADS], ((0, 0), (0, DT_PAD - 2 * SSD_HEADS)))
    w_in_r = jnp.concatenate([w_in0[:, :dt_lo], w_in0[:, dt_lo + 2 * SSD_HEADS:], dt_cols], axis=1).astype(BF16)
    pad_dt = lambda f, b: jnp.pad(jnp.concatenate([f, b]).astype(F32), (0, DT_PAD - 2 * SSD_HEADS)).reshape(1, DT_PAD)
    ssd_prm = {
        "conv_w": jnp.pad(ssd_conv_w[0].astype(F32), ((0, 5), (0, 0))),
        "conv_b": ssd_conv_b[0].astype(F32).reshape(1, SSD_CONV_CH),
        "dt_bias": pad_dt(ssd_dt_bias_fwd[0], ssd_dt_bias_bwd[0]),
        "a": pad_dt(-jnp.exp(ssd_a_log_fwd[0].astype(F32)), -jnp.exp(ssd_a_log_bwd[0].astype(F32))),
        "d_skip": jnp.repeat(ssd_d[0].astype(F32), SSD_HEAD_DIM).reshape(1, SSD_WIDTH),
        "norm_g": ssd_norm_g[0].astype(F32).reshape(1, SSD_WIDTH),
    }
    prm = {
        "norm1_g": norm1_g[0], "w_in_r": w_in_r, "ssd": ssd_prm,
        "s5_f": _s5_params(s5_a_re_fwd[0], s5_a_im_fwd[0], s5_log_dt_fwd[0], s5_b_re_fwd[0],
                           s5_b_im_fwd[0], s5_c_re_fwd[0], s5_c_im_fwd[0]),
        "s5_b": _s5_params(s5_a_re_bwd[0], s5_a_im_bwd[0], s5_log_dt_bwd[0], s5_b_re_bwd[0],
                           s5_b_im_bwd[0], s5_c_re_bwd[0], s5_c_im_bwd[0]),
        "glu": {"d": s5_d[0].astype(F32).reshape(1, S5_WIDTH), "w": w_glu[0].astype(BF16),
                "b": b_glu[0].astype(F32).reshape(1, 2 * S5_WIDTH)},
        "w_out": w_out[0].astype(BF16), "norm2_g": norm2_g[0],
        "w_router": w_router[0].astype(F32), "b_router": b_router[0].astype(F32),
    }

    conds = jnp.concatenate([c_ctx[None, :], c], axis=0)
    conds = jnp.pad(conds, ((0, (-conds.shape[0]) % 8), (0, 0)))
    mod = _ada_call(conds, w_ada[0], b_ada[0])
    mod_ctx = mod[0:1].reshape(1, 1, 6 * D_MODEL)
    mod_lat = mod[1:1 + nb_lat].reshape(nb_lat, 1, 6 * D_MODEL)

    zero_ssd = jnp.zeros((nb_ctx, SSD_HEADS, SSD_HEAD_DIM, SSD_STATE), F32)
    zero_s5 = jnp.zeros((nb_ctx, S5_GROUPS, S5_STATE, 2), F32)
    cnt0 = jnp.zeros((1, N_EXPERTS), F32)
    x1_c, h2_c, ti_c, tw_c, rk_c, cnt_c, st_c = _layer_front(
        x_prompt, mod_ctx, zero_ssd, zero_ssd, zero_s5, zero_s5, False, prm, cnt0)
    x1_l, h2_l, ti_l, tw_l, rk_l, cnt_l, _ = _layer_front(
        x_sample, mod_lat, state_ssd_fwd[:, 0], state_ssd_bwd[:, 0], state_s5_fwd[:, 0],
        state_s5_bwd[:, 0], True, prm, cnt_c)

    t_ctx = nb_ctx * l_ctx
    top_i = jnp.concatenate([ti_c.reshape(-1, TOP_K), ti_l.reshape(-1, TOP_K)], axis=0)
    rank = jnp.concatenate([rk_c.reshape(-1, TOP_K), rk_l.reshape(-1, TOP_K)], axis=0)
    counts = cnt_l.reshape(N_EXPERTS).astype(jnp.int32)
    dest, src_tok, tile_expert, n_used = _route(top_i, rank, counts)
    h2 = jnp.concatenate([h2_c.reshape(-1, D_MODEL), h2_l.reshape(-1, D_MODEL)], axis=0)
    xs = jnp.take(h2, src_tok, axis=0)
    ys = _moe_call(tile_expert, n_used, xs, w_gate_up[0], b_gate_up[0], w_down[0], b_down[0])
    yg = jnp.take(ys, dest.reshape(-1), axis=0).reshape(-1, TOP_K * D_MODEL)
    y_prompt = _combine_call(x1_c, yg[:t_ctx].reshape(nb_ctx, l_ctx, TOP_K * D_MODEL), tw_c, mod_ctx, norm_f_g)
    y_sample = _combine_call(x1_l, yg[t_ctx:].reshape(nb_lat, l_lat, TOP_K * D_MODEL), tw_l, mod_lat, norm_f_g)

    ssd_f, ssd_b, s5_f, s5_b = st_c
    return (y_prompt, y_sample, ssd_f[:, None], ssd_b[:, None], s5_f[:, None], s5_b[:, None])
```

```python
import functools
import math

import jax
import jax.numpy as jnp
from jax import lax
from jax.experimental import pallas as pl
from jax.experimental.pallas import tpu as pltpu

F32 = jnp.float32
BF16 = jnp.bfloat16

D_MODEL = 1024
GRID_W = 64
SSD_WIDTH = 512
SSD_HEAD_DIM = 64
SSD_HEADS = 8
SSD_GROUPS = 2
SSD_HEADS_PER_GROUP = SSD_HEADS // SSD_GROUPS
SSD_STATE = 128
SSD_CHUNK = 128
SSD_CONV_CH = SSD_WIDTH + 2 * SSD_GROUPS * SSD_STATE
S5_WIDTH = 512
S5_CH = 16
S5_GROUPS = 32
S5_STATE = 64
N_EXPERTS = 32
TOP_K = 4
D_FF = 1024
SWIGLU_LIMIT = 7.0
SWIGLU_ALPHA = 1.702
RMS_EPS = 1e-5

LANES = 128
DT_PAD = 128
S5_BATCH = 8
S5_COLS = 2 * S5_GROUPS * S5_STATE
S5_BLK = 512
S5_SUB_STEPS = 32
S5_BLOCK_STEPS = 256
MOE_TM = 256
GATHER_SRC_ROWS = 32768
VMEM_LIMIT = 56 * 1024 * 1024


def _sigmoid(x):
    return 1.0 / (1.0 + jnp.exp(-x))


def _split3(x):
    hi = x.astype(BF16)
    r1 = x - hi.astype(F32)
    mid = r1.astype(BF16)
    lo = (r1 - mid.astype(F32)).astype(BF16)
    return hi, mid, lo


def _dot(a, b):
    return jnp.dot(a, b, preferred_element_type=F32)


def _ada_kernel(c_ref, w_ref, b_ref, o_ref):
    c = c_ref[...]
    s = c * _sigmoid(c)
    o_ref[...] = _dot(s.astype(BF16), w_ref[...].astype(BF16)) + b_ref[...]


def _ada_call(conds, w_ada, b_ada):
    n = w_ada.shape[1]
    tn = 1536
    rows = conds.shape[0]
    return pl.pallas_call(
        _ada_kernel,
        grid=(n // tn,),
        in_specs=[pl.BlockSpec((rows, D_MODEL), lambda j: (0, 0)),
                  pl.BlockSpec((D_MODEL, tn), lambda j: (0, j)),
                  pl.BlockSpec((1, tn), lambda j: (0, j))],
        out_specs=pl.BlockSpec((rows, tn), lambda j: (0, j)),
        out_shape=jax.ShapeDtypeStruct((rows, n), F32),
        compiler_params=pltpu.CompilerParams(dimension_semantics=("arbitrary",),
                                             vmem_limit_bytes=VMEM_LIMIT),
        name="ada_mod",
    )(conds, w_ada, b_ada.reshape(1, n))


def _inproj_kernel(x_ref, mod_ref, g_ref, w_ref, z_ref, xbc_ref, u_ref, dt_ref):
    x = x_ref[0]
    ms = jnp.mean(x * x, axis=-1, keepdims=True)
    y = x * lax.rsqrt(ms + RMS_EPS) * g_ref[...]
    sh = mod_ref[0, :, 0:D_MODEL]
    sc = mod_ref[0, :, D_MODEL:2 * D_MODEL]
    h = (y * (1.0 + sc) + sh).astype(BF16)
    z_ref[0] = _dot(h, w_ref[:, 0:512]).astype(BF16)
    xbc_ref[0] = _dot(h, w_ref[:, 512:1536]).astype(BF16)
    u_ref[0] = _dot(h, w_ref[:, 1536:2048])
    dt_ref[0] = _dot(h, w_ref[:, 2048:2048 + DT_PAD])


def _inproj_call(x, mod, norm_g, w_in_r):
    b, l, _ = x.shape
    tl = min(l, 512)
    per_batch = mod.shape[0] > 1
    mod_map = (lambda i, j: (i, 0, 0)) if per_batch else (lambda i, j: (0, 0, 0))
    ncol = w_in_r.shape[1]
    tok = lambda w: pl.BlockSpec((1, tl, w), lambda i, j: (i, j, 0))
    return pl.pallas_call(
        _inproj_kernel,
        grid=(b, l // tl),
        in_specs=[tok(D_MODEL),
                  pl.BlockSpec((1, 1, 6 * D_MODEL), mod_map),
                  pl.BlockSpec((1, D_MODEL), lambda i, j: (0, 0)),
                  pl.BlockSpec((D_MODEL, ncol), lambda i, j: (0, 0))],
        out_specs=[tok(SSD_WIDTH), tok(SSD_CONV_CH), tok(S5_WIDTH), tok(DT_PAD)],
        out_shape=[jax.ShapeDtypeStruct((b, l, SSD_WIDTH), BF16),
                   jax.ShapeDtypeStruct((b, l, SSD_CONV_CH), BF16),
                   jax.ShapeDtypeStruct((b, l, S5_WIDTH), F32),
                   jax.ShapeDtypeStruct((b, l, DT_PAD), F32)],
        compiler_params=pltpu.CompilerParams(dimension_semantics=("arbitrary", "arbitrary"),
                                             vmem_limit_bytes=VMEM_LIMIT),
        name="norm1_inproj",
    )(x, mod, norm_g.reshape(1, D_MODEL), w_in_r)


def _softplus(x):
    return jnp.maximum(x, 0.0) + jnp.log1p(jnp.exp(-jnp.abs(x)))


def _ssd_kernel(z_ref, xbc_ref, dt_ref, s0f_ref, s0b_ref, cw_ref, cb_ref, dtb_ref, a_ref,
                dskip_ref, ng_ref, y_ref, sf_ref, sb_ref, xc_s, dts_s, yf_s, st_s, *, seq):
    q = SSD_CHUNK
    nc = seq // q
    row_i = lax.broadcasted_iota(jnp.int32, (q, 1), 0)

    def conv_body(c, carry):
        r0 = pl.multiple_of(c * q, q)
        cur = xbc_ref[0, pl.ds(r0, q), :].astype(F32)
        p0 = pl.multiple_of(jnp.maximum(r0 - 16, 0), 16)
        n0 = pl.multiple_of(jnp.minimum(r0 + q, seq - 16), 16)
        prev_row = xbc_ref[0, pl.ds(p0, 16), :].astype(F32)[15:16, :]
        next_row = xbc_ref[0, pl.ds(n0, 16), :].astype(F32)[0:1, :]
        prev_row = jnp.where(c > 0, prev_row, 0.0)
        next_row = jnp.where(c < nc - 1, next_row, 0.0)
        down = jnp.where(row_i == 0, prev_row, pltpu.roll(cur, 1, axis=0))
        up = jnp.where(row_i == q - 1, next_row, pltpu.roll(cur, q - 1, axis=0))
        v = cw_ref[0:1, :] * down + cw_ref[1:2, :] * cur + cw_ref[2:3, :] * up + cb_ref[...]
        xc_s[pl.ds(r0, q), :] = v * _sigmoid(v)
        dts_s[pl.ds(r0, q), :] = _softplus(dt_ref[0, pl.ds(r0, q), :] + dtb_ref[...])
        return carry

    lax.fori_loop(0, nc, conv_body, 0)

    li = lax.broadcasted_iota(jnp.int32, (q, q), 0)
    si = lax.broadcasted_iota(jnp.int32, (q, q), 1)

    def chunk(c, direction):
        r0 = pl.multiple_of(c * q, q)
        causal = (li >= si) if direction == 0 else (li <= si)
        tri = jnp.where(causal, 1.0, 0.0).astype(BF16)
        dts = dts_s[pl.ds(r0, q), :]
        dta = dts * a_ref[...]
        d_hi, d_mid, d_lo = _split3(dta)
        cs = _dot(tri, d_hi) + _dot(tri, d_mid) + _dot(tri, d_lo)
        tot = cs[q - 1:q, :] if direction == 0 else cs[0:1, :]
        cs_t = cs.T
        ecs = jnp.exp(cs)
        tail = jnp.exp(tot - cs)
        etot = jnp.exp(tot)
        for g in range(SSD_GROUPS):
            b_off = SSD_WIDTH + g * SSD_STATE
            c_off = SSD_WIDTH + SSD_GROUPS * SSD_STATE + g * SSD_STATE
            bm = xc_s[pl.ds(r0, q), b_off:b_off + SSD_STATE]
            cm = xc_s[pl.ds(r0, q), c_off:c_off + SSD_STATE].astype(BF16)
            bm_t = bm.T.astype(BF16)
            cb = lax.dot_general(cm, bm.astype(BF16), (((1,), (1,)), ((), ())),
                                 preferred_element_type=F32)
            st_g = st_s[g]
            y_off = _dot(cm, st_g.astype(BF16))
            for hh in range(SSD_HEADS_PER_GROUP):
                h = g * SSD_HEADS_PER_GROUP + hh
                col = h + SSD_HEADS * direction
                lo, hi = h * SSD_HEAD_DIM, (h + 1) * SSD_HEAD_DIM
                slo, shi = hh * SSD_HEAD_DIM, (hh + 1) * SSD_HEAD_DIM
                x_h = xc_s[pl.ds(r0, q), lo:hi]
                xdt = x_h * dts[:, col:col + 1]
                seg = cs[:, col:col + 1] - cs_t[col:col + 1, :]
                dec = jnp.exp(jnp.where(causal, seg, -1e30))
                scores = (cb * dec).astype(BF16)
                y_h = _dot(scores, xdt.astype(BF16)) + y_off[:, slo:shi] * ecs[:, col:col + 1]
                xw = (xdt * tail[:, col:col + 1]).astype(BF16)
                st_s[g, :, slo:shi] = etot[:, col:col + 1] * st_g[:, slo:shi] + _dot(bm_t, xw)
                if direction == 0:
                    yf_s[pl.ds(r0, q), lo:hi] = y_h
                else:
                    yf_s[pl.ds(r0, q), lo:hi] = yf_s[pl.ds(r0, q), lo:hi] + y_h
        if direction == 1:
            xs = xc_s[pl.ds(r0, q), 0:SSD_WIDTH]
            zz = z_ref[0, pl.ds(r0, q), :].astype(F32)
            y = (yf_s[pl.ds(r0, q), :] + dskip_ref[...] * xs) * (zz * _sigmoid(zz))
            ms = jnp.mean(y * y, axis=-1, keepdims=True)
            y_ref[0, pl.ds(r0, q), :] = (y * lax.rsqrt(ms + RMS_EPS) * ng_ref[...]).astype(y_ref.dtype)

    st_s[...] = s0f_ref[0]
    lax.fori_loop(0, nc, lambda i, carry: (chunk(i, 0), carry)[1], 0)
    sf_ref[0] = st_s[...]
    st_s[...] = s0b_ref[0]
    lax.fori_loop(0, nc, lambda i, carry: (chunk(nc - 1 - i, 1), carry)[1], 0)
    sb_ref[0] = st_s[...]


def _ssd_state_in(s):
    b = s.shape[0]
    s = s.astype(F32).reshape(b, SSD_GROUPS, SSD_HEADS_PER_GROUP, SSD_HEAD_DIM, SSD_STATE)
    return s.transpose(0, 1, 4, 2, 3).reshape(b, SSD_GROUPS, SSD_STATE, SSD_HEADS_PER_GROUP * SSD_HEAD_DIM)


def _ssd_state_out(s):
    b = s.shape[0]
    s = s.reshape(b, SSD_GROUPS, SSD_STATE, SSD_HEADS_PER_GROUP, SSD_HEAD_DIM)
    return s.transpose(0, 1, 3, 4, 2).reshape(b, SSD_HEADS, SSD_HEAD_DIM, SSD_STATE)


def _ssd_call(z, xbc, dt, s0f, s0b, prm):
    b, l, _ = z.shape
    st_shape = (SSD_GROUPS, SSD_STATE, SSD_HEADS_PER_GROUP * SSD_HEAD_DIM)
    tok = lambda w: pl.BlockSpec((1, l, w), lambda i: (i, 0, 0))
    st_spec = pl.BlockSpec((1,) + st_shape, lambda i: (i, 0, 0, 0))
    par = lambda r, w: pl.BlockSpec((r, w), lambda i: (0, 0))
    y, sf, sb = pl.pallas_call(
        functools.partial(_ssd_kernel, seq=l),
        grid=(b,),
        in_specs=[tok(SSD_WIDTH), tok(SSD_CONV_CH), tok(DT_PAD), st_spec, st_spec,
                  par(8, SSD_CONV_CH), par(1, SSD_CONV_CH), par(1, DT_PAD), par(1, DT_PAD),
                  par(1, SSD_WIDTH), par(1, SSD_WIDTH)],
        out_specs=[tok(SSD_WIDTH), st_spec, st_spec],
        out_shape=[jax.ShapeDtypeStruct((b, l, SSD_WIDTH), BF16),
                   jax.ShapeDtypeStruct((b,) + st_shape, F32),
                   jax.ShapeDtypeStruct((b,) + st_shape, F32)],
        scratch_shapes=[pltpu.VMEM((l, SSD_CONV_CH), F32),
                        pltpu.VMEM((l, DT_PAD), F32),
                        pltpu.VMEM((l, SSD_WIDTH), F32),
                        pltpu.VMEM(st_shape, F32)],
        compiler_params=pltpu.CompilerParams(dimension_semantics=("arbitrary",),
                                             vmem_limit_bytes=VMEM_LIMIT),
        name="ssd_mixer",
    )(z, xbc, dt, _ssd_state_in(s0f), _ssd_state_in(s0b),
      prm["conv_w"], prm["conv_b"], prm["dt_bias"], prm["a"], prm["d_skip"], prm["norm_g"])
    return y, _ssd_state_out(sf), _ssd_state_out(sb)


def _gelu_tanh(x):
    return 0.5 * x * (1.0 + jnp.tanh(math.sqrt(2.0 / math.pi) * (x + 0.044715 * (x * x * x))))


def _s5_kernel(*refs, grid_cols, reverse, final):
    if final:
        (u_ref, s0_ref, lr_ref, li_ref, wb_ref, wc_ref, yf_ref, d_ref, wg_ref, bg_ref,
         o_ref, sfin_ref, ur_s, bu_s, yr_s, st_s) = refs
    else:
        (u_ref, s0_ref, lr_ref, li_ref, wb_ref, wc_ref,
         o_ref, sfin_ref, ur_s, bu_s, yr_s, st_s) = refs
    nb = S5_BATCH
    steps = S5_BLOCK_STEPS
    rows = steps * nb
    sub_rows = S5_SUB_STEPS * nb
    n_sub = steps // S5_SUB_STEPS
    n_rows = u_ref.shape[1] if grid_cols else None
    j = pl.program_id(1)

    @pl.when(j == 0)
    def _():
        st_s[...] = s0_ref[0]

    nlt = S5_WIDTH // LANES

    def put_strided(dst, start, count, stride, val):
        for k in range(nlt):
            dst[k, pl.ds(start, count, stride=stride), :] = val[:, k * LANES:(k + 1) * LANES]

    def get_strided(src, start, count, stride):
        return jnp.concatenate([src[k, pl.ds(start, count, stride=stride), :] for k in range(nlt)], axis=1)

    def get_rows(src, r0, n):
        return jnp.concatenate([src[k, pl.ds(r0, n), :] for k in range(nlt)], axis=1)

    if grid_cols:
        n_cols = u_ref.shape[2]
        for b in range(nb):
            for r in range(n_rows):
                put_strided(ur_s, r * nb + b, n_cols, n_rows * nb, u_ref[b, r])
    else:
        for b in range(nb):
            put_strided(ur_s, b, steps, nb, u_ref[b])

    def sub_chunk(i, carry):
        sidx = (n_sub - 1 - i) if reverse else i
        r0 = pl.multiple_of(sidx * sub_rows, sub_rows)
        for hf in range(2):
            uh = jnp.concatenate([ur_s[2 * hf, pl.ds(r0, sub_rows), :],
                                  ur_s[2 * hf + 1, pl.ds(r0, sub_rows), :]], axis=1).astype(BF16)
            bu_s[:, hf * 2048:(hf + 1) * 2048] = _dot(uh, wb_ref[hf])
        for blk in range(S5_COLS // S5_BLK):
            c0 = blk * S5_BLK
            lam_r = jnp.broadcast_to(lr_ref[:, c0:c0 + S5_BLK], (nb, S5_BLK))
            lam_i = jnp.broadcast_to(li_ref[:, c0:c0 + S5_BLK], (nb, S5_BLK))

            def step(t, s):
                tt = (S5_SUB_STEPS - 1 - t) if reverse else t
                rr = pl.multiple_of(tt * nb, nb)
                sw = jnp.concatenate([s[:, S5_BLK // 2:], s[:, :S5_BLK // 2]], axis=1)
                s_new = lam_r * s + lam_i * sw + bu_s[pl.ds(rr, nb), c0:c0 + S5_BLK]
                bu_s[pl.ds(rr, nb), c0:c0 + S5_BLK] = s_new
                return s_new

            st_s[:, c0:c0 + S5_BLK] = lax.fori_loop(0, S5_SUB_STEPS, step, st_s[:, c0:c0 + S5_BLK],
                                                    unroll=4)
            half = (blk % 2) * 64
            yr_s[blk // 2, pl.ds(r0, sub_rows), half:half + 64] = _dot(
                bu_s[:, c0:c0 + S5_BLK].astype(BF16), wc_ref[blk])
        return carry

    lax.fori_loop(0, n_sub, sub_chunk, 0)

    @pl.when(j == pl.num_programs(1) - 1)
    def _():
        sfin_ref[0] = st_s[...]

    if not final:
        for k in range(nlt):
            o_ref[0, :, k * LANES:(k + 1) * LANES] = yr_s[k]
        return

    def glu_chunk(i, carry):
        r0 = pl.multiple_of(i * sub_rows, sub_rows)
        y = (get_rows(yr_s, r0, sub_rows) + yf_ref[0, pl.ds(r0, sub_rows), :]
             + d_ref[...] * get_rows(ur_s, r0, sub_rows))
        y = _gelu_tanh(y).astype(BF16)
        val = _dot(y, wg_ref[:, 0:S5_WIDTH]) + bg_ref[:, 0:S5_WIDTH]
        gate = _dot(y, wg_ref[:, S5_WIDTH:2 * S5_WIDTH]) + bg_ref[:, S5_WIDTH:2 * S5_WIDTH]
        res = val * _sigmoid(gate)
        for k in range(nlt):
            yr_s[k, pl.ds(r0, sub_rows), :] = res[:, k * LANES:(k + 1) * LANES]
        return carry

    lax.fori_loop(0, n_sub, glu_chunk, 0)

    if grid_cols:
        for b in range(nb):
            for r in range(n_rows):
                o_ref[b, r] = get_strided(yr_s, r * nb + b, n_cols, n_rows * nb)
    else:
        for b in range(nb):
            o_ref[b] = get_strided(yr_s, b, steps, nb)


def _s5_call(u, s0, prm, grid_cols, reverse, y_first=None, glu=None):
    b, l, _ = u.shape
    nbg = b // S5_BATCH
    nblk = l // S5_BLOCK_STEPS
    rows = S5_BLOCK_STEPS * S5_BATCH
    final = y_first is not None
    blk_of = (lambda j: nblk - 1 - j) if reverse else (lambda j: j)
    if grid_cols:
        n_rows = l // GRID_W
        u_in = u.reshape(b, n_rows, GRID_W, S5_WIDTH)
        cols_per_blk = S5_BLOCK_STEPS // n_rows
        tok_spec = pl.BlockSpec((S5_BATCH, n_rows, cols_per_blk, S5_WIDTH),
                                lambda i, j: (i, 0, blk_of(j), 0))
        out_nat = jax.ShapeDtypeStruct((b, n_rows, GRID_W, S5_WIDTH), F32)
    else:
        u_in = u
        tok_spec = pl.BlockSpec((S5_BATCH, S5_BLOCK_STEPS, S5_WIDTH), lambda i, j: (i, blk_of(j), 0))
        out_nat = jax.ShapeDtypeStruct((b, l, S5_WIDTH), F32)
    scan_spec = pl.BlockSpec((1, rows, S5_WIDTH), lambda i, j: (i, blk_of(j), 0))
    st_spec = pl.BlockSpec((1, S5_BATCH, S5_COLS), lambda i, j: (i, 0, 0))
    full = lambda shape: pl.BlockSpec(shape, lambda i, j: (0,) * len(shape))
    in_specs = [tok_spec, st_spec, full((1, S5_COLS)), full((1, S5_COLS)),
                full((2, 256, 2048)), full((S5_COLS // S5_BLK, S5_BLK, 64))]
    args = [u_in, s0, prm["lam_r"], prm["lam_i"], prm["wb"], prm["wc"]]
    if final:
        in_specs += [scan_spec, full((1, S5_WIDTH)), full((S5_WIDTH, 2 * S5_WIDTH)), full((1, 2 * S5_WIDTH))]
        args += [y_first, glu["d"], glu["w"], glu["b"]]
        out_specs = [tok_spec, st_spec]
        out_shape = [out_nat, jax.ShapeDtypeStruct((nbg, S5_BATCH, S5_COLS), F32)]
    else:
        out_specs = [scan_spec, st_spec]
        out_shape = [jax.ShapeDtypeStruct((nbg, l * S5_BATCH, S5_WIDTH), F32),
                     jax.ShapeDtypeStruct((nbg, S5_BATCH, S5_COLS), F32)]
    y, sfin = pl.pallas_call(
        functools.partial(_s5_kernel, grid_cols=grid_cols, reverse=reverse, final=final),
        grid=(nbg, nblk),
        in_specs=in_specs,
        out_specs=out_specs,
        out_shape=out_shape,
        scratch_shapes=[pltpu.VMEM((S5_WIDTH // LANES, rows, LANES), F32),
                        pltpu.VMEM((S5_SUB_STEPS * S5_BATCH, S5_COLS), F32),
                        pltpu.VMEM((S5_WIDTH // LANES, rows, LANES), F32),
                        pltpu.VMEM((S5_BATCH, S5_COLS), F32)],
        compiler_params=pltpu.CompilerParams(dimension_semantics=("arbitrary", "arbitrary"),
                                             vmem_limit_bytes=VMEM_LIMIT),
        name="s5_final" if final else "s5_first",
    )(*args)
    if final:
        y = y.reshape(b, l, S5_WIDTH)
    return y, sfin


def _s5_cols(t):
    lead = t.shape[:-2]
    return t.reshape(lead + (S5_COLS // S5_BLK, S5_BLK // 2))


def _s5_state_in(s0):
    b = s0.shape[0]
    re, im = _s5_cols(s0[..., 0].astype(F32)), _s5_cols(s0[..., 1].astype(F32))
    return jnp.stack([re, im], axis=-2).reshape(b // S5_BATCH, S5_BATCH, S5_COLS)


def _s5_state_out(s):
    b = s.shape[0] * S5_BATCH
    s = s.reshape(b, S5_COLS // S5_BLK, 2, S5_BLK // 2)
    re = s[:, :, 0].reshape(b, S5_GROUPS, S5_STATE)
    im = s[:, :, 1].reshape(b, S5_GROUPS, S5_STATE)
    return jnp.stack([re, im], axis=-1)


def _s5_params(a_re, a_im, log_dt, b_re, b_im, c_re, c_im):
    a_re, a_im = a_re.astype(F32), a_im.astype(F32)
    delta = jnp.exp(log_dt.astype(F32))[:, None]
    mag = jnp.exp(a_re * delta)
    lam_re, lam_im = mag * jnp.cos(a_im * delta), mag * jnp.sin(a_im * delta)
    den = a_re * a_re + a_im * a_im
    nr = lam_re - 1.0
    f_re = (nr * a_re + lam_im * a_im) / den
    f_im = (lam_im * a_re - nr * a_im) / den
    b_re, b_im = b_re.astype(F32), b_im.astype(F32)
    bb_re = f_re[..., None] * b_re - f_im[..., None] * b_im
    bb_im = f_re[..., None] * b_im + f_im[..., None] * b_re
    nblk = S5_COLS // S5_BLK
    lr = _s5_cols(lam_re)
    li = _s5_cols(lam_im)
    lam_r = jnp.stack([lr, lr], axis=-2).reshape(1, S5_COLS)
    lam_i = jnp.stack([-li, li], axis=-2).reshape(1, S5_COLS)
    gq = S5_BLK // 2 // S5_STATE
    eye = jnp.eye(gq, dtype=F32)

    def in_block(bb):
        t = bb.reshape(nblk, gq, S5_STATE, S5_CH)
        return jnp.einsum("gh,kgpc->kgchp", eye, t).reshape(nblk, gq * S5_CH, gq * S5_STATE)

    blk_in = jnp.concatenate([in_block(bb_re), in_block(bb_im)], axis=-1)
    half = nblk // 2
    eye_h = jnp.eye(half, dtype=F32)
    wb = jnp.einsum("jk,hkcn->hjckn", eye_h, blk_in.reshape(2, half, gq * S5_CH, S5_BLK))
    wb = wb.reshape(2, half * gq * S5_CH, half * S5_BLK).astype(BF16)

    def out_block(cc):
        t = cc.astype(F32).reshape(nblk, gq, S5_CH, S5_STATE)
        return jnp.einsum("gh,kgcp->kgphc", eye, t).reshape(nblk, gq * S5_STATE, gq * S5_CH)

    wc = jnp.concatenate([out_block(c_re), -out_block(c_im)], axis=1).astype(BF16)
    return {"lam_r": lam_r, "lam_i": lam_i, "wb": wb, "wc": wc}


def _post_kernel(x_ref, ys_ref, y5_ref, mod_ref, wo_ref, g_ref, wr_ref, br_ref, cnt_in_ref,
                 x1_ref, h_ref, ti_ref, tw_ref, rk_ref, cnt_ref, cnt_s):
    first = jnp.logical_and(pl.program_id(0) == 0, pl.program_id(1) == 0)

    @pl.when(first)
    def _():
        cnt_s[...] = cnt_in_ref[...]

    tl = x_ref.shape[1]
    mixed = (_dot(ys_ref[0], wo_ref[0:SSD_WIDTH, :])
             + _dot(y5_ref[0].astype(BF16), wo_ref[SSD_WIDTH:SSD_WIDTH + S5_WIDTH, :]))
    g1 = mod_ref[0, :, 2 * D_MODEL:3 * D_MODEL]
    sh2 = mod_ref[0, :, 3 * D_MODEL:4 * D_MODEL]
    sc2 = mod_ref[0, :, 4 * D_MODEL:5 * D_MODEL]
    x1 = x_ref[0] + g1 * mixed
    x1_ref[0] = x1
    ms = jnp.mean(x1 * x1, axis=-1, keepdims=True)
    h = x1 * lax.rsqrt(ms + RMS_EPS) * g_ref[...] * (1.0 + sc2) + sh2
    h_ref[0] = h.astype(BF16)

    h_hi, h_mid, _ = _split3(h)
    wr = wr_ref[...]
    w_hi, w_mid, _ = _split3(wr)
    logits = _dot(h_hi, w_hi) + _dot(h_hi, w_mid) + _dot(h_mid, w_hi) + br_ref[...]

    lane = lax.broadcasted_iota(jnp.int32, (tl, N_EXPERTS), 1).astype(F32)
    slot = lax.broadcasted_iota(jnp.int32, (tl, TOP_K), 1)
    work = logits
    onehot = jnp.zeros((tl, N_EXPERTS), F32)
    sels, vals = [], []
    top_i = jnp.zeros((tl, TOP_K), jnp.int32)
    for k in range(TOP_K):
        m = jnp.max(work, axis=-1, keepdims=True)
        idx = jnp.min(jnp.where(work == m, lane, float(N_EXPERTS)), axis=-1, keepdims=True)
        sel = lane == idx
        sels.append(sel)
        vals.append(m)
        top_i = jnp.where(slot == k, idx.astype(jnp.int32), top_i)
        onehot = jnp.where(sel, 1.0, onehot)
        work = jnp.where(sel, -jnp.inf, work)
    es = [jnp.exp(v - vals[0]) for v in vals]
    den = es[0] + es[1] + es[2] + es[3]
    top_w = jnp.zeros((tl, TOP_K), F32)
    for k in range(TOP_K):
        top_w = jnp.where(slot == k, es[k] / den, top_w)

    ri = lax.broadcasted_iota(jnp.int32, (tl, tl), 0)
    ci = lax.broadcasted_iota(jnp.int32, (tl, tl), 1)
    strict = jnp.where(ri > ci, 1.0, 0.0).astype(BF16)
    before = _dot(strict, onehot.astype(BF16)) + cnt_s[...]
    rank = jnp.zeros((tl, TOP_K), jnp.int32)
    for k in range(TOP_K):
        rk = jnp.sum(jnp.where(sels[k], before, 0.0), axis=-1, keepdims=True)
        rank = jnp.where(slot == k, rk.astype(jnp.int32), rank)
    cnt_s[...] = cnt_s[...] + jnp.sum(onehot, axis=0, keepdims=True)
    ti_ref[0] = top_i
    tw_ref[0] = top_w
    rk_ref[0] = rank
    cnt_ref[...] = cnt_s[...]


def _post_call(x, y_ssd, y_s5, mod, w_out, norm_g, w_router, b_router, cnt_in):
    b, l, _ = x.shape
    tl = 256
    per_batch = mod.shape[0] > 1
    mod_map = (lambda i, j: (i, 0, 0)) if per_batch else (lambda i, j: (0, 0, 0))
    tok = lambda w: pl.BlockSpec((1, tl, w), lambda i, j: (i, j, 0))
    full = lambda r, w: pl.BlockSpec((r, w), lambda i, j: (0, 0))
    return pl.pallas_call(
        _post_kernel,
        grid=(b, l // tl),
        in_specs=[tok(D_MODEL), tok(SSD_WIDTH), tok(S5_WIDTH),
                  pl.BlockSpec((1, 1, 6 * D_MODEL), mod_map),
                  full(D_MODEL, D_MODEL), full(1, D_MODEL), full(D_MODEL, N_EXPERTS),
                  full(1, N_EXPERTS), full(1, N_EXPERTS)],
        out_specs=[tok(D_MODEL), tok(D_MODEL), tok(TOP_K), tok(TOP_K), tok(TOP_K), full(1, N_EXPERTS)],
        out_shape=[jax.ShapeDtypeStruct((b, l, D_MODEL), F32),
                   jax.ShapeDtypeStruct((b, l, D_MODEL), BF16),
                   jax.ShapeDtypeStruct((b, l, TOP_K), jnp.int32),
                   jax.ShapeDtypeStruct((b, l, TOP_K), F32),
                   jax.ShapeDtypeStruct((b, l, TOP_K), jnp.int32),
                   jax.ShapeDtypeStruct((1, N_EXPERTS), F32)],
        scratch_shapes=[pltpu.VMEM((1, N_EXPERTS), F32)],
        compiler_params=pltpu.CompilerParams(dimension_semantics=("arbitrary", "arbitrary"),
                                             vmem_limit_bytes=VMEM_LIMIT),
        name="outproj_norm2_router",
    )(x, y_ssd, y_s5, mod, w_out, norm_g.reshape(1, D_MODEL), w_router,
      b_router.reshape(1, N_EXPERTS), cnt_in)


def _moe_kernel(te_ref, nu_ref, x_ref, wgu_ref, bgu_ref, wd_ref, bd_ref, o_ref, gu_s, dn_s):
    i = pl.program_id(0)
    prev = te_ref[jnp.maximum(i - 1, 0)]
    new_expert = jnp.logical_or(i == 0, te_ref[i] != prev)

    @pl.when(new_expert)
    def _():
        for r in range(D_MODEL // 128):
            gu_s[r * 128:(r + 1) * 128, :] = wgu_ref[0, r * 128:(r + 1) * 128, :].astype(BF16)
            dn_s[r * 128:(r + 1) * 128, :] = wd_ref[0, r * 128:(r + 1) * 128, :].astype(BF16)

    @pl.when(i < nu_ref[0])
    def _():
        x = x_ref[...]
        gate = _dot(x, gu_s[:, 0:D_FF]) + bgu_ref[0, :, 0:D_FF]
        up = _dot(x, gu_s[:, D_FF:2 * D_FF]) + bgu_ref[0, :, D_FF:2 * D_FF]
        gate = jnp.minimum(gate, SWIGLU_LIMIT)
        up = jnp.clip(up, -SWIGLU_LIMIT, SWIGLU_LIMIT)
        act = gate * _sigmoid(SWIGLU_ALPHA * gate) * (up + 1.0)
        o_ref[...] = (_dot(act.astype(BF16), dn_s[...]) + bd_ref[0]).astype(o_ref.dtype)

    @pl.when(i >= nu_ref[0])
    def _():
        o_ref[...] = jnp.zeros(o_ref.shape, o_ref.dtype)


def _moe_call(tile_expert, n_used, xs, w_gate_up, b_gate_up, w_down, b_down):
    p = xs.shape[0]
    n_tiles = p // MOE_TM
    grid_spec = pltpu.PrefetchScalarGridSpec(
        num_scalar_prefetch=2,
        grid=(n_tiles,),
        in_specs=[pl.BlockSpec((MOE_TM, D_MODEL), lambda i, te, nu: (i, 0)),
                  pl.BlockSpec((1, D_MODEL, 2 * D_FF), lambda i, te, nu: (te[i], 0, 0)),
                  pl.BlockSpec((1, 1, 2 * D_FF), lambda i, te, nu: (te[i], 0, 0)),
                  pl.BlockSpec((1, D_FF, D_MODEL), lambda i, te, nu: (te[i], 0, 0)),
                  pl.BlockSpec((1, 1, D_MODEL), lambda i, te, nu: (te[i], 0, 0))],
        out_specs=pl.BlockSpec((MOE_TM, D_MODEL), lambda i, te, nu: (i, 0)),
        scratch_shapes=[pltpu.VMEM((D_MODEL, 2 * D_FF), BF16), pltpu.VMEM((D_FF, D_MODEL), BF16)],
    )
    return pl.pallas_call(
        _moe_kernel,
        grid_spec=grid_spec,
        out_shape=jax.ShapeDtypeStruct((p, D_MODEL), BF16),
        compiler_params=pltpu.CompilerParams(dimension_semantics=("arbitrary",),
                                             vmem_limit_bytes=VMEM_LIMIT),
        name="moe_experts",
    )(tile_expert, n_used, xs, w_gate_up, b_gate_up.reshape(N_EXPERTS, 1, 2 * D_FF),
      w_down, b_down.reshape(N_EXPERTS, 1, D_MODEL))


def _combine_kernel(x1_ref, yg_ref, tw_ref, mod_ref, g_ref, o_ref):
    tw = tw_ref[0]
    acc = tw[:, 0:1] * yg_ref[0, 0].astype(F32)
    for k in range(1, TOP_K):
        acc = acc + tw[:, k:k + 1] * yg_ref[k, 0].astype(F32)
    g2 = mod_ref[0, :, 5 * D_MODEL:6 * D_MODEL]
    x2 = x1_ref[0] + g2 * acc
    ms = jnp.mean(x2 * x2, axis=-1, keepdims=True)
    o_ref[0] = x2 * lax.rsqrt(ms + RMS_EPS) * g_ref[...]


def _combine_call(x1, yg, top_w, mod, norm_g):
    b, l, _ = x1.shape
    tl = 256
    per_batch = mod.shape[0] > 1
    mod_map = (lambda i, j: (i, 0, 0)) if per_batch else (lambda i, j: (0, 0, 0))
    tok = lambda w: pl.BlockSpec((1, tl, w), lambda i, j: (i, j, 0))
    return pl.pallas_call(
        _combine_kernel,
        grid=(b, l // tl),
        in_specs=[tok(D_MODEL),
                  pl.BlockSpec((TOP_K, 1, tl, D_MODEL), lambda i, j: (0, i, j, 0)),
                  tok(TOP_K),
                  pl.BlockSpec((1, 1, 6 * D_MODEL), mod_map),
                  pl.BlockSpec((1, D_MODEL), lambda i, j: (0, 0))],
        out_specs=tok(D_MODEL),
        out_shape=jax.ShapeDtypeStruct((b, l, D_MODEL), F32),
        compiler_params=pltpu.CompilerParams(dimension_semantics=("arbitrary", "arbitrary"),
                                             vmem_limit_bytes=VMEM_LIMIT),
        name="moe_combine_final_norm",
    )(x1, yg, top_w, mod, norm_g.reshape(1, D_MODEL))


def _route(top_i, rank, counts):
    t = top_i.shape[0]
    p = t * TOP_K + N_EXPERTS * MOE_TM
    n_tiles = p // MOE_TM
    padded = ((counts + MOE_TM - 1) // MOE_TM) * MOE_TM
    ends = jnp.cumsum(padded)
    starts = ends - padded
    dest = starts.at[top_i].get(mode="promise_in_bounds") + rank
    tile_start = jnp.arange(n_tiles, dtype=jnp.int32) * MOE_TM
    tile_expert = jnp.minimum(jnp.sum((ends[None, :] <= tile_start[:, None]).astype(jnp.int32), axis=1),
                              N_EXPERTS - 1)
    n_used = (ends[-1] // MOE_TM).reshape(1)
    tok = jnp.broadcast_to(jnp.arange(t, dtype=jnp.int32)[:, None], (t, TOP_K))
    src_tok = jnp.zeros((p,), jnp.int32).at[dest.reshape(-1)].set(tok.reshape(-1), unique_indices=True)
    return dest, src_tok, tile_expert.astype(jnp.int32), n_used.astype(jnp.int32)


def _layer_front(x, mod, s_ssd_f, s_ssd_b, s_s5_f, s_s5_b, grid_cols, prm, cnt_in):
    z, xbc, u, dt = _inproj_call(x, mod, prm["norm1_g"], prm["w_in_r"])
    y_ssd, n_ssd_f, n_ssd_b = _ssd_call(z, xbc, dt, s_ssd_f, s_ssd_b, prm["ssd"])
    y_f, n_s5_f = _s5_call(u, _s5_state_in(s_s5_f), prm["s5_f"], grid_cols, False)
    y_s5, n_s5_b = _s5_call(u, _s5_state_in(s_s5_b), prm["s5_b"], grid_cols, True,
                            y_first=y_f, glu=prm["glu"])
    x1, h2, top_i, top_w, rank, cnt = _post_call(x, y_ssd, y_s5, mod, prm["w_out"], prm["norm2_g"],
                                                 prm["w_router"], prm["b_router"], cnt_in)
    states = (n_ssd_f, n_ssd_b, _s5_state_out(n_s5_f), _s5_state_out(n_s5_b))
    return x1, h2, top_i, top_w, rank, cnt, states


def kernel(x_prompt, x_sample, state_ssd_fwd, state_ssd_bwd, state_s5_fwd, state_s5_bwd, c, c_ctx, w_ada, b_ada, norm1_g, w_in, ssd_conv_w, ssd_conv_b, ssd_dt_bias_fwd, ssd_dt_bias_bwd, ssd_a_log_fwd, ssd_a_log_bwd, ssd_d, ssd_norm_g, s5_a_re_fwd, s5_a_im_fwd, s5_log_dt_fwd, s5_b_re_fwd, s5_b_im_fwd, s5_c_re_fwd, s5_c_im_fwd, s5_a_re_bwd, s5_a_im_bwd, s5_log_dt_bwd, s5_b_re_bwd, s5_b_im_bwd, s5_c_re_bwd, s5_c_im_bwd, s5_d, w_glu, b_glu, w_out, norm2_g, w_router, b_router, w_gate_up, b_gate_up, w_down, b_down, norm_f_g):
    depth = w_ada.shape[0]
    assert depth == 1, "single trunk layer"
    nb_ctx, l_ctx, _ = x_prompt.shape
    nb_lat, l_lat, _ = x_sample.shape

    w_in0 = w_in[0]
    dt_lo = SSD_WIDTH + SSD_CONV_CH
    dt_cols = jnp.pad(w_in0[:, dt_lo:dt_lo + 2 * SSD_HEADS], ((0, 0), (0, DT_PAD - 2 * SSD_HEADS)))
    w_in_r = jnp.concatenate([w_in0[:, :dt_lo], w_in0[:, dt_lo + 2 * SSD_HEADS:], dt_cols], axis=1).astype(BF16)
    pad_dt = lambda f, b: jnp.pad(jnp.concatenate([f, b]).astype(F32), (0, DT_PAD - 2 * SSD_HEADS)).reshape(1, DT_PAD)
    ssd_prm = {
        "conv_w": jnp.pad(ssd_conv_w[0].astype(F32), ((0, 5), (0, 0))),
        "conv_b": ssd_conv_b[0].astype(F32).reshape(1, SSD_CONV_CH),
        "dt_bias": pad_dt(ssd_dt_bias_fwd[0], ssd_dt_bias_bwd[0]),
        "a": pad_dt(-jnp.exp(ssd_a_log_fwd[0].astype(F32)), -jnp.exp(ssd_a_log_bwd[0].astype(F32))),
        "d_skip": jnp.repeat(ssd_d[0].astype(F32), SSD_HEAD_DIM).reshape(1, SSD_WIDTH),
        "norm_g": ssd_norm_g[0].astype(F32).reshape(1, SSD_WIDTH),
    }
    prm = {
        "norm1_g": norm1_g[0], "w_in_r": w_in_r, "ssd": ssd_prm,
        "s5_f": _s5_params(s5_a_re_fwd[0], s5_a_im_fwd[0], s5_log_dt_fwd[0], s5_b_re_fwd[0],
                           s5_b_im_fwd[0], s5_c_re_fwd[0], s5_c_im_fwd[0]),
        "s5_b": _s5_params(s5_a_re_bwd[0], s5_a_im_bwd[0], s5_log_dt_bwd[0], s5_b_re_bwd[0],
                           s5_b_im_bwd[0], s5_c_re_bwd[0], s5_c_im_bwd[0]),
        "glu": {"d": s5_d[0].astype(F32).reshape(1, S5_WIDTH), "w": w_glu[0].astype(BF16),
                "b": b_glu[0].astype(F32).reshape(1, 2 * S5_WIDTH)},
        "w_out": w_out[0].astype(BF16), "norm2_g": norm2_g[0],
        "w_router": w_router[0].astype(F32), "b_router": b_router[0].astype(F32),
    }

    conds = jnp.concatenate([c_ctx[None, :], c], axis=0)
    conds = jnp.pad(conds, ((0, (-conds.shape[0]) % 8), (0, 0)))
    mod = _ada_call(conds, w_ada[0], b_ada[0])
    mod_ctx = mod[0:1].reshape(1, 1, 6 * D_MODEL)
    mod_lat = mod[1:1 + nb_lat].reshape(nb_lat, 1, 6 * D_MODEL)

    zero_ssd = jnp.zeros((nb_ctx, SSD_HEADS, SSD_HEAD_DIM, SSD_STATE), F32)
    zero_s5 = jnp.zeros((nb_ctx, S5_GROUPS, S5_STATE, 2), F32)
    cnt0 = jnp.zeros((1, N_EXPERTS), F32)
    x1_c, h2_c, ti_c, tw_c, rk_c, cnt_c, st_c = _layer_front(
        x_prompt, mod_ctx, zero_ssd, zero_ssd, zero_s5, zero_s5, False, prm, cnt0)
    x1_l, h2_l, ti_l, tw_l, rk_l, cnt_l, _ = _layer_front(
        x_sample, mod_lat, state_ssd_fwd[:, 0], state_ssd_bwd[:, 0], state_s5_fwd[:, 0],
        state_s5_bwd[:, 0], True, prm, cnt_c)

    t_ctx = nb_ctx * l_ctx
    top_i = jnp.concatenate([ti_c.reshape(-1, TOP_K), ti_l.reshape(-1, TOP_K)], axis=0)
    rank = jnp.concatenate([rk_c.reshape(-1, TOP_K), rk_l.reshape(-1, TOP_K)], axis=0)
    counts = cnt_l.reshape(N_EXPERTS).astype(jnp.int32)
    dest, src_tok, tile_expert, n_used = _route(top_i, rank, counts)
    n_tok = t_ctx + nb_lat * l_lat
    h2 = jnp.concatenate([h2_c.reshape(-1, D_MODEL), h2_l.reshape(-1, D_MODEL),
                          jnp.zeros((max(GATHER_SRC_ROWS - n_tok, 0), D_MODEL), BF16)], axis=0)
    xs = h2.at[src_tok].get(mode="promise_in_bounds")
    ys = _moe_call(tile_expert, n_used, xs, w_gate_up[0], b_gate_up[0], w_down[0], b_down[0])

    def picked_rows(d, nb, l):
        rows = ys.at[d.T.reshape(-1)].get(mode="promise_in_bounds")
        return rows.reshape(TOP_K, nb, l, D_MODEL)

    y_prompt = _combine_call(x1_c, picked_rows(dest[:t_ctx], nb_ctx, l_ctx), tw_c, mod_ctx, norm_f_g)
    y_sample = _combine_call(x1_l, picked_rows(dest[t_ctx:], nb_lat, l_lat), tw_l, mod_lat, norm_f_g)

    ssd_f, ssd_b, s5_f, s5_b = st_c
    return (y_prompt, y_sample, ssd_f[:, None], ssd_b[:, None], s5_f[:, None], s5_b[:, None])
```

```python
import functools
import math

import jax
import jax.numpy as jnp
from jax import lax
from jax.experimental import pallas as pl
from jax.experimental.pallas import tpu as pltpu

F32 = jnp.float32
BF16 = jnp.bfloat16

D_MODEL = 1024
GRID_W = 64
SSD_WIDTH = 512
SSD_HEAD_DIM = 64
SSD_HEADS = 8
SSD_GROUPS = 2
SSD_HEADS_PER_GROUP = SSD_HEADS // SSD_GROUPS
SSD_STATE = 128
SSD_CHUNK = 128
SSD_CONV_CH = SSD_WIDTH + 2 * SSD_GROUPS * SSD_STATE
S5_WIDTH = 512
S5_CH = 16
S5_GROUPS = 32
S5_STATE = 64
N_EXPERTS = 32
TOP_K = 4
D_FF = 1024
SWIGLU_LIMIT = 7.0
SWIGLU_ALPHA = 1.702
RMS_EPS = 1e-5

LANES = 128
DT_PAD = 128
S5_BATCH = 8
S5_COLS = 2 * S5_GROUPS * S5_STATE
S5_BLK = 512
S5_SUB_STEPS = 32
S5_BLOCK_STEPS = 256
MOE_TM = 256
SEG_TOKENS = 8192
GATHER_SRC_ROWS = 32768
VMEM_LIMIT = 56 * 1024 * 1024


def _sigmoid(x):
    return 1.0 / (1.0 + jnp.exp(-x))


def _split3(x):
    hi = x.astype(BF16)
    r1 = x - hi.astype(F32)
    mid = r1.astype(BF16)
    lo = (r1 - mid.astype(F32)).astype(BF16)
    return hi, mid, lo


def _dot(a, b):
    return jnp.dot(a, b, preferred_element_type=F32)


def _ada_kernel(c_ref, w_ref, b_ref, o_ref):
    c = c_ref[...]
    s = c * _sigmoid(c)
    o_ref[...] = _dot(s.astype(BF16), w_ref[...].astype(BF16)) + b_ref[...]


def _ada_call(conds, w_ada, b_ada):
    n = w_ada.shape[1]
    tn = 1536
    rows = conds.shape[0]
    return pl.pallas_call(
        _ada_kernel,
        grid=(n // tn,),
        in_specs=[pl.BlockSpec((rows, D_MODEL), lambda j: (0, 0)),
                  pl.BlockSpec((D_MODEL, tn), lambda j: (0, j)),
                  pl.BlockSpec((1, tn), lambda j: (0, j))],
        out_specs=pl.BlockSpec((rows, tn), lambda j: (0, j)),
        out_shape=jax.ShapeDtypeStruct((rows, n), F32),
        compiler_params=pltpu.CompilerParams(dimension_semantics=("arbitrary",),
                                             vmem_limit_bytes=VMEM_LIMIT),
        name="ada_mod",
    )(conds, w_ada, b_ada.reshape(1, n))


def _inproj_kernel(x_ref, mod_ref, g_ref, w_ref, z_ref, xbc_ref, u_ref, dt_ref):
    x = x_ref[0]
    ms = jnp.mean(x * x, axis=-1, keepdims=True)
    y = x * lax.rsqrt(ms + RMS_EPS) * g_ref[...]
    sh = mod_ref[0, :, 0:D_MODEL]
    sc = mod_ref[0, :, D_MODEL:2 * D_MODEL]
    h = (y * (1.0 + sc) + sh).astype(BF16)
    z_ref[0] = _dot(h, w_ref[:, 0:512]).astype(BF16)
    xbc_ref[0] = _dot(h, w_ref[:, 512:1536]).astype(BF16)
    u_ref[0] = _dot(h, w_ref[:, 1536:2048])
    dt_ref[0] = _dot(h, w_ref[:, 2048:2048 + DT_PAD])


def _inproj_call(x, mod, norm_g, w_in_r):
    b, l, _ = x.shape
    tl = min(l, 512)
    per_batch = mod.shape[0] > 1
    mod_map = (lambda i, j: (i, 0, 0)) if per_batch else (lambda i, j: (0, 0, 0))
    ncol = w_in_r.shape[1]
    tok = lambda w: pl.BlockSpec((1, tl, w), lambda i, j: (i, j, 0))
    return pl.pallas_call(
        _inproj_kernel,
        grid=(b, l // tl),
        in_specs=[tok(D_MODEL),
                  pl.BlockSpec((1, 1, 6 * D_MODEL), mod_map),
                  pl.BlockSpec((1, D_MODEL), lambda i, j: (0, 0)),
                  pl.BlockSpec((D_MODEL, ncol), lambda i, j: (0, 0))],
        out_specs=[tok(SSD_WIDTH), tok(SSD_CONV_CH), tok(S5_WIDTH), tok(DT_PAD)],
        out_shape=[jax.ShapeDtypeStruct((b, l, SSD_WIDTH), BF16),
                   jax.ShapeDtypeStruct((b, l, SSD_CONV_CH), BF16),
                   jax.ShapeDtypeStruct((b, l, S5_WIDTH), F32),
                   jax.ShapeDtypeStruct((b, l, DT_PAD), F32)],
        compiler_params=pltpu.CompilerParams(dimension_semantics=("arbitrary", "arbitrary"),
                                             vmem_limit_bytes=VMEM_LIMIT),
        name="norm1_inproj",
    )(x, mod, norm_g.reshape(1, D_MODEL), w_in_r)


def _softplus(x):
    return jnp.maximum(x, 0.0) + jnp.log1p(jnp.exp(-jnp.abs(x)))


def _ssd_kernel(z_ref, xbc_ref, dt_ref, s0f_ref, s0b_ref, cw_ref, cb_ref, dtb_ref, a_ref,
                dskip_ref, ng_ref, y_ref, sf_ref, sb_ref, xc_s, dts_s, yf_s, st_s, *, seq):
    q = SSD_CHUNK
    nc = seq // q
    row_i = lax.broadcasted_iota(jnp.int32, (q, 1), 0)

    def conv_body(c, carry):
        r0 = pl.multiple_of(c * q, q)
        cur = xbc_ref[0, pl.ds(r0, q), :].astype(F32)
        p0 = pl.multiple_of(jnp.maximum(r0 - 16, 0), 16)
        n0 = pl.multiple_of(jnp.minimum(r0 + q, seq - 16), 16)
        prev_row = xbc_ref[0, pl.ds(p0, 16), :].astype(F32)[15:16, :]
        next_row = xbc_ref[0, pl.ds(n0, 16), :].astype(F32)[0:1, :]
        prev_row = jnp.where(c > 0, prev_row, 0.0)
        next_row = jnp.where(c < nc - 1, next_row, 0.0)
        down = jnp.where(row_i == 0, prev_row, pltpu.roll(cur, 1, axis=0))
        up = jnp.where(row_i == q - 1, next_row, pltpu.roll(cur, q - 1, axis=0))
        v = cw_ref[0:1, :] * down + cw_ref[1:2, :] * cur + cw_ref[2:3, :] * up + cb_ref[...]
        xc_s[pl.ds(r0, q), :] = v * _sigmoid(v)
        dts_s[pl.ds(r0, q), :] = _softplus(dt_ref[0, pl.ds(r0, q), :] + dtb_ref[...])
        return carry

    lax.fori_loop(0, nc, conv_body, 0)

    li = lax.broadcasted_iota(jnp.int32, (q, q), 0)
    si = lax.broadcasted_iota(jnp.int32, (q, q), 1)

    def chunk(c, direction):
        r0 = pl.multiple_of(c * q, q)
        causal = (li >= si) if direction == 0 else (li <= si)
        tri = jnp.where(causal, 1.0, 0.0).astype(BF16)
        dts = dts_s[pl.ds(r0, q), :]
        dta = dts * a_ref[...]
        d_hi, d_mid, d_lo = _split3(dta)
        cs = _dot(tri, d_hi) + _dot(tri, d_mid) + _dot(tri, d_lo)
        tot = cs[q - 1:q, :] if direction == 0 else cs[0:1, :]
        cs_t = cs.T
        ecs = jnp.exp(cs)
        tail = jnp.exp(tot - cs)
        etot = jnp.exp(tot)
        for g in range(SSD_GROUPS):
            b_off = SSD_WIDTH + g * SSD_STATE
            c_off = SSD_WIDTH + SSD_GROUPS * SSD_STATE + g * SSD_STATE
            bm = xc_s[pl.ds(r0, q), b_off:b_off + SSD_STATE]
            cm = xc_s[pl.ds(r0, q), c_off:c_off + SSD_STATE].astype(BF16)
            bm_t = bm.T.astype(BF16)
            cb = lax.dot_general(cm, bm.astype(BF16), (((1,), (1,)), ((), ())),
                                 preferred_element_type=F32)
            st_g = st_s[g]
            y_off = _dot(cm, st_g.astype(BF16))
            for hh in range(SSD_HEADS_PER_GROUP):
                h = g * SSD_HEADS_PER_GROUP + hh
                col = h + SSD_HEADS * direction
                lo, hi = h * SSD_HEAD_DIM, (h + 1) * SSD_HEAD_DIM
                slo, shi = hh * SSD_HEAD_DIM, (hh + 1) * SSD_HEAD_DIM
                x_h = xc_s[pl.ds(r0, q), lo:hi]
                xdt = x_h * dts[:, col:col + 1]
                seg = cs[:, col:col + 1] - cs_t[col:col + 1, :]
                dec = jnp.exp(jnp.where(causal, seg, -1e30))
                scores = (cb * dec).astype(BF16)
                y_h = _dot(scores, xdt.astype(BF16)) + y_off[:, slo:shi] * ecs[:, col:col + 1]
                xw = (xdt * tail[:, col:col + 1]).astype(BF16)
                st_s[g, :, slo:shi] = etot[:, col:col + 1] * st_g[:, slo:shi] + _dot(bm_t, xw)
                if direction == 0:
                    yf_s[pl.ds(r0, q), lo:hi] = y_h
                else:
                    yf_s[pl.ds(r0, q), lo:hi] = yf_s[pl.ds(r0, q), lo:hi] + y_h
        if direction == 1:
            xs = xc_s[pl.ds(r0, q), 0:SSD_WIDTH]
            zz = z_ref[0, pl.ds(r0, q), :].astype(F32)
            y = (yf_s[pl.ds(r0, q), :] + dskip_ref[...] * xs) * (zz * _sigmoid(zz))
            ms = jnp.mean(y * y, axis=-1, keepdims=True)
            y_ref[0, pl.ds(r0, q), :] = (y * lax.rsqrt(ms + RMS_EPS) * ng_ref[...]).astype(y_ref.dtype)

    st_s[...] = s0f_ref[0]
    lax.fori_loop(0, nc, lambda i, carry: (chunk(i, 0), carry)[1], 0)
    sf_ref[0] = st_s[...]
    st_s[...] = s0b_ref[0]
    lax.fori_loop(0, nc, lambda i, carry: (chunk(nc - 1 - i, 1), carry)[1], 0)
    sb_ref[0] = st_s[...]


def _ssd_state_in(s):
    b = s.shape[0]
    s = s.astype(F32).reshape(b, SSD_GROUPS, SSD_HEADS_PER_GROUP, SSD_HEAD_DIM, SSD_STATE)
    return s.transpose(0, 1, 4, 2, 3).reshape(b, SSD_GROUPS, SSD_STATE, SSD_HEADS_PER_GROUP * SSD_HEAD_DIM)


def _ssd_state_out(s):
    b = s.shape[0]
    s = s.reshape(b, SSD_GROUPS, SSD_STATE, SSD_HEADS_PER_GROUP, SSD_HEAD_DIM)
    return s.transpose(0, 1, 3, 4, 2).reshape(b, SSD_HEADS, SSD_HEAD_DIM, SSD_STATE)


def _ssd_call(z, xbc, dt, s0f, s0b, prm):
    b, l, _ = z.shape
    st_shape = (SSD_GROUPS, SSD_STATE, SSD_HEADS_PER_GROUP * SSD_HEAD_DIM)
    tok = lambda w: pl.BlockSpec((1, l, w), lambda i: (i, 0, 0))
    st_spec = pl.BlockSpec((1,) + st_shape, lambda i: (i, 0, 0, 0))
    par = lambda r, w: pl.BlockSpec((r, w), lambda i: (0, 0))
    y, sf, sb = pl.pallas_call(
        functools.partial(_ssd_kernel, seq=l),
        grid=(b,),
        in_specs=[tok(SSD_WIDTH), tok(SSD_CONV_CH), tok(DT_PAD), st_spec, st_spec,
                  par(8, SSD_CONV_CH), par(1, SSD_CONV_CH), par(1, DT_PAD), par(1, DT_PAD),
                  par(1, SSD_WIDTH), par(1, SSD_WIDTH)],
        out_specs=[tok(SSD_WIDTH), st_spec, st_spec],
        out_shape=[jax.ShapeDtypeStruct((b, l, SSD_WIDTH), BF16),
                   jax.ShapeDtypeStruct((b,) + st_shape, F32),
                   jax.ShapeDtypeStruct((b,) + st_shape, F32)],
        scratch_shapes=[pltpu.VMEM((l, SSD_CONV_CH), F32),
                        pltpu.VMEM((l, DT_PAD), F32),
                        pltpu.VMEM((l, SSD_WIDTH), F32),
                        pltpu.VMEM(st_shape, F32)],
        compiler_params=pltpu.CompilerParams(dimension_semantics=("arbitrary",),
                                             vmem_limit_bytes=VMEM_LIMIT),
        name="ssd_mixer",
    )(z, xbc, dt, _ssd_state_in(s0f), _ssd_state_in(s0b),
      prm["conv_w"], prm["conv_b"], prm["dt_bias"], prm["a"], prm["d_skip"], prm["norm_g"])
    return y, _ssd_state_out(sf), _ssd_state_out(sb)


def _gelu_tanh(x):
    return 0.5 * x * (1.0 + jnp.tanh(math.sqrt(2.0 / math.pi) * (x + 0.044715 * (x * x * x))))


def _s5_kernel(*refs, grid_cols, reverse, final):
    if final:
        (u_ref, s0_ref, lr_ref, li_ref, wb_ref, wc_ref, yf_ref, d_ref, wg_ref, bg_ref,
         o_ref, sfin_ref, ur_s, bu_s, yr_s, st_s) = refs
    else:
        (u_ref, s0_ref, lr_ref, li_ref, wb_ref, wc_ref,
         o_ref, sfin_ref, ur_s, bu_s, yr_s, st_s) = refs
    nb = S5_BATCH
    steps = S5_BLOCK_STEPS
    rows = steps * nb
    sub_rows = S5_SUB_STEPS * nb
    n_sub = steps // S5_SUB_STEPS
    n_rows = u_ref.shape[1] if grid_cols else None
    j = pl.program_id(1)

    @pl.when(j == 0)
    def _():
        st_s[...] = s0_ref[0]

    nlt = S5_WIDTH // LANES

    def put_strided(dst, start, count, stride, val):
        for k in range(nlt):
            dst[k, pl.ds(start, count, stride=stride), :] = val[:, k * LANES:(k + 1) * LANES]

    def get_strided(src, start, count, stride):
        return jnp.concatenate([src[k, pl.ds(start, count, stride=stride), :] for k in range(nlt)], axis=1)

    def get_rows(src, r0, n):
        return jnp.concatenate([src[k, pl.ds(r0, n), :] for k in range(nlt)], axis=1)

    if grid_cols:
        n_cols = u_ref.shape[2]
        for b in range(nb):
            for r in range(n_rows):
                put_strided(ur_s, r * nb + b, n_cols, n_rows * nb, u_ref[b, r])
    else:
        for b in range(nb):
            put_strided(ur_s, b, steps, nb, u_ref[b])

    def sub_chunk(i, carry):
        sidx = (n_sub - 1 - i) if reverse else i
        r0 = pl.multiple_of(sidx * sub_rows, sub_rows)
        for hf in range(2):
            uh = jnp.concatenate([ur_s[2 * hf, pl.ds(r0, sub_rows), :],
                                  ur_s[2 * hf + 1, pl.ds(r0, sub_rows), :]], axis=1).astype(BF16)
            bu_s[:, hf * 2048:(hf + 1) * 2048] = _dot(uh, wb_ref[hf])
        for blk in range(S5_COLS // S5_BLK):
            c0 = blk * S5_BLK
            lam_r = jnp.broadcast_to(lr_ref[:, c0:c0 + S5_BLK], (nb, S5_BLK))
            lam_i = jnp.broadcast_to(li_ref[:, c0:c0 + S5_BLK], (nb, S5_BLK))

            def step(t, s):
                tt = (S5_SUB_STEPS - 1 - t) if reverse else t
                rr = pl.multiple_of(tt * nb, nb)
                sw = jnp.concatenate([s[:, S5_BLK // 2:], s[:, :S5_BLK // 2]], axis=1)
                s_new = lam_r * s + lam_i * sw + bu_s[pl.ds(rr, nb), c0:c0 + S5_BLK]
                bu_s[pl.ds(rr, nb), c0:c0 + S5_BLK] = s_new
                return s_new

            st_s[:, c0:c0 + S5_BLK] = lax.fori_loop(0, S5_SUB_STEPS, step, st_s[:, c0:c0 + S5_BLK],
                                                    unroll=4)
            half = (blk % 2) * 64
            yr_s[blk // 2, pl.ds(r0, sub_rows), half:half + 64] = _dot(
                bu_s[:, c0:c0 + S5_BLK].astype(BF16), wc_ref[blk])
        return carry

    lax.fori_loop(0, n_sub, sub_chunk, 0)

    @pl.when(j == pl.num_programs(1) - 1)
    def _():
        sfin_ref[0] = st_s[...]

    if not final:
        for k in range(nlt):
            o_ref[0, :, k * LANES:(k + 1) * LANES] = yr_s[k]
        return

    def glu_chunk(i, carry):
        r0 = pl.multiple_of(i * sub_rows, sub_rows)
        y = (get_rows(yr_s, r0, sub_rows) + yf_ref[0, pl.ds(r0, sub_rows), :]
             + d_ref[...] * get_rows(ur_s, r0, sub_rows))
        y = _gelu_tanh(y).astype(BF16)
        val = _dot(y, wg_ref[:, 0:S5_WIDTH]) + bg_ref[:, 0:S5_WIDTH]
        gate = _dot(y, wg_ref[:, S5_WIDTH:2 * S5_WIDTH]) + bg_ref[:, S5_WIDTH:2 * S5_WIDTH]
        res = val * _sigmoid(gate)
        for k in range(nlt):
            yr_s[k, pl.ds(r0, sub_rows), :] = res[:, k * LANES:(k + 1) * LANES]
        return carry

    lax.fori_loop(0, n_sub, glu_chunk, 0)

    if grid_cols:
        for b in range(nb):
            for r in range(n_rows):
                o_ref[b, r] = get_strided(yr_s, r * nb + b, n_cols, n_rows * nb)
    else:
        for b in range(nb):
            o_ref[b] = get_strided(yr_s, b, steps, nb)


def _s5_call(u, s0, prm, grid_cols, reverse, y_first=None, glu=None):
    b, l, _ = u.shape
    nbg = b // S5_BATCH
    nblk = l // S5_BLOCK_STEPS
    rows = S5_BLOCK_STEPS * S5_BATCH
    final = y_first is not None
    blk_of = (lambda j: nblk - 1 - j) if reverse else (lambda j: j)
    if grid_cols:
        n_rows = l // GRID_W
        u_in = u.reshape(b, n_rows, GRID_W, S5_WIDTH)
        cols_per_blk = S5_BLOCK_STEPS // n_rows
        tok_spec = pl.BlockSpec((S5_BATCH, n_rows, cols_per_blk, S5_WIDTH),
                                lambda i, j: (i, 0, blk_of(j), 0))
        out_nat = jax.ShapeDtypeStruct((b, n_rows, GRID_W, S5_WIDTH), F32)
    else:
        u_in = u
        tok_spec = pl.BlockSpec((S5_BATCH, S5_BLOCK_STEPS, S5_WIDTH), lambda i, j: (i, blk_of(j), 0))
        out_nat = jax.ShapeDtypeStruct((b, l, S5_WIDTH), F32)
    scan_spec = pl.BlockSpec((1, rows, S5_WIDTH), lambda i, j: (i, blk_of(j), 0))
    st_spec = pl.BlockSpec((1, S5_BATCH, S5_COLS), lambda i, j: (i, 0, 0))
    full = lambda shape: pl.BlockSpec(shape, lambda i, j: (0,) * len(shape))
    in_specs = [tok_spec, st_spec, full((1, S5_COLS)), full((1, S5_COLS)),
                full((2, 256, 2048)), full((S5_COLS // S5_BLK, S5_BLK, 64))]
    args = [u_in, s0, prm["lam_r"], prm["lam_i"], prm["wb"], prm["wc"]]
    if final:
        in_specs += [scan_spec, full((1, S5_WIDTH)), full((S5_WIDTH, 2 * S5_WIDTH)), full((1, 2 * S5_WIDTH))]
        args += [y_first, glu["d"], glu["w"], glu["b"]]
        out_specs = [tok_spec, st_spec]
        out_shape = [out_nat, jax.ShapeDtypeStruct((nbg, S5_BATCH, S5_COLS), F32)]
    else:
        out_specs = [scan_spec, st_spec]
        out_shape = [jax.ShapeDtypeStruct((nbg, l * S5_BATCH, S5_WIDTH), F32),
                     jax.ShapeDtypeStruct((nbg, S5_BATCH, S5_COLS), F32)]
    y, sfin = pl.pallas_call(
        functools.partial(_s5_kernel, grid_cols=grid_cols, reverse=reverse, final=final),
        grid=(nbg, nblk),
        in_specs=in_specs,
        out_specs=out_specs,
        out_shape=out_shape,
        scratch_shapes=[pltpu.VMEM((S5_WIDTH // LANES, rows, LANES), F32),
                        pltpu.VMEM((S5_SUB_STEPS * S5_BATCH, S5_COLS), F32),
                        pltpu.VMEM((S5_WIDTH // LANES, rows, LANES), F32),
                        pltpu.VMEM((S5_BATCH, S5_COLS), F32)],
        compiler_params=pltpu.CompilerParams(dimension_semantics=("arbitrary", "arbitrary"),
                                             vmem_limit_bytes=VMEM_LIMIT),
        name="s5_final" if final else "s5_first",
    )(*args)
    if final:
        y = y.reshape(b, l, S5_WIDTH)
    return y, sfin


def _s5_cols(t):
    lead = t.shape[:-2]
    return t.reshape(lead + (S5_COLS // S5_BLK, S5_BLK // 2))


def _s5_state_in(s0):
    b = s0.shape[0]
    re, im = _s5_cols(s0[..., 0].astype(F32)), _s5_cols(s0[..., 1].astype(F32))
    return jnp.stack([re, im], axis=-2).reshape(b // S5_BATCH, S5_BATCH, S5_COLS)


def _s5_state_out(s):
    b = s.shape[0] * S5_BATCH
    s = s.reshape(b, S5_COLS // S5_BLK, 2, S5_BLK // 2)
    re = s[:, :, 0].reshape(b, S5_GROUPS, S5_STATE)
    im = s[:, :, 1].reshape(b, S5_GROUPS, S5_STATE)
    return jnp.stack([re, im], axis=-1)


def _s5_params(a_re, a_im, log_dt, b_re, b_im, c_re, c_im):
    a_re, a_im = a_re.astype(F32), a_im.astype(F32)
    delta = jnp.exp(log_dt.astype(F32))[:, None]
    mag = jnp.exp(a_re * delta)
    lam_re, lam_im = mag * jnp.cos(a_im * delta), mag * jnp.sin(a_im * delta)
    den = a_re * a_re + a_im * a_im
    nr = lam_re - 1.0
    f_re = (nr * a_re + lam_im * a_im) / den
    f_im = (lam_im * a_re - nr * a_im) / den
    b_re, b_im = b_re.astype(F32), b_im.astype(F32)
    bb_re = f_re[..., None] * b_re - f_im[..., None] * b_im
    bb_im = f_re[..., None] * b_im + f_im[..., None] * b_re
    nblk = S5_COLS // S5_BLK
    lr = _s5_cols(lam_re)
    li = _s5_cols(lam_im)
    lam_r = jnp.stack([lr, lr], axis=-2).reshape(1, S5_COLS)
    lam_i = jnp.stack([-li, li], axis=-2).reshape(1, S5_COLS)
    gq = S5_BLK // 2 // S5_STATE
    eye = jnp.eye(gq, dtype=F32)

    def in_block(bb):
        t = bb.reshape(nblk, gq, S5_STATE, S5_CH)
        return jnp.einsum("gh,kgpc->kgchp", eye, t).reshape(nblk, gq * S5_CH, gq * S5_STATE)

    blk_in = jnp.concatenate([in_block(bb_re), in_block(bb_im)], axis=-1)
    half = nblk // 2
    eye_h = jnp.eye(half, dtype=F32)
    wb = jnp.einsum("jk,hkcn->hjckn", eye_h, blk_in.reshape(2, half, gq * S5_CH, S5_BLK))
    wb = wb.reshape(2, half * gq * S5_CH, half * S5_BLK).astype(BF16)

    def out_block(cc):
        t = cc.astype(F32).reshape(nblk, gq, S5_CH, S5_STATE)
        return jnp.einsum("gh,kgcp->kgphc", eye, t).reshape(nblk, gq * S5_STATE, gq * S5_CH)

    wc = jnp.concatenate([out_block(c_re), -out_block(c_im)], axis=1).astype(BF16)
    return {"lam_r": lam_r, "lam_i": lam_i, "wb": wb, "wc": wc}


def _post_kernel(x_ref, ys_ref, y5_ref, mod_ref, wo_ref, g_ref, wr_ref, br_ref,
                 x1_ref, h_ref, ti_ref, tw_ref, rk_ref, cnt_ref, cnt_s, *, seg_batches):
    first = jnp.logical_and(pl.program_id(0) % seg_batches == 0, pl.program_id(1) == 0)

    @pl.when(first)
    def _():
        cnt_s[...] = jnp.zeros(cnt_s.shape, cnt_s.dtype)

    tl = x_ref.shape[1]
    mixed = (_dot(ys_ref[0], wo_ref[0:SSD_WIDTH, :])
             + _dot(y5_ref[0].astype(BF16), wo_ref[SSD_WIDTH:SSD_WIDTH + S5_WIDTH, :]))
    g1 = mod_ref[0, :, 2 * D_MODEL:3 * D_MODEL]
    sh2 = mod_ref[0, :, 3 * D_MODEL:4 * D_MODEL]
    sc2 = mod_ref[0, :, 4 * D_MODEL:5 * D_MODEL]
    x1 = x_ref[0] + g1 * mixed
    x1_ref[0] = x1
    ms = jnp.mean(x1 * x1, axis=-1, keepdims=True)
    h = x1 * lax.rsqrt(ms + RMS_EPS) * g_ref[...] * (1.0 + sc2) + sh2
    h_ref[0] = h.astype(BF16)

    h_hi, h_mid, _ = _split3(h)
    wr = wr_ref[...]
    w_hi, w_mid, _ = _split3(wr)
    logits = _dot(h_hi, w_hi) + _dot(h_hi, w_mid) + _dot(h_mid, w_hi) + br_ref[...]

    lane = lax.broadcasted_iota(jnp.int32, (tl, N_EXPERTS), 1).astype(F32)
    slot = lax.broadcasted_iota(jnp.int32, (tl, TOP_K), 1)
    work = logits
    onehot = jnp.zeros((tl, N_EXPERTS), F32)
    sels, vals = [], []
    top_i = jnp.zeros((tl, TOP_K), jnp.int32)
    for k in range(TOP_K):
        m = jnp.max(work, axis=-1, keepdims=True)
        idx = jnp.min(jnp.where(work == m, lane, float(N_EXPERTS)), axis=-1, keepdims=True)
        sel = lane == idx
        sels.append(sel)
        vals.append(m)
        top_i = jnp.where(slot == k, idx.astype(jnp.int32), top_i)
        onehot = jnp.where(sel, 1.0, onehot)
        work = jnp.where(sel, -jnp.inf, work)
    es = [jnp.exp(v - vals[0]) for v in vals]
    den = es[0] + es[1] + es[2] + es[3]
    top_w = jnp.zeros((tl, TOP_K), F32)
    for k in range(TOP_K):
        top_w = jnp.where(slot == k, es[k] / den, top_w)

    ri = lax.broadcasted_iota(jnp.int32, (tl, tl), 0)
    ci = lax.broadcasted_iota(jnp.int32, (tl, tl), 1)
    strict = jnp.where(ri > ci, 1.0, 0.0).astype(BF16)
    before = _dot(strict, onehot.astype(BF16)) + cnt_s[...]
    rank = jnp.zeros((tl, TOP_K), jnp.int32)
    for k in range(TOP_K):
        rk = jnp.sum(jnp.where(sels[k], before, 0.0), axis=-1, keepdims=True)
        rank = jnp.where(slot == k, rk.astype(jnp.int32), rank)
    cnt_s[...] = cnt_s[...] + jnp.sum(onehot, axis=0, keepdims=True)
    ti_ref[0] = top_i
    tw_ref[0] = top_w
    rk_ref[0] = rank
    cnt_ref[0] = cnt_s[...]


def _post_call(x, y_ssd, y_s5, mod, w_out, norm_g, w_router, b_router, seg_batches):
    b, l, _ = x.shape
    tl = 256
    n_seg = b // seg_batches
    per_batch = mod.shape[0] > 1
    mod_map = (lambda i, j: (i, 0, 0)) if per_batch else (lambda i, j: (0, 0, 0))
    tok = lambda w: pl.BlockSpec((1, tl, w), lambda i, j: (i, j, 0))
    full = lambda r, w: pl.BlockSpec((r, w), lambda i, j: (0, 0))
    return pl.pallas_call(
        functools.partial(_post_kernel, seg_batches=seg_batches),
        grid=(b, l // tl),
        in_specs=[tok(D_MODEL), tok(SSD_WIDTH), tok(S5_WIDTH),
                  pl.BlockSpec((1, 1, 6 * D_MODEL), mod_map),
                  full(D_MODEL, D_MODEL), full(1, D_MODEL), full(D_MODEL, N_EXPERTS),
                  full(1, N_EXPERTS)],
        out_specs=[tok(D_MODEL), tok(D_MODEL), tok(TOP_K), tok(TOP_K), tok(TOP_K),
                   pl.BlockSpec((1, 1, N_EXPERTS), lambda i, j: (i // seg_batches, 0, 0))],
        out_shape=[jax.ShapeDtypeStruct((b, l, D_MODEL), F32),
                   jax.ShapeDtypeStruct((b, l, D_MODEL), BF16),
                   jax.ShapeDtypeStruct((b, l, TOP_K), jnp.int32),
                   jax.ShapeDtypeStruct((b, l, TOP_K), F32),
                   jax.ShapeDtypeStruct((b, l, TOP_K), jnp.int32),
                   jax.ShapeDtypeStruct((n_seg, 1, N_EXPERTS), F32)],
        scratch_shapes=[pltpu.VMEM((1, N_EXPERTS), F32)],
        compiler_params=pltpu.CompilerParams(dimension_semantics=("arbitrary", "arbitrary"),
                                             vmem_limit_bytes=VMEM_LIMIT),
        name="outproj_norm2_router",
    )(x, y_ssd, y_s5, mod, w_out, norm_g.reshape(1, D_MODEL), w_router,
      b_router.reshape(1, N_EXPERTS))


def _moe_kernel(te_ref, nu_ref, x_ref, wgu_ref, bgu_ref, wd_ref, bd_ref, o_ref, gu_s, dn_s):
    i = pl.program_id(0)
    prev = te_ref[jnp.maximum(i - 1, 0)]
    new_expert = jnp.logical_or(i == 0, te_ref[i] != prev)

    @pl.when(new_expert)
    def _():
        for r in range(D_MODEL // 128):
            gu_s[r * 128:(r + 1) * 128, :] = wgu_ref[0, r * 128:(r + 1) * 128, :].astype(BF16)
            dn_s[r * 128:(r + 1) * 128, :] = wd_ref[0, r * 128:(r + 1) * 128, :].astype(BF16)

    @pl.when(i < nu_ref[0])
    def _():
        x = x_ref[...]
        gate = _dot(x, gu_s[:, 0:D_FF]) + bgu_ref[0, :, 0:D_FF]
        up = _dot(x, gu_s[:, D_FF:2 * D_FF]) + bgu_ref[0, :, D_FF:2 * D_FF]
        gate = jnp.minimum(gate, SWIGLU_LIMIT)
        up = jnp.clip(up, -SWIGLU_LIMIT, SWIGLU_LIMIT)
        act = gate * _sigmoid(SWIGLU_ALPHA * gate) * (up + 1.0)
        o_ref[...] = (_dot(act.astype(BF16), dn_s[...]) + bd_ref[0]).astype(o_ref.dtype)

    @pl.when(i >= nu_ref[0])
    def _():
        o_ref[...] = jnp.zeros(o_ref.shape, o_ref.dtype)


def _moe_call(tile_expert, n_used, xs, w_gate_up, b_gate_up, w_down, b_down):
    p = xs.shape[0]
    n_tiles = p // MOE_TM
    grid_spec = pltpu.PrefetchScalarGridSpec(
        num_scalar_prefetch=2,
        grid=(n_tiles,),
        in_specs=[pl.BlockSpec((MOE_TM, D_MODEL), lambda i, te, nu: (i, 0)),
                  pl.BlockSpec((1, D_MODEL, 2 * D_FF), lambda i, te, nu: (te[i], 0, 0)),
                  pl.BlockSpec((1, 1, 2 * D_FF), lambda i, te, nu: (te[i], 0, 0)),
                  pl.BlockSpec((1, D_FF, D_MODEL), lambda i, te, nu: (te[i], 0, 0)),
                  pl.BlockSpec((1, 1, D_MODEL), lambda i, te, nu: (te[i], 0, 0))],
        out_specs=pl.BlockSpec((MOE_TM, D_MODEL), lambda i, te, nu: (i, 0)),
        scratch_shapes=[pltpu.VMEM((D_MODEL, 2 * D_FF), BF16), pltpu.VMEM((D_FF, D_MODEL), BF16)],
    )
    return pl.pallas_call(
        _moe_kernel,
        grid_spec=grid_spec,
        out_shape=jax.ShapeDtypeStruct((p, D_MODEL), BF16),
        compiler_params=pltpu.CompilerParams(dimension_semantics=("arbitrary",),
                                             vmem_limit_bytes=VMEM_LIMIT),
        name="moe_experts",
    )(tile_expert, n_used, xs, w_gate_up, b_gate_up.reshape(N_EXPERTS, 1, 2 * D_FF),
      w_down, b_down.reshape(N_EXPERTS, 1, D_MODEL))


def _combine_kernel(*refs, n_seg, seg_batches):
    x1_ref = refs[0]
    yg_refs = refs[1:1 + n_seg]
    tw_ref, mod_ref, g_ref, o_ref = refs[1 + n_seg:]
    seg = pl.program_id(0) // seg_batches
    for s in range(n_seg):
        @pl.when(seg == s)
        def _(yg_ref=yg_refs[s]):
            tw = tw_ref[0]
            acc = tw[:, 0:1] * yg_ref[0, 0].astype(F32)
            for k in range(1, TOP_K):
                acc = acc + tw[:, k:k + 1] * yg_ref[k, 0].astype(F32)
            g2 = mod_ref[0, :, 5 * D_MODEL:6 * D_MODEL]
            x2 = x1_ref[0] + g2 * acc
            ms = jnp.mean(x2 * x2, axis=-1, keepdims=True)
            o_ref[0] = x2 * lax.rsqrt(ms + RMS_EPS) * g_ref[...]


def _combine_call(x1, yg_segs, top_w, mod, norm_g):
    b, l, _ = x1.shape
    tl = 256
    n_seg = len(yg_segs)
    seg_batches = b // n_seg
    per_batch = mod.shape[0] > 1
    mod_map = (lambda i, j: (i, 0, 0)) if per_batch else (lambda i, j: (0, 0, 0))
    tok = lambda w: pl.BlockSpec((1, tl, w), lambda i, j: (i, j, 0))

    def yg_spec(s):
        def index_map(i, j):
            inside = (i // seg_batches) == s
            return (0, jnp.where(inside, i - s * seg_batches, 0), jnp.where(inside, j, 0), 0)
        return pl.BlockSpec((TOP_K, 1, tl, D_MODEL), index_map)

    return pl.pallas_call(
        functools.partial(_combine_kernel, n_seg=n_seg, seg_batches=seg_batches),
        grid=(b, l // tl),
        in_specs=[tok(D_MODEL)] + [yg_spec(s) for s in range(n_seg)] + [
            tok(TOP_K),
            pl.BlockSpec((1, 1, 6 * D_MODEL), mod_map),
            pl.BlockSpec((1, D_MODEL), lambda i, j: (0, 0))],
        out_specs=tok(D_MODEL),
        out_shape=jax.ShapeDtypeStruct((b, l, D_MODEL), F32),
        compiler_params=pltpu.CompilerParams(dimension_semantics=("arbitrary", "arbitrary"),
                                             vmem_limit_bytes=VMEM_LIMIT),
        name="moe_combine_final_norm",
    )(x1, *yg_segs, top_w, mod, norm_g.reshape(1, D_MODEL))


def _route(top_i, rank, counts):
    t = top_i.shape[0]
    p = t * TOP_K + N_EXPERTS * MOE_TM
    n_tiles = p // MOE_TM
    padded = ((counts + MOE_TM - 1) // MOE_TM) * MOE_TM
    ends = jnp.cumsum(padded)
    starts = ends - padded
    dest = starts.at[top_i].get(mode="promise_in_bounds") + rank
    tile_start = jnp.arange(n_tiles, dtype=jnp.int32) * MOE_TM
    tile_expert = jnp.minimum(jnp.sum((ends[None, :] <= tile_start[:, None]).astype(jnp.int32), axis=1),
                              N_EXPERTS - 1)
    n_used = (ends[-1] // MOE_TM).reshape(1)
    tok = jnp.broadcast_to(jnp.arange(t, dtype=jnp.int32)[:, None], (t, TOP_K))
    src_tok = jnp.zeros((p,), jnp.int32).at[dest.reshape(-1)].set(tok.reshape(-1), unique_indices=True)
    return dest, src_tok, tile_expert.astype(jnp.int32), n_used.astype(jnp.int32)


def _layer_front(x, mod, s_ssd_f, s_ssd_b, s_s5_f, s_s5_b, grid_cols, prm):
    z, xbc, u, dt = _inproj_call(x, mod, prm["norm1_g"], prm["w_in_r"])
    y_ssd, n_ssd_f, n_ssd_b = _ssd_call(z, xbc, dt, s_ssd_f, s_ssd_b, prm["ssd"])
    y_f, n_s5_f = _s5_call(u, _s5_state_in(s_s5_f), prm["s5_f"], grid_cols, False)
    y_s5, n_s5_b = _s5_call(u, _s5_state_in(s_s5_b), prm["s5_b"], grid_cols, True,
                            y_first=y_f, glu=prm["glu"])
    seg_batches = max(1, min(x.shape[0], SEG_TOKENS // x.shape[1]))
    x1, h2, top_i, top_w, rank, cnt = _post_call(x, y_ssd, y_s5, mod, prm["w_out"], prm["norm2_g"],
                                                 prm["w_router"], prm["b_router"], seg_batches)
    states = (n_ssd_f, n_ssd_b, _s5_state_out(n_s5_f), _s5_state_out(n_s5_b))
    return x1, (h2, top_i, rank, cnt, seg_batches), top_w, states


def _moe_segments(routed, moe_w):
    h2, top_i, rank, cnt, seg_batches = routed
    b, l, _ = h2.shape
    t = seg_batches * l
    out = []
    for s in range(b // seg_batches):
        sl = slice(s * seg_batches, (s + 1) * seg_batches)
        dest, src_tok, tile_expert, n_used = _route(
            top_i[sl].reshape(t, TOP_K), rank[sl].reshape(t, TOP_K), cnt[s].reshape(N_EXPERTS).astype(jnp.int32))
        src = jnp.concatenate([h2[sl].reshape(t, D_MODEL),
                               jnp.zeros((max(GATHER_SRC_ROWS - t, 0), D_MODEL), BF16)], axis=0)
        xs = src.at[src_tok].get(mode="promise_in_bounds")
        ys = _moe_call(tile_expert, n_used, xs, *moe_w)
        rows = ys.at[dest.T.reshape(-1)].get(mode="promise_in_bounds")
        out.append(rows.reshape(TOP_K, seg_batches, l, D_MODEL))
    return out


def kernel(x_prompt, x_sample, state_ssd_fwd, state_ssd_bwd, state_s5_fwd, state_s5_bwd, c, c_ctx, w_ada, b_ada, norm1_g, w_in, ssd_conv_w, ssd_conv_b, ssd_dt_bias_fwd, ssd_dt_bias_bwd, ssd_a_log_fwd, ssd_a_log_bwd, ssd_d, ssd_norm_g, s5_a_re_fwd, s5_a_im_fwd, s5_log_dt_fwd, s5_b_re_fwd, s5_b_im_fwd, s5_c_re_fwd, s5_c_im_fwd, s5_a_re_bwd, s5_a_im_bwd, s5_log_dt_bwd, s5_b_re_bwd, s5_b_im_bwd, s5_c_re_bwd, s5_c_im_bwd, s5_d, w_glu, b_glu, w_out, norm2_g, w_router, b_router, w_gate_up, b_gate_up, w_down, b_down, norm_f_g):
    depth = w_ada.shape[0]
    assert depth == 1, "single trunk layer"
    nb_ctx, l_ctx, _ = x_prompt.shape
    nb_lat, l_lat, _ = x_sample.shape

    w_in0 = w_in[0]
    dt_lo = SSD_WIDTH + SSD_CONV_CH
    dt_cols = jnp.pad(w_in0[:, dt_lo:dt_lo + 2 * SSD_HEADS], ((0, 0), (0, DT_PAD - 2 * SSD_HEADS)))
    w_in_r = jnp.concatenate([w_in0[:, :dt_lo], w_in0[:, dt_lo + 2 * SSD_HEADS:], dt_cols], axis=1).astype(BF16)
    pad_dt = lambda f, b: jnp.pad(jnp.concatenate([f, b]).astype(F32), (0, DT_PAD - 2 * SSD_HEADS)).reshape(1, DT_PAD)
    ssd_prm = {
        "conv_w": jnp.pad(ssd_conv_w[0].astype(F32), ((0, 5), (0, 0))),
        "conv_b": ssd_conv_b[0].astype(F32).reshape(1, SSD_CONV_CH),
        "dt_bias": pad_dt(ssd_dt_bias_fwd[0], ssd_dt_bias_bwd[0]),
        "a": pad_dt(-jnp.exp(ssd_a_log_fwd[0].astype(F32)), -jnp.exp(ssd_a_log_bwd[0].astype(F32))),
        "d_skip": jnp.repeat(ssd_d[0].astype(F32), SSD_HEAD_DIM).reshape(1, SSD_WIDTH),
        "norm_g": ssd_norm_g[0].astype(F32).reshape(1, SSD_WIDTH),
    }
    prm = {
        "norm1_g": norm1_g[0], "w_in_r": w_in_r, "ssd": ssd_prm,
        "s5_f": _s5_params(s5_a_re_fwd[0], s5_a_im_fwd[0], s5_log_dt_fwd[0], s5_b_re_fwd[0],
                           s5_b_im_fwd[0], s5_c_re_fwd[0], s5_c_im_fwd[0]),
        "s5_b": _s5_params(s5_a_re_bwd[0], s5_a_im_bwd[0], s5_log_dt_bwd[0], s5_b_re_bwd[0],
                           s5_b_im_bwd[0], s5_c_re_bwd[0], s5_c_im_bwd[0]),
        "glu": {"d": s5_d[0].astype(F32).reshape(1, S5_WIDTH), "w": w_glu[0].astype(BF16),
                "b": b_glu[0].astype(F32).reshape(1, 2 * S5_WIDTH)},
        "w_out": w_out[0].astype(BF16), "norm2_g": norm2_g[0],
        "w_router": w_router[0].astype(F32), "b_router": b_router[0].astype(F32),
    }

    conds = jnp.concatenate([c_ctx[None, :], c], axis=0)
    conds = jnp.pad(conds, ((0, (-conds.shape[0]) % 8), (0, 0)))
    mod = _ada_call(conds, w_ada[0], b_ada[0])
    mod_ctx = mod[0:1].reshape(1, 1, 6 * D_MODEL)
    mod_lat = mod[1:1 + nb_lat].reshape(nb_lat, 1, 6 * D_MODEL)

    zero_ssd = jnp.zeros((nb_ctx, SSD_HEADS, SSD_HEAD_DIM, SSD_STATE), F32)
    zero_s5 = jnp.zeros((nb_ctx, S5_GROUPS, S5_STATE, 2), F32)
    x1_c, routed_c, tw_c, st_c = _layer_front(
        x_prompt, mod_ctx, zero_ssd, zero_ssd, zero_s5, zero_s5, False, prm)
    x1_l, routed_l, tw_l, _ = _layer_front(
        x_sample, mod_lat, state_ssd_fwd[:, 0], state_ssd_bwd[:, 0], state_s5_fwd[:, 0],
        state_s5_bwd[:, 0], True, prm)

    moe_w = (w_gate_up[0], b_gate_up[0], w_down[0], b_down[0])
    y_prompt = _combine_call(x1_c, _moe_segments(routed_c, moe_w), tw_c, mod_ctx, norm_f_g)
    y_sample = _combine_call(x1_l, _moe_segments(routed_l, moe_w), tw_l, mod_lat, norm_f_g)

    ssd_f, ssd_b, s5_f, s5_b = st_c
    return (y_prompt, y_sample, ssd_f[:, None], ssd_b[:, None], s5_f[:, None], s5_b[:, None])
```

```python
import functools
import math

import jax
import jax.numpy as jnp
from jax import lax
from jax.experimental import pallas as pl
from jax.experimental.pallas import tpu as pltpu

F32 = jnp.float32
BF16 = jnp.bfloat16

D_MODEL = 1024
GRID_W = 64
SSD_WIDTH = 512
SSD_HEAD_DIM = 64
SSD_HEADS = 8
SSD_GROUPS = 2
SSD_HEADS_PER_GROUP = SSD_HEADS // SSD_GROUPS
SSD_STATE = 128
SSD_CHUNK = 128
SSD_CONV_CH = SSD_WIDTH + 2 * SSD_GROUPS * SSD_STATE
S5_WIDTH = 512
S5_CH = 16
S5_GROUPS = 32
S5_STATE = 64
N_EXPERTS = 32
TOP_K = 4
D_FF = 1024
SWIGLU_LIMIT = 7.0
SWIGLU_ALPHA = 1.702
RMS_EPS = 1e-5

LANES = 128
DT_PAD = 128
S5_BATCH = 8
S5_COLS = 2 * S5_GROUPS * S5_STATE
S5_BLK = 512
S5_SUB_STEPS = 32
S5_BLOCK_STEPS = 256
MOE_TM = 256
GATHER_SRC_ROWS = 32768
VMEM_LIMIT = 56 * 1024 * 1024


def _sigmoid(x):
    return 1.0 / (1.0 + jnp.exp(-x))


def _split3(x):
    hi = x.astype(BF16)
    r1 = x - hi.astype(F32)
    mid = r1.astype(BF16)
    lo = (r1 - mid.astype(F32)).astype(BF16)
    return hi, mid, lo


def _split2(x):
    hi = x.astype(BF16)
    return hi, (x - hi.astype(F32)).astype(BF16)


def _dot(a, b):
    return jnp.dot(a, b, preferred_element_type=F32)


def _ada_kernel(c_ref, w_ref, b_ref, o_ref):
    c = c_ref[...]
    s = c * _sigmoid(c)
    o_ref[...] = _dot(s.astype(BF16), w_ref[...].astype(BF16)) + b_ref[...]


def _ada_call(conds, w_ada, b_ada):
    n = w_ada.shape[1]
    tn = 1536
    rows = conds.shape[0]
    return pl.pallas_call(
        _ada_kernel,
        grid=(n // tn,),
        in_specs=[pl.BlockSpec((rows, D_MODEL), lambda j: (0, 0)),
                  pl.BlockSpec((D_MODEL, tn), lambda j: (0, j)),
                  pl.BlockSpec((1, tn), lambda j: (0, j))],
        out_specs=pl.BlockSpec((rows, tn), lambda j: (0, j)),
        out_shape=jax.ShapeDtypeStruct((rows, n), F32),
        compiler_params=pltpu.CompilerParams(dimension_semantics=("arbitrary",),
                                             vmem_limit_bytes=VMEM_LIMIT),
        name="ada_mod",
    )(conds, w_ada, b_ada.reshape(1, n))


def _inproj_kernel(x_ref, mod_ref, g_ref, w_ref, z_ref, xbc_ref, u_ref, dt_ref):
    x = x_ref[0]
    ms = jnp.mean(x * x, axis=-1, keepdims=True)
    y = x * lax.rsqrt(ms + RMS_EPS) * g_ref[...]
    sh = mod_ref[0, :, 0:D_MODEL]
    sc = mod_ref[0, :, D_MODEL:2 * D_MODEL]
    h = (y * (1.0 + sc) + sh).astype(BF16)
    z_ref[0] = _dot(h, w_ref[:, 0:512]).astype(BF16)
    xbc_ref[0] = _dot(h, w_ref[:, 512:1536]).astype(BF16)
    u_ref[0] = _dot(h, w_ref[:, 1536:2048])
    dt_ref[0] = _dot(h, w_ref[:, 2048:2048 + DT_PAD])


def _inproj_call(x, mod, norm_g, w_in_r):
    b, l, _ = x.shape
    tl = min(l, 512)
    per_batch = mod.shape[0] > 1
    mod_map = (lambda i, j: (i, 0, 0)) if per_batch else (lambda i, j: (0, 0, 0))
    ncol = w_in_r.shape[1]
    tok = lambda w: pl.BlockSpec((1, tl, w), lambda i, j: (i, j, 0))
    return pl.pallas_call(
        _inproj_kernel,
        grid=(b, l // tl),
        in_specs=[tok(D_MODEL),
                  pl.BlockSpec((1, 1, 6 * D_MODEL), mod_map),
                  pl.BlockSpec((1, D_MODEL), lambda i, j: (0, 0)),
                  pl.BlockSpec((D_MODEL, ncol), lambda i, j: (0, 0))],
        out_specs=[tok(SSD_WIDTH), tok(SSD_CONV_CH), tok(S5_WIDTH), tok(DT_PAD)],
        out_shape=[jax.ShapeDtypeStruct((b, l, SSD_WIDTH), BF16),
                   jax.ShapeDtypeStruct((b, l, SSD_CONV_CH), BF16),
                   jax.ShapeDtypeStruct((b, l, S5_WIDTH), F32),
                   jax.ShapeDtypeStruct((b, l, DT_PAD), F32)],
        compiler_params=pltpu.CompilerParams(dimension_semantics=("arbitrary", "arbitrary"),
                                             vmem_limit_bytes=VMEM_LIMIT),
        name="norm1_inproj",
    )(x, mod, norm_g.reshape(1, D_MODEL), w_in_r)


def _softplus(x):
    return jnp.maximum(x, 0.0) + jnp.log1p(jnp.exp(-jnp.abs(x)))


def _ssd_kernel(z_ref, xbc_ref, dt_ref, s0f_ref, s0b_ref, cw_ref, cb_ref, dtb_ref, a_ref,
                dskip_ref, ng_ref, ex_ref, y_ref, sf_ref, sb_ref,
                xc_s, dts_s, yf_s, yb_s, stf_s, stb_s, *, seq):
    q = SSD_CHUNK
    nc = seq // q
    gw = SSD_HEADS_PER_GROUP * SSD_HEAD_DIM
    row_i = lax.broadcasted_iota(jnp.int32, (q, 1), 0)

    def conv_body(c, carry):
        r0 = pl.multiple_of(c * q, q)
        cur = xbc_ref[0, pl.ds(r0, q), :].astype(F32)
        p0 = pl.multiple_of(jnp.maximum(r0 - 16, 0), 16)
        n0 = pl.multiple_of(jnp.minimum(r0 + q, seq - 16), 16)
        prev_row = xbc_ref[0, pl.ds(p0, 16), :].astype(F32)[15:16, :]
        next_row = xbc_ref[0, pl.ds(n0, 16), :].astype(F32)[0:1, :]
        prev_row = jnp.where(c > 0, prev_row, 0.0)
        next_row = jnp.where(c < nc - 1, next_row, 0.0)
        down = jnp.where(row_i == 0, prev_row, pltpu.roll(cur, 1, axis=0))
        up = jnp.where(row_i == q - 1, next_row, pltpu.roll(cur, q - 1, axis=0))
        v = cw_ref[0:1, :] * down + cw_ref[1:2, :] * cur + cw_ref[2:3, :] * up + cb_ref[...]
        xc_s[pl.ds(r0, q), :] = v * _sigmoid(v)
        dts_s[pl.ds(r0, q), :] = _softplus(dt_ref[0, pl.ds(r0, q), :] + dtb_ref[...])
        return carry

    lax.fori_loop(0, nc, conv_body, 0)

    li = lax.broadcasted_iota(jnp.int32, (q, q), 0)
    si = lax.broadcasted_iota(jnp.int32, (q, q), 1)
    head_of_lane = lax.broadcasted_iota(jnp.int32, (q, gw), 1) // SSD_HEAD_DIM

    def expand(v, e_mat):
        hi, lo = _split2(v)
        return _dot(hi, e_mat) + _dot(lo, e_mat)

    def chunk(c, direction):
        r0 = pl.multiple_of(c * q, q)
        st_ref = stf_s if direction == 0 else stb_s
        y_dst = yf_s if direction == 0 else yb_s
        e_mat = ex_ref[direction]
        causal = (li >= si) if direction == 0 else (li <= si)
        tri = jnp.where(causal, 1.0, 0.0).astype(BF16)
        dts = dts_s[pl.ds(r0, q), :]
        dta = dts * a_ref[...]
        d_hi, d_mid, d_lo = _split3(dta)
        cs = _dot(tri, d_hi) + _dot(tri, d_mid) + _dot(tri, d_lo)
        tot = cs[q - 1:q, :] if direction == 0 else cs[0:1, :]
        cs_t = cs.T
        ecs = jnp.exp(cs)
        tail = jnp.exp(tot - cs)
        dt_x = expand(dts, e_mat)
        dtw_x = expand(dts * tail, e_mat)
        etot_x = expand(jnp.broadcast_to(jnp.exp(tot), (8, DT_PAD)), e_mat)[0:1, :]
        for g in range(SSD_GROUPS):
            b_off = SSD_WIDTH + g * SSD_STATE
            c_off = SSD_WIDTH + SSD_GROUPS * SSD_STATE + g * SSD_STATE
            x_g = xc_s[pl.ds(r0, q), g * gw:(g + 1) * gw]
            bm = xc_s[pl.ds(r0, q), b_off:b_off + SSD_STATE]
            cm = xc_s[pl.ds(r0, q), c_off:c_off + SSD_STATE]
            cb = lax.dot_general(cm.astype(BF16), bm.astype(BF16), (((1,), (1,)), ((), ())),
                                 preferred_element_type=F32)
            st_g = st_ref[g]
            xdt = x_g * dt_x[:, g * gw:(g + 1) * gw]
            lhs, rhs = [], []
            for hh in range(SSD_HEADS_PER_GROUP):
                col = g * SSD_HEADS_PER_GROUP + hh + SSD_HEADS * direction
                seg = jnp.broadcast_to(cs[:, col:col + 1], (q, q)) - cs_t[col:col + 1, :]
                dec = jnp.exp(jnp.where(causal, seg, -1e30))
                lhs.append((cb * dec).astype(BF16))
                lhs.append((cm * jnp.broadcast_to(ecs[:, col:col + 1], (q, q))).astype(BF16))
                own = head_of_lane == hh
                rhs.append(jnp.where(own, xdt, 0.0).astype(BF16))
                rhs.append(jnp.where(own, st_g, 0.0).astype(BF16))
            y_dst[pl.ds(r0, q), g * gw:(g + 1) * gw] = _dot(jnp.concatenate(lhs, axis=1),
                                                             jnp.concatenate(rhs, axis=0))
            xw = (x_g * dtw_x[:, g * gw:(g + 1) * gw]).astype(BF16)
            st_ref[g] = etot_x[:, g * gw:(g + 1) * gw] * st_g + _dot(bm.T.astype(BF16), xw)

    stf_s[...] = s0f_ref[0]
    stb_s[...] = s0b_ref[0]

    def both(i, carry):
        chunk(i, 0)
        chunk(nc - 1 - i, 1)
        return carry

    lax.fori_loop(0, nc, both, 0)
    sf_ref[0] = stf_s[...]
    sb_ref[0] = stb_s[...]

    def finish(c, carry):
        r0 = pl.multiple_of(c * q, q)
        xs = xc_s[pl.ds(r0, q), 0:SSD_WIDTH]
        zz = z_ref[0, pl.ds(r0, q), :].astype(F32)
        y = (yf_s[pl.ds(r0, q), :] + yb_s[pl.ds(r0, q), :] + dskip_ref[...] * xs) * (zz * _sigmoid(zz))
        ms = jnp.mean(y * y, axis=-1, keepdims=True)
        y_ref[0, pl.ds(r0, q), :] = (y * lax.rsqrt(ms + RMS_EPS) * ng_ref[...]).astype(y_ref.dtype)
        return carry

    lax.fori_loop(0, nc, finish, 0)


def _ssd_state_in(s):
    b = s.shape[0]
    s = s.astype(F32).reshape(b, SSD_GROUPS, SSD_HEADS_PER_GROUP, SSD_HEAD_DIM, SSD_STATE)
    return s.transpose(0, 1, 4, 2, 3).reshape(b, SSD_GROUPS, SSD_STATE, SSD_HEADS_PER_GROUP * SSD_HEAD_DIM)


def _ssd_state_out(s):
    b = s.shape[0]
    s = s.reshape(b, SSD_GROUPS, SSD_STATE, SSD_HEADS_PER_GROUP, SSD_HEAD_DIM)
    return s.transpose(0, 1, 3, 4, 2).reshape(b, SSD_HEADS, SSD_HEAD_DIM, SSD_STATE)


def _ssd_call(z, xbc, dt, s0f, s0b, prm):
    b, l, _ = z.shape
    st_shape = (SSD_GROUPS, SSD_STATE, SSD_HEADS_PER_GROUP * SSD_HEAD_DIM)
    tok = lambda w: pl.BlockSpec((1, l, w), lambda i: (i, 0, 0))
    st_spec = pl.BlockSpec((1,) + st_shape, lambda i: (i, 0, 0, 0))
    par = lambda r, w: pl.BlockSpec((r, w), lambda i: (0, 0))
    y, sf, sb = pl.pallas_call(
        functools.partial(_ssd_kernel, seq=l),
        grid=(b,),
        in_specs=[tok(SSD_WIDTH), tok(SSD_CONV_CH), tok(DT_PAD), st_spec, st_spec,
                  par(8, SSD_CONV_CH), par(1, SSD_CONV_CH), par(1, DT_PAD), par(1, DT_PAD),
                  par(1, SSD_WIDTH), par(1, SSD_WIDTH),
                  pl.BlockSpec((2, DT_PAD, SSD_WIDTH), lambda i: (0, 0, 0))],
        out_specs=[tok(SSD_WIDTH), st_spec, st_spec],
        out_shape=[jax.ShapeDtypeStruct((b, l, SSD_WIDTH), BF16),
                   jax.ShapeDtypeStruct((b,) + st_shape, F32),
                   jax.ShapeDtypeStruct((b,) + st_shape, F32)],
        scratch_shapes=[pltpu.VMEM((l, SSD_CONV_CH), F32),
                        pltpu.VMEM((l, DT_PAD), F32),
                        pltpu.VMEM((l, SSD_WIDTH), F32),
                        pltpu.VMEM((l, SSD_WIDTH), F32),
                        pltpu.VMEM(st_shape, F32),
                        pltpu.VMEM(st_shape, F32)],
        compiler_params=pltpu.CompilerParams(dimension_semantics=("arbitrary",),
                                             vmem_limit_bytes=VMEM_LIMIT),
        name="ssd_mixer",
    )(z, xbc, dt, _ssd_state_in(s0f), _ssd_state_in(s0b),
      prm["conv_w"], prm["conv_b"], prm["dt_bias"], prm["a"], prm["d_skip"], prm["norm_g"], prm["expand"])
    return y, _ssd_state_out(sf), _ssd_state_out(sb)


def _gelu_tanh(x):
    return 0.5 * x * (1.0 + jnp.tanh(math.sqrt(2.0 / math.pi) * (x + 0.044715 * (x * x * x))))


def _s5_kernel(*refs, grid_cols, reverse, final):
    if final:
        (u_ref, s0_ref, lr_ref, li_ref, wb_ref, wc_ref, yf_ref, d_ref, wg_ref, bg_ref,
         o_ref, sfin_ref, ur_s, bu_s, yr_s, st_s) = refs
    else:
        (u_ref, s0_ref, lr_ref, li_ref, wb_ref, wc_ref,
         o_ref, sfin_ref, ur_s, bu_s, yr_s, st_s) = refs
    nb = S5_BATCH
    steps = S5_BLOCK_STEPS
    rows = steps * nb
    sub_rows = S5_SUB_STEPS * nb
    n_sub = steps // S5_SUB_STEPS
    n_rows = u_ref.shape[1] if grid_cols else None
    j = pl.program_id(1)

    @pl.when(j == 0)
    def _():
        st_s[...] = s0_ref[0]

    nlt = S5_WIDTH // LANES

    def put_strided(dst, start, count, stride, val):
        for k in range(nlt):
            dst[k, pl.ds(start, count, stride=stride), :] = val[:, k * LANES:(k + 1) * LANES]

    def get_strided(src, start, count, stride):
        return jnp.concatenate([src[k, pl.ds(start, count, stride=stride), :] for k in range(nlt)], axis=1)

    def get_rows(src, r0, n):
        return jnp.concatenate([src[k, pl.ds(r0, n), :] for k in range(nlt)], axis=1)

    if grid_cols:
        n_cols = u_ref.shape[2]
        for b in range(nb):
            for r in range(n_rows):
                put_strided(ur_s, r * nb + b, n_cols, n_rows * nb, u_ref[b, r])
    else:
        for b in range(nb):
            put_strided(ur_s, b, steps, nb, u_ref[b])

    def sub_chunk(i, carry):
        sidx = (n_sub - 1 - i) if reverse else i
        r0 = pl.multiple_of(sidx * sub_rows, sub_rows)
        for hf in range(2):
            uh = jnp.concatenate([ur_s[2 * hf, pl.ds(r0, sub_rows), :],
                                  ur_s[2 * hf + 1, pl.ds(r0, sub_rows), :]], axis=1).astype(BF16)
            bu_s[:, hf * 2048:(hf + 1) * 2048] = _dot(uh, wb_ref[hf])
        for blk in range(S5_COLS // S5_BLK):
            c0 = blk * S5_BLK
            lam_r = jnp.broadcast_to(lr_ref[:, c0:c0 + S5_BLK], (nb, S5_BLK))
            lam_i = jnp.broadcast_to(li_ref[:, c0:c0 + S5_BLK], (nb, S5_BLK))

            def step(t, s):
                tt = (S5_SUB_STEPS - 1 - t) if reverse else t
                rr = pl.multiple_of(tt * nb, nb)
                sw = jnp.concatenate([s[:, S5_BLK // 2:], s[:, :S5_BLK // 2]], axis=1)
                s_new = lam_r * s + lam_i * sw + bu_s[pl.ds(rr, nb), c0:c0 + S5_BLK]
                bu_s[pl.ds(rr, nb), c0:c0 + S5_BLK] = s_new
                return s_new

            st_s[:, c0:c0 + S5_BLK] = lax.fori_loop(0, S5_SUB_STEPS, step, st_s[:, c0:c0 + S5_BLK],
                                                    unroll=4)
            half = (blk % 2) * 64
            yr_s[blk // 2, pl.ds(r0, sub_rows), half:half + 64] = _dot(
                bu_s[:, c0:c0 + S5_BLK].astype(BF16), wc_ref[blk])
        return carry

    lax.fori_loop(0, n_sub, sub_chunk, 0)

    @pl.when(j == pl.num_programs(1) - 1)
    def _():
        sfin_ref[0] = st_s[...]

    if not final:
        for k in range(nlt):
            o_ref[0, :, k * LANES:(k + 1) * LANES] = yr_s[k]
        return

    def glu_chunk(i, carry):
        r0 = pl.multiple_of(i * sub_rows, sub_rows)
        y = (get_rows(yr_s, r0, sub_rows) + yf_ref[0, pl.ds(r0, sub_rows), :]
             + d_ref[...] * get_rows(ur_s, r0, sub_rows))
        y = _gelu_tanh(y).astype(BF16)
        val = _dot(y, wg_ref[:, 0:S5_WIDTH]) + bg_ref[:, 0:S5_WIDTH]
        gate = _dot(y, wg_ref[:, S5_WIDTH:2 * S5_WIDTH]) + bg_ref[:, S5_WIDTH:2 * S5_WIDTH]
        res = val * _sigmoid(gate)
        for k in range(nlt):
            yr_s[k, pl.ds(r0, sub_rows), :] = res[:, k * LANES:(k + 1) * LANES]
        return carry

    lax.fori_loop(0, n_sub, glu_chunk, 0)

    if grid_cols:
        for b in range(nb):
            for r in range(n_rows):
                o_ref[b, r] = get_strided(yr_s, r * nb + b, n_cols, n_rows * nb)
    else:
        for b in range(nb):
            o_ref[b] = get_strided(yr_s, b, steps, nb)


def _s5_call(u, s0, prm, grid_cols, reverse, y_first=None, glu=None):
    b, l, _ = u.shape
    nbg = b // S5_BATCH
    nblk = l // S5_BLOCK_STEPS
    rows = S5_BLOCK_STEPS * S5_BATCH
    final = y_first is not None
    blk_of = (lambda j: nblk - 1 - j) if reverse else (lambda j: j)
    if grid_cols:
        n_rows = l // GRID_W
        u_in = u.reshape(b, n_rows, GRID_W, S5_WIDTH)
        cols_per_blk = S5_BLOCK_STEPS // n_rows
        tok_spec = pl.BlockSpec((S5_BATCH, n_rows, cols_per_blk, S5_WIDTH),
                                lambda i, j: (i, 0, blk_of(j), 0))
        out_nat = jax.ShapeDtypeStruct((b, n_rows, GRID_W, S5_WIDTH), F32)
    else:
        u_in = u
        tok_spec = pl.BlockSpec((S5_BATCH, S5_BLOCK_STEPS, S5_WIDTH), lambda i, j: (i, blk_of(j), 0))
        out_nat = jax.ShapeDtypeStruct((b, l, S5_WIDTH), F32)
    scan_spec = pl.BlockSpec((1, rows, S5_WIDTH), lambda i, j: (i, blk_of(j), 0))
    st_spec = pl.BlockSpec((1, S5_BATCH, S5_COLS), lambda i, j: (i, 0, 0))
    full = lambda shape: pl.BlockSpec(shape, lambda i, j: (0,) * len(shape))
    in_specs = [tok_spec, st_spec, full((1, S5_COLS)), full((1, S5_COLS)),
                full((2, 256, 2048)), full((S5_COLS // S5_BLK, S5_BLK, 64))]
    args = [u_in, s0, prm["lam_r"], prm["lam_i"], prm["wb"], prm["wc"]]
    if final:
        in_specs += [scan_spec, full((1, S5_WIDTH)), full((S5_WIDTH, 2 * S5_WIDTH)), full((1, 2 * S5_WIDTH))]
        args += [y_first, glu["d"], glu["w"], glu["b"]]
        out_specs = [tok_spec, st_spec]
        out_shape = [out_nat, jax.ShapeDtypeStruct((nbg, S5_BATCH, S5_COLS), F32)]
    else:
        out_specs = [scan_spec, st_spec]
        out_shape = [jax.ShapeDtypeStruct((nbg, l * S5_BATCH, S5_WIDTH), F32),
                     jax.ShapeDtypeStruct((nbg, S5_BATCH, S5_COLS), F32)]
    y, sfin = pl.pallas_call(
        functools.partial(_s5_kernel, grid_cols=grid_cols, reverse=reverse, final=final),
        grid=(nbg, nblk),
        in_specs=in_specs,
        out_specs=out_specs,
        out_shape=out_shape,
        scratch_shapes=[pltpu.VMEM((S5_WIDTH // LANES, rows, LANES), F32),
                        pltpu.VMEM((S5_SUB_STEPS * S5_BATCH, S5_COLS), F32),
                        pltpu.VMEM((S5_WIDTH // LANES, rows, LANES), F32),
                        pltpu.VMEM((S5_BATCH, S5_COLS), F32)],
        compiler_params=pltpu.CompilerParams(dimension_semantics=("arbitrary", "arbitrary"),
                                             vmem_limit_bytes=VMEM_LIMIT),
        name="s5_final" if final else "s5_first",
    )(*args)
    if final:
        y = y.reshape(b, l, S5_WIDTH)
    return y, sfin


def _s5_cols(t):
    lead = t.shape[:-2]
    return t.reshape(lead + (S5_COLS // S5_BLK, S5_BLK // 2))


def _s5_state_in(s0):
    b = s0.shape[0]
    re, im = _s5_cols(s0[..., 0].astype(F32)), _s5_cols(s0[..., 1].astype(F32))
    return jnp.stack([re, im], axis=-2).reshape(b // S5_BATCH, S5_BATCH, S5_COLS)


def _s5_state_out(s):
    b = s.shape[0] * S5_BATCH
    s = s.reshape(b, S5_COLS // S5_BLK, 2, S5_BLK // 2)
    re = s[:, :, 0].reshape(b, S5_GROUPS, S5_STATE)
    im = s[:, :, 1].reshape(b, S5_GROUPS, S5_STATE)
    return jnp.stack([re, im], axis=-1)


def _s5_params(a_re, a_im, log_dt, b_re, b_im, c_re, c_im):
    a_re, a_im = a_re.astype(F32), a_im.astype(F32)
    delta = jnp.exp(log_dt.astype(F32))[:, None]
    mag = jnp.exp(a_re * delta)
    lam_re, lam_im = mag * jnp.cos(a_im * delta), mag * jnp.sin(a_im * delta)
    den = a_re * a_re + a_im * a_im
    nr = lam_re - 1.0
    f_re = (nr * a_re + lam_im * a_im) / den
    f_im = (lam_im * a_re - nr * a_im) / den
    b_re, b_im = b_re.astype(F32), b_im.astype(F32)
    bb_re = f_re[..., None] * b_re - f_im[..., None] * b_im
    bb_im = f_re[..., None] * b_im + f_im[..., None] * b_re
    nblk = S5_COLS // S5_BLK
    lr = _s5_cols(lam_re)
    li = _s5_cols(lam_im)
    lam_r = jnp.stack([lr, lr], axis=-2).reshape(1, S5_COLS)
    lam_i = jnp.stack([-li, li], axis=-2).reshape(1, S5_COLS)
    gq = S5_BLK // 2 // S5_STATE
    eye = jnp.eye(gq, dtype=F32)

    def in_block(bb):
        t = bb.reshape(nblk, gq, S5_STATE, S5_CH)
        return jnp.einsum("gh,kgpc->kgchp", eye, t).reshape(nblk, gq * S5_CH, gq * S5_STATE)

    blk_in = jnp.concatenate([in_block(bb_re), in_block(bb_im)], axis=-1)
    half = nblk // 2
    eye_h = jnp.eye(half, dtype=F32)
    wb = jnp.einsum("jk,hkcn->hjckn", eye_h, blk_in.reshape(2, half, gq * S5_CH, S5_BLK))
    wb = wb.reshape(2, half * gq * S5_CH, half * S5_BLK).astype(BF16)

    def out_block(cc):
        t = cc.astype(F32).reshape(nblk, gq, S5_CH, S5_STATE)
        return jnp.einsum("gh,kgcp->kgphc", eye, t).reshape(nblk, gq * S5_STATE, gq * S5_CH)

    wc = jnp.concatenate([out_block(c_re), -out_block(c_im)], axis=1).astype(BF16)
    return {"lam_r": lam_r, "lam_i": lam_i, "wb": wb, "wc": wc}


def _post_kernel(x_ref, ys_ref, y5_ref, mod_ref, wo_ref, g_ref, wr_ref, br_ref,
                 x1_ref, h_ref, ti_ref, tw_ref, rk_ref, cnt_ref, cnt_s):
    first = jnp.logical_and(pl.program_id(0) == 0, pl.program_id(1) == 0)

    @pl.when(first)
    def _():
        cnt_s[...] = jnp.zeros(cnt_s.shape, cnt_s.dtype)

    tl = x_ref.shape[1]
    mixed = (_dot(ys_ref[0], wo_ref[0:SSD_WIDTH, :])
             + _dot(y5_ref[0].astype(BF16), wo_ref[SSD_WIDTH:SSD_WIDTH + S5_WIDTH, :]))
    g1 = mod_ref[0, :, 2 * D_MODEL:3 * D_MODEL]
    sh2 = mod_ref[0, :, 3 * D_MODEL:4 * D_MODEL]
    sc2 = mod_ref[0, :, 4 * D_MODEL:5 * D_MODEL]
    x1 = x_ref[0] + g1 * mixed
    x1_ref[0] = x1
    ms = jnp.mean(x1 * x1, axis=-1, keepdims=True)
    h = x1 * lax.rsqrt(ms + RMS_EPS) * g_ref[...] * (1.0 + sc2) + sh2
    h_ref[0] = h.astype(BF16)

    h_hi, h_mid, _ = _split3(h)
    wr = wr_ref[...]
    w_hi, w_mid, _ = _split3(wr)
    logits = _dot(h_hi, w_hi) + _dot(h_hi, w_mid) + _dot(h_mid, w_hi) + br_ref[...]

    lane = lax.broadcasted_iota(jnp.int32, (tl, N_EXPERTS), 1).astype(F32)
    slot = lax.broadcasted_iota(jnp.int32, (tl, TOP_K), 1)
    work = logits
    onehot = jnp.zeros((tl, N_EXPERTS), F32)
    sels, vals = [], []
    top_i = jnp.zeros((tl, TOP_K), jnp.int32)
    for k in range(TOP_K):
        m = jnp.max(work, axis=-1, keepdims=True)
        idx = jnp.min(jnp.where(work == m, lane, float(N_EXPERTS)), axis=-1, keepdims=True)
        sel = lane == idx
        sels.append(sel)
        vals.append(m)
        top_i = jnp.where(slot == k, idx.astype(jnp.int32), top_i)
        onehot = jnp.where(sel, 1.0, onehot)
        work = jnp.where(sel, -jnp.inf, work)
    es = [jnp.exp(v - vals[0]) for v in vals]
    den = es[0] + es[1] + es[2] + es[3]
    top_w = jnp.zeros((tl, TOP_K), F32)
    for k in range(TOP_K):
        top_w = jnp.where(slot == k, es[k] / den, top_w)

    ri = lax.broadcasted_iota(jnp.int32, (tl, tl), 0)
    ci = lax.broadcasted_iota(jnp.int32, (tl, tl), 1)
    strict = jnp.where(ri > ci, 1.0, 0.0).astype(BF16)
    before = _dot(strict, onehot.astype(BF16)) + cnt_s[...]
    rank = jnp.zeros((tl, TOP_K), jnp.int32)
    for k in range(TOP_K):
        rk = jnp.sum(jnp.where(sels[k], before, 0.0), axis=-1, keepdims=True)
        rank = jnp.where(slot == k, rk.astype(jnp.int32), rank)
    cnt_s[...] = cnt_s[...] + jnp.sum(onehot, axis=0, keepdims=True)
    ti_ref[0] = top_i
    tw_ref[0] = top_w
    rk_ref[0] = rank
    cnt_ref[...] = cnt_s[...]


def _post_call(x, y_ssd, y_s5, mod, w_out, norm_g, w_router, b_router):
    b, l, _ = x.shape
    tl = 256
    per_batch = mod.shape[0] > 1
    mod_map = (lambda i, j: (i, 0, 0)) if per_batch else (lambda i, j: (0, 0, 0))
    tok = lambda w: pl.BlockSpec((1, tl, w), lambda i, j: (i, j, 0))
    full = lambda r, w: pl.BlockSpec((r, w), lambda i, j: (0, 0))
    return pl.pallas_call(
        _post_kernel,
        grid=(b, l // tl),
        in_specs=[tok(D_MODEL), tok(SSD_WIDTH), tok(S5_WIDTH),
                  pl.BlockSpec((1, 1, 6 * D_MODEL), mod_map),
                  full(D_MODEL, D_MODEL), full(1, D_MODEL), full(D_MODEL, N_EXPERTS),
                  full(1, N_EXPERTS)],
        out_specs=[tok(D_MODEL), tok(D_MODEL), tok(TOP_K), tok(TOP_K), tok(TOP_K), full(1, N_EXPERTS)],
        out_shape=[jax.ShapeDtypeStruct((b, l, D_MODEL), F32),
                   jax.ShapeDtypeStruct((b, l, D_MODEL), BF16),
                   jax.ShapeDtypeStruct((b, l, TOP_K), jnp.int32),
                   jax.ShapeDtypeStruct((b, l, TOP_K), F32),
                   jax.ShapeDtypeStruct((b, l, TOP_K), jnp.int32),
                   jax.ShapeDtypeStruct((1, N_EXPERTS), F32)],
        scratch_shapes=[pltpu.VMEM((1, N_EXPERTS), F32)],
        compiler_params=pltpu.CompilerParams(dimension_semantics=("arbitrary", "arbitrary"),
                                             vmem_limit_bytes=VMEM_LIMIT),
        name="outproj_norm2_router",
    )(x, y_ssd, y_s5, mod, w_out, norm_g.reshape(1, D_MODEL), w_router,
      b_router.reshape(1, N_EXPERTS))


def _moe_kernel(te_ref, nu_ref, x_ref, wgu_ref, bgu_ref, wd_ref, bd_ref, o_ref, gu_s, dn_s):
    i = pl.program_id(0)
    prev = te_ref[jnp.maximum(i - 1, 0)]
    new_expert = jnp.logical_or(i == 0, te_ref[i] != prev)

    @pl.when(new_expert)
    def _():
        for r in range(D_MODEL // 128):
            gu_s[r * 128:(r + 1) * 128, :] = wgu_ref[0, r * 128:(r + 1) * 128, :].astype(BF16)
            dn_s[r * 128:(r + 1) * 128, :] = wd_ref[0, r * 128:(r + 1) * 128, :].astype(BF16)

    @pl.when(i < nu_ref[0])
    def _():
        x = x_ref[...]
        gate = _dot(x, gu_s[:, 0:D_FF]) + bgu_ref[0, :, 0:D_FF]
        up = _dot(x, gu_s[:, D_FF:2 * D_FF]) + bgu_ref[0, :, D_FF:2 * D_FF]
        gate = jnp.minimum(gate, SWIGLU_LIMIT)
        up = jnp.clip(up, -SWIGLU_LIMIT, SWIGLU_LIMIT)
        act = gate * _sigmoid(SWIGLU_ALPHA * gate) * (up + 1.0)
        o_ref[...] = (_dot(act.astype(BF16), dn_s[...]) + bd_ref[0]).astype(o_ref.dtype)

    @pl.when(i >= nu_ref[0])
    def _():
        o_ref[...] = jnp.zeros(o_ref.shape, o_ref.dtype)


def _moe_call(tile_expert, n_used, xs, w_gate_up, b_gate_up, w_down, b_down):
    p = xs.shape[0]
    n_tiles = p // MOE_TM
    grid_spec = pltpu.PrefetchScalarGridSpec(
        num_scalar_prefetch=2,
        grid=(n_tiles,),
        in_specs=[pl.BlockSpec((MOE_TM, D_MODEL), lambda i, te, nu: (i, 0)),
                  pl.BlockSpec((1, D_MODEL, 2 * D_FF), lambda i, te, nu: (te[i], 0, 0)),
                  pl.BlockSpec((1, 1, 2 * D_FF), lambda i, te, nu: (te[i], 0, 0)),
                  pl.BlockSpec((1, D_FF, D_MODEL), lambda i, te, nu: (te[i], 0, 0)),
                  pl.BlockSpec((1, 1, D_MODEL), lambda i, te, nu: (te[i], 0, 0))],
        out_specs=pl.BlockSpec((MOE_TM, D_MODEL), lambda i, te, nu: (i, 0)),
        scratch_shapes=[pltpu.VMEM((D_MODEL, 2 * D_FF), BF16), pltpu.VMEM((D_FF, D_MODEL), BF16)],
    )
    return pl.pallas_call(
        _moe_kernel,
        grid_spec=grid_spec,
        out_shape=jax.ShapeDtypeStruct((p, D_MODEL), BF16),
        compiler_params=pltpu.CompilerParams(dimension_semantics=("arbitrary",),
                                             vmem_limit_bytes=VMEM_LIMIT),
        name="moe_experts",
    )(tile_expert, n_used, xs, w_gate_up, b_gate_up.reshape(N_EXPERTS, 1, 2 * D_FF),
      w_down, b_down.reshape(N_EXPERTS, 1, D_MODEL))


def _combine_kernel(x1_ref, yg_ref, tw_ref, mod_ref, g_ref, o_ref):
    tw = tw_ref[0]
    acc = tw[:, 0:1] * yg_ref[0, 0].astype(F32)
    for k in range(1, TOP_K):
        acc = acc + tw[:, k:k + 1] * yg_ref[k, 0].astype(F32)
    g2 = mod_ref[0, :, 5 * D_MODEL:6 * D_MODEL]
    x2 = x1_ref[0] + g2 * acc
    ms = jnp.mean(x2 * x2, axis=-1, keepdims=True)
    o_ref[0] = x2 * lax.rsqrt(ms + RMS_EPS) * g_ref[...]


def _combine_call(x1, yg, top_w, mod, norm_g):
    b, l, _ = x1.shape
    tl = 256
    per_batch = mod.shape[0] > 1
    mod_map = (lambda i, j: (i, 0, 0)) if per_batch else (lambda i, j: (0, 0, 0))
    tok = lambda w: pl.BlockSpec((1, tl, w), lambda i, j: (i, j, 0))
    return pl.pallas_call(
        _combine_kernel,
        grid=(b, l // tl),
        in_specs=[tok(D_MODEL),
                  pl.BlockSpec((TOP_K, 1, tl, D_MODEL), lambda i, j: (0, i, j, 0)),
                  tok(TOP_K),
                  pl.BlockSpec((1, 1, 6 * D_MODEL), mod_map),
                  pl.BlockSpec((1, D_MODEL), lambda i, j: (0, 0))],
        out_specs=tok(D_MODEL),
        out_shape=jax.ShapeDtypeStruct((b, l, D_MODEL), F32),
        compiler_params=pltpu.CompilerParams(dimension_semantics=("arbitrary", "arbitrary"),
                                             vmem_limit_bytes=VMEM_LIMIT),
        name="moe_combine_final_norm",
    )(x1, yg, top_w, mod, norm_g.reshape(1, D_MODEL))


def _route(top_i, rank, counts):
    t = top_i.shape[0]
    p = t * TOP_K + N_EXPERTS * MOE_TM
    n_tiles = p // MOE_TM
    padded = ((counts + MOE_TM - 1) // MOE_TM) * MOE_TM
    ends = jnp.cumsum(padded)
    starts = ends - padded
    dest = starts.at[top_i].get(mode="promise_in_bounds") + rank
    tile_start = jnp.arange(n_tiles, dtype=jnp.int32) * MOE_TM
    tile_expert = jnp.minimum(jnp.sum((ends[None, :] <= tile_start[:, None]).astype(jnp.int32), axis=1),
                              N_EXPERTS - 1)
    n_used = (ends[-1] // MOE_TM).reshape(1)
    tok = jnp.broadcast_to(jnp.arange(t, dtype=jnp.int32)[:, None], (t, TOP_K))
    src_tok = jnp.zeros((p,), jnp.int32).at[dest.reshape(-1)].set(tok.reshape(-1), unique_indices=True)
    return dest, src_tok, tile_expert.astype(jnp.int32), n_used.astype(jnp.int32)


def _layer_front(x, mod, s_ssd_f, s_ssd_b, s_s5_f, s_s5_b, grid_cols, prm):
    z, xbc, u, dt = _inproj_call(x, mod, prm["norm1_g"], prm["w_in_r"])
    y_ssd, n_ssd_f, n_ssd_b = _ssd_call(z, xbc, dt, s_ssd_f, s_ssd_b, prm["ssd"])
    y_f, n_s5_f = _s5_call(u, _s5_state_in(s_s5_f), prm["s5_f"], grid_cols, False)
    y_s5, n_s5_b = _s5_call(u, _s5_state_in(s_s5_b), prm["s5_b"], grid_cols, True,
                            y_first=y_f, glu=prm["glu"])
    x1, h2, top_i, top_w, rank, cnt = _post_call(x, y_ssd, y_s5, mod, prm["w_out"], prm["norm2_g"],
                                                 prm["w_router"], prm["b_router"])
    states = (n_ssd_f, n_ssd_b, _s5_state_out(n_s5_f), _s5_state_out(n_s5_b))
    return x1, (h2, top_i, rank, cnt), top_w, states


def kernel(x_prompt, x_sample, state_ssd_fwd, state_ssd_bwd, state_s5_fwd, state_s5_bwd, c, c_ctx, w_ada, b_ada, norm1_g, w_in, ssd_conv_w, ssd_conv_b, ssd_dt_bias_fwd, ssd_dt_bias_bwd, ssd_a_log_fwd, ssd_a_log_bwd, ssd_d, ssd_norm_g, s5_a_re_fwd, s5_a_im_fwd, s5_log_dt_fwd, s5_b_re_fwd, s5_b_im_fwd, s5_c_re_fwd, s5_c_im_fwd, s5_a_re_bwd, s5_a_im_bwd, s5_log_dt_bwd, s5_b_re_bwd, s5_b_im_bwd, s5_c_re_bwd, s5_c_im_bwd, s5_d, w_glu, b_glu, w_out, norm2_g, w_router, b_router, w_gate_up, b_gate_up, w_down, b_down, norm_f_g):
    depth = w_ada.shape[0]
    assert depth == 1, "single trunk layer"
    nb_ctx, l_ctx, _ = x_prompt.shape
    nb_lat, l_lat, _ = x_sample.shape

    w_in0 = w_in[0]
    dt_lo = SSD_WIDTH + SSD_CONV_CH
    dt_cols = jnp.pad(w_in0[:, dt_lo:dt_lo + 2 * SSD_HEADS], ((0, 0), (0, DT_PAD - 2 * SSD_HEADS)))
    w_in_r = jnp.concatenate([w_in0[:, :dt_lo], w_in0[:, dt_lo + 2 * SSD_HEADS:], dt_cols], axis=1).astype(BF16)
    pad_dt = lambda f, b: jnp.pad(jnp.concatenate([f, b]).astype(F32), (0, DT_PAD - 2 * SSD_HEADS)).reshape(1, DT_PAD)
    ssd_prm = {
        "conv_w": jnp.pad(ssd_conv_w[0].astype(F32), ((0, 5), (0, 0))),
        "conv_b": ssd_conv_b[0].astype(F32).reshape(1, SSD_CONV_CH),
        "dt_bias": pad_dt(ssd_dt_bias_fwd[0], ssd_dt_bias_bwd[0]),
        "a": pad_dt(-jnp.exp(ssd_a_log_fwd[0].astype(F32)), -jnp.exp(ssd_a_log_bwd[0].astype(F32))),
        "d_skip": jnp.repeat(ssd_d[0].astype(F32), SSD_HEAD_DIM).reshape(1, SSD_WIDTH),
        "norm_g": ssd_norm_g[0].astype(F32).reshape(1, SSD_WIDTH),
        "expand": (jnp.arange(DT_PAD)[None, :, None]
                   == (jnp.arange(SSD_WIDTH) // SSD_HEAD_DIM)[None, None, :]
                   + SSD_HEADS * jnp.arange(2)[:, None, None]).astype(BF16),
    }
    prm = {
        "norm1_g": norm1_g[0], "w_in_r": w_in_r, "ssd": ssd_prm,
        "s5_f": _s5_params(s5_a_re_fwd[0], s5_a_im_fwd[0], s5_log_dt_fwd[0], s5_b_re_fwd[0],
                           s5_b_im_fwd[0], s5_c_re_fwd[0], s5_c_im_fwd[0]),
        "s5_b": _s5_params(s5_a_re_bwd[0], s5_a_im_bwd[0], s5_log_dt_bwd[0], s5_b_re_bwd[0],
                           s5_b_im_bwd[0], s5_c_re_bwd[0], s5_c_im_bwd[0]),
        "glu": {"d": s5_d[0].astype(F32).reshape(1, S5_WIDTH), "w": w_glu[0].astype(BF16),
                "b": b_glu[0].astype(F32).reshape(1, 2 * S5_WIDTH)},
        "w_out": w_out[0].astype(BF16), "norm2_g": norm2_g[0],
        "w_router": w_router[0].astype(F32), "b_router": b_router[0].astype(F32),
    }

    conds = jnp.concatenate([c_ctx[None, :], c], axis=0)
    conds = jnp.pad(conds, ((0, (-conds.shape[0]) % 8), (0, 0)))
    mod = _ada_call(conds, w_ada[0], b_ada[0])
    mod_ctx = mod[0:1].reshape(1, 1, 6 * D_MODEL)
    mod_lat = mod[1:1 + nb_lat].reshape(nb_lat, 1, 6 * D_MODEL)

    zero_ssd = jnp.zeros((nb_ctx, SSD_HEADS, SSD_HEAD_DIM, SSD_STATE), F32)
    zero_s5 = jnp.zeros((nb_ctx, S5_GROUPS, S5_STATE, 2), F32)
    x1_c, routed_c, tw_c, st_c = _layer_front(
        x_prompt, mod_ctx, zero_ssd, zero_ssd, zero_s5, zero_s5, False, prm)
    x1_l, routed_l, tw_l, _ = _layer_front(
        x_sample, mod_lat, state_ssd_fwd[:, 0], state_ssd_bwd[:, 0], state_s5_fwd[:, 0],
        state_s5_bwd[:, 0], True, prm)

    h2_c, ti_c, rk_c, cnt_c = routed_c
    h2_l, ti_l, rk_l, cnt_l = routed_l
    t_ctx = nb_ctx * l_ctx
    n_tok = t_ctx + nb_lat * l_lat
    cnt_c = cnt_c.reshape(N_EXPERTS).astype(jnp.int32)
    cnt_l = cnt_l.reshape(N_EXPERTS).astype(jnp.int32)
    ti_l2 = ti_l.reshape(-1, TOP_K)
    top_i = jnp.concatenate([ti_c.reshape(-1, TOP_K), ti_l2], axis=0)
    rank = jnp.concatenate([rk_c.reshape(-1, TOP_K),
                            rk_l.reshape(-1, TOP_K) + cnt_c.at[ti_l2].get(mode="promise_in_bounds")], axis=0)
    dest, src_tok, tile_expert, n_used = _route(top_i, rank, cnt_c + cnt_l)
    h2 = jnp.concatenate([h2_c.reshape(-1, D_MODEL), h2_l.reshape(-1, D_MODEL),
                          jnp.zeros((max(GATHER_SRC_ROWS - n_tok, 0), D_MODEL), BF16)], axis=0)
    xs = h2.at[src_tok].get(mode="promise_in_bounds")
    ys = _moe_call(tile_expert, n_used, xs, w_gate_up[0], b_gate_up[0], w_down[0], b_down[0])

    def picked_rows(d, nb, l):
        rows = ys.at[d.T.reshape(-1)].get(mode="promise_in_bounds")
        return rows.reshape(TOP_K, nb, l, D_MODEL)

    y_prompt = _combine_call(x1_c, picked_rows(dest[:t_ctx], nb_ctx, l_ctx), tw_c, mod_ctx, norm_f_g)
    y_sample = _combine_call(x1_l, picked_rows(dest[t_ctx:], nb_lat, l_lat), tw_l, mod_lat, norm_f_g)

    ssd_f, ssd_b, s5_f, s5_b = st_c
    return (y_prompt, y_sample, ssd_f[:, None], ssd_b[:, None], s5_f[:, None], s5_b[:, None])
```

```python
import functools
import math

import jax
import jax.numpy as jnp
from jax import lax
from jax.experimental import pallas as pl
from jax.experimental.pallas import tpu as pltpu

F32 = jnp.float32
BF16 = jnp.bfloat16

D_MODEL = 1024
GRID_W = 64
SSD_WIDTH = 512
SSD_HEAD_DIM = 64
SSD_HEADS = 8
SSD_GROUPS = 2
SSD_HEADS_PER_GROUP = SSD_HEADS // SSD_GROUPS
SSD_STATE = 128
SSD_CHUNK = 128
SSD_CONV_CH = SSD_WIDTH + 2 * SSD_GROUPS * SSD_STATE
S5_WIDTH = 512
S5_CH = 16
S5_GROUPS = 32
S5_STATE = 64
N_EXPERTS = 32
TOP_K = 4
D_FF = 1024
SWIGLU_LIMIT = 7.0
SWIGLU_ALPHA = 1.702
RMS_EPS = 1e-5

LANES = 128
DT_PAD = 128
S5_BATCH = 8
S5_COLS = 2 * S5_GROUPS * S5_STATE
S5_BLK = 512
S5_SUB_STEPS = 32
S5_BLOCK_STEPS = 256
MOE_TM = 256
GATHER_SRC_ROWS = 32768
VMEM_LIMIT = 56 * 1024 * 1024


def _sigmoid(x):
    return 1.0 / (1.0 + jnp.exp(-x))


def _split3(x):
    hi = x.astype(BF16)
    r1 = x - hi.astype(F32)
    mid = r1.astype(BF16)
    lo = (r1 - mid.astype(F32)).astype(BF16)
    return hi, mid, lo


def _split2(x):
    hi = x.astype(BF16)
    return hi, (x - hi.astype(F32)).astype(BF16)


def _dot(a, b):
    return jnp.dot(a, b, preferred_element_type=F32)


def _ada_kernel(c_ref, w_ref, b_ref, o_ref):
    c = c_ref[...]
    s = c * _sigmoid(c)
    o_ref[...] = _dot(s.astype(BF16), w_ref[...].astype(BF16)) + b_ref[...]


def _ada_call(conds, w_ada, b_ada):
    n = w_ada.shape[1]
    tn = 1536
    rows = conds.shape[0]
    return pl.pallas_call(
        _ada_kernel,
        grid=(n // tn,),
        in_specs=[pl.BlockSpec((rows, D_MODEL), lambda j: (0, 0)),
                  pl.BlockSpec((D_MODEL, tn), lambda j: (0, j)),
                  pl.BlockSpec((1, tn), lambda j: (0, j))],
        out_specs=pl.BlockSpec((rows, tn), lambda j: (0, j)),
        out_shape=jax.ShapeDtypeStruct((rows, n), F32),
        compiler_params=pltpu.CompilerParams(dimension_semantics=("arbitrary",),
                                             vmem_limit_bytes=VMEM_LIMIT),
        name="ada_mod",
    )(conds, w_ada, b_ada.reshape(1, n))


def _inproj_kernel(x_ref, mod_ref, g_ref, w_ref, z_ref, xbc_ref, u_ref, dt_ref):
    x = x_ref[0]
    ms = jnp.mean(x * x, axis=-1, keepdims=True)
    y = x * lax.rsqrt(ms + RMS_EPS) * g_ref[...]
    sh = mod_ref[0, :, 0:D_MODEL]
    sc = mod_ref[0, :, D_MODEL:2 * D_MODEL]
    h = (y * (1.0 + sc) + sh).astype(BF16)
    z_ref[0] = _dot(h, w_ref[:, 0:512]).astype(BF16)
    xbc_ref[0] = _dot(h, w_ref[:, 512:1536]).astype(BF16)
    u_ref[0] = _dot(h, w_ref[:, 1536:2048])
    dt_ref[0] = _dot(h, w_ref[:, 2048:2048 + DT_PAD])


def _inproj_call(x, mod, norm_g, w_in_r):
    b, l, _ = x.shape
    tl = min(l, 512)
    per_batch = mod.shape[0] > 1
    mod_map = (lambda i, j: (i, 0, 0)) if per_batch else (lambda i, j: (0, 0, 0))
    ncol = w_in_r.shape[1]
    tok = lambda w: pl.BlockSpec((1, tl, w), lambda i, j: (i, j, 0))
    return pl.pallas_call(
        _inproj_kernel,
        grid=(b, l // tl),
        in_specs=[tok(D_MODEL),
                  pl.BlockSpec((1, 1, 6 * D_MODEL), mod_map),
                  pl.BlockSpec((1, D_MODEL), lambda i, j: (0, 0)),
                  pl.BlockSpec((D_MODEL, ncol), lambda i, j: (0, 0))],
        out_specs=[tok(SSD_WIDTH), tok(SSD_CONV_CH), tok(S5_WIDTH), tok(DT_PAD)],
        out_shape=[jax.ShapeDtypeStruct((b, l, SSD_WIDTH), BF16),
                   jax.ShapeDtypeStruct((b, l, SSD_CONV_CH), BF16),
                   jax.ShapeDtypeStruct((b, l, S5_WIDTH), F32),
                   jax.ShapeDtypeStruct((b, l, DT_PAD), F32)],
        compiler_params=pltpu.CompilerParams(dimension_semantics=("arbitrary", "arbitrary"),
                                             vmem_limit_bytes=VMEM_LIMIT),
        name="norm1_inproj",
    )(x, mod, norm_g.reshape(1, D_MODEL), w_in_r)


def _softplus(x):
    return jnp.maximum(x, 0.0) + jnp.log1p(jnp.exp(-jnp.abs(x)))


def _ssd_kernel(z_ref, xbc_ref, dt_ref, s0f_ref, s0b_ref, cw_ref, cb_ref, dtb_ref, a_ref,
                dskip_ref, ng_ref, ex_ref, y_ref, sf_ref, sb_ref,
                xc_s, dts_s, yf_s, yb_s, stf_s, stb_s, *, seq):
    q = SSD_CHUNK
    nc = seq // q
    gw = SSD_HEADS_PER_GROUP * SSD_HEAD_DIM
    row_i = lax.broadcasted_iota(jnp.int32, (q, 1), 0)

    def conv_body(c, carry):
        r0 = pl.multiple_of(c * q, q)
        cur = xbc_ref[0, pl.ds(r0, q), :].astype(F32)
        p0 = pl.multiple_of(jnp.maximum(r0 - 16, 0), 16)
        n0 = pl.multiple_of(jnp.minimum(r0 + q, seq - 16), 16)
        prev_row = xbc_ref[0, pl.ds(p0, 16), :].astype(F32)[15:16, :]
        next_row = xbc_ref[0, pl.ds(n0, 16), :].astype(F32)[0:1, :]
        prev_row = jnp.where(c > 0, prev_row, 0.0)
        next_row = jnp.where(c < nc - 1, next_row, 0.0)
        down = jnp.where(row_i == 0, prev_row, pltpu.roll(cur, 1, axis=0))
        up = jnp.where(row_i == q - 1, next_row, pltpu.roll(cur, q - 1, axis=0))
        v = cw_ref[0:1, :] * down + cw_ref[1:2, :] * cur + cw_ref[2:3, :] * up + cb_ref[...]
        xc_s[pl.ds(r0, q), :] = v * _sigmoid(v)
        dts_s[pl.ds(r0, q), :] = _softplus(dt_ref[0, pl.ds(r0, q), :] + dtb_ref[...])
        return carry

    lax.fori_loop(0, nc, conv_body, 0)

    li = lax.broadcasted_iota(jnp.int32, (q, q), 0)
    si = lax.broadcasted_iota(jnp.int32, (q, q), 1)
    head_of_lane = lax.broadcasted_iota(jnp.int32, (q, gw), 1) // SSD_HEAD_DIM

    def expand(v, e_mat):
        hi, lo = _split2(v)
        return _dot(hi, e_mat) + _dot(lo, e_mat)

    def chunk(c, direction):
        r0 = pl.multiple_of(c * q, q)
        st_ref = stf_s if direction == 0 else stb_s
        y_dst = yf_s if direction == 0 else yb_s
        e_mat = ex_ref[direction]
        causal = (li >= si) if direction == 0 else (li <= si)
        tri = jnp.where(causal, 1.0, 0.0).astype(BF16)
        dts = dts_s[pl.ds(r0, q), :]
        dta = dts * a_ref[...]
        d_hi, d_mid, d_lo = _split3(dta)
        cs = _dot(tri, d_hi) + _dot(tri, d_mid) + _dot(tri, d_lo)
        tot = cs[q - 1:q, :] if direction == 0 else cs[0:1, :]
        cs_t = cs.T
        ecs = jnp.exp(cs)
        tail = jnp.exp(tot - cs)
        dt_x = expand(dts, e_mat)
        dtw_x = expand(dts * tail, e_mat)
        etot_x = expand(jnp.broadcast_to(jnp.exp(tot), (8, DT_PAD)), e_mat)[0:1, :]
        for g in range(SSD_GROUPS):
            b_off = SSD_WIDTH + g * SSD_STATE
            c_off = SSD_WIDTH + SSD_GROUPS * SSD_STATE + g * SSD_STATE
            x_g = xc_s[pl.ds(r0, q), g * gw:(g + 1) * gw]
            bm = xc_s[pl.ds(r0, q), b_off:b_off + SSD_STATE]
            cm = xc_s[pl.ds(r0, q), c_off:c_off + SSD_STATE]
            cb = lax.dot_general(cm.astype(BF16), bm.astype(BF16), (((1,), (1,)), ((), ())),
                                 preferred_element_type=F32)
            st_g = st_ref[g]
            xdt = x_g * dt_x[:, g * gw:(g + 1) * gw]
            lhs, rhs = [], []
            for hh in range(SSD_HEADS_PER_GROUP):
                col = g * SSD_HEADS_PER_GROUP + hh + SSD_HEADS * direction
                seg = jnp.broadcast_to(cs[:, col:col + 1], (q, q)) - cs_t[col:col + 1, :]
                dec = jnp.exp(jnp.where(causal, seg, -1e30))
                lhs.append((cb * dec).astype(BF16))
                lhs.append((cm * jnp.broadcast_to(ecs[:, col:col + 1], (q, q))).astype(BF16))
                own = head_of_lane == hh
                rhs.append(jnp.where(own, xdt, 0.0).astype(BF16))
                rhs.append(jnp.where(own, st_g, 0.0).astype(BF16))
            y_dst[pl.ds(r0, q), g * gw:(g + 1) * gw] = _dot(jnp.concatenate(lhs, axis=1),
                                                             jnp.concatenate(rhs, axis=0))
            xw = (x_g * dtw_x[:, g * gw:(g + 1) * gw]).astype(BF16)
            st_ref[g] = etot_x[:, g * gw:(g + 1) * gw] * st_g + _dot(bm.T.astype(BF16), xw)

    stf_s[...] = s0f_ref[0]
    stb_s[...] = s0b_ref[0]

    def both(i, carry):
        chunk(i, 0)
        chunk(nc - 1 - i, 1)
        return carry

    lax.fori_loop(0, nc, both, 0)
    sf_ref[0] = stf_s[...]
    sb_ref[0] = stb_s[...]

    def finish(c, carry):
        r0 = pl.multiple_of(c * q, q)
        xs = xc_s[pl.ds(r0, q), 0:SSD_WIDTH]
        zz = z_ref[0, pl.ds(r0, q), :].astype(F32)
        y = (yf_s[pl.ds(r0, q), :] + yb_s[pl.ds(r0, q), :] + dskip_ref[...] * xs) * (zz * _sigmoid(zz))
        ms = jnp.mean(y * y, axis=-1, keepdims=True)
        y_ref[0, pl.ds(r0, q), :] = (y * lax.rsqrt(ms + RMS_EPS) * ng_ref[...]).astype(y_ref.dtype)
        return carry

    lax.fori_loop(0, nc, finish, 0)


def _ssd_state_in(s):
    b = s.shape[0]
    s = s.astype(F32).reshape(b, SSD_GROUPS, SSD_HEADS_PER_GROUP, SSD_HEAD_DIM, SSD_STATE)
    return s.transpose(0, 1, 4, 2, 3).reshape(b, SSD_GROUPS, SSD_STATE, SSD_HEADS_PER_GROUP * SSD_HEAD_DIM)


def _ssd_state_out(s):
    b = s.shape[0]
    s = s.reshape(b, SSD_GROUPS, SSD_STATE, SSD_HEADS_PER_GROUP, SSD_HEAD_DIM)
    return s.transpose(0, 1, 3, 4, 2).reshape(b, SSD_HEADS, SSD_HEAD_DIM, SSD_STATE)


def _ssd_call(z, xbc, dt, s0f, s0b, prm):
    b, l, _ = z.shape
    st_shape = (SSD_GROUPS, SSD_STATE, SSD_HEADS_PER_GROUP * SSD_HEAD_DIM)
    tok = lambda w: pl.BlockSpec((1, l, w), lambda i: (i, 0, 0))
    st_spec = pl.BlockSpec((1,) + st_shape, lambda i: (i, 0, 0, 0))
    par = lambda r, w: pl.BlockSpec((r, w), lambda i: (0, 0))
    y, sf, sb = pl.pallas_call(
        functools.partial(_ssd_kernel, seq=l),
        grid=(b,),
        in_specs=[tok(SSD_WIDTH), tok(SSD_CONV_CH), tok(DT_PAD), st_spec, st_spec,
                  par(8, SSD_CONV_CH), par(1, SSD_CONV_CH), par(1, DT_PAD), par(1, DT_PAD),
                  par(1, SSD_WIDTH), par(1, SSD_WIDTH),
                  pl.BlockSpec((2, DT_PAD, SSD_WIDTH), lambda i: (0, 0, 0))],
        out_specs=[tok(SSD_WIDTH), st_spec, st_spec],
        out_shape=[jax.ShapeDtypeStruct((b, l, SSD_WIDTH), BF16),
                   jax.ShapeDtypeStruct((b,) + st_shape, F32),
                   jax.ShapeDtypeStruct((b,) + st_shape, F32)],
        scratch_shapes=[pltpu.VMEM((l, SSD_CONV_CH), F32),
                        pltpu.VMEM((l, DT_PAD), F32),
                        pltpu.VMEM((l, SSD_WIDTH), F32),
                        pltpu.VMEM((l, SSD_WIDTH), F32),
                        pltpu.VMEM(st_shape, F32),
                        pltpu.VMEM(st_shape, F32)],
        compiler_params=pltpu.CompilerParams(dimension_semantics=("arbitrary",),
                                             vmem_limit_bytes=VMEM_LIMIT),
        name="ssd_mixer",
    )(z, xbc, dt, _ssd_state_in(s0f), _ssd_state_in(s0b),
      prm["conv_w"], prm["conv_b"], prm["dt_bias"], prm["a"], prm["d_skip"], prm["norm_g"], prm["expand"])
    return y, _ssd_state_out(sf), _ssd_state_out(sb)


def _gelu_tanh(x):
    return 0.5 * x * (1.0 + jnp.tanh(math.sqrt(2.0 / math.pi) * (x + 0.044715 * (x * x * x))))


def _s5_kernel(*refs, grid_cols, reverse, final):
    if final:
        (u_ref, s0_ref, lr_ref, li_ref, wb_ref, wc_ref, yf_ref, d_ref, wg_ref, bg_ref,
         o_ref, sfin_ref, ur_s, bu_s, yr_s, st_s) = refs
    else:
        (u_ref, s0_ref, lr_ref, li_ref, wb_ref, wc_ref,
         o_ref, sfin_ref, ur_s, bu_s, yr_s, st_s) = refs
    nb = S5_BATCH
    steps = S5_BLOCK_STEPS
    rows = steps * nb
    sub_rows = S5_SUB_STEPS * nb
    n_sub = steps // S5_SUB_STEPS
    n_rows = u_ref.shape[1] if grid_cols else None
    j = pl.program_id(1)

    @pl.when(j == 0)
    def _():
        st_s[...] = s0_ref[0]

    nlt = S5_WIDTH // LANES

    def put_strided(dst, start, count, stride, val):
        for k in range(nlt):
            dst[k, pl.ds(start, count, stride=stride), :] = val[:, k * LANES:(k + 1) * LANES]

    def get_strided(src, start, count, stride):
        return jnp.concatenate([src[k, pl.ds(start, count, stride=stride), :] for k in range(nlt)], axis=1)

    def get_rows(src, r0, n):
        return jnp.concatenate([src[k, pl.ds(r0, n), :] for k in range(nlt)], axis=1)

    if grid_cols:
        n_cols = u_ref.shape[2]
        for b in range(nb):
            for r in range(n_rows):
                put_strided(ur_s, r * nb + b, n_cols, n_rows * nb, u_ref[b, r])
    else:
        for b in range(nb):
            put_strided(ur_s, b, steps, nb, u_ref[b])

    def sub_chunk(i, carry):
        sidx = (n_sub - 1 - i) if reverse else i
        r0 = pl.multiple_of(sidx * sub_rows, sub_rows)
        for hf in range(2):
            uh = jnp.concatenate([ur_s[2 * hf, pl.ds(r0, sub_rows), :],
                                  ur_s[2 * hf + 1, pl.ds(r0, sub_rows), :]], axis=1).astype(BF16)
            bu_s[:, hf * 2048:(hf + 1) * 2048] = _dot(uh, wb_ref[hf])
        for blk in range(S5_COLS // S5_BLK):
            c0 = blk * S5_BLK
            lam_r = jnp.broadcast_to(lr_ref[:, c0:c0 + S5_BLK], (nb, S5_BLK))
            lam_i = jnp.broadcast_to(li_ref[:, c0:c0 + S5_BLK], (nb, S5_BLK))

            def step(t, s):
                tt = (S5_SUB_STEPS - 1 - t) if reverse else t
                rr = pl.multiple_of(tt * nb, nb)
                sw = jnp.concatenate([s[:, S5_BLK // 2:], s[:, :S5_BLK // 2]], axis=1)
                s_new = lam_r * s + lam_i * sw + bu_s[pl.ds(rr, nb), c0:c0 + S5_BLK]
                bu_s[pl.ds(rr, nb), c0:c0 + S5_BLK] = s_new
                return s_new

            st_s[:, c0:c0 + S5_BLK] = lax.fori_loop(0, S5_SUB_STEPS, step, st_s[:, c0:c0 + S5_BLK],
                                                    unroll=8)
            half = (blk % 2) * 64
            yr_s[blk // 2, pl.ds(r0, sub_rows), half:half + 64] = _dot(
                bu_s[:, c0:c0 + S5_BLK].astype(BF16), wc_ref[blk])
        return carry

    lax.fori_loop(0, n_sub, sub_chunk, 0)

    @pl.when(j == pl.num_programs(1) - 1)
    def _():
        sfin_ref[0] = st_s[...]

    if not final:
        for k in range(nlt):
            o_ref[0, :, k * LANES:(k + 1) * LANES] = yr_s[k]
        return

    def glu_chunk(i, carry):
        r0 = pl.multiple_of(i * sub_rows, sub_rows)
        y = (get_rows(yr_s, r0, sub_rows) + yf_ref[0, pl.ds(r0, sub_rows), :]
             + d_ref[...] * get_rows(ur_s, r0, sub_rows))
        y = _gelu_tanh(y).astype(BF16)
        val = _dot(y, wg_ref[:, 0:S5_WIDTH]) + bg_ref[:, 0:S5_WIDTH]
        gate = _dot(y, wg_ref[:, S5_WIDTH:2 * S5_WIDTH]) + bg_ref[:, S5_WIDTH:2 * S5_WIDTH]
        res = val * _sigmoid(gate)
        for k in range(nlt):
            yr_s[k, pl.ds(r0, sub_rows), :] = res[:, k * LANES:(k + 1) * LANES]
        return carry

    lax.fori_loop(0, n_sub, glu_chunk, 0)

    if grid_cols:
        for b in range(nb):
            for r in range(n_rows):
                o_ref[b, r] = get_strided(yr_s, r * nb + b, n_cols, n_rows * nb)
    else:
        for b in range(nb):
            o_ref[b] = get_strided(yr_s, b, steps, nb)


def _s5_call(u, s0, prm, grid_cols, reverse, y_first=None, glu=None):
    b, l, _ = u.shape
    nbg = b // S5_BATCH
    nblk = l // S5_BLOCK_STEPS
    rows = S5_BLOCK_STEPS * S5_BATCH
    final = y_first is not None
    blk_of = (lambda j: nblk - 1 - j) if reverse else (lambda j: j)
    if grid_cols:
        n_rows = l // GRID_W
        u_in = u.reshape(b, n_rows, GRID_W, S5_WIDTH)
        cols_per_blk = S5_BLOCK_STEPS // n_rows
        tok_spec = pl.BlockSpec((S5_BATCH, n_rows, cols_per_blk, S5_WIDTH),
                                lambda i, j: (i, 0, blk_of(j), 0))
        out_nat = jax.ShapeDtypeStruct((b, n_rows, GRID_W, S5_WIDTH), F32)
    else:
        u_in = u
        tok_spec = pl.BlockSpec((S5_BATCH, S5_BLOCK_STEPS, S5_WIDTH), lambda i, j: (i, blk_of(j), 0))
        out_nat = jax.ShapeDtypeStruct((b, l, S5_WIDTH), F32)
    scan_spec = pl.BlockSpec((1, rows, S5_WIDTH), lambda i, j: (i, blk_of(j), 0))
    st_spec = pl.BlockSpec((1, S5_BATCH, S5_COLS), lambda i, j: (i, 0, 0))
    full = lambda shape: pl.BlockSpec(shape, lambda i, j: (0,) * len(shape))
    in_specs = [tok_spec, st_spec, full((1, S5_COLS)), full((1, S5_COLS)),
                full((2, 256, 2048)), full((S5_COLS // S5_BLK, S5_BLK, 64))]
    args = [u_in, s0, prm["lam_r"], prm["lam_i"], prm["wb"], prm["wc"]]
    if final:
        in_specs += [scan_spec, full((1, S5_WIDTH)), full((S5_WIDTH, 2 * S5_WIDTH)), full((1, 2 * S5_WIDTH))]
        args += [y_first, glu["d"], glu["w"], glu["b"]]
        out_specs = [tok_spec, st_spec]
        out_shape = [out_nat, jax.ShapeDtypeStruct((nbg, S5_BATCH, S5_COLS), F32)]
    else:
        out_specs = [scan_spec, st_spec]
        out_shape = [jax.ShapeDtypeStruct((nbg, l * S5_BATCH, S5_WIDTH), F32),
                     jax.ShapeDtypeStruct((nbg, S5_BATCH, S5_COLS), F32)]
    y, sfin = pl.pallas_call(
        functools.partial(_s5_kernel, grid_cols=grid_cols, reverse=reverse, final=final),
        grid=(nbg, nblk),
        in_specs=in_specs,
        out_specs=out_specs,
        out_shape=out_shape,
        scratch_shapes=[pltpu.VMEM((S5_WIDTH // LANES, rows, LANES), F32),
                        pltpu.VMEM((S5_SUB_STEPS * S5_BATCH, S5_COLS), F32),
                        pltpu.VMEM((S5_WIDTH // LANES, rows, LANES), F32),
                        pltpu.VMEM((S5_BATCH, S5_COLS), F32)],
        compiler_params=pltpu.CompilerParams(dimension_semantics=("arbitrary", "arbitrary"),
                                             vmem_limit_bytes=VMEM_LIMIT),
        name="s5_final" if final else "s5_first",
    )(*args)
    if final:
        y = y.reshape(b, l, S5_WIDTH)
    return y, sfin


def _s5_cols(t):
    lead = t.shape[:-2]
    return t.reshape(lead + (S5_COLS // S5_BLK, S5_BLK // 2))


def _s5_state_in(s0):
    b = s0.shape[0]
    re, im = _s5_cols(s0[..., 0].astype(F32)), _s5_cols(s0[..., 1].astype(F32))
    return jnp.stack([re, im], axis=-2).reshape(b // S5_BATCH, S5_BATCH, S5_COLS)


def _s5_state_out(s):
    b = s.shape[0] * S5_BATCH
    s = s.reshape(b, S5_COLS // S5_BLK, 2, S5_BLK // 2)
    re = s[:, :, 0].reshape(b, S5_GROUPS, S5_STATE)
    im = s[:, :, 1].reshape(b, S5_GROUPS, S5_STATE)
    return jnp.stack([re, im], axis=-1)


def _s5_params(a_re, a_im, log_dt, b_re, b_im, c_re, c_im):
    a_re, a_im = a_re.astype(F32), a_im.astype(F32)
    delta = jnp.exp(log_dt.astype(F32))[:, None]
    mag = jnp.exp(a_re * delta)
    lam_re, lam_im = mag * jnp.cos(a_im * delta), mag * jnp.sin(a_im * delta)
    den = a_re * a_re + a_im * a_im
    nr = lam_re - 1.0
    f_re = (nr * a_re + lam_im * a_im) / den
    f_im = (lam_im * a_re - nr * a_im) / den
    b_re, b_im = b_re.astype(F32), b_im.astype(F32)
    bb_re = f_re[..., None] * b_re - f_im[..., None] * b_im
    bb_im = f_re[..., None] * b_im + f_im[..., None] * b_re
    nblk = S5_COLS // S5_BLK
    lr = _s5_cols(lam_re)
    li = _s5_cols(lam_im)
    lam_r = jnp.stack([lr, lr], axis=-2).reshape(1, S5_COLS)
    lam_i = jnp.stack([-li, li], axis=-2).reshape(1, S5_COLS)
    gq = S5_BLK // 2 // S5_STATE
    eye = jnp.eye(gq, dtype=F32)

    def in_block(bb):
        t = bb.reshape(nblk, gq, S5_STATE, S5_CH)
        return jnp.einsum("gh,kgpc->kgchp", eye, t).reshape(nblk, gq * S5_CH, gq * S5_STATE)

    blk_in = jnp.concatenate([in_block(bb_re), in_block(bb_im)], axis=-1)
    half = nblk // 2
    eye_h = jnp.eye(half, dtype=F32)
    wb = jnp.einsum("jk,hkcn->hjckn", eye_h, blk_in.reshape(2, half, gq * S5_CH, S5_BLK))
    wb = wb.reshape(2, half * gq * S5_CH, half * S5_BLK).astype(BF16)

    def out_block(cc):
        t = cc.astype(F32).reshape(nblk, gq, S5_CH, S5_STATE)
        return jnp.einsum("gh,kgcp->kgphc", eye, t).reshape(nblk, gq * S5_STATE, gq * S5_CH)

    wc = jnp.concatenate([out_block(c_re), -out_block(c_im)], axis=1).astype(BF16)
    return {"lam_r": lam_r, "lam_i": lam_i, "wb": wb, "wc": wc}


def _post_kernel(x_ref, ys_ref, y5_ref, mod_ref, wo_ref, g_ref, wr_ref, br_ref,
                 x1_ref, h_ref, ti_ref, tw_ref, rk_ref, cnt_ref, cnt_s):
    first = jnp.logical_and(pl.program_id(0) == 0, pl.program_id(1) == 0)

    @pl.when(first)
    def _():
        cnt_s[...] = jnp.zeros(cnt_s.shape, cnt_s.dtype)

    tl = x_ref.shape[1]
    mixed = (_dot(ys_ref[0], wo_ref[0:SSD_WIDTH, :])
             + _dot(y5_ref[0].astype(BF16), wo_ref[SSD_WIDTH:SSD_WIDTH + S5_WIDTH, :]))
    g1 = mod_ref[0, :, 2 * D_MODEL:3 * D_MODEL]
    sh2 = mod_ref[0, :, 3 * D_MODEL:4 * D_MODEL]
    sc2 = mod_ref[0, :, 4 * D_MODEL:5 * D_MODEL]
    x1 = x_ref[0] + g1 * mixed
    x1_ref[0] = x1
    ms = jnp.mean(x1 * x1, axis=-1, keepdims=True)
    h = x1 * lax.rsqrt(ms + RMS_EPS) * g_ref[...] * (1.0 + sc2) + sh2
    h_ref[0] = h.astype(BF16)

    h_hi, h_mid, _ = _split3(h)
    wr = wr_ref[...]
    w_hi, w_mid, _ = _split3(wr)
    logits = _dot(h_hi, w_hi) + _dot(h_hi, w_mid) + _dot(h_mid, w_hi) + br_ref[...]

    lane = lax.broadcasted_iota(jnp.int32, (tl, N_EXPERTS), 1).astype(F32)
    slot = lax.broadcasted_iota(jnp.int32, (tl, TOP_K), 1)
    work = logits
    onehot = jnp.zeros((tl, N_EXPERTS), F32)
    sels, vals = [], []
    top_i = jnp.zeros((tl, TOP_K), jnp.int32)
    for k in range(TOP_K):
        m = jnp.max(work, axis=-1, keepdims=True)
        idx = jnp.min(jnp.where(work == m, lane, float(N_EXPERTS)), axis=-1, keepdims=True)
        sel = lane == idx
        sels.append(sel)
        vals.append(m)
        top_i = jnp.where(slot == k, idx.astype(jnp.int32), top_i)
        onehot = jnp.where(sel, 1.0, onehot)
        work = jnp.where(sel, -jnp.inf, work)
    es = [jnp.exp(v - vals[0]) for v in vals]
    den = es[0] + es[1] + es[2] + es[3]
    top_w = jnp.zeros((tl, TOP_K), F32)
    for k in range(TOP_K):
        top_w = jnp.where(slot == k, es[k] / den, top_w)

    ri = lax.broadcasted_iota(jnp.int32, (tl, tl), 0)
    ci = lax.broadcasted_iota(jnp.int32, (tl, tl), 1)
    strict = jnp.where(ri > ci, 1.0, 0.0).astype(BF16)
    before = _dot(strict, onehot.astype(BF16)) + cnt_s[...]
    rank = jnp.zeros((tl, TOP_K), jnp.int32)
    for k in range(TOP_K):
        rk = jnp.sum(jnp.where(sels[k], before, 0.0), axis=-1, keepdims=True)
        rank = jnp.where(slot == k, rk.astype(jnp.int32), rank)
    cnt_s[...] = cnt_s[...] + jnp.sum(onehot, axis=0, keepdims=True)
    ti_ref[0] = top_i
    tw_ref[0] = top_w
    rk_ref[0] = rank
    cnt_ref[...] = cnt_s[...]


def _post_call(x, y_ssd, y_s5, mod, w_out, norm_g, w_router, b_router):
    b, l, _ = x.shape
    tl = 256
    per_batch = mod.shape[0] > 1
    mod_map = (lambda i, j: (i, 0, 0)) if per_batch else (lambda i, j: (0, 0, 0))
    tok = lambda w: pl.BlockSpec((1, tl, w), lambda i, j: (i, j, 0))
    full = lambda r, w: pl.BlockSpec((r, w), lambda i, j: (0, 0))
    return pl.pallas_call(
        _post_kernel,
        grid=(b, l // tl),
        in_specs=[tok(D_MODEL), tok(SSD_WIDTH), tok(S5_WIDTH),
                  pl.BlockSpec((1, 1, 6 * D_MODEL), mod_map),
                  full(D_MODEL, D_MODEL), full(1, D_MODEL), full(D_MODEL, N_EXPERTS),
                  full(1, N_EXPERTS)],
        out_specs=[tok(D_MODEL), tok(D_MODEL), tok(TOP_K), tok(TOP_K), tok(TOP_K), full(1, N_EXPERTS)],
        out_shape=[jax.ShapeDtypeStruct((b, l, D_MODEL), F32),
                   jax.ShapeDtypeStruct((b, l, D_MODEL), BF16),
                   jax.ShapeDtypeStruct((b, l, TOP_K), jnp.int32),
                   jax.ShapeDtypeStruct((b, l, TOP_K), F32),
                   jax.ShapeDtypeStruct((b, l, TOP_K), jnp.int32),
                   jax.ShapeDtypeStruct((1, N_EXPERTS), F32)],
        scratch_shapes=[pltpu.VMEM((1, N_EXPERTS), F32)],
        compiler_params=pltpu.CompilerParams(dimension_semantics=("arbitrary", "arbitrary"),
                                             vmem_limit_bytes=VMEM_LIMIT),
        name="outproj_norm2_router",
    )(x, y_ssd, y_s5, mod, w_out, norm_g.reshape(1, D_MODEL), w_router,
      b_router.reshape(1, N_EXPERTS))


def _moe_kernel(te_ref, nu_ref, x_ref, wgu_ref, bgu_ref, wd_ref, bd_ref, o_ref, gu_s, dn_s):
    i = pl.program_id(0)
    prev = te_ref[jnp.maximum(i - 1, 0)]
    new_expert = jnp.logical_or(i == 0, te_ref[i] != prev)

    @pl.when(new_expert)
    def _():
        for r in range(D_MODEL // 128):
            gu_s[r * 128:(r + 1) * 128, :] = wgu_ref[0, r * 128:(r + 1) * 128, :].astype(BF16)
            dn_s[r * 128:(r + 1) * 128, :] = wd_ref[0, r * 128:(r + 1) * 128, :].astype(BF16)

    @pl.when(i < nu_ref[0])
    def _():
        x = x_ref[...]
        gate = _dot(x, gu_s[:, 0:D_FF]) + bgu_ref[0, :, 0:D_FF]
        up = _dot(x, gu_s[:, D_FF:2 * D_FF]) + bgu_ref[0, :, D_FF:2 * D_FF]
        gate = jnp.minimum(gate, SWIGLU_LIMIT)
        up = jnp.clip(up, -SWIGLU_LIMIT, SWIGLU_LIMIT)
        act = gate * _sigmoid(SWIGLU_ALPHA * gate) * (up + 1.0)
        o_ref[...] = (_dot(act.astype(BF16), dn_s[...]) + bd_ref[0]).astype(o_ref.dtype)

    @pl.when(i >= nu_ref[0])
    def _():
        o_ref[...] = jnp.zeros(o_ref.shape, o_ref.dtype)


def _moe_call(tile_expert, n_used, xs, w_gate_up, b_gate_up, w_down, b_down):
    p = xs.shape[0]
    n_tiles = p // MOE_TM
    grid_spec = pltpu.PrefetchScalarGridSpec(
        num_scalar_prefetch=2,
        grid=(n_tiles,),
        in_specs=[pl.BlockSpec((MOE_TM, D_MODEL), lambda i, te, nu: (i, 0)),
                  pl.BlockSpec((1, D_MODEL, 2 * D_FF), lambda i, te, nu: (te[i], 0, 0)),
                  pl.BlockSpec((1, 1, 2 * D_FF), lambda i, te, nu: (te[i], 0, 0)),
                  pl.BlockSpec((1, D_FF, D_MODEL), lambda i, te, nu: (te[i], 0, 0)),
                  pl.BlockSpec((1, 1, D_MODEL), lambda i, te, nu: (te[i], 0, 0))],
        out_specs=pl.BlockSpec((MOE_TM, D_MODEL), lambda i, te, nu: (i, 0)),
        scratch_shapes=[pltpu.VMEM((D_MODEL, 2 * D_FF), BF16), pltpu.VMEM((D_FF, D_MODEL), BF16)],
    )
    return pl.pallas_call(
        _moe_kernel,
        grid_spec=grid_spec,
        out_shape=jax.ShapeDtypeStruct((p, D_MODEL), BF16),
        compiler_params=pltpu.CompilerParams(dimension_semantics=("arbitrary",),
                                             vmem_limit_bytes=VMEM_LIMIT),
        name="moe_experts",
    )(tile_expert, n_used, xs, w_gate_up, b_gate_up.reshape(N_EXPERTS, 1, 2 * D_FF),
      w_down, b_down.reshape(N_EXPERTS, 1, D_MODEL))


def _combine_kernel(x1_ref, yg_ref, tw_ref, mod_ref, g_ref, o_ref):
    tw = tw_ref[0]
    acc = tw[:, 0:1] * yg_ref[0, 0].astype(F32)
    for k in range(1, TOP_K):
        acc = acc + tw[:, k:k + 1] * yg_ref[k, 0].astype(F32)
    g2 = mod_ref[0, :, 5 * D_MODEL:6 * D_MODEL]
    x2 = x1_ref[0] + g2 * acc
    ms = jnp.mean(x2 * x2, axis=-1, keepdims=True)
    o_ref[0] = x2 * lax.rsqrt(ms + RMS_EPS) * g_ref[...]


def _combine_call(x1, yg, top_w, mod, norm_g):
    b, l, _ = x1.shape
    tl = 256
    per_batch = mod.shape[0] > 1
    mod_map = (lambda i, j: (i, 0, 0)) if per_batch else (lambda i, j: (0, 0, 0))
    tok = lambda w: pl.BlockSpec((1, tl, w), lambda i, j: (i, j, 0))
    return pl.pallas_call(
        _combine_kernel,
        grid=(b, l // tl),
        in_specs=[tok(D_MODEL),
                  pl.BlockSpec((TOP_K, 1, tl, D_MODEL), lambda i, j: (0, i, j, 0)),
                  tok(TOP_K),
                  pl.BlockSpec((1, 1, 6 * D_MODEL), mod_map),
                  pl.BlockSpec((1, D_MODEL), lambda i, j: (0, 0))],
        out_specs=tok(D_MODEL),
        out_shape=jax.ShapeDtypeStruct((b, l, D_MODEL), F32),
        compiler_params=pltpu.CompilerParams(dimension_semantics=("arbitrary", "arbitrary"),
                                             vmem_limit_bytes=VMEM_LIMIT),
        name="moe_combine_final_norm",
    )(x1, yg, top_w, mod, norm_g.reshape(1, D_MODEL))


def _route(top_i, rank, counts):
    t = top_i.shape[0]
    p = t * TOP_K + N_EXPERTS * MOE_TM
    n_tiles = p // MOE_TM
    padded = ((counts + MOE_TM - 1) // MOE_TM) * MOE_TM
    ends = jnp.cumsum(padded)
    starts = ends - padded
    dest = starts.at[top_i].get(mode="promise_in_bounds") + rank
    tile_start = jnp.arange(n_tiles, dtype=jnp.int32) * MOE_TM
    tile_expert = jnp.minimum(jnp.sum((ends[None, :] <= tile_start[:, None]).astype(jnp.int32), axis=1),
                              N_EXPERTS - 1)
    n_used = (ends[-1] // MOE_TM).reshape(1)
    tok = jnp.broadcast_to(jnp.arange(t, dtype=jnp.int32)[:, None], (t, TOP_K))
    spread = jnp.arange(p, dtype=jnp.int32) % t
    src_tok = spread.at[dest.reshape(-1)].set(tok.reshape(-1), unique_indices=True)
    return dest, src_tok, tile_expert.astype(jnp.int32), n_used.astype(jnp.int32)


def _layer_front(x, mod, s_ssd_f, s_ssd_b, s_s5_f, s_s5_b, grid_cols, prm):
    z, xbc, u, dt = _inproj_call(x, mod, prm["norm1_g"], prm["w_in_r"])
    y_ssd, n_ssd_f, n_ssd_b = _ssd_call(z, xbc, dt, s_ssd_f, s_ssd_b, prm["ssd"])
    y_f, n_s5_f = _s5_call(u, _s5_state_in(s_s5_f), prm["s5_f"], grid_cols, False)
    y_s5, n_s5_b = _s5_call(u, _s5_state_in(s_s5_b), prm["s5_b"], grid_cols, True,
                            y_first=y_f, glu=prm["glu"])
    x1, h2, top_i, top_w, rank, cnt = _post_call(x, y_ssd, y_s5, mod, prm["w_out"], prm["norm2_g"],
                                                 prm["w_router"], prm["b_router"])
    states = (n_ssd_f, n_ssd_b, _s5_state_out(n_s5_f), _s5_state_out(n_s5_b))
    return x1, (h2, top_i, rank, cnt), top_w, states


def kernel(x_prompt, x_sample, state_ssd_fwd, state_ssd_bwd, state_s5_fwd, state_s5_bwd, c, c_ctx, w_ada, b_ada, norm1_g, w_in, ssd_conv_w, ssd_conv_b, ssd_dt_bias_fwd, ssd_dt_bias_bwd, ssd_a_log_fwd, ssd_a_log_bwd, ssd_d, ssd_norm_g, s5_a_re_fwd, s5_a_im_fwd, s5_log_dt_fwd, s5_b_re_fwd, s5_b_im_fwd, s5_c_re_fwd, s5_c_im_fwd, s5_a_re_bwd, s5_a_im_bwd, s5_log_dt_bwd, s5_b_re_bwd, s5_b_im_bwd, s5_c_re_bwd, s5_c_im_bwd, s5_d, w_glu, b_glu, w_out, norm2_g, w_router, b_router, w_gate_up, b_gate_up, w_down, b_down, norm_f_g):
    depth = w_ada.shape[0]
    assert depth == 1, "single trunk layer"
    nb_ctx, l_ctx, _ = x_prompt.shape
    nb_lat, l_lat, _ = x_sample.shape

    w_in0 = w_in[0]
    dt_lo = SSD_WIDTH + SSD_CONV_CH
    dt_cols = jnp.pad(w_in0[:, dt_lo:dt_lo + 2 * SSD_HEADS], ((0, 0), (0, DT_PAD - 2 * SSD_HEADS)))
    w_in_r = jnp.concatenate([w_in0[:, :dt_lo], w_in0[:, dt_lo + 2 * SSD_HEADS:], dt_cols], axis=1).astype(BF16)
    pad_dt = lambda f, b: jnp.pad(jnp.concatenate([f, b]).astype(F32), (0, DT_PAD - 2 * SSD_HEADS)).reshape(1, DT_PAD)
    ssd_prm = {
        "conv_w": jnp.pad(ssd_conv_w[0].astype(F32), ((0, 5), (0, 0))),
        "conv_b": ssd_conv_b[0].astype(F32).reshape(1, SSD_CONV_CH),
        "dt_bias": pad_dt(ssd_dt_bias_fwd[0], ssd_dt_bias_bwd[0]),
        "a": pad_dt(-jnp.exp(ssd_a_log_fwd[0].astype(F32)), -jnp.exp(ssd_a_log_bwd[0].astype(F32))),
        "d_skip": jnp.repeat(ssd_d[0].astype(F32), SSD_HEAD_DIM).reshape(1, SSD_WIDTH),
        "norm_g": ssd_norm_g[0].astype(F32).reshape(1, SSD_WIDTH),
        "expand": (jnp.arange(DT_PAD)[None, :, None]
                   == (jnp.arange(SSD_WIDTH) // SSD_HEAD_DIM)[None, None, :]
                   + SSD_HEADS * jnp.arange(2)[:, None, None]).astype(BF16),
    }
    prm = {
        "norm1_g": norm1_g[0], "w_in_r": w_in_r, "ssd": ssd_prm,
        "s5_f": _s5_params(s5_a_re_fwd[0], s5_a_im_fwd[0], s5_log_dt_fwd[0], s5_b_re_fwd[0],
                           s5_b_im_fwd[0], s5_c_re_fwd[0], s5_c_im_fwd[0]),
        "s5_b": _s5_params(s5_a_re_bwd[0], s5_a_im_bwd[0], s5_log_dt_bwd[0], s5_b_re_bwd[0],
                           s5_b_im_bwd[0], s5_c_re_bwd[0], s5_c_im_bwd[0]),
        "glu": {"d": s5_d[0].astype(F32).reshape(1, S5_WIDTH), "w": w_glu[0].astype(BF16),
                "b": b_glu[0].astype(F32).reshape(1, 2 * S5_WIDTH)},
        "w_out": w_out[0].astype(BF16), "norm2_g": norm2_g[0],
        "w_router": w_router[0].astype(F32), "b_router": b_router[0].astype(F32),
    }

    conds = jnp.concatenate([c_ctx[None, :], c], axis=0)
    conds = jnp.pad(conds, ((0, (-conds.shape[0]) % 8), (0, 0)))
    mod = _ada_call(conds, w_ada[0], b_ada[0])
    mod_ctx = mod[0:1].reshape(1, 1, 6 * D_MODEL)
    mod_lat = mod[1:1 + nb_lat].reshape(nb_lat, 1, 6 * D_MODEL)

    zero_ssd = jnp.zeros((nb_ctx, SSD_HEADS, SSD_HEAD_DIM, SSD_STATE), F32)
    zero_s5 = jnp.zeros((nb_ctx, S5_GROUPS, S5_STATE, 2), F32)
    x1_c, routed_c, tw_c, st_c = _layer_front(
        x_prompt, mod_ctx, zero_ssd, zero_ssd, zero_s5, zero_s5, False, prm)
    x1_l, routed_l, tw_l, _ = _layer_front(
        x_sample, mod_lat, state_ssd_fwd[:, 0], state_ssd_bwd[:, 0], state_s5_fwd[:, 0],
        state_s5_bwd[:, 0], True, prm)

    h2_c, ti_c, rk_c, cnt_c = routed_c
    h2_l, ti_l, rk_l, cnt_l = routed_l
    t_ctx = nb_ctx * l_ctx
    n_tok = t_ctx + nb_lat * l_lat
    cnt_c = cnt_c.reshape(N_EXPERTS).astype(jnp.int32)
    cnt_l = cnt_l.reshape(N_EXPERTS).astype(jnp.int32)
    ti_l2 = ti_l.reshape(-1, TOP_K)
    top_i = jnp.concatenate([ti_c.reshape(-1, TOP_K), ti_l2], axis=0)
    rank = jnp.concatenate([rk_c.reshape(-1, TOP_K),
                            rk_l.reshape(-1, TOP_K) + cnt_c.at[ti_l2].get(mode="promise_in_bounds")], axis=0)
    dest, src_tok, tile_expert, n_used = _route(top_i, rank, cnt_c + cnt_l)
    h2 = jnp.concatenate([h2_c.reshape(-1, D_MODEL), h2_l.reshape(-1, D_MODEL),
                          jnp.zeros((max(GATHER_SRC_ROWS - n_tok, 0), D_MODEL), BF16)], axis=0)
    xs = h2.at[src_tok].get(mode="promise_in_bounds")
    ys = _moe_call(tile_expert, n_used, xs, w_gate_up[0], b_gate_up[0], w_down[0], b_down[0])

    def picked_rows(d, nb, l):
        rows = ys.at[d.T.reshape(-1)].get(mode="promise_in_bounds")
        return rows.reshape(TOP_K, nb, l, D_MODEL)

    y_prompt = _combine_call(x1_c, picked_rows(dest[:t_ctx], nb_ctx, l_ctx), tw_c, mod_ctx, norm_f_g)
    y_sample = _combine_call(x1_l, picked_rows(dest[t_ctx:], nb_lat, l_lat), tw_l, mod_lat, norm_f_g)

    ssd_f, ssd_b, s5_f, s5_b = st_c
    return (y_prompt, y_sample, ssd_f[:, None], ssd_b[:, None], s5_f[:, None], s5_b[:, None])
```

```python
import functools
import math

import jax
import jax.numpy as jnp
from jax import lax
from jax.experimental import pallas as pl
from jax.experimental.pallas import tpu as pltpu

F32 = jnp.float32
BF16 = jnp.bfloat16

D_MODEL = 1024
GRID_W = 64
SSD_WIDTH = 512
SSD_HEAD_DIM = 64
SSD_HEADS = 8
SSD_GROUPS = 2
SSD_HEADS_PER_GROUP = SSD_HEADS // SSD_GROUPS
SSD_STATE = 128
SSD_CHUNK = 128
SSD_CONV_CH = SSD_WIDTH + 2 * SSD_GROUPS * SSD_STATE
S5_WIDTH = 512
S5_CH = 16
S5_GROUPS = 32
S5_STATE = 64
N_EXPERTS = 32
TOP_K = 4
D_FF = 1024
SWIGLU_LIMIT = 7.0
SWIGLU_ALPHA = 1.702
RMS_EPS = 1e-5

LANES = 128
DT_PAD = 128
S5_BATCH = 8
S5_COLS = 2 * S5_GROUPS * S5_STATE
S5_BLK = 512
S5_SUB_STEPS = 32
S5_BLOCK_STEPS = 256
MOE_TM = 256
GATHER_SRC_ROWS = 32768
VMEM_LIMIT = 56 * 1024 * 1024


def _sigmoid(x):
    return 1.0 / (1.0 + jnp.exp(-x))


def _split3(x):
    hi = x.astype(BF16)
    r1 = x - hi.astype(F32)
    mid = r1.astype(BF16)
    lo = (r1 - mid.astype(F32)).astype(BF16)
    return hi, mid, lo


def _split2(x):
    hi = x.astype(BF16)
    return hi, (x - hi.astype(F32)).astype(BF16)


def _dot(a, b):
    return jnp.dot(a, b, preferred_element_type=F32)


def _ada_kernel(c_ref, w_ref, b_ref, o_ref):
    c = c_ref[...]
    s = c * _sigmoid(c)
    o_ref[...] = _dot(s.astype(BF16), w_ref[...].astype(BF16)) + b_ref[...]


def _ada_call(conds, w_ada, b_ada):
    n = w_ada.shape[1]
    tn = 1536
    rows = conds.shape[0]
    return pl.pallas_call(
        _ada_kernel,
        grid=(n // tn,),
        in_specs=[pl.BlockSpec((rows, D_MODEL), lambda j: (0, 0)),
                  pl.BlockSpec((D_MODEL, tn), lambda j: (0, j)),
                  pl.BlockSpec((1, tn), lambda j: (0, j))],
        out_specs=pl.BlockSpec((rows, tn), lambda j: (0, j)),
        out_shape=jax.ShapeDtypeStruct((rows, n), F32),
        compiler_params=pltpu.CompilerParams(dimension_semantics=("arbitrary",),
                                             vmem_limit_bytes=VMEM_LIMIT),
        name="ada_mod",
    )(conds, w_ada, b_ada.reshape(1, n))


def _inproj_kernel(x_ref, mod_ref, g_ref, w_ref, z_ref, xbc_ref, u_ref, dt_ref):
    x = x_ref[0]
    ms = jnp.mean(x * x, axis=-1, keepdims=True)
    y = x * lax.rsqrt(ms + RMS_EPS) * g_ref[...]
    sh = mod_ref[0, :, 0:D_MODEL]
    sc = mod_ref[0, :, D_MODEL:2 * D_MODEL]
    h = (y * (1.0 + sc) + sh).astype(BF16)
    z_ref[0] = _dot(h, w_ref[:, 0:512]).astype(BF16)
    xbc_ref[0] = _dot(h, w_ref[:, 512:1536]).astype(BF16)
    u_ref[0] = _dot(h, w_ref[:, 1536:2048])
    dt_ref[0] = _dot(h, w_ref[:, 2048:2048 + DT_PAD])


def _inproj_call(x, mod, norm_g, w_in_r):
    b, l, _ = x.shape
    tl = min(l, 512)
    per_batch = mod.shape[0] > 1
    mod_map = (lambda i, j: (i, 0, 0)) if per_batch else (lambda i, j: (0, 0, 0))
    ncol = w_in_r.shape[1]
    tok = lambda w: pl.BlockSpec((1, tl, w), lambda i, j: (i, j, 0))
    return pl.pallas_call(
        _inproj_kernel,
        grid=(b, l // tl),
        in_specs=[tok(D_MODEL),
                  pl.BlockSpec((1, 1, 6 * D_MODEL), mod_map),
                  pl.BlockSpec((1, D_MODEL), lambda i, j: (0, 0)),
                  pl.BlockSpec((D_MODEL, ncol), lambda i, j: (0, 0))],
        out_specs=[tok(SSD_WIDTH), tok(SSD_CONV_CH), tok(S5_WIDTH), tok(DT_PAD)],
        out_shape=[jax.ShapeDtypeStruct((b, l, SSD_WIDTH), BF16),
                   jax.ShapeDtypeStruct((b, l, SSD_CONV_CH), BF16),
                   jax.ShapeDtypeStruct((b, l, S5_WIDTH), F32),
                   jax.ShapeDtypeStruct((b, l, DT_PAD), F32)],
        compiler_params=pltpu.CompilerParams(dimension_semantics=("arbitrary", "arbitrary"),
                                             vmem_limit_bytes=VMEM_LIMIT),
        name="norm1_inproj",
    )(x, mod, norm_g.reshape(1, D_MODEL), w_in_r)


def _softplus(x):
    return jnp.maximum(x, 0.0) + jnp.log1p(jnp.exp(-jnp.abs(x)))


def _ssd_kernel(z_ref, xbc_ref, dt_ref, s0f_ref, s0b_ref, cw_ref, cb_ref, dtb_ref, a_ref,
                dskip_ref, ng_ref, ex_ref, y_ref, sf_ref, sb_ref,
                xc_s, dts_s, yf_s, yb_s, stf_s, stb_s, *, seq):
    q = SSD_CHUNK
    nc = seq // q
    gw = SSD_HEADS_PER_GROUP * SSD_HEAD_DIM
    row_i = lax.broadcasted_iota(jnp.int32, (q, 1), 0)

    def conv_body(c, carry):
        r0 = pl.multiple_of(c * q, q)
        cur = xbc_ref[0, pl.ds(r0, q), :].astype(F32)
        p0 = pl.multiple_of(jnp.maximum(r0 - 16, 0), 16)
        n0 = pl.multiple_of(jnp.minimum(r0 + q, seq - 16), 16)
        prev_row = xbc_ref[0, pl.ds(p0, 16), :].astype(F32)[15:16, :]
        next_row = xbc_ref[0, pl.ds(n0, 16), :].astype(F32)[0:1, :]
        prev_row = jnp.where(c > 0, prev_row, 0.0)
        next_row = jnp.where(c < nc - 1, next_row, 0.0)
        down = jnp.where(row_i == 0, prev_row, pltpu.roll(cur, 1, axis=0))
        up = jnp.where(row_i == q - 1, next_row, pltpu.roll(cur, q - 1, axis=0))
        v = cw_ref[0:1, :] * down + cw_ref[1:2, :] * cur + cw_ref[2:3, :] * up + cb_ref[...]
        xc_s[pl.ds(r0, q), :] = v * _sigmoid(v)
        dts_s[pl.ds(r0, q), :] = _softplus(dt_ref[0, pl.ds(r0, q), :] + dtb_ref[...])
        return carry

    lax.fori_loop(0, nc, conv_body, 0)

    li = lax.broadcasted_iota(jnp.int32, (q, q), 0)
    si = lax.broadcasted_iota(jnp.int32, (q, q), 1)
    head_of_lane = lax.broadcasted_iota(jnp.int32, (q, gw), 1) // SSD_HEAD_DIM

    def expand(v, e_mat):
        hi, lo = _split2(v)
        return _dot(hi, e_mat) + _dot(lo, e_mat)

    def chunk(c, direction):
        r0 = pl.multiple_of(c * q, q)
        st_ref = stf_s if direction == 0 else stb_s
        y_dst = yf_s if direction == 0 else yb_s
        e_mat = ex_ref[direction]
        causal = (li >= si) if direction == 0 else (li <= si)
        tri = jnp.where(causal, 1.0, 0.0).astype(BF16)
        dts = dts_s[pl.ds(r0, q), :]
        dta = dts * a_ref[...]
        d_hi, d_mid, d_lo = _split3(dta)
        cs = _dot(tri, d_hi) + _dot(tri, d_mid) + _dot(tri, d_lo)
        tot = cs[q - 1:q, :] if direction == 0 else cs[0:1, :]
        cs_t = cs.T
        ecs = jnp.exp(cs)
        tail = jnp.exp(tot - cs)
        dt_x = expand(dts, e_mat)
        dtw_x = expand(dts * tail, e_mat)
        etot_x = expand(jnp.broadcast_to(jnp.exp(tot), (8, DT_PAD)), e_mat)[0:1, :]
        for g in range(SSD_GROUPS):
            b_off = SSD_WIDTH + g * SSD_STATE
            c_off = SSD_WIDTH + SSD_GROUPS * SSD_STATE + g * SSD_STATE
            x_g = xc_s[pl.ds(r0, q), g * gw:(g + 1) * gw]
            bm = xc_s[pl.ds(r0, q), b_off:b_off + SSD_STATE]
            cm = xc_s[pl.ds(r0, q), c_off:c_off + SSD_STATE]
            cb = lax.dot_general(cm.astype(BF16), bm.astype(BF16), (((1,), (1,)), ((), ())),
                                 preferred_element_type=F32)
            st_g = st_ref[g]
            xdt = x_g * dt_x[:, g * gw:(g + 1) * gw]
            lhs, rhs = [], []
            for hh in range(SSD_HEADS_PER_GROUP):
                col = g * SSD_HEADS_PER_GROUP + hh + SSD_HEADS * direction
                seg = jnp.broadcast_to(cs[:, col:col + 1], (q, q)) - cs_t[col:col + 1, :]
                dec = jnp.exp(jnp.where(causal, seg, -1e30))
                lhs.append((cb * dec).astype(BF16))
                lhs.append((cm * jnp.broadcast_to(ecs[:, col:col + 1], (q, q))).astype(BF16))
                own = head_of_lane == hh
                rhs.append(jnp.where(own, xdt, 0.0).astype(BF16))
                rhs.append(jnp.where(own, st_g, 0.0).astype(BF16))
            y_dst[pl.ds(r0, q), g * gw:(g + 1) * gw] = _dot(jnp.concatenate(lhs, axis=1),
                                                             jnp.concatenate(rhs, axis=0))
            xw = (x_g * dtw_x[:, g * gw:(g + 1) * gw]).astype(BF16)
            st_ref[g] = etot_x[:, g * gw:(g + 1) * gw] * st_g + _dot(bm.T.astype(BF16), xw)

    stf_s[...] = s0f_ref[0]
    stb_s[...] = s0b_ref[0]

    def both(i, carry):
        chunk(i, 0)
        chunk(nc - 1 - i, 1)
        return carry

    lax.fori_loop(0, nc, both, 0)
    sf_ref[0] = stf_s[...]
    sb_ref[0] = stb_s[...]

    def finish(c, carry):
        r0 = pl.multiple_of(c * q, q)
        xs = xc_s[pl.ds(r0, q), 0:SSD_WIDTH]
        zz = z_ref[0, pl.ds(r0, q), :].astype(F32)
        y = (yf_s[pl.ds(r0, q), :] + yb_s[pl.ds(r0, q), :] + dskip_ref[...] * xs) * (zz * _sigmoid(zz))
        ms = jnp.mean(y * y, axis=-1, keepdims=True)
        y_ref[0, pl.ds(r0, q), :] = (y * lax.rsqrt(ms + RMS_EPS) * ng_ref[...]).astype(y_ref.dtype)
        return carry

    lax.fori_loop(0, nc, finish, 0)


def _ssd_state_in(s):
    b = s.shape[0]
    s = s.astype(F32).reshape(b, SSD_GROUPS, SSD_HEADS_PER_GROUP, SSD_HEAD_DIM, SSD_STATE)
    return s.transpose(0, 1, 4, 2, 3).reshape(b, SSD_GROUPS, SSD_STATE, SSD_HEADS_PER_GROUP * SSD_HEAD_DIM)


def _ssd_state_out(s):
    b = s.shape[0]
    s = s.reshape(b, SSD_GROUPS, SSD_STATE, SSD_HEADS_PER_GROUP, SSD_HEAD_DIM)
    return s.transpose(0, 1, 3, 4, 2).reshape(b, SSD_HEADS, SSD_HEAD_DIM, SSD_STATE)


def _ssd_call(z, xbc, dt, s0f, s0b, prm):
    b, l, _ = z.shape
    st_shape = (SSD_GROUPS, SSD_STATE, SSD_HEADS_PER_GROUP * SSD_HEAD_DIM)
    tok = lambda w: pl.BlockSpec((1, l, w), lambda i: (i, 0, 0))
    st_spec = pl.BlockSpec((1,) + st_shape, lambda i: (i, 0, 0, 0))
    par = lambda r, w: pl.BlockSpec((r, w), lambda i: (0, 0))
    y, sf, sb = pl.pallas_call(
        functools.partial(_ssd_kernel, seq=l),
        grid=(b,),
        in_specs=[tok(SSD_WIDTH), tok(SSD_CONV_CH), tok(DT_PAD), st_spec, st_spec,
                  par(8, SSD_CONV_CH), par(1, SSD_CONV_CH), par(1, DT_PAD), par(1, DT_PAD),
                  par(1, SSD_WIDTH), par(1, SSD_WIDTH),
                  pl.BlockSpec((2, DT_PAD, SSD_WIDTH), lambda i: (0, 0, 0))],
        out_specs=[tok(SSD_WIDTH), st_spec, st_spec],
        out_shape=[jax.ShapeDtypeStruct((b, l, SSD_WIDTH), BF16),
                   jax.ShapeDtypeStruct((b,) + st_shape, F32),
                   jax.ShapeDtypeStruct((b,) + st_shape, F32)],
        scratch_shapes=[pltpu.VMEM((l, SSD_CONV_CH), F32),
                        pltpu.VMEM((l, DT_PAD), F32),
                        pltpu.VMEM((l, SSD_WIDTH), F32),
                        pltpu.VMEM((l, SSD_WIDTH), F32),
                        pltpu.VMEM(st_shape, F32),
                        pltpu.VMEM(st_shape, F32)],
        compiler_params=pltpu.CompilerParams(dimension_semantics=("arbitrary",),
                                             vmem_limit_bytes=VMEM_LIMIT),
        name="ssd_mixer",
    )(z, xbc, dt, _ssd_state_in(s0f), _ssd_state_in(s0b),
      prm["conv_w"], prm["conv_b"], prm["dt_bias"], prm["a"], prm["d_skip"], prm["norm_g"], prm["expand"])
    return y, _ssd_state_out(sf), _ssd_state_out(sb)


def _gelu_tanh(x):
    return 0.5 * x * (1.0 + jnp.tanh(math.sqrt(2.0 / math.pi) * (x + 0.044715 * (x * x * x))))


def _s5_kernel(*refs, grid_cols, reverse, final):
    if final:
        (u_ref, s0_ref, lr_ref, li_ref, wb_ref, wc_ref, yf_ref, d_ref, wg_ref, bg_ref,
         o_ref, sfin_ref, ur_s, yr_s, st_s) = refs
    else:
        (u_ref, s0_ref, lr_ref, li_ref, wb_ref, wc_ref,
         o_ref, sfin_ref, ur_s, yr_s, st_s) = refs
    nb = S5_BATCH
    steps = S5_BLOCK_STEPS
    rows = steps * nb
    sub_rows = S5_SUB_STEPS * nb
    n_sub = steps // S5_SUB_STEPS
    j = pl.program_id(1)

    @pl.when(j == 0)
    def _():
        st_s[...] = s0_ref[0]

    def to_scan(v):
        v = jnp.transpose(v, (2, 1, 0, 3)) if grid_cols else jnp.swapaxes(v, 0, 1)
        return v.reshape(rows, S5_WIDTH)

    def from_scan(v):
        if grid_cols:
            return jnp.transpose(v.reshape(u_ref.shape[2], u_ref.shape[1], nb, S5_WIDTH), (2, 1, 0, 3))
        return jnp.swapaxes(v.reshape(steps, nb, S5_WIDTH), 0, 1)

    ur_s[...] = to_scan(u_ref[...])

    order = list(range(S5_SUB_STEPS))[::-1] if reverse else list(range(S5_SUB_STEPS))
    blk_per_half = S5_COLS // S5_BLK // 2

    def by_groups(v, perm):
        return jnp.concatenate([v[g * nb:(g + 1) * nb] for g in perm], axis=0)

    def sub_chunk(i, carry):
        sidx = (n_sub - 1 - i) if reverse else i
        r0 = pl.multiple_of(sidx * sub_rows, sub_rows)
        for hf in range(2):
            u_half = ur_s[pl.ds(r0, sub_rows), hf * 256:(hf + 1) * 256]
            if reverse:
                u_half = by_groups(u_half, order)
            uh = u_half.astype(BF16)
            for bq in range(blk_per_half):
                blk = hf * blk_per_half + bq
                c0 = blk * S5_BLK
                lam_r = jnp.broadcast_to(lr_ref[:, c0:c0 + S5_BLK], (nb, S5_BLK))
                lam_i = jnp.broadcast_to(li_ref[:, c0:c0 + S5_BLK], (nb, S5_BLK))
                bu = _dot(uh, wb_ref[hf, :, bq * S5_BLK:(bq + 1) * S5_BLK])
                s = st_s[:, c0:c0 + S5_BLK]
                states = []
                for k in range(S5_SUB_STEPS):
                    sw = jnp.concatenate([s[:, S5_BLK // 2:], s[:, :S5_BLK // 2]], axis=1)
                    s = lam_r * s + lam_i * sw + bu[k * nb:(k + 1) * nb, :]
                    states.append(s)
                st_s[:, c0:c0 + S5_BLK] = s
                y_blk = _dot(jnp.concatenate(states, axis=0).astype(BF16), wc_ref[blk])
                if reverse:
                    y_blk = by_groups(y_blk, order)
                yr_s[pl.ds(r0, sub_rows), blk * 64:(blk + 1) * 64] = y_blk
        return carry

    lax.fori_loop(0, n_sub, sub_chunk, 0)

    @pl.when(j == pl.num_programs(1) - 1)
    def _():
        sfin_ref[0] = st_s[...]

    if not final:
        o_ref[0] = yr_s[...]
        return

    def glu_chunk(i, carry):
        r0 = pl.multiple_of(i * sub_rows, sub_rows)
        y = (yr_s[pl.ds(r0, sub_rows), :] + yf_ref[0, pl.ds(r0, sub_rows), :]
             + d_ref[...] * ur_s[pl.ds(r0, sub_rows), :])
        y = _gelu_tanh(y).astype(BF16)
        val = _dot(y, wg_ref[:, 0:S5_WIDTH]) + bg_ref[:, 0:S5_WIDTH]
        gate = _dot(y, wg_ref[:, S5_WIDTH:2 * S5_WIDTH]) + bg_ref[:, S5_WIDTH:2 * S5_WIDTH]
        yr_s[pl.ds(r0, sub_rows), :] = val * _sigmoid(gate)
        return carry

    lax.fori_loop(0, n_sub, glu_chunk, 0)

    o_ref[...] = from_scan(yr_s[...])


def _s5_call(u, s0, prm, grid_cols, reverse, y_first=None, glu=None):
    b, l, _ = u.shape
    nbg = b // S5_BATCH
    nblk = l // S5_BLOCK_STEPS
    rows = S5_BLOCK_STEPS * S5_BATCH
    final = y_first is not None
    blk_of = (lambda j: nblk - 1 - j) if reverse else (lambda j: j)
    if grid_cols:
        n_rows = l // GRID_W
        u_in = u.reshape(b, n_rows, GRID_W, S5_WIDTH)
        cols_per_blk = S5_BLOCK_STEPS // n_rows
        tok_spec = pl.BlockSpec((S5_BATCH, n_rows, cols_per_blk, S5_WIDTH),
                                lambda i, j: (i, 0, blk_of(j), 0))
        out_nat = jax.ShapeDtypeStruct((b, n_rows, GRID_W, S5_WIDTH), F32)
    else:
        u_in = u
        tok_spec = pl.BlockSpec((S5_BATCH, S5_BLOCK_STEPS, S5_WIDTH), lambda i, j: (i, blk_of(j), 0))
        out_nat = jax.ShapeDtypeStruct((b, l, S5_WIDTH), F32)
    scan_spec = pl.BlockSpec((1, rows, S5_WIDTH), lambda i, j: (i, blk_of(j), 0))
    st_spec = pl.BlockSpec((1, S5_BATCH, S5_COLS), lambda i, j: (i, 0, 0))
    full = lambda shape: pl.BlockSpec(shape, lambda i, j: (0,) * len(shape))
    in_specs = [tok_spec, st_spec, full((1, S5_COLS)), full((1, S5_COLS)),
                full((2, 256, 2048)), full((S5_COLS // S5_BLK, S5_BLK, 64))]
    args = [u_in, s0, prm["lam_r"], prm["lam_i"], prm["wb"], prm["wc"]]
    if final:
        in_specs += [scan_spec, full((1, S5_WIDTH)), full((S5_WIDTH, 2 * S5_WIDTH)), full((1, 2 * S5_WIDTH))]
        args += [y_first, glu["d"], glu["w"], glu["b"]]
        out_specs = [tok_spec, st_spec]
        out_shape = [out_nat, jax.ShapeDtypeStruct((nbg, S5_BATCH, S5_COLS), F32)]
    else:
        out_specs = [scan_spec, st_spec]
        out_shape = [jax.ShapeDtypeStruct((nbg, l * S5_BATCH, S5_WIDTH), F32),
                     jax.ShapeDtypeStruct((nbg, S5_BATCH, S5_COLS), F32)]
    y, sfin = pl.pallas_call(
        functools.partial(_s5_kernel, grid_cols=grid_cols, reverse=reverse, final=final),
        grid=(nbg, nblk),
        in_specs=in_specs,
        out_specs=out_specs,
        out_shape=out_shape,
        scratch_shapes=[pltpu.VMEM((rows, S5_WIDTH), F32),
                        pltpu.VMEM((rows, S5_WIDTH), F32),
                        pltpu.VMEM((S5_BATCH, S5_COLS), F32)],
        compiler_params=pltpu.CompilerParams(dimension_semantics=("arbitrary", "arbitrary"),
                                             vmem_limit_bytes=VMEM_LIMIT),
        name="s5_final" if final else "s5_first",
    )(*args)
    if final:
        y = y.reshape(b, l, S5_WIDTH)
    return y, sfin


def _s5_cols(t):
    lead = t.shape[:-2]
    return t.reshape(lead + (S5_COLS // S5_BLK, S5_BLK // 2))


def _s5_state_in(s0):
    b = s0.shape[0]
    re, im = _s5_cols(s0[..., 0].astype(F32)), _s5_cols(s0[..., 1].astype(F32))
    return jnp.stack([re, im], axis=-2).reshape(b // S5_BATCH, S5_BATCH, S5_COLS)


def _s5_state_out(s):
    b = s.shape[0] * S5_BATCH
    s = s.reshape(b, S5_COLS // S5_BLK, 2, S5_BLK // 2)
    re = s[:, :, 0].reshape(b, S5_GROUPS, S5_STATE)
    im = s[:, :, 1].reshape(b, S5_GROUPS, S5_STATE)
    return jnp.stack([re, im], axis=-1)


def _s5_params(a_re, a_im, log_dt, b_re, b_im, c_re, c_im):
    a_re, a_im = a_re.astype(F32), a_im.astype(F32)
    delta = jnp.exp(log_dt.astype(F32))[:, None]
    mag = jnp.exp(a_re * delta)
    lam_re, lam_im = mag * jnp.cos(a_im * delta), mag * jnp.sin(a_im * delta)
    den = a_re * a_re + a_im * a_im
    nr = lam_re - 1.0
    f_re = (nr * a_re + lam_im * a_im) / den
    f_im = (lam_im * a_re - nr * a_im) / den
    b_re, b_im = b_re.astype(F32), b_im.astype(F32)
    bb_re = f_re[..., None] * b_re - f_im[..., None] * b_im
    bb_im = f_re[..., None] * b_im + f_im[..., None] * b_re
    nblk = S5_COLS // S5_BLK
    lr = _s5_cols(lam_re)
    li = _s5_cols(lam_im)
    lam_r = jnp.stack([lr, lr], axis=-2).reshape(1, S5_COLS)
    lam_i = jnp.stack([-li, li], axis=-2).reshape(1, S5_COLS)
    gq = S5_BLK // 2 // S5_STATE
    eye = jnp.eye(gq, dtype=F32)

    def in_block(bb):
        t = bb.reshape(nblk, gq, S5_STATE, S5_CH)
        return jnp.einsum("gh,kgpc->kgchp", eye, t).reshape(nblk, gq * S5_CH, gq * S5_STATE)

    blk_in = jnp.concatenate([in_block(bb_re), in_block(bb_im)], axis=-1)
    half = nblk // 2
    eye_h = jnp.eye(half, dtype=F32)
    wb = jnp.einsum("jk,hkcn->hjckn", eye_h, blk_in.reshape(2, half, gq * S5_CH, S5_BLK))
    wb = wb.reshape(2, half * gq * S5_CH, half * S5_BLK).astype(BF16)

    def out_block(cc):
        t = cc.astype(F32).reshape(nblk, gq, S5_CH, S5_STATE)
        return jnp.einsum("gh,kgcp->kgphc", eye, t).reshape(nblk, gq * S5_STATE, gq * S5_CH)

    wc = jnp.concatenate([out_block(c_re), -out_block(c_im)], axis=1).astype(BF16)
    return {"lam_r": lam_r, "lam_i": lam_i, "wb": wb, "wc": wc}


def _post_kernel(x_ref, ys_ref, y5_ref, mod_ref, wo_ref, g_ref, wr_ref, br_ref,
                 x1_ref, h_ref, ti_ref, tw_ref, rk_ref, cnt_ref, cnt_s):
    first = jnp.logical_and(pl.program_id(0) == 0, pl.program_id(1) == 0)

    @pl.when(first)
    def _():
        cnt_s[...] = jnp.zeros(cnt_s.shape, cnt_s.dtype)

    tl = x_ref.shape[1]
    mixed = (_dot(ys_ref[0], wo_ref[0:SSD_WIDTH, :])
             + _dot(y5_ref[0].astype(BF16), wo_ref[SSD_WIDTH:SSD_WIDTH + S5_WIDTH, :]))
    g1 = mod_ref[0, :, 2 * D_MODEL:3 * D_MODEL]
    sh2 = mod_ref[0, :, 3 * D_MODEL:4 * D_MODEL]
    sc2 = mod_ref[0, :, 4 * D_MODEL:5 * D_MODEL]
    x1 = x_ref[0] + g1 * mixed
    x1_ref[0] = x1
    ms = jnp.mean(x1 * x1, axis=-1, keepdims=True)
    h = x1 * lax.rsqrt(ms + RMS_EPS) * g_ref[...] * (1.0 + sc2) + sh2
    h_ref[0] = h.astype(BF16)

    h_hi, h_mid, _ = _split3(h)
    wr = wr_ref[...]
    w_hi, w_mid, _ = _split3(wr)
    logits = _dot(h_hi, w_hi) + _dot(h_hi, w_mid) + _dot(h_mid, w_hi) + br_ref[...]

    lane = lax.broadcasted_iota(jnp.int32, (tl, N_EXPERTS), 1).astype(F32)
    slot = lax.broadcasted_iota(jnp.int32, (tl, TOP_K), 1)
    work = logits
    onehot = jnp.zeros((tl, N_EXPERTS), F32)
    sels, vals = [], []
    top_i = jnp.zeros((tl, TOP_K), jnp.int32)
    for k in range(TOP_K):
        m = jnp.max(work, axis=-1, keepdims=True)
        idx = jnp.min(jnp.where(work == m, lane, float(N_EXPERTS)), axis=-1, keepdims=True)
        sel = lane == idx
        sels.append(sel)
        vals.append(m)
        top_i = jnp.where(slot == k, idx.astype(jnp.int32), top_i)
        onehot = jnp.where(sel, 1.0, onehot)
        work = jnp.where(sel, -jnp.inf, work)
    es = [jnp.exp(v - vals[0]) for v in vals]
    den = es[0] + es[1] + es[2] + es[3]
    top_w = jnp.zeros((tl, TOP_K), F32)
    for k in range(TOP_K):
        top_w = jnp.where(slot == k, es[k] / den, top_w)

    ri = lax.broadcasted_iota(jnp.int32, (tl, tl), 0)
    ci = lax.broadcasted_iota(jnp.int32, (tl, tl), 1)
    strict = jnp.where(ri > ci, 1.0, 0.0).astype(BF16)
    before = _dot(strict, onehot.astype(BF16)) + cnt_s[...]
    rank = jnp.zeros((tl, TOP_K), jnp.int32)
    for k in range(TOP_K):
        rk = jnp.sum(jnp.where(sels[k], before, 0.0), axis=-1, keepdims=True)
        rank = jnp.where(slot == k, rk.astype(jnp.int32), rank)
    cnt_s[...] = cnt_s[...] + jnp.sum(onehot, axis=0, keepdims=True)
    ti_ref[0] = top_i
    tw_ref[0] = top_w
    rk_ref[0] = rank
    cnt_ref[...] = cnt_s[...]


def _post_call(x, y_ssd, y_s5, mod, w_out, norm_g, w_router, b_router):
    b, l, _ = x.shape
    tl = 256
    per_batch = mod.shape[0] > 1
    mod_map = (lambda i, j: (i, 0, 0)) if per_batch else (lambda i, j: (0, 0, 0))
    tok = lambda w: pl.BlockSpec((1, tl, w), lambda i, j: (i, j, 0))
    full = lambda r, w: pl.BlockSpec((r, w), lambda i, j: (0, 0))
    return pl.pallas_call(
        _post_kernel,
        grid=(b, l // tl),
        in_specs=[tok(D_MODEL), tok(SSD_WIDTH), tok(S5_WIDTH),
                  pl.BlockSpec((1, 1, 6 * D_MODEL), mod_map),
                  full(D_MODEL, D_MODEL), full(1, D_MODEL), full(D_MODEL, N_EXPERTS),
                  full(1, N_EXPERTS)],
        out_specs=[tok(D_MODEL), tok(D_MODEL), tok(TOP_K), tok(TOP_K), tok(TOP_K), full(1, N_EXPERTS)],
        out_shape=[jax.ShapeDtypeStruct((b, l, D_MODEL), F32),
                   jax.ShapeDtypeStruct((b, l, D_MODEL), BF16),
                   jax.ShapeDtypeStruct((b, l, TOP_K), jnp.int32),
                   jax.ShapeDtypeStruct((b, l, TOP_K), F32),
                   jax.ShapeDtypeStruct((b, l, TOP_K), jnp.int32),
                   jax.ShapeDtypeStruct((1, N_EXPERTS), F32)],
        scratch_shapes=[pltpu.VMEM((1, N_EXPERTS), F32)],
        compiler_params=pltpu.CompilerParams(dimension_semantics=("arbitrary", "arbitrary"),
                                             vmem_limit_bytes=VMEM_LIMIT),
        name="outproj_norm2_router",
    )(x, y_ssd, y_s5, mod, w_out, norm_g.reshape(1, D_MODEL), w_router,
      b_router.reshape(1, N_EXPERTS))


def _moe_kernel(te_ref, nu_ref, x_ref, wgu_ref, bgu_ref, wd_ref, bd_ref, o_ref, gu_s, dn_s):
    i = pl.program_id(0)
    prev = te_ref[jnp.maximum(i - 1, 0)]
    new_expert = jnp.logical_or(i == 0, te_ref[i] != prev)

    @pl.when(new_expert)
    def _():
        for r in range(D_MODEL // 128):
            gu_s[r * 128:(r + 1) * 128, :] = wgu_ref[0, r * 128:(r + 1) * 128, :].astype(BF16)
            dn_s[r * 128:(r + 1) * 128, :] = wd_ref[0, r * 128:(r + 1) * 128, :].astype(BF16)

    @pl.when(i < nu_ref[0])
    def _():
        x = x_ref[...]
        gate = _dot(x, gu_s[:, 0:D_FF]) + bgu_ref[0, :, 0:D_FF]
        up = _dot(x, gu_s[:, D_FF:2 * D_FF]) + bgu_ref[0, :, D_FF:2 * D_FF]
        gate = jnp.minimum(gate, SWIGLU_LIMIT)
        up = jnp.clip(up, -SWIGLU_LIMIT, SWIGLU_LIMIT)
        act = gate * _sigmoid(SWIGLU_ALPHA * gate) * (up + 1.0)
        o_ref[...] = (_dot(act.astype(BF16), dn_s[...]) + bd_ref[0]).astype(o_ref.dtype)

    @pl.when(i >= nu_ref[0])
    def _():
        o_ref[...] = jnp.zeros(o_ref.shape, o_ref.dtype)


def _moe_call(tile_expert, n_used, xs, w_gate_up, b_gate_up, w_down, b_down):
    p = xs.shape[0]
    n_tiles = p // MOE_TM
    grid_spec = pltpu.PrefetchScalarGridSpec(
        num_scalar_prefetch=2,
        grid=(n_tiles,),
        in_specs=[pl.BlockSpec((MOE_TM, D_MODEL), lambda i, te, nu: (i, 0)),
                  pl.BlockSpec((1, D_MODEL, 2 * D_FF), lambda i, te, nu: (te[i], 0, 0)),
                  pl.BlockSpec((1, 1, 2 * D_FF), lambda i, te, nu: (te[i], 0, 0)),
                  pl.BlockSpec((1, D_FF, D_MODEL), lambda i, te, nu: (te[i], 0, 0)),
                  pl.BlockSpec((1, 1, D_MODEL), lambda i, te, nu: (te[i], 0, 0))],
        out_specs=pl.BlockSpec((MOE_TM, D_MODEL), lambda i, te, nu: (i, 0)),
        scratch_shapes=[pltpu.VMEM((D_MODEL, 2 * D_FF), BF16), pltpu.VMEM((D_FF, D_MODEL), BF16)],
    )
    return pl.pallas_call(
        _moe_kernel,
        grid_spec=grid_spec,
        out_shape=jax.ShapeDtypeStruct((p, D_MODEL), BF16),
        compiler_params=pltpu.CompilerParams(dimension_semantics=("arbitrary",),
                                             vmem_limit_bytes=VMEM_LIMIT),
        name="moe_experts",
    )(tile_expert, n_used, xs, w_gate_up, b_gate_up.reshape(N_EXPERTS, 1, 2 * D_FF),
      w_down, b_down.reshape(N_EXPERTS, 1, D_MODEL))


def _combine_kernel(x1_ref, yg_ref, tw_ref, mod_ref, g_ref, o_ref):
    tw = tw_ref[0]
    acc = tw[:, 0:1] * yg_ref[0, 0].astype(F32)
    for k in range(1, TOP_K):
        acc = acc + tw[:, k:k + 1] * yg_ref[k, 0].astype(F32)
    g2 = mod_ref[0, :, 5 * D_MODEL:6 * D_MODEL]
    x2 = x1_ref[0] + g2 * acc
    ms = jnp.mean(x2 * x2, axis=-1, keepdims=True)
    o_ref[0] = x2 * lax.rsqrt(ms + RMS_EPS) * g_ref[...]


def _combine_call(x1, yg, top_w, mod, norm_g):
    b, l, _ = x1.shape
    tl = 256
    per_batch = mod.shape[0] > 1
    mod_map = (lambda i, j: (i, 0, 0)) if per_batch else (lambda i, j: (0, 0, 0))
    tok = lambda w: pl.BlockSpec((1, tl, w), lambda i, j: (i, j, 0))
    return pl.pallas_call(
        _combine_kernel,
        grid=(b, l // tl),
        in_specs=[tok(D_MODEL),
                  pl.BlockSpec((TOP_K, 1, tl, D_MODEL), lambda i, j: (0, i, j, 0)),
                  tok(TOP_K),
                  pl.BlockSpec((1, 1, 6 * D_MODEL), mod_map),
                  pl.BlockSpec((1, D_MODEL), lambda i, j: (0, 0))],
        out_specs=tok(D_MODEL),
        out_shape=jax.ShapeDtypeStruct((b, l, D_MODEL), F32),
        compiler_params=pltpu.CompilerParams(dimension_semantics=("arbitrary", "arbitrary"),
                                             vmem_limit_bytes=VMEM_LIMIT),
        name="moe_combine_final_norm",
    )(x1, yg, top_w, mod, norm_g.reshape(1, D_MODEL))


def _route(top_i, rank, counts):
    t = top_i.shape[0]
    p = t * TOP_K + N_EXPERTS * MOE_TM
    n_tiles = p // MOE_TM
    padded = ((counts + MOE_TM - 1) // MOE_TM) * MOE_TM
    ends = jnp.cumsum(padded)
    starts = ends - padded
    dest = starts.at[top_i].get(mode="promise_in_bounds") + rank
    tile_start = jnp.arange(n_tiles, dtype=jnp.int32) * MOE_TM
    tile_expert = jnp.minimum(jnp.sum((ends[None, :] <= tile_start[:, None]).astype(jnp.int32), axis=1),
                              N_EXPERTS - 1)
    n_used = (ends[-1] // MOE_TM).reshape(1)
    tok = jnp.broadcast_to(jnp.arange(t, dtype=jnp.int32)[:, None], (t, TOP_K))
    spread = jnp.arange(p, dtype=jnp.int32) % t
    src_tok = spread.at[dest.reshape(-1)].set(tok.reshape(-1), unique_indices=True)
    return dest, src_tok, tile_expert.astype(jnp.int32), n_used.astype(jnp.int32)


def _layer_front(x, mod, s_ssd_f, s_ssd_b, s_s5_f, s_s5_b, grid_cols, prm):
    z, xbc, u, dt = _inproj_call(x, mod, prm["norm1_g"], prm["w_in_r"])
    y_ssd, n_ssd_f, n_ssd_b = _ssd_call(z, xbc, dt, s_ssd_f, s_ssd_b, prm["ssd"])
    y_f, n_s5_f = _s5_call(u, _s5_state_in(s_s5_f), prm["s5_f"], grid_cols, False)
    y_s5, n_s5_b = _s5_call(u, _s5_state_in(s_s5_b), prm["s5_b"], grid_cols, True,
                            y_first=y_f, glu=prm["glu"])
    x1, h2, top_i, top_w, rank, cnt = _post_call(x, y_ssd, y_s5, mod, prm["w_out"], prm["norm2_g"],
                                                 prm["w_router"], prm["b_router"])
    states = (n_ssd_f, n_ssd_b, _s5_state_out(n_s5_f), _s5_state_out(n_s5_b))
    return x1, (h2, top_i, rank, cnt), top_w, states


def kernel(x_prompt, x_sample, state_ssd_fwd, state_ssd_bwd, state_s5_fwd, state_s5_bwd, c, c_ctx, w_ada, b_ada, norm1_g, w_in, ssd_conv_w, ssd_conv_b, ssd_dt_bias_fwd, ssd_dt_bias_bwd, ssd_a_log_fwd, ssd_a_log_bwd, ssd_d, ssd_norm_g, s5_a_re_fwd, s5_a_im_fwd, s5_log_dt_fwd, s5_b_re_fwd, s5_b_im_fwd, s5_c_re_fwd, s5_c_im_fwd, s5_a_re_bwd, s5_a_im_bwd, s5_log_dt_bwd, s5_b_re_bwd, s5_b_im_bwd, s5_c_re_bwd, s5_c_im_bwd, s5_d, w_glu, b_glu, w_out, norm2_g, w_router, b_router, w_gate_up, b_gate_up, w_down, b_down, norm_f_g):
    depth = w_ada.shape[0]
    assert depth == 1, "single trunk layer"
    nb_ctx, l_ctx, _ = x_prompt.shape
    nb_lat, l_lat, _ = x_sample.shape

    w_in0 = w_in[0]
    dt_lo = SSD_WIDTH + SSD_CONV_CH
    dt_cols = jnp.pad(w_in0[:, dt_lo:dt_lo + 2 * SSD_HEADS], ((0, 0), (0, DT_PAD - 2 * SSD_HEADS)))
    w_in_r = jnp.concatenate([w_in0[:, :dt_lo], w_in0[:, dt_lo + 2 * SSD_HEADS:], dt_cols], axis=1).astype(BF16)
    pad_dt = lambda f, b: jnp.pad(jnp.concatenate([f, b]).astype(F32), (0, DT_PAD - 2 * SSD_HEADS)).reshape(1, DT_PAD)
    ssd_prm = {
        "conv_w": jnp.pad(ssd_conv_w[0].astype(F32), ((0, 5), (0, 0))),
        "conv_b": ssd_conv_b[0].astype(F32).reshape(1, SSD_CONV_CH),
        "dt_bias": pad_dt(ssd_dt_bias_fwd[0], ssd_dt_bias_bwd[0]),
        "a": pad_dt(-jnp.exp(ssd_a_log_fwd[0].astype(F32)), -jnp.exp(ssd_a_log_bwd[0].astype(F32))),
        "d_skip": jnp.repeat(ssd_d[0].astype(F32), SSD_HEAD_DIM).reshape(1, SSD_WIDTH),
        "norm_g": ssd_norm_g[0].astype(F32).reshape(1, SSD_WIDTH),
        "expand": (jnp.arange(DT_PAD)[None, :, None]
                   == (jnp.arange(SSD_WIDTH) // SSD_HEAD_DIM)[None, None, :]
                   + SSD_HEADS * jnp.arange(2)[:, None, None]).astype(BF16),
    }
    prm = {
        "norm1_g": norm1_g[0], "w_in_r": w_in_r, "ssd": ssd_prm,
        "s5_f": _s5_params(s5_a_re_fwd[0], s5_a_im_fwd[0], s5_log_dt_fwd[0], s5_b_re_fwd[0],
                           s5_b_im_fwd[0], s5_c_re_fwd[0], s5_c_im_fwd[0]),
        "s5_b": _s5_params(s5_a_re_bwd[0], s5_a_im_bwd[0], s5_log_dt_bwd[0], s5_b_re_bwd[0],
                           s5_b_im_bwd[0], s5_c_re_bwd[0], s5_c_im_bwd[0]),
        "glu": {"d": s5_d[0].astype(F32).reshape(1, S5_WIDTH), "w": w_glu[0].astype(BF16),
                "b": b_glu[0].astype(F32).reshape(1, 2 * S5_WIDTH)},
        "w_out": w_out[0].astype(BF16), "norm2_g": norm2_g[0],
        "w_router": w_router[0].astype(F32), "b_router": b_router[0].astype(F32),
    }

    conds = jnp.concatenate([c_ctx[None, :], c], axis=0)
    conds = jnp.pad(conds, ((0, (-conds.shape[0]) % 8), (0, 0)))
    mod = _ada_call(conds, w_ada[0], b_ada[0])
    mod_ctx = mod[0:1].reshape(1, 1, 6 * D_MODEL)
    mod_lat = mod[1:1 + nb_lat].reshape(nb_lat, 1, 6 * D_MODEL)

    zero_ssd = jnp.zeros((nb_ctx, SSD_HEADS, SSD_HEAD_DIM, SSD_STATE), F32)
    zero_s5 = jnp.zeros((nb_ctx, S5_GROUPS, S5_STATE, 2), F32)
    x1_c, routed_c, tw_c, st_c = _layer_front(
        x_prompt, mod_ctx, zero_ssd, zero_ssd, zero_s5, zero_s5, False, prm)
    x1_l, routed_l, tw_l, _ = _layer_front(
        x_sample, mod_lat, state_ssd_fwd[:, 0], state_ssd_bwd[:, 0], state_s5_fwd[:, 0],
        state_s5_bwd[:, 0], True, prm)

    h2_c, ti_c, rk_c, cnt_c = routed_c
    h2_l, ti_l, rk_l, cnt_l = routed_l
    t_ctx = nb_ctx * l_ctx
    n_tok = t_ctx + nb_lat * l_lat
    cnt_c = cnt_c.reshape(N_EXPERTS).astype(jnp.int32)
    cnt_l = cnt_l.reshape(N_EXPERTS).astype(jnp.int32)
    ti_l2 = ti_l.reshape(-1, TOP_K)
    top_i = jnp.concatenate([ti_c.reshape(-1, TOP_K), ti_l2], axis=0)
    rank = jnp.concatenate([rk_c.reshape(-1, TOP_K),
                            rk_l.reshape(-1, TOP_K) + cnt_c.at[ti_l2].get(mode="promise_in_bounds")], axis=0)
    dest, src_tok, tile_expert, n_used = _route(top_i, rank, cnt_c + cnt_l)
    h2 = jnp.concatenate([h2_c.reshape(-1, D_MODEL), h2_l.reshape(-1, D_MODEL),
                          jnp.zeros((max(GATHER_SRC_ROWS - n_tok, 0), D_MODEL), BF16)], axis=0)
    xs = h2.at[src_tok].get(mode="promise_in_bounds")
    ys = _moe_call(tile_expert, n_used, xs, w_gate_up[0], b_gate_up[0], w_down[0], b_down[0])

    def picked_rows(d, nb, l):
        rows = ys.at[d.T.reshape(-1)].get(mode="promise_in_bounds")
        return rows.reshape(TOP_K, nb, l, D_MODEL)

    y_prompt = _combine_call(x1_c, picked_rows(dest[:t_ctx], nb_ctx, l_ctx), tw_c, mod_ctx, norm_f_g)
    y_sample = _combine_call(x1_l, picked_rows(dest[t_ctx:], nb_lat, l_lat), tw_l, mod_lat, norm_f_g)

    ssd_f, ssd_b, s5_f, s5_b = st_c
    return (y_prompt, y_sample, ssd_f[:, None], ssd_b[:, None], s5_f[:, None], s5_b[:, None])
```

```python
import functools
import math

import jax
import jax.numpy as jnp
from jax import lax
from jax.experimental import pallas as pl
from jax.experimental.pallas import tpu as pltpu

F32 = jnp.float32
BF16 = jnp.bfloat16

D_MODEL = 1024
GRID_W = 64
SSD_WIDTH = 512
SSD_HEAD_DIM = 64
SSD_HEADS = 8
SSD_GROUPS = 2
SSD_HEADS_PER_GROUP = SSD_HEADS // SSD_GROUPS
SSD_STATE = 128
SSD_CHUNK = 128
SSD_CONV_CH = SSD_WIDTH + 2 * SSD_GROUPS * SSD_STATE
S5_WIDTH = 512
S5_CH = 16
S5_GROUPS = 32
S5_STATE = 64
N_EXPERTS = 32
TOP_K = 4
D_FF = 1024
SWIGLU_LIMIT = 7.0
SWIGLU_ALPHA = 1.702
RMS_EPS = 1e-5

LANES = 128
DT_PAD = 128
S5_BATCH = 8
S5_COLS = 2 * S5_GROUPS * S5_STATE
S5_BLK = 512
S5_SUB_STEPS = 32
S5_BLOCK_STEPS = 256
MOE_TM = 256
GATHER_SRC_ROWS = 32768
VMEM_LIMIT = 56 * 1024 * 1024


def _sigmoid(x):
    return 1.0 / (1.0 + jnp.exp(-x))


def _split3(x):
    hi = x.astype(BF16)
    r1 = x - hi.astype(F32)
    mid = r1.astype(BF16)
    lo = (r1 - mid.astype(F32)).astype(BF16)
    return hi, mid, lo


def _split2(x):
    hi = x.astype(BF16)
    return hi, (x - hi.astype(F32)).astype(BF16)


def _dot(a, b):
    return jnp.dot(a, b, preferred_element_type=F32)


def _ada_kernel(c_ref, w_ref, b_ref, o_ref):
    c = c_ref[...]
    s = c * _sigmoid(c)
    o_ref[...] = _dot(s.astype(BF16), w_ref[...].astype(BF16)) + b_ref[...]


def _ada_call(conds, w_ada, b_ada):
    n = w_ada.shape[1]
    tn = 1536
    rows = conds.shape[0]
    return pl.pallas_call(
        _ada_kernel,
        grid=(n // tn,),
        in_specs=[pl.BlockSpec((rows, D_MODEL), lambda j: (0, 0)),
                  pl.BlockSpec((D_MODEL, tn), lambda j: (0, j)),
                  pl.BlockSpec((1, tn), lambda j: (0, j))],
        out_specs=pl.BlockSpec((rows, tn), lambda j: (0, j)),
        out_shape=jax.ShapeDtypeStruct((rows, n), F32),
        compiler_params=pltpu.CompilerParams(dimension_semantics=("arbitrary",),
                                             vmem_limit_bytes=VMEM_LIMIT),
        name="ada_mod",
    )(conds, w_ada, b_ada.reshape(1, n))


def _inproj_kernel(x_ref, mod_ref, g_ref, w_ref, z_ref, xbc_ref, u_ref, dt_ref):
    x = x_ref[0]
    ms = jnp.mean(x * x, axis=-1, keepdims=True)
    y = x * lax.rsqrt(ms + RMS_EPS) * g_ref[...]
    sh = mod_ref[0, :, 0:D_MODEL]
    sc = mod_ref[0, :, D_MODEL:2 * D_MODEL]
    h = (y * (1.0 + sc) + sh).astype(BF16)
    z_ref[0] = _dot(h, w_ref[:, 0:512]).astype(BF16)
    xbc_ref[0] = _dot(h, w_ref[:, 512:1536]).astype(BF16)
    u_ref[0] = _dot(h, w_ref[:, 1536:2048])
    dt_ref[0] = _dot(h, w_ref[:, 2048:2048 + DT_PAD])


def _inproj_call(x, mod, norm_g, w_in_r):
    b, l, _ = x.shape
    tl = min(l, 512)
    per_batch = mod.shape[0] > 1
    mod_map = (lambda i, j: (i, 0, 0)) if per_batch else (lambda i, j: (0, 0, 0))
    ncol = w_in_r.shape[1]
    tok = lambda w: pl.BlockSpec((1, tl, w), lambda i, j: (i, j, 0))
    return pl.pallas_call(
        _inproj_kernel,
        grid=(b, l // tl),
        in_specs=[tok(D_MODEL),
                  pl.BlockSpec((1, 1, 6 * D_MODEL), mod_map),
                  pl.BlockSpec((1, D_MODEL), lambda i, j: (0, 0)),
                  pl.BlockSpec((D_MODEL, ncol), lambda i, j: (0, 0))],
        out_specs=[tok(SSD_WIDTH), tok(SSD_CONV_CH), tok(S5_WIDTH), tok(DT_PAD)],
        out_shape=[jax.ShapeDtypeStruct((b, l, SSD_WIDTH), BF16),
                   jax.ShapeDtypeStruct((b, l, SSD_CONV_CH), BF16),
                   jax.ShapeDtypeStruct((b, l, S5_WIDTH), F32),
                   jax.ShapeDtypeStruct((b, l, DT_PAD), F32)],
        compiler_params=pltpu.CompilerParams(dimension_semantics=("arbitrary", "arbitrary"),
                                             vmem_limit_bytes=VMEM_LIMIT),
        name="norm1_inproj",
    )(x, mod, norm_g.reshape(1, D_MODEL), w_in_r)


def _softplus(x):
    return jnp.maximum(x, 0.0) + jnp.log1p(jnp.exp(-jnp.abs(x)))


def _ssd_kernel(z_ref, xbc_ref, dt_ref, s0f_ref, s0b_ref, cw_ref, cb_ref, dtb_ref, a_ref,
                dskip_ref, ng_ref, ex_ref, y_ref, sf_ref, sb_ref,
                xc_s, dts_s, yf_s, yb_s, stf_s, stb_s, *, seq):
    q = SSD_CHUNK
    nc = seq // q
    gw = SSD_HEADS_PER_GROUP * SSD_HEAD_DIM
    row_i = lax.broadcasted_iota(jnp.int32, (q, 1), 0)

    def conv_body(c, carry):
        r0 = pl.multiple_of(c * q, q)
        cur = xbc_ref[0, pl.ds(r0, q), :].astype(F32)
        p0 = pl.multiple_of(jnp.maximum(r0 - 16, 0), 16)
        n0 = pl.multiple_of(jnp.minimum(r0 + q, seq - 16), 16)
        prev_row = xbc_ref[0, pl.ds(p0, 16), :].astype(F32)[15:16, :]
        next_row = xbc_ref[0, pl.ds(n0, 16), :].astype(F32)[0:1, :]
        prev_row = jnp.where(c > 0, prev_row, 0.0)
        next_row = jnp.where(c < nc - 1, next_row, 0.0)
        down = jnp.where(row_i == 0, prev_row, pltpu.roll(cur, 1, axis=0))
        up = jnp.where(row_i == q - 1, next_row, pltpu.roll(cur, q - 1, axis=0))
        v = cw_ref[0:1, :] * down + cw_ref[1:2, :] * cur + cw_ref[2:3, :] * up + cb_ref[...]
        xc_s[pl.ds(r0, q), :] = v * _sigmoid(v)
        dts_s[pl.ds(r0, q), :] = _softplus(dt_ref[0, pl.ds(r0, q), :] + dtb_ref[...])
        return carry

    lax.fori_loop(0, nc, conv_body, 0)

    li = lax.broadcasted_iota(jnp.int32, (q, q), 0)
    si = lax.broadcasted_iota(jnp.int32, (q, q), 1)
    head_of_lane = lax.broadcasted_iota(jnp.int32, (q, gw), 1) // SSD_HEAD_DIM

    def expand(v, e_mat):
        hi, lo = _split2(v)
        return _dot(hi, e_mat) + _dot(lo, e_mat)

    def chunk(c, direction):
        r0 = pl.multiple_of(c * q, q)
        st_ref = stf_s if direction == 0 else stb_s
        y_dst = yf_s if direction == 0 else yb_s
        e_mat = ex_ref[direction]
        causal = (li >= si) if direction == 0 else (li <= si)
        tri = jnp.where(causal, 1.0, 0.0).astype(BF16)
        dts = dts_s[pl.ds(r0, q), :]
        dta = dts * a_ref[...]
        d_hi, d_mid, d_lo = _split3(dta)
        cs = _dot(tri, d_hi) + _dot(tri, d_mid) + _dot(tri, d_lo)
        tot = cs[q - 1:q, :] if direction == 0 else cs[0:1, :]
        cs_t = cs.T
        ecs = jnp.exp(cs)
        tail = jnp.exp(tot - cs)
        dt_x = expand(dts, e_mat)
        dtw_x = expand(dts * tail, e_mat)
        etot_x = expand(jnp.broadcast_to(jnp.exp(tot), (8, DT_PAD)), e_mat)[0:1, :]
        for g in range(SSD_GROUPS):
            b_off = SSD_WIDTH + g * SSD_STATE
            c_off = SSD_WIDTH + SSD_GROUPS * SSD_STATE + g * SSD_STATE
            x_g = xc_s[pl.ds(r0, q), g * gw:(g + 1) * gw]
            bm = xc_s[pl.ds(r0, q), b_off:b_off + SSD_STATE]
            cm = xc_s[pl.ds(r0, q), c_off:c_off + SSD_STATE]
            cb = lax.dot_general(cm.astype(BF16), bm.astype(BF16), (((1,), (1,)), ((), ())),
                                 preferred_element_type=F32)
            st_g = st_ref[g]
            xdt = x_g * dt_x[:, g * gw:(g + 1) * gw]
            lhs, rhs = [], []
            for hh in range(SSD_HEADS_PER_GROUP):
                col = g * SSD_HEADS_PER_GROUP + hh + SSD_HEADS * direction
                seg = jnp.broadcast_to(cs[:, col:col + 1], (q, q)) - cs_t[col:col + 1, :]
                dec = jnp.exp(jnp.where(causal, seg, -1e30))
                lhs.append((cb * dec).astype(BF16))
                lhs.append((cm * jnp.broadcast_to(ecs[:, col:col + 1], (q, q))).astype(BF16))
                own = head_of_lane == hh
                rhs.append(jnp.where(own, xdt, 0.0).astype(BF16))
                rhs.append(jnp.where(own, st_g, 0.0).astype(BF16))
            y_dst[pl.ds(r0, q), g * gw:(g + 1) * gw] = _dot(jnp.concatenate(lhs, axis=1),
                                                             jnp.concatenate(rhs, axis=0))
            xw = (x_g * dtw_x[:, g * gw:(g + 1) * gw]).astype(BF16)
            st_ref[g] = etot_x[:, g * gw:(g + 1) * gw] * st_g + _dot(bm.T.astype(BF16), xw)

    stf_s[...] = s0f_ref[0]
    stb_s[...] = s0b_ref[0]

    def both(i, carry):
        chunk(i, 0)
        chunk(nc - 1 - i, 1)
        return carry

    lax.fori_loop(0, nc, both, 0)
    sf_ref[0] = stf_s[...]
    sb_ref[0] = stb_s[...]

    def finish(c, carry):
        r0 = pl.multiple_of(c * q, q)
        xs = xc_s[pl.ds(r0, q), 0:SSD_WIDTH]
        zz = z_ref[0, pl.ds(r0, q), :].astype(F32)
        y = (yf_s[pl.ds(r0, q), :] + yb_s[pl.ds(r0, q), :] + dskip_ref[...] * xs) * (zz * _sigmoid(zz))
        ms = jnp.mean(y * y, axis=-1, keepdims=True)
        y_ref[0, pl.ds(r0, q), :] = (y * lax.rsqrt(ms + RMS_EPS) * ng_ref[...]).astype(y_ref.dtype)
        return carry

    lax.fori_loop(0, nc, finish, 0)


def _ssd_state_in(s):
    b = s.shape[0]
    s = s.astype(F32).reshape(b, SSD_GROUPS, SSD_HEADS_PER_GROUP, SSD_HEAD_DIM, SSD_STATE)
    return s.transpose(0, 1, 4, 2, 3).reshape(b, SSD_GROUPS, SSD_STATE, SSD_HEADS_PER_GROUP * SSD_HEAD_DIM)


def _ssd_state_out(s):
    b = s.shape[0]
    s = s.reshape(b, SSD_GROUPS, SSD_STATE, SSD_HEADS_PER_GROUP, SSD_HEAD_DIM)
    return s.transpose(0, 1, 3, 4, 2).reshape(b, SSD_HEADS, SSD_HEAD_DIM, SSD_STATE)


def _ssd_call(z, xbc, dt, s0f, s0b, prm):
    b, l, _ = z.shape
    st_shape = (SSD_GROUPS, SSD_STATE, SSD_HEADS_PER_GROUP * SSD_HEAD_DIM)
    tok = lambda w: pl.BlockSpec((1, l, w), lambda i: (i, 0, 0))
    st_spec = pl.BlockSpec((1,) + st_shape, lambda i: (i, 0, 0, 0))
    par = lambda r, w: pl.BlockSpec((r, w), lambda i: (0, 0))
    y, sf, sb = pl.pallas_call(
        functools.partial(_ssd_kernel, seq=l),
        grid=(b,),
        in_specs=[tok(SSD_WIDTH), tok(SSD_CONV_CH), tok(DT_PAD), st_spec, st_spec,
                  par(8, SSD_CONV_CH), par(1, SSD_CONV_CH), par(1, DT_PAD), par(1, DT_PAD),
                  par(1, SSD_WIDTH), par(1, SSD_WIDTH),
                  pl.BlockSpec((2, DT_PAD, SSD_WIDTH), lambda i: (0, 0, 0))],
        out_specs=[tok(SSD_WIDTH), st_spec, st_spec],
        out_shape=[jax.ShapeDtypeStruct((b, l, SSD_WIDTH), BF16),
                   jax.ShapeDtypeStruct((b,) + st_shape, F32),
                   jax.ShapeDtypeStruct((b,) + st_shape, F32)],
        scratch_shapes=[pltpu.VMEM((l, SSD_CONV_CH), F32),
                        pltpu.VMEM((l, DT_PAD), F32),
                        pltpu.VMEM((l, SSD_WIDTH), F32),
                        pltpu.VMEM((l, SSD_WIDTH), F32),
                        pltpu.VMEM(st_shape, F32),
                        pltpu.VMEM(st_shape, F32)],
        compiler_params=pltpu.CompilerParams(dimension_semantics=("arbitrary",),
                                             vmem_limit_bytes=VMEM_LIMIT),
        name="ssd_mixer",
    )(z, xbc, dt, _ssd_state_in(s0f), _ssd_state_in(s0b),
      prm["conv_w"], prm["conv_b"], prm["dt_bias"], prm["a"], prm["d_skip"], prm["norm_g"], prm["expand"])
    return y, _ssd_state_out(sf), _ssd_state_out(sb)


def _gelu_tanh(x):
    return 0.5 * x * (1.0 + jnp.tanh(math.sqrt(2.0 / math.pi) * (x + 0.044715 * (x * x * x))))


def _s5_kernel(*refs, grid_cols, reverse, final):
    if final:
        (u_ref, s0_ref, lr_ref, li_ref, wb_ref, wc_ref, yf_ref, d_ref, wg_ref, bg_ref,
         o_ref, sfin_ref, ur_s, yr_s, st_s) = refs
    else:
        (u_ref, s0_ref, lr_ref, li_ref, wb_ref, wc_ref,
         o_ref, sfin_ref, ur_s, yr_s, st_s) = refs
    nb = S5_BATCH
    steps = S5_BLOCK_STEPS
    rows = steps * nb
    sub_rows = S5_SUB_STEPS * nb
    n_sub = steps // S5_SUB_STEPS
    j = pl.program_id(1)

    @pl.when(j == 0)
    def _():
        st_s[...] = s0_ref[0]

    def to_scan(v):
        v = jnp.transpose(v, (2, 1, 0, 3)) if grid_cols else jnp.swapaxes(v, 0, 1)
        return v.reshape(rows, S5_WIDTH)

    def from_scan(v):
        if grid_cols:
            return jnp.transpose(v.reshape(u_ref.shape[2], u_ref.shape[1], nb, S5_WIDTH), (2, 1, 0, 3))
        return jnp.swapaxes(v.reshape(steps, nb, S5_WIDTH), 0, 1)

    ur_s[...] = to_scan(u_ref[...])

    order = list(range(S5_SUB_STEPS))[::-1] if reverse else list(range(S5_SUB_STEPS))
    blk_per_half = S5_COLS // S5_BLK // 2

    def by_groups(v, perm):
        return jnp.concatenate([v[g * nb:(g + 1) * nb] for g in perm], axis=0)

    def sub_chunk(i, carry):
        sidx = (n_sub - 1 - i) if reverse else i
        r0 = pl.multiple_of(sidx * sub_rows, sub_rows)
        for hf in range(2):
            u_half = ur_s[pl.ds(r0, sub_rows), hf * 256:(hf + 1) * 256]
            if reverse:
                u_half = by_groups(u_half, order)
            uh = u_half.astype(BF16)
            for bq in range(blk_per_half):
                blk = hf * blk_per_half + bq
                c0 = blk * S5_BLK
                lam_r = jnp.broadcast_to(lr_ref[:, c0:c0 + S5_BLK], (nb, S5_BLK))
                lam_i = jnp.broadcast_to(li_ref[:, c0:c0 + S5_BLK], (nb, S5_BLK))
                bu = _dot(uh, wb_ref[hf, :, bq * S5_BLK:(bq + 1) * S5_BLK])
                s = st_s[:, c0:c0 + S5_BLK]
                states = []
                for k in range(S5_SUB_STEPS):
                    sw = jnp.concatenate([s[:, S5_BLK // 2:], s[:, :S5_BLK // 2]], axis=1)
                    s = lam_r * s + lam_i * sw + bu[k * nb:(k + 1) * nb, :]
                    states.append(s)
                st_s[:, c0:c0 + S5_BLK] = s
                y_blk = _dot(jnp.concatenate(states, axis=0).astype(BF16), wc_ref[blk])
                if reverse:
                    y_blk = by_groups(y_blk, order)
                yr_s[pl.ds(r0, sub_rows), blk * 64:(blk + 1) * 64] = y_blk
        return carry

    lax.fori_loop(0, n_sub, sub_chunk, 0)

    @pl.when(j == pl.num_programs(1) - 1)
    def _():
        sfin_ref[0] = st_s[...]

    if not final:
        o_ref[0] = yr_s[...]
        return

    def glu_chunk(i, carry):
        r0 = pl.multiple_of(i * sub_rows, sub_rows)
        y = (yr_s[pl.ds(r0, sub_rows), :] + yf_ref[0, pl.ds(r0, sub_rows), :]
             + d_ref[...] * ur_s[pl.ds(r0, sub_rows), :])
        y = _gelu_tanh(y).astype(BF16)
        val = _dot(y, wg_ref[:, 0:S5_WIDTH]) + bg_ref[:, 0:S5_WIDTH]
        gate = _dot(y, wg_ref[:, S5_WIDTH:2 * S5_WIDTH]) + bg_ref[:, S5_WIDTH:2 * S5_WIDTH]
        yr_s[pl.ds(r0, sub_rows), :] = val * _sigmoid(gate)
        return carry

    lax.fori_loop(0, n_sub, glu_chunk, 0)

    o_ref[...] = from_scan(yr_s[...])


def _s5_call(u, s0, prm, grid_cols, reverse, y_first=None, glu=None):
    b, l, _ = u.shape
    nbg = b // S5_BATCH
    nblk = l // S5_BLOCK_STEPS
    rows = S5_BLOCK_STEPS * S5_BATCH
    final = y_first is not None
    blk_of = (lambda j: nblk - 1 - j) if reverse else (lambda j: j)
    if grid_cols:
        n_rows = l // GRID_W
        u_in = u.reshape(b, n_rows, GRID_W, S5_WIDTH)
        cols_per_blk = S5_BLOCK_STEPS // n_rows
        tok_spec = pl.BlockSpec((S5_BATCH, n_rows, cols_per_blk, S5_WIDTH),
                                lambda i, j: (i, 0, blk_of(j), 0))
        out_nat = jax.ShapeDtypeStruct((b, n_rows, GRID_W, S5_WIDTH), F32)
    else:
        u_in = u
        tok_spec = pl.BlockSpec((S5_BATCH, S5_BLOCK_STEPS, S5_WIDTH), lambda i, j: (i, blk_of(j), 0))
        out_nat = jax.ShapeDtypeStruct((b, l, S5_WIDTH), F32)
    scan_spec = pl.BlockSpec((1, rows, S5_WIDTH), lambda i, j: (i, blk_of(j), 0))
    st_spec = pl.BlockSpec((1, S5_BATCH, S5_COLS), lambda i, j: (i, 0, 0))
    full = lambda shape: pl.BlockSpec(shape, lambda i, j: (0,) * len(shape))
    in_specs = [tok_spec, st_spec, full((1, S5_COLS)), full((1, S5_COLS)),
                full((2, 256, 2048)), full((S5_COLS // S5_BLK, S5_BLK, 64))]
    args = [u_in, s0, prm["lam_r"], prm["lam_i"], prm["wb"], prm["wc"]]
    if final:
        in_specs += [scan_spec, full((1, S5_WIDTH)), full((S5_WIDTH, 2 * S5_WIDTH)), full((1, 2 * S5_WIDTH))]
        args += [y_first, glu["d"], glu["w"], glu["b"]]
        out_specs = [tok_spec, st_spec]
        out_shape = [out_nat, jax.ShapeDtypeStruct((nbg, S5_BATCH, S5_COLS), F32)]
    else:
        out_specs = [scan_spec, st_spec]
        out_shape = [jax.ShapeDtypeStruct((nbg, l * S5_BATCH, S5_WIDTH), F32),
                     jax.ShapeDtypeStruct((nbg, S5_BATCH, S5_COLS), F32)]
    y, sfin = pl.pallas_call(
        functools.partial(_s5_kernel, grid_cols=grid_cols, reverse=reverse, final=final),
        grid=(nbg, nblk),
        in_specs=in_specs,
        out_specs=out_specs,
        out_shape=out_shape,
        scratch_shapes=[pltpu.VMEM((rows, S5_WIDTH), F32),
                        pltpu.VMEM((rows, S5_WIDTH), F32),
                        pltpu.VMEM((S5_BATCH, S5_COLS), F32)],
        compiler_params=pltpu.CompilerParams(dimension_semantics=("arbitrary", "arbitrary"),
                                             vmem_limit_bytes=VMEM_LIMIT),
        name="s5_final" if final else "s5_first",
    )(*args)
    if final:
        y = y.reshape(b, l, S5_WIDTH)
    return y, sfin


def _s5_cols(t):
    lead = t.shape[:-2]
    return t.reshape(lead + (S5_COLS // S5_BLK, S5_BLK // 2))


def _s5_state_in(s0):
    b = s0.shape[0]
    re, im = _s5_cols(s0[..., 0].astype(F32)), _s5_cols(s0[..., 1].astype(F32))
    return jnp.stack([re, im], axis=-2).reshape(b // S5_BATCH, S5_BATCH, S5_COLS)


def _s5_state_out(s):
    b = s.shape[0] * S5_BATCH
    s = s.reshape(b, S5_COLS // S5_BLK, 2, S5_BLK // 2)
    re = s[:, :, 0].reshape(b, S5_GROUPS, S5_STATE)
    im = s[:, :, 1].reshape(b, S5_GROUPS, S5_STATE)
    return jnp.stack([re, im], axis=-1)


def _s5_params(a_re, a_im, log_dt, b_re, b_im, c_re, c_im):
    a_re, a_im = a_re.astype(F32), a_im.astype(F32)
    delta = jnp.exp(log_dt.astype(F32))[:, None]
    mag = jnp.exp(a_re * delta)
    lam_re, lam_im = mag * jnp.cos(a_im * delta), mag * jnp.sin(a_im * delta)
    den = a_re * a_re + a_im * a_im
    nr = lam_re - 1.0
    f_re = (nr * a_re + lam_im * a_im) / den
    f_im = (lam_im * a_re - nr * a_im) / den
    b_re, b_im = b_re.astype(F32), b_im.astype(F32)
    bb_re = f_re[..., None] * b_re - f_im[..., None] * b_im
    bb_im = f_re[..., None] * b_im + f_im[..., None] * b_re
    nblk = S5_COLS // S5_BLK
    lr = _s5_cols(lam_re)
    li = _s5_cols(lam_im)
    lam_r = jnp.stack([lr, lr], axis=-2).reshape(1, S5_COLS)
    lam_i = jnp.stack([-li, li], axis=-2).reshape(1, S5_COLS)
    gq = S5_BLK // 2 // S5_STATE
    eye = jnp.eye(gq, dtype=F32)

    def in_block(bb):
        t = bb.reshape(nblk, gq, S5_STATE, S5_CH)
        return jnp.einsum("gh,kgpc->kgchp", eye, t).reshape(nblk, gq * S5_CH, gq * S5_STATE)

    blk_in = jnp.concatenate([in_block(bb_re), in_block(bb_im)], axis=-1)
    half = nblk // 2
    eye_h = jnp.eye(half, dtype=F32)
    wb = jnp.einsum("jk,hkcn->hjckn", eye_h, blk_in.reshape(2, half, gq * S5_CH, S5_BLK))
    wb = wb.reshape(2, half * gq * S5_CH, half * S5_BLK).astype(BF16)

    def out_block(cc):
        t = cc.astype(F32).reshape(nblk, gq, S5_CH, S5_STATE)
        return jnp.einsum("gh,kgcp->kgphc", eye, t).reshape(nblk, gq * S5_STATE, gq * S5_CH)

    wc = jnp.concatenate([out_block(c_re), -out_block(c_im)], axis=1).astype(BF16)
    return {"lam_r": lam_r, "lam_i": lam_i, "wb": wb, "wc": wc}


def _post_kernel(x_ref, ys_ref, y5_ref, mod_ref, wo_ref, g_ref, wr_ref, br_ref,
                 x1_ref, h_ref, ti_ref, tw_ref, rk_ref, cnt_ref, cnt_s):
    first = jnp.logical_and(pl.program_id(0) == 0, pl.program_id(1) == 0)

    @pl.when(first)
    def _():
        cnt_s[...] = jnp.zeros(cnt_s.shape, cnt_s.dtype)

    tl = x_ref.shape[1]
    mixed = (_dot(ys_ref[0], wo_ref[0:SSD_WIDTH, :])
             + _dot(y5_ref[0].astype(BF16), wo_ref[SSD_WIDTH:SSD_WIDTH + S5_WIDTH, :]))
    g1 = mod_ref[0, :, 2 * D_MODEL:3 * D_MODEL]
    sh2 = mod_ref[0, :, 3 * D_MODEL:4 * D_MODEL]
    sc2 = mod_ref[0, :, 4 * D_MODEL:5 * D_MODEL]
    x1 = x_ref[0] + g1 * mixed
    x1_ref[0] = x1
    ms = jnp.mean(x1 * x1, axis=-1, keepdims=True)
    h = x1 * lax.rsqrt(ms + RMS_EPS) * g_ref[...] * (1.0 + sc2) + sh2
    h_ref[0] = h.astype(BF16)

    h_hi, h_mid, _ = _split3(h)
    wr = wr_ref[...]
    w_hi, w_mid, _ = _split3(wr)
    logits = _dot(h_hi, w_hi) + _dot(h_hi, w_mid) + _dot(h_mid, w_hi) + br_ref[...]

    lane = lax.broadcasted_iota(jnp.int32, (tl, N_EXPERTS), 1).astype(F32)
    slot = lax.broadcasted_iota(jnp.int32, (tl, TOP_K), 1)
    work = logits
    onehot = jnp.zeros((tl, N_EXPERTS), F32)
    sels, vals = [], []
    top_i = jnp.zeros((tl, TOP_K), jnp.int32)
    for k in range(TOP_K):
        m = jnp.max(work, axis=-1, keepdims=True)
        idx = jnp.min(jnp.where(work == m, lane, float(N_EXPERTS)), axis=-1, keepdims=True)
        sel = lane == idx
        sels.append(sel)
        vals.append(m)
        top_i = jnp.where(slot == k, idx.astype(jnp.int32), top_i)
        onehot = jnp.where(sel, 1.0, onehot)
        work = jnp.where(sel, -jnp.inf, work)
    es = [jnp.exp(v - vals[0]) for v in vals]
    den = es[0] + es[1] + es[2] + es[3]
    top_w = jnp.zeros((tl, TOP_K), F32)
    for k in range(TOP_K):
        top_w = jnp.where(slot == k, es[k] / den, top_w)

    ri = lax.broadcasted_iota(jnp.int32, (tl, tl), 0)
    ci = lax.broadcasted_iota(jnp.int32, (tl, tl), 1)
    strict = jnp.where(ri > ci, 1.0, 0.0).astype(BF16)
    before = _dot(strict, onehot.astype(BF16)) + cnt_s[...]
    rank = jnp.zeros((tl, TOP_K), jnp.int32)
    for k in range(TOP_K):
        rk = jnp.sum(jnp.where(sels[k], before, 0.0), axis=-1, keepdims=True)
        rank = jnp.where(slot == k, rk.astype(jnp.int32), rank)
    cnt_s[...] = cnt_s[...] + jnp.sum(onehot, axis=0, keepdims=True)
    ti_ref[0] = top_i
    tw_ref[0] = top_w
    rk_ref[0] = rank
    cnt_ref[...] = cnt_s[...]


def _post_call(x, y_ssd, y_s5, mod, w_out, norm_g, w_router, b_router):
    b, l, _ = x.shape
    tl = 256
    per_batch = mod.shape[0] > 1
    mod_map = (lambda i, j: (i, 0, 0)) if per_batch else (lambda i, j: (0, 0, 0))
    tok = lambda w: pl.BlockSpec((1, tl, w), lambda i, j: (i, j, 0))
    full = lambda r, w: pl.BlockSpec((r, w), lambda i, j: (0, 0))
    return pl.pallas_call(
        _post_kernel,
        grid=(b, l // tl),
        in_specs=[tok(D_MODEL), tok(SSD_WIDTH), tok(S5_WIDTH),
                  pl.BlockSpec((1, 1, 6 * D_MODEL), mod_map),
                  full(D_MODEL, D_MODEL), full(1, D_MODEL), full(D_MODEL, N_EXPERTS),
                  full(1, N_EXPERTS)],
        out_specs=[tok(D_MODEL), tok(D_MODEL), tok(TOP_K), tok(TOP_K), tok(TOP_K), full(1, N_EXPERTS)],
        out_shape=[jax.ShapeDtypeStruct((b, l, D_MODEL), F32),
                   jax.ShapeDtypeStruct((b, l, D_MODEL), BF16),
                   jax.ShapeDtypeStruct((b, l, TOP_K), jnp.int32),
                   jax.ShapeDtypeStruct((b, l, TOP_K), F32),
                   jax.ShapeDtypeStruct((b, l, TOP_K), jnp.int32),
                   jax.ShapeDtypeStruct((1, N_EXPERTS), F32)],
        scratch_shapes=[pltpu.VMEM((1, N_EXPERTS), F32)],
        compiler_params=pltpu.CompilerParams(dimension_semantics=("arbitrary", "arbitrary"),
                                             vmem_limit_bytes=VMEM_LIMIT),
        name="outproj_norm2_router",
    )(x, y_ssd, y_s5, mod, w_out, norm_g.reshape(1, D_MODEL), w_router,
      b_router.reshape(1, N_EXPERTS))


PLAN_EXPERT, PLAN_FIRST, PLAN_NEXT, PLAN_SLOT = range(4)


def _moe_kernel(plan_ref, nu_ref, x_ref, wgu_hbm, bgu_ref, wd_hbm, bd_ref, o_ref,
                gu_f32, dn_f32, gu_s, dn_s, sem):
    i = pl.program_id(0)
    used = i < nu_ref[0]

    def weight_copies(expert, slot):
        return (pltpu.make_async_copy(wgu_hbm.at[expert], gu_f32.at[slot], sem.at[0, slot]),
                pltpu.make_async_copy(wd_hbm.at[expert], dn_f32.at[slot], sem.at[1, slot]))

    @pl.when(jnp.logical_and(used, plan_ref[PLAN_FIRST, i] == 1))
    def _():
        slot = plan_ref[PLAN_SLOT, i]
        nxt = plan_ref[PLAN_NEXT, i]

        @pl.when(i == 0)
        def _():
            for cp in weight_copies(plan_ref[PLAN_EXPERT, 0], plan_ref[PLAN_SLOT, 0]):
                cp.start()

        for cp in weight_copies(plan_ref[PLAN_EXPERT, i], slot):
            cp.wait()

        @pl.when(nxt >= 0)
        def _():
            for cp in weight_copies(nxt, 1 - slot):
                cp.start()

        for r in range(D_MODEL // 128):
            gu_s[r * 128:(r + 1) * 128, :] = gu_f32[slot, r * 128:(r + 1) * 128, :].astype(BF16)
            dn_s[r * 128:(r + 1) * 128, :] = dn_f32[slot, r * 128:(r + 1) * 128, :].astype(BF16)

    @pl.when(used)
    def _():
        x = x_ref[...]
        gate = _dot(x, gu_s[:, 0:D_FF]) + bgu_ref[0, :, 0:D_FF]
        up = _dot(x, gu_s[:, D_FF:2 * D_FF]) + bgu_ref[0, :, D_FF:2 * D_FF]
        gate = jnp.minimum(gate, SWIGLU_LIMIT)
        up = jnp.clip(up, -SWIGLU_LIMIT, SWIGLU_LIMIT)
        act = gate * _sigmoid(SWIGLU_ALPHA * gate) * (up + 1.0)
        o_ref[...] = (_dot(act.astype(BF16), dn_s[...]) + bd_ref[0]).astype(o_ref.dtype)

    @pl.when(jnp.logical_not(used))
    def _():
        o_ref[...] = jnp.zeros(o_ref.shape, o_ref.dtype)


def _moe_call(plan, n_used, xs, w_gate_up, b_gate_up, w_down, b_down):
    p = xs.shape[0]
    n_tiles = p // MOE_TM
    expert_of = lambda i, plan, nu: (plan[PLAN_EXPERT, i], 0, 0)
    grid_spec = pltpu.PrefetchScalarGridSpec(
        num_scalar_prefetch=2,
        grid=(n_tiles,),
        in_specs=[pl.BlockSpec((MOE_TM, D_MODEL), lambda i, plan, nu: (i, 0)),
                  pl.BlockSpec(memory_space=pl.ANY),
                  pl.BlockSpec((1, 1, 2 * D_FF), expert_of),
                  pl.BlockSpec(memory_space=pl.ANY),
                  pl.BlockSpec((1, 1, D_MODEL), expert_of)],
        out_specs=pl.BlockSpec((MOE_TM, D_MODEL), lambda i, plan, nu: (i, 0)),
        scratch_shapes=[pltpu.VMEM((2, D_MODEL, 2 * D_FF), F32), pltpu.VMEM((2, D_FF, D_MODEL), F32),
                        pltpu.VMEM((D_MODEL, 2 * D_FF), BF16), pltpu.VMEM((D_FF, D_MODEL), BF16),
                        pltpu.SemaphoreType.DMA((2, 2))],
    )
    return pl.pallas_call(
        _moe_kernel,
        grid_spec=grid_spec,
        out_shape=jax.ShapeDtypeStruct((p, D_MODEL), BF16),
        compiler_params=pltpu.CompilerParams(dimension_semantics=("arbitrary",),
                                             vmem_limit_bytes=VMEM_LIMIT),
        name="moe_experts",
    )(plan, n_used, xs, w_gate_up, b_gate_up.reshape(N_EXPERTS, 1, 2 * D_FF),
      w_down, b_down.reshape(N_EXPERTS, 1, D_MODEL))


def _combine_kernel(x1_ref, yg_ref, tw_ref, mod_ref, g_ref, o_ref):
    tw = tw_ref[0]
    acc = tw[:, 0:1] * yg_ref[0, 0].astype(F32)
    for k in range(1, TOP_K):
        acc = acc + tw[:, k:k + 1] * yg_ref[k, 0].astype(F32)
    g2 = mod_ref[0, :, 5 * D_MODEL:6 * D_MODEL]
    x2 = x1_ref[0] + g2 * acc
    ms = jnp.mean(x2 * x2, axis=-1, keepdims=True)
    o_ref[0] = x2 * lax.rsqrt(ms + RMS_EPS) * g_ref[...]


def _combine_call(x1, yg, top_w, mod, norm_g):
    b, l, _ = x1.shape
    tl = 256
    per_batch = mod.shape[0] > 1
    mod_map = (lambda i, j: (i, 0, 0)) if per_batch else (lambda i, j: (0, 0, 0))
    tok = lambda w: pl.BlockSpec((1, tl, w), lambda i, j: (i, j, 0))
    return pl.pallas_call(
        _combine_kernel,
        grid=(b, l // tl),
        in_specs=[tok(D_MODEL),
                  pl.BlockSpec((TOP_K, 1, tl, D_MODEL), lambda i, j: (0, i, j, 0)),
                  tok(TOP_K),
                  pl.BlockSpec((1, 1, 6 * D_MODEL), mod_map),
                  pl.BlockSpec((1, D_MODEL), lambda i, j: (0, 0))],
        out_specs=tok(D_MODEL),
        out_shape=jax.ShapeDtypeStruct((b, l, D_MODEL), F32),
        compiler_params=pltpu.CompilerParams(dimension_semantics=("arbitrary", "arbitrary"),
                                             vmem_limit_bytes=VMEM_LIMIT),
        name="moe_combine_final_norm",
    )(x1, yg, top_w, mod, norm_g.reshape(1, D_MODEL))


def _route(top_i, rank, counts):
    t = top_i.shape[0]
    p = t * TOP_K + N_EXPERTS * MOE_TM
    n_tiles = p // MOE_TM
    padded = ((counts + MOE_TM - 1) // MOE_TM) * MOE_TM
    ends = jnp.cumsum(padded)
    starts = ends - padded
    dest = starts.at[top_i].get(mode="promise_in_bounds") + rank
    tile_start = jnp.arange(n_tiles, dtype=jnp.int32) * MOE_TM
    tile_expert = jnp.minimum(jnp.sum((ends[None, :] <= tile_start[:, None]).astype(jnp.int32), axis=1),
                              N_EXPERTS - 1)
    n_used = (ends[-1] // MOE_TM).reshape(1)
    tile_id = jnp.arange(n_tiles, dtype=jnp.int32)
    prev_expert = jnp.concatenate([jnp.full((1,), -1, tile_expert.dtype), tile_expert[:-1]])
    first = ((tile_expert != prev_expert) & (tile_id < n_used[0])).astype(jnp.int32)
    slot = (jnp.cumsum(first) - 1) % 2
    eid = jnp.arange(N_EXPERTS, dtype=jnp.int32)
    later = jnp.where((eid[None, :] > eid[:, None]) & (counts[None, :] > 0), eid[None, :], N_EXPERTS)
    next_of = jnp.min(later, axis=1)
    next_of = jnp.where(next_of < N_EXPERTS, next_of, -1)
    nxt = jnp.sum(jnp.where(tile_expert[:, None] == eid[None, :], next_of[None, :], 0), axis=1)
    plan = jnp.stack([tile_expert, first, nxt, slot]).astype(jnp.int32)
    tok = jnp.broadcast_to(jnp.arange(t, dtype=jnp.int32)[:, None], (t, TOP_K))
    spread = jnp.arange(p, dtype=jnp.int32) % t
    src_tok = spread.at[dest.reshape(-1)].set(tok.reshape(-1), unique_indices=True)
    return dest, src_tok, plan, n_used.astype(jnp.int32)


def _layer_front(x, mod, s_ssd_f, s_ssd_b, s_s5_f, s_s5_b, grid_cols, prm):
    z, xbc, u, dt = _inproj_call(x, mod, prm["norm1_g"], prm["w_in_r"])
    y_ssd, n_ssd_f, n_ssd_b = _ssd_call(z, xbc, dt, s_ssd_f, s_ssd_b, prm["ssd"])
    y_f, n_s5_f = _s5_call(u, _s5_state_in(s_s5_f), prm["s5_f"], grid_cols, False)
    y_s5, n_s5_b = _s5_call(u, _s5_state_in(s_s5_b), prm["s5_b"], grid_cols, True,
                            y_first=y_f, glu=prm["glu"])
    x1, h2, top_i, top_w, rank, cnt = _post_call(x, y_ssd, y_s5, mod, prm["w_out"], prm["norm2_g"],
                                                 prm["w_router"], prm["b_router"])
    states = (n_ssd_f, n_ssd_b, _s5_state_out(n_s5_f), _s5_state_out(n_s5_b))
    return x1, (h2, top_i, rank, cnt), top_w, states


def kernel(x_prompt, x_sample, state_ssd_fwd, state_ssd_bwd, state_s5_fwd, state_s5_bwd, c, c_ctx, w_ada, b_ada, norm1_g, w_in, ssd_conv_w, ssd_conv_b, ssd_dt_bias_fwd, ssd_dt_bias_bwd, ssd_a_log_fwd, ssd_a_log_bwd, ssd_d, ssd_norm_g, s5_a_re_fwd, s5_a_im_fwd, s5_log_dt_fwd, s5_b_re_fwd, s5_b_im_fwd, s5_c_re_fwd, s5_c_im_fwd, s5_a_re_bwd, s5_a_im_bwd, s5_log_dt_bwd, s5_b_re_bwd, s5_b_im_bwd, s5_c_re_bwd, s5_c_im_bwd, s5_d, w_glu, b_glu, w_out, norm2_g, w_router, b_router, w_gate_up, b_gate_up, w_down, b_down, norm_f_g):
    depth = w_ada.shape[0]
    assert depth == 1, "single trunk layer"
    nb_ctx, l_ctx, _ = x_prompt.shape
    nb_lat, l_lat, _ = x_sample.shape

    w_in0 = w_in[0]
    dt_lo = SSD_WIDTH + SSD_CONV_CH
    dt_cols = jnp.pad(w_in0[:, dt_lo:dt_lo + 2 * SSD_HEADS], ((0, 0), (0, DT_PAD - 2 * SSD_HEADS)))
    w_in_r = jnp.concatenate([w_in0[:, :dt_lo], w_in0[:, dt_lo + 2 * SSD_HEADS:], dt_cols], axis=1).astype(BF16)
    pad_dt = lambda f, b: jnp.pad(jnp.concatenate([f, b]).astype(F32), (0, DT_PAD - 2 * SSD_HEADS)).reshape(1, DT_PAD)
    ssd_prm = {
        "conv_w": jnp.pad(ssd_conv_w[0].astype(F32), ((0, 5), (0, 0))),
        "conv_b": ssd_conv_b[0].astype(F32).reshape(1, SSD_CONV_CH),
        "dt_bias": pad_dt(ssd_dt_bias_fwd[0], ssd_dt_bias_bwd[0]),
        "a": pad_dt(-jnp.exp(ssd_a_log_fwd[0].astype(F32)), -jnp.exp(ssd_a_log_bwd[0].astype(F32))),
        "d_skip": jnp.repeat(ssd_d[0].astype(F32), SSD_HEAD_DIM).reshape(1, SSD_WIDTH),
        "norm_g": ssd_norm_g[0].astype(F32).reshape(1, SSD_WIDTH),
        "expand": (jnp.arange(DT_PAD)[None, :, None]
                   == (jnp.arange(SSD_WIDTH) // SSD_HEAD_DIM)[None, None, :]
                   + SSD_HEADS * jnp.arange(2)[:, None, None]).astype(BF16),
    }
    prm = {
        "norm1_g": norm1_g[0], "w_in_r": w_in_r, "ssd": ssd_prm,
        "s5_f": _s5_params(s5_a_re_fwd[0], s5_a_im_fwd[0], s5_log_dt_fwd[0], s5_b_re_fwd[0],
                           s5_b_im_fwd[0], s5_c_re_fwd[0], s5_c_im_fwd[0]),
        "s5_b": _s5_params(s5_a_re_bwd[0], s5_a_im_bwd[0], s5_log_dt_bwd[0], s5_b_re_bwd[0],
                           s5_b_im_bwd[0], s5_c_re_bwd[0], s5_c_im_bwd[0]),
        "glu": {"d": s5_d[0].astype(F32).reshape(1, S5_WIDTH), "w": w_glu[0].astype(BF16),
                "b": b_glu[0].astype(F32).reshape(1, 2 * S5_WIDTH)},
        "w_out": w_out[0].astype(BF16), "norm2_g": norm2_g[0],
        "w_router": w_router[0].astype(F32), "b_router": b_router[0].astype(F32),
    }

    conds = jnp.concatenate([c_ctx[None, :], c], axis=0)
    conds = jnp.pad(conds, ((0, (-conds.shape[0]) % 8), (0, 0)))
    mod = _ada_call(conds, w_ada[0], b_ada[0])
    mod_ctx = mod[0:1].reshape(1, 1, 6 * D_MODEL)
    mod_lat = mod[1:1 + nb_lat].reshape(nb_lat, 1, 6 * D_MODEL)

    zero_ssd = jnp.zeros((nb_ctx, SSD_HEADS, SSD_HEAD_DIM, SSD_STATE), F32)
    zero_s5 = jnp.zeros((nb_ctx, S5_GROUPS, S5_STATE, 2), F32)
    x1_c, routed_c, tw_c, st_c = _layer_front(
        x_prompt, mod_ctx, zero_ssd, zero_ssd, zero_s5, zero_s5, False, prm)
    x1_l, routed_l, tw_l, _ = _layer_front(
        x_sample, mod_lat, state_ssd_fwd[:, 0], state_ssd_bwd[:, 0], state_s5_fwd[:, 0],
        state_s5_bwd[:, 0], True, prm)

    h2_c, ti_c, rk_c, cnt_c = routed_c
    h2_l, ti_l, rk_l, cnt_l = routed_l
    t_ctx = nb_ctx * l_ctx
    n_tok = t_ctx + nb_lat * l_lat
    cnt_c = cnt_c.reshape(N_EXPERTS).astype(jnp.int32)
    cnt_l = cnt_l.reshape(N_EXPERTS).astype(jnp.int32)
    ti_l2 = ti_l.reshape(-1, TOP_K)
    top_i = jnp.concatenate([ti_c.reshape(-1, TOP_K), ti_l2], axis=0)
    rank = jnp.concatenate([rk_c.reshape(-1, TOP_K),
                            rk_l.reshape(-1, TOP_K) + cnt_c.at[ti_l2].get(mode="promise_in_bounds")], axis=0)
    dest, src_tok, plan, n_used = _route(top_i, rank, cnt_c + cnt_l)
    h2 = jnp.concatenate([h2_c.reshape(-1, D_MODEL), h2_l.reshape(-1, D_MODEL),
                          jnp.zeros((max(GATHER_SRC_ROWS - n_tok, 0), D_MODEL), BF16)], axis=0)
    xs = h2.at[src_tok].get(mode="promise_in_bounds")
    ys = _moe_call(plan, n_used, xs, w_gate_up[0], b_gate_up[0], w_down[0], b_down[0])

    def picked_rows(d, nb, l):
        rows = ys.at[d.T.reshape(-1)].get(mode="promise_in_bounds")
        return rows.reshape(TOP_K, nb, l, D_MODEL)

    y_prompt = _combine_call(x1_c, picked_rows(dest[:t_ctx], nb_ctx, l_ctx), tw_c, mod_ctx, norm_f_g)
    y_sample = _combine_call(x1_l, picked_rows(dest[t_ctx:], nb_lat, l_lat), tw_l, mod_lat, norm_f_g)

    ssd_f, ssd_b, s5_f, s5_b = st_c
    return (y_prompt, y_sample, ssd_f[:, None], ssd_b[:, None], s5_f[:, None], s5_b[:, None])
```

```python
import functools
import math

import jax
import jax.numpy as jnp
from jax import lax
from jax.experimental import pallas as pl
from jax.experimental.pallas import tpu as pltpu

F32 = jnp.float32
BF16 = jnp.bfloat16

D_MODEL = 1024
GRID_W = 64
SSD_WIDTH = 512
SSD_HEAD_DIM = 64
SSD_HEADS = 8
SSD_GROUPS = 2
SSD_HEADS_PER_GROUP = SSD_HEADS // SSD_GROUPS
SSD_STATE = 128
SSD_CHUNK = 128
SSD_CONV_CH = SSD_WIDTH + 2 * SSD_GROUPS * SSD_STATE
S5_WIDTH = 512
S5_CH = 16
S5_GROUPS = 32
S5_STATE = 64
N_EXPERTS = 32
TOP_K = 4
D_FF = 1024
SWIGLU_LIMIT = 7.0
SWIGLU_ALPHA = 1.702
RMS_EPS = 1e-5

LANES = 128
DT_PAD = 128
S5_BATCH = 8
S5_COLS = 2 * S5_GROUPS * S5_STATE
S5_BLK = 512
S5_SUB_STEPS = 32
S5_BLOCK_STEPS = 256
MOE_TM = 256
GATHER_SRC_ROWS = 32768
VMEM_LIMIT = 56 * 1024 * 1024


def _sigmoid(x):
    return 1.0 / (1.0 + jnp.exp(-x))


def _split3(x):
    hi = x.astype(BF16)
    r1 = x - hi.astype(F32)
    mid = r1.astype(BF16)
    lo = (r1 - mid.astype(F32)).astype(BF16)
    return hi, mid, lo


def _split2(x):
    hi = x.astype(BF16)
    return hi, (x - hi.astype(F32)).astype(BF16)


def _dot(a, b):
    return jnp.dot(a, b, preferred_element_type=F32)


def _ada_kernel(c_ref, w_ref, b_ref, o_ref):
    c = c_ref[...]
    s = c * _sigmoid(c)
    o_ref[...] = _dot(s.astype(BF16), w_ref[...].astype(BF16)) + b_ref[...]


def _ada_call(conds, w_ada, b_ada):
    n = w_ada.shape[1]
    tn = 1536
    rows = conds.shape[0]
    return pl.pallas_call(
        _ada_kernel,
        grid=(n // tn,),
        in_specs=[pl.BlockSpec((rows, D_MODEL), lambda j: (0, 0)),
                  pl.BlockSpec((D_MODEL, tn), lambda j: (0, j)),
                  pl.BlockSpec((1, tn), lambda j: (0, j))],
        out_specs=pl.BlockSpec((rows, tn), lambda j: (0, j)),
        out_shape=jax.ShapeDtypeStruct((rows, n), F32),
        compiler_params=pltpu.CompilerParams(dimension_semantics=("arbitrary",),
                                             vmem_limit_bytes=VMEM_LIMIT),
        name="ada_mod",
    )(conds, w_ada, b_ada.reshape(1, n))


def _inproj_kernel(x_ref, mod_ref, g_ref, w_ref, z_ref, xbc_ref, u_ref, dt_ref):
    x = x_ref[0]
    ms = jnp.mean(x * x, axis=-1, keepdims=True)
    y = x * lax.rsqrt(ms + RMS_EPS) * g_ref[...]
    sh = mod_ref[0, :, 0:D_MODEL]
    sc = mod_ref[0, :, D_MODEL:2 * D_MODEL]
    h = (y * (1.0 + sc) + sh).astype(BF16)
    z_ref[0] = _dot(h, w_ref[:, 0:512]).astype(BF16)
    xbc_ref[0] = _dot(h, w_ref[:, 512:1536]).astype(BF16)
    u_ref[0] = _dot(h, w_ref[:, 1536:2048])
    dt_ref[0] = _dot(h, w_ref[:, 2048:2048 + DT_PAD])


def _inproj_call(x, mod, norm_g, w_in_r):
    b, l, _ = x.shape
    tl = min(l, 512)
    per_batch = mod.shape[0] > 1
    mod_map = (lambda i, j: (i, 0, 0)) if per_batch else (lambda i, j: (0, 0, 0))
    ncol = w_in_r.shape[1]
    tok = lambda w: pl.BlockSpec((1, tl, w), lambda i, j: (i, j, 0))
    return pl.pallas_call(
        _inproj_kernel,
        grid=(b, l // tl),
        in_specs=[tok(D_MODEL),
                  pl.BlockSpec((1, 1, 6 * D_MODEL), mod_map),
                  pl.BlockSpec((1, D_MODEL), lambda i, j: (0, 0)),
                  pl.BlockSpec((D_MODEL, ncol), lambda i, j: (0, 0))],
        out_specs=[tok(SSD_WIDTH), tok(SSD_CONV_CH), tok(S5_WIDTH), tok(DT_PAD)],
        out_shape=[jax.ShapeDtypeStruct((b, l, SSD_WIDTH), BF16),
                   jax.ShapeDtypeStruct((b, l, SSD_CONV_CH), BF16),
                   jax.ShapeDtypeStruct((b, l, S5_WIDTH), F32),
                   jax.ShapeDtypeStruct((b, l, DT_PAD), F32)],
        compiler_params=pltpu.CompilerParams(dimension_semantics=("arbitrary", "arbitrary"),
                                             vmem_limit_bytes=VMEM_LIMIT),
        name="norm1_inproj",
    )(x, mod, norm_g.reshape(1, D_MODEL), w_in_r)


def _softplus(x):
    return jnp.maximum(x, 0.0) + jnp.log1p(jnp.exp(-jnp.abs(x)))


def _ssd_kernel(z_ref, xbc_ref, dt_ref, s0f_ref, s0b_ref, cw_ref, cb_ref, dtb_ref, a_ref,
                dskip_ref, ng_ref, ex_ref, y_ref, sf_ref, sb_ref,
                xc_s, dts_s, yf_s, yb_s, stf_s, stb_s, *, seq):
    q = SSD_CHUNK
    nc = seq // q
    gw = SSD_HEADS_PER_GROUP * SSD_HEAD_DIM
    row_i = lax.broadcasted_iota(jnp.int32, (q, 1), 0)

    def conv_body(c, carry):
        r0 = pl.multiple_of(c * q, q)
        cur = xbc_ref[0, pl.ds(r0, q), :].astype(F32)
        p0 = pl.multiple_of(jnp.maximum(r0 - 16, 0), 16)
        n0 = pl.multiple_of(jnp.minimum(r0 + q, seq - 16), 16)
        prev_row = xbc_ref[0, pl.ds(p0, 16), :].astype(F32)[15:16, :]
        next_row = xbc_ref[0, pl.ds(n0, 16), :].astype(F32)[0:1, :]
        prev_row = jnp.where(c > 0, prev_row, 0.0)
        next_row = jnp.where(c < nc - 1, next_row, 0.0)
        down = jnp.where(row_i == 0, prev_row, pltpu.roll(cur, 1, axis=0))
        up = jnp.where(row_i == q - 1, next_row, pltpu.roll(cur, q - 1, axis=0))
        v = cw_ref[0:1, :] * down + cw_ref[1:2, :] * cur + cw_ref[2:3, :] * up + cb_ref[...]
        xc_s[pl.ds(r0, q), :] = v * _sigmoid(v)
        dts_s[pl.ds(r0, q), :] = _softplus(dt_ref[0, pl.ds(r0, q), :] + dtb_ref[...])
        return carry

    lax.fori_loop(0, nc, conv_body, 0)

    li = lax.broadcasted_iota(jnp.int32, (q, q), 0)
    si = lax.broadcasted_iota(jnp.int32, (q, q), 1)
    head_of_lane = lax.broadcasted_iota(jnp.int32, (q, gw), 1) // SSD_HEAD_DIM

    def expand(v, e_mat):
        hi, lo = _split2(v)
        return _dot(hi, e_mat) + _dot(lo, e_mat)

    def chunk(c, direction):
        r0 = pl.multiple_of(c * q, q)
        st_ref = stf_s if direction == 0 else stb_s
        y_dst = yf_s if direction == 0 else yb_s
        e_mat = ex_ref[direction]
        causal = (li >= si) if direction == 0 else (li <= si)
        tri = jnp.where(causal, 1.0, 0.0).astype(BF16)
        dts = dts_s[pl.ds(r0, q), :]
        dta = dts * a_ref[...]
        d_hi, d_mid, d_lo = _split3(dta)
        cs = _dot(tri, d_hi) + _dot(tri, d_mid) + _dot(tri, d_lo)
        tot = cs[q - 1:q, :] if direction == 0 else cs[0:1, :]
        cs_t = cs.T
        ecs = jnp.exp(cs)
        tail = jnp.exp(tot - cs)
        dt_x = expand(dts, e_mat)
        dtw_x = expand(dts * tail, e_mat)
        etot_x = expand(jnp.broadcast_to(jnp.exp(tot), (8, DT_PAD)), e_mat)[0:1, :]
        for g in range(SSD_GROUPS):
            b_off = SSD_WIDTH + g * SSD_STATE
            c_off = SSD_WIDTH + SSD_GROUPS * SSD_STATE + g * SSD_STATE
            x_g = xc_s[pl.ds(r0, q), g * gw:(g + 1) * gw]
            bm = xc_s[pl.ds(r0, q), b_off:b_off + SSD_STATE]
            cm = xc_s[pl.ds(r0, q), c_off:c_off + SSD_STATE]
            cb = lax.dot_general(cm.astype(BF16), bm.astype(BF16), (((1,), (1,)), ((), ())),
                                 preferred_element_type=F32)
            st_g = st_ref[g]
            xdt = x_g * dt_x[:, g * gw:(g + 1) * gw]
            lhs, rhs = [], []
            for hh in range(SSD_HEADS_PER_GROUP):
                col = g * SSD_HEADS_PER_GROUP + hh + SSD_HEADS * direction
                seg = jnp.broadcast_to(cs[:, col:col + 1], (q, q)) - cs_t[col:col + 1, :]
                dec = jnp.exp(jnp.where(causal, seg, -1e30))
                lhs.append((cb * dec).astype(BF16))
                lhs.append((cm * jnp.broadcast_to(ecs[:, col:col + 1], (q, q))).astype(BF16))
                own = head_of_lane == hh
                rhs.append(jnp.where(own, xdt, 0.0).astype(BF16))
                rhs.append(jnp.where(own, st_g, 0.0).astype(BF16))
            y_dst[pl.ds(r0, q), g * gw:(g + 1) * gw] = _dot(jnp.concatenate(lhs, axis=1),
                                                             jnp.concatenate(rhs, axis=0))
            xw = (x_g * dtw_x[:, g * gw:(g + 1) * gw]).astype(BF16)
            st_ref[g] = etot_x[:, g * gw:(g + 1) * gw] * st_g + _dot(bm.T.astype(BF16), xw)

    stf_s[...] = s0f_ref[0]
    stb_s[...] = s0b_ref[0]

    def both(i, carry):
        chunk(i, 0)
        chunk(nc - 1 - i, 1)
        return carry

    lax.fori_loop(0, nc, both, 0)
    sf_ref[0] = stf_s[...]
    sb_ref[0] = stb_s[...]

    def finish(c, carry):
        r0 = pl.multiple_of(c * q, q)
        xs = xc_s[pl.ds(r0, q), 0:SSD_WIDTH]
        zz = z_ref[0, pl.ds(r0, q), :].astype(F32)
        y = (yf_s[pl.ds(r0, q), :] + yb_s[pl.ds(r0, q), :] + dskip_ref[...] * xs) * (zz * _sigmoid(zz))
        ms = jnp.mean(y * y, axis=-1, keepdims=True)
        y_ref[0, pl.ds(r0, q), :] = (y * lax.rsqrt(ms + RMS_EPS) * ng_ref[...]).astype(y_ref.dtype)
        return carry

    lax.fori_loop(0, nc, finish, 0)


def _ssd_state_in(s):
    b = s.shape[0]
    s = s.astype(F32).reshape(b, SSD_GROUPS, SSD_HEADS_PER_GROUP, SSD_HEAD_DIM, SSD_STATE)
    return s.transpose(0, 1, 4, 2, 3).reshape(b, SSD_GROUPS, SSD_STATE, SSD_HEADS_PER_GROUP * SSD_HEAD_DIM)


def _ssd_state_out(s):
    b = s.shape[0]
    s = s.reshape(b, SSD_GROUPS, SSD_STATE, SSD_HEADS_PER_GROUP, SSD_HEAD_DIM)
    return s.transpose(0, 1, 3, 4, 2).reshape(b, SSD_HEADS, SSD_HEAD_DIM, SSD_STATE)


def _ssd_call(z, xbc, dt, s0f, s0b, prm):
    b, l, _ = z.shape
    st_shape = (SSD_GROUPS, SSD_STATE, SSD_HEADS_PER_GROUP * SSD_HEAD_DIM)
    tok = lambda w: pl.BlockSpec((1, l, w), lambda i: (i, 0, 0))
    st_spec = pl.BlockSpec((1,) + st_shape, lambda i: (i, 0, 0, 0))
    par = lambda r, w: pl.BlockSpec((r, w), lambda i: (0, 0))
    y, sf, sb = pl.pallas_call(
        functools.partial(_ssd_kernel, seq=l),
        grid=(b,),
        in_specs=[tok(SSD_WIDTH), tok(SSD_CONV_CH), tok(DT_PAD), st_spec, st_spec,
                  par(8, SSD_CONV_CH), par(1, SSD_CONV_CH), par(1, DT_PAD), par(1, DT_PAD),
                  par(1, SSD_WIDTH), par(1, SSD_WIDTH),
                  pl.BlockSpec((2, DT_PAD, SSD_WIDTH), lambda i: (0, 0, 0))],
        out_specs=[tok(SSD_WIDTH), st_spec, st_spec],
        out_shape=[jax.ShapeDtypeStruct((b, l, SSD_WIDTH), BF16),
                   jax.ShapeDtypeStruct((b,) + st_shape, F32),
                   jax.ShapeDtypeStruct((b,) + st_shape, F32)],
        scratch_shapes=[pltpu.VMEM((l, SSD_CONV_CH), F32),
                        pltpu.VMEM((l, DT_PAD), F32),
                        pltpu.VMEM((l, SSD_WIDTH), F32),
                        pltpu.VMEM((l, SSD_WIDTH), F32),
                        pltpu.VMEM(st_shape, F32),
                        pltpu.VMEM(st_shape, F32)],
        compiler_params=pltpu.CompilerParams(dimension_semantics=("arbitrary",),
                                             vmem_limit_bytes=VMEM_LIMIT),
        name="ssd_mixer",
    )(z, xbc, dt, _ssd_state_in(s0f), _ssd_state_in(s0b),
      prm["conv_w"], prm["conv_b"], prm["dt_bias"], prm["a"], prm["d_skip"], prm["norm_g"], prm["expand"])
    return y, _ssd_state_out(sf), _ssd_state_out(sb)


def _gelu_tanh(x):
    return 0.5 * x * (1.0 + jnp.tanh(math.sqrt(2.0 / math.pi) * (x + 0.044715 * (x * x * x))))


def _s5_kernel(*refs, grid_cols, reverse, final):
    if final:
        (u_ref, s0_ref, lr_ref, li_ref, wb_ref, wc_ref, yf_ref, d_ref, wg_ref, bg_ref,
         o_ref, sfin_ref, ur_s, yr_s, st_s) = refs
    else:
        (u_ref, s0_ref, lr_ref, li_ref, wb_ref, wc_ref,
         o_ref, sfin_ref, ur_s, yr_s, st_s) = refs
    nb = S5_BATCH
    steps = S5_BLOCK_STEPS
    rows = steps * nb
    sub_rows = S5_SUB_STEPS * nb
    n_sub = steps // S5_SUB_STEPS
    j = pl.program_id(1)

    @pl.when(j == 0)
    def _():
        st_s[...] = s0_ref[0]

    def to_scan(v):
        v = jnp.transpose(v, (2, 1, 0, 3)) if grid_cols else jnp.swapaxes(v, 0, 1)
        return v.reshape(rows, S5_WIDTH)

    def from_scan(v):
        if grid_cols:
            return jnp.transpose(v.reshape(u_ref.shape[2], u_ref.shape[1], nb, S5_WIDTH), (2, 1, 0, 3))
        return jnp.swapaxes(v.reshape(steps, nb, S5_WIDTH), 0, 1)

    ur_s[...] = to_scan(u_ref[...])

    order = list(range(S5_SUB_STEPS))[::-1] if reverse else list(range(S5_SUB_STEPS))
    blk_per_half = S5_COLS // S5_BLK // 2

    def by_groups(v, perm):
        return jnp.concatenate([v[g * nb:(g + 1) * nb] for g in perm], axis=0)

    def sub_chunk(i, carry):
        sidx = (n_sub - 1 - i) if reverse else i
        r0 = pl.multiple_of(sidx * sub_rows, sub_rows)
        for hf in range(2):
            u_half = ur_s[pl.ds(r0, sub_rows), hf * 256:(hf + 1) * 256]
            if reverse:
                u_half = by_groups(u_half, order)
            uh = u_half.astype(BF16)
            for bq in range(blk_per_half):
                blk = hf * blk_per_half + bq
                c0 = blk * S5_BLK
                lam_r = jnp.broadcast_to(lr_ref[:, c0:c0 + S5_BLK], (nb, S5_BLK))
                lam_i = jnp.broadcast_to(li_ref[:, c0:c0 + S5_BLK], (nb, S5_BLK))
                bu = _dot(uh, wb_ref[hf, :, bq * S5_BLK:(bq + 1) * S5_BLK])
                s = st_s[:, c0:c0 + S5_BLK]
                states = []
                for k in range(S5_SUB_STEPS):
                    sw = jnp.concatenate([s[:, S5_BLK // 2:], s[:, :S5_BLK // 2]], axis=1)
                    s = lam_r * s + lam_i * sw + bu[k * nb:(k + 1) * nb, :]
                    states.append(s)
                st_s[:, c0:c0 + S5_BLK] = s
                y_blk = _dot(jnp.concatenate(states, axis=0).astype(BF16), wc_ref[blk])
                if reverse:
                    y_blk = by_groups(y_blk, order)
                yr_s[pl.ds(r0, sub_rows), blk * 64:(blk + 1) * 64] = y_blk
        return carry

    lax.fori_loop(0, n_sub, sub_chunk, 0)

    @pl.when(j == pl.num_programs(1) - 1)
    def _():
        sfin_ref[0] = st_s[...]

    if not final:
        o_ref[0] = yr_s[...]
        return

    def glu_chunk(i, carry):
        r0 = pl.multiple_of(i * sub_rows, sub_rows)
        y = (yr_s[pl.ds(r0, sub_rows), :] + yf_ref[0, pl.ds(r0, sub_rows), :]
             + d_ref[...] * ur_s[pl.ds(r0, sub_rows), :])
        y = _gelu_tanh(y).astype(BF16)
        val = _dot(y, wg_ref[:, 0:S5_WIDTH]) + bg_ref[:, 0:S5_WIDTH]
        gate = _dot(y, wg_ref[:, S5_WIDTH:2 * S5_WIDTH]) + bg_ref[:, S5_WIDTH:2 * S5_WIDTH]
        yr_s[pl.ds(r0, sub_rows), :] = val * _sigmoid(gate)
        return carry

    lax.fori_loop(0, n_sub, glu_chunk, 0)

    o_ref[...] = from_scan(yr_s[...])


def _s5_call(u, s0, prm, grid_cols, reverse, y_first=None, glu=None):
    b, l, _ = u.shape
    nbg = b // S5_BATCH
    nblk = l // S5_BLOCK_STEPS
    rows = S5_BLOCK_STEPS * S5_BATCH
    final = y_first is not None
    blk_of = (lambda j: nblk - 1 - j) if reverse else (lambda j: j)
    if grid_cols:
        n_rows = l // GRID_W
        u_in = u.reshape(b, n_rows, GRID_W, S5_WIDTH)
        cols_per_blk = S5_BLOCK_STEPS // n_rows
        tok_spec = pl.BlockSpec((S5_BATCH, n_rows, cols_per_blk, S5_WIDTH),
                                lambda i, j: (i, 0, blk_of(j), 0))
        out_nat = jax.ShapeDtypeStruct((b, n_rows, GRID_W, S5_WIDTH), F32)
    else:
        u_in = u
        tok_spec = pl.BlockSpec((S5_BATCH, S5_BLOCK_STEPS, S5_WIDTH), lambda i, j: (i, blk_of(j), 0))
        out_nat = jax.ShapeDtypeStruct((b, l, S5_WIDTH), F32)
    scan_spec = pl.BlockSpec((1, rows, S5_WIDTH), lambda i, j: (i, blk_of(j), 0))
    st_spec = pl.BlockSpec((1, S5_BATCH, S5_COLS), lambda i, j: (i, 0, 0))
    full = lambda shape: pl.BlockSpec(shape, lambda i, j: (0,) * len(shape))
    in_specs = [tok_spec, st_spec, full((1, S5_COLS)), full((1, S5_COLS)),
                full((2, 256, 2048)), full((S5_COLS // S5_BLK, S5_BLK, 64))]
    args = [u_in, s0, prm["lam_r"], prm["lam_i"], prm["wb"], prm["wc"]]
    if final:
        in_specs += [scan_spec, full((1, S5_WIDTH)), full((S5_WIDTH, 2 * S5_WIDTH)), full((1, 2 * S5_WIDTH))]
        args += [y_first, glu["d"], glu["w"], glu["b"]]
        out_specs = [tok_spec, st_spec]
        out_shape = [out_nat, jax.ShapeDtypeStruct((nbg, S5_BATCH, S5_COLS), F32)]
    else:
        out_specs = [scan_spec, st_spec]
        out_shape = [jax.ShapeDtypeStruct((nbg, l * S5_BATCH, S5_WIDTH), F32),
                     jax.ShapeDtypeStruct((nbg, S5_BATCH, S5_COLS), F32)]
    y, sfin = pl.pallas_call(
        functools.partial(_s5_kernel, grid_cols=grid_cols, reverse=reverse, final=final),
        grid=(nbg, nblk),
        in_specs=in_specs,
        out_specs=out_specs,
        out_shape=out_shape,
        scratch_shapes=[pltpu.VMEM((rows, S5_WIDTH), F32),
                        pltpu.VMEM((rows, S5_WIDTH), F32),
                        pltpu.VMEM((S5_BATCH, S5_COLS), F32)],
        compiler_params=pltpu.CompilerParams(dimension_semantics=("arbitrary", "arbitrary"),
                                             vmem_limit_bytes=VMEM_LIMIT),
        name="s5_final" if final else "s5_first",
    )(*args)
    if final:
        y = y.reshape(b, l, S5_WIDTH)
    return y, sfin


def _s5_cols(t):
    lead = t.shape[:-2]
    return t.reshape(lead + (S5_COLS // S5_BLK, S5_BLK // 2))


def _s5_state_in(s0):
    b = s0.shape[0]
    re, im = _s5_cols(s0[..., 0].astype(F32)), _s5_cols(s0[..., 1].astype(F32))
    return jnp.stack([re, im], axis=-2).reshape(b // S5_BATCH, S5_BATCH, S5_COLS)


def _s5_state_out(s):
    b = s.shape[0] * S5_BATCH
    s = s.reshape(b, S5_COLS // S5_BLK, 2, S5_BLK // 2)
    re = s[:, :, 0].reshape(b, S5_GROUPS, S5_STATE)
    im = s[:, :, 1].reshape(b, S5_GROUPS, S5_STATE)
    return jnp.stack([re, im], axis=-1)


def _s5_params(a_re, a_im, log_dt, b_re, b_im, c_re, c_im):
    a_re, a_im = a_re.astype(F32), a_im.astype(F32)
    delta = jnp.exp(log_dt.astype(F32))[:, None]
    mag = jnp.exp(a_re * delta)
    lam_re, lam_im = mag * jnp.cos(a_im * delta), mag * jnp.sin(a_im * delta)
    den = a_re * a_re + a_im * a_im
    nr = lam_re - 1.0
    f_re = (nr * a_re + lam_im * a_im) / den
    f_im = (lam_im * a_re - nr * a_im) / den
    b_re, b_im = b_re.astype(F32), b_im.astype(F32)
    bb_re = f_re[..., None] * b_re - f_im[..., None] * b_im
    bb_im = f_re[..., None] * b_im + f_im[..., None] * b_re
    nblk = S5_COLS // S5_BLK
    lr = _s5_cols(lam_re)
    li = _s5_cols(lam_im)
    lam_r = jnp.stack([lr, lr], axis=-2).reshape(1, S5_COLS)
    lam_i = jnp.stack([-li, li], axis=-2).reshape(1, S5_COLS)
    gq = S5_BLK // 2 // S5_STATE
    eye = jnp.eye(gq, dtype=F32)

    def in_block(bb):
        t = bb.reshape(nblk, gq, S5_STATE, S5_CH)
        return jnp.einsum("gh,kgpc->kgchp", eye, t).reshape(nblk, gq * S5_CH, gq * S5_STATE)

    blk_in = jnp.concatenate([in_block(bb_re), in_block(bb_im)], axis=-1)
    half = nblk // 2
    eye_h = jnp.eye(half, dtype=F32)
    wb = jnp.einsum("jk,hkcn->hjckn", eye_h, blk_in.reshape(2, half, gq * S5_CH, S5_BLK))
    wb = wb.reshape(2, half * gq * S5_CH, half * S5_BLK).astype(BF16)

    def out_block(cc):
        t = cc.astype(F32).reshape(nblk, gq, S5_CH, S5_STATE)
        return jnp.einsum("gh,kgcp->kgphc", eye, t).reshape(nblk, gq * S5_STATE, gq * S5_CH)

    wc = jnp.concatenate([out_block(c_re), -out_block(c_im)], axis=1).astype(BF16)
    return {"lam_r": lam_r, "lam_i": lam_i, "wb": wb, "wc": wc}


def _post_kernel(x_ref, ys_ref, y5_ref, mod_ref, wo_ref, g_ref, wr_ref, br_ref,
                 x1_ref, h_ref, ti_ref, tw_ref, rk_ref, cnt_ref, cnt_s):
    first = jnp.logical_and(pl.program_id(0) == 0, pl.program_id(1) == 0)

    @pl.when(first)
    def _():
        cnt_s[...] = jnp.zeros(cnt_s.shape, cnt_s.dtype)

    tl = x_ref.shape[1]
    mixed = (_dot(ys_ref[0], wo_ref[0:SSD_WIDTH, :])
             + _dot(y5_ref[0].astype(BF16), wo_ref[SSD_WIDTH:SSD_WIDTH + S5_WIDTH, :]))
    g1 = mod_ref[0, :, 2 * D_MODEL:3 * D_MODEL]
    sh2 = mod_ref[0, :, 3 * D_MODEL:4 * D_MODEL]
    sc2 = mod_ref[0, :, 4 * D_MODEL:5 * D_MODEL]
    x1 = x_ref[0] + g1 * mixed
    x1_ref[0] = x1
    ms = jnp.mean(x1 * x1, axis=-1, keepdims=True)
    h = x1 * lax.rsqrt(ms + RMS_EPS) * g_ref[...] * (1.0 + sc2) + sh2
    h_ref[0] = h.astype(BF16)

    h_hi, h_mid, _ = _split3(h)
    wr = wr_ref[...]
    w_hi, w_mid, _ = _split3(wr)
    logits = _dot(h_hi, w_hi) + _dot(h_hi, w_mid) + _dot(h_mid, w_hi) + br_ref[...]

    lane = lax.broadcasted_iota(jnp.int32, (tl, N_EXPERTS), 1).astype(F32)
    slot = lax.broadcasted_iota(jnp.int32, (tl, TOP_K), 1)
    work = logits
    onehot = jnp.zeros((tl, N_EXPERTS), F32)
    sels, vals = [], []
    top_i = jnp.zeros((tl, TOP_K), jnp.int32)
    for k in range(TOP_K):
        m = jnp.max(work, axis=-1, keepdims=True)
        idx = jnp.min(jnp.where(work == m, lane, float(N_EXPERTS)), axis=-1, keepdims=True)
        sel = lane == idx
        sels.append(sel)
        vals.append(m)
        top_i = jnp.where(slot == k, idx.astype(jnp.int32), top_i)
        onehot = jnp.where(sel, 1.0, onehot)
        work = jnp.where(sel, -jnp.inf, work)
    es = [jnp.exp(v - vals[0]) for v in vals]
    den = es[0] + es[1] + es[2] + es[3]
    top_w = jnp.zeros((tl, TOP_K), F32)
    for k in range(TOP_K):
        top_w = jnp.where(slot == k, es[k] / den, top_w)

    ri = lax.broadcasted_iota(jnp.int32, (tl, tl), 0)
    ci = lax.broadcasted_iota(jnp.int32, (tl, tl), 1)
    strict = jnp.where(ri > ci, 1.0, 0.0).astype(BF16)
    before = _dot(strict, onehot.astype(BF16)) + cnt_s[...]
    rank = jnp.zeros((tl, TOP_K), jnp.int32)
    for k in range(TOP_K):
        rk = jnp.sum(jnp.where(sels[k], before, 0.0), axis=-1, keepdims=True)
        rank = jnp.where(slot == k, rk.astype(jnp.int32), rank)
    cnt_s[...] = cnt_s[...] + jnp.sum(onehot, axis=0, keepdims=True)
    ti_ref[0] = top_i
    tw_ref[0] = top_w
    rk_ref[0] = rank
    cnt_ref[...] = cnt_s[...]


def _post_call(x, y_ssd, y_s5, mod, w_out, norm_g, w_router, b_router):
    b, l, _ = x.shape
    tl = 256
    per_batch = mod.shape[0] > 1
    mod_map = (lambda i, j: (i, 0, 0)) if per_batch else (lambda i, j: (0, 0, 0))
    tok = lambda w: pl.BlockSpec((1, tl, w), lambda i, j: (i, j, 0))
    full = lambda r, w: pl.BlockSpec((r, w), lambda i, j: (0, 0))
    return pl.pallas_call(
        _post_kernel,
        grid=(b, l // tl),
        in_specs=[tok(D_MODEL), tok(SSD_WIDTH), tok(S5_WIDTH),
                  pl.BlockSpec((1, 1, 6 * D_MODEL), mod_map),
                  full(D_MODEL, D_MODEL), full(1, D_MODEL), full(D_MODEL, N_EXPERTS),
                  full(1, N_EXPERTS)],
        out_specs=[tok(D_MODEL), tok(D_MODEL), tok(TOP_K), tok(TOP_K), tok(TOP_K), full(1, N_EXPERTS)],
        out_shape=[jax.ShapeDtypeStruct((b, l, D_MODEL), F32),
                   jax.ShapeDtypeStruct((b, l, D_MODEL), BF16),
                   jax.ShapeDtypeStruct((b, l, TOP_K), jnp.int32),
                   jax.ShapeDtypeStruct((b, l, TOP_K), F32),
                   jax.ShapeDtypeStruct((b, l, TOP_K), jnp.int32),
                   jax.ShapeDtypeStruct((1, N_EXPERTS), F32)],
        scratch_shapes=[pltpu.VMEM((1, N_EXPERTS), F32)],
        compiler_params=pltpu.CompilerParams(dimension_semantics=("arbitrary", "arbitrary"),
                                             vmem_limit_bytes=VMEM_LIMIT),
        name="outproj_norm2_router",
    )(x, y_ssd, y_s5, mod, w_out, norm_g.reshape(1, D_MODEL), w_router,
      b_router.reshape(1, N_EXPERTS))


PLAN_EXPERT, PLAN_FIRST, PLAN_NEXT, PLAN_SLOT = range(4)


def _moe_kernel(plan_ref, nu_ref, x_ref, wgu_hbm, bgu_ref, wd_hbm, bd_ref, o_ref,
                gu_f32, dn_f32, gu_s, dn_s, sem):
    i = pl.program_id(0)
    used = i < nu_ref[0]

    def weight_copies(expert, slot):
        return (pltpu.make_async_copy(wgu_hbm.at[expert], gu_f32.at[slot], sem.at[0, slot]),
                pltpu.make_async_copy(wd_hbm.at[expert], dn_f32.at[slot], sem.at[1, slot]))

    @pl.when(jnp.logical_and(used, plan_ref[PLAN_FIRST, i] == 1))
    def _():
        slot = plan_ref[PLAN_SLOT, i]
        nxt = plan_ref[PLAN_NEXT, i]

        @pl.when(i == 0)
        def _():
            for cp in weight_copies(plan_ref[PLAN_EXPERT, 0], plan_ref[PLAN_SLOT, 0]):
                cp.start()

        for cp in weight_copies(plan_ref[PLAN_EXPERT, i], slot):
            cp.wait()

        @pl.when(nxt >= 0)
        def _():
            for cp in weight_copies(nxt, 1 - slot):
                cp.start()

        for r in range(D_MODEL // 128):
            gu_s[r * 128:(r + 1) * 128, :] = gu_f32[slot, r * 128:(r + 1) * 128, :].astype(BF16)
            dn_s[r * 128:(r + 1) * 128, :] = dn_f32[slot, r * 128:(r + 1) * 128, :].astype(BF16)

    @pl.when(used)
    def _():
        x = x_ref[...]
        gate = _dot(x, gu_s[:, 0:D_FF]) + bgu_ref[0, :, 0:D_FF]
        up = _dot(x, gu_s[:, D_FF:2 * D_FF]) + bgu_ref[0, :, D_FF:2 * D_FF]
        gate = jnp.minimum(gate, SWIGLU_LIMIT)
        up = jnp.clip(up, -SWIGLU_LIMIT, SWIGLU_LIMIT)
        act = gate * _sigmoid(SWIGLU_ALPHA * gate) * (up + 1.0)
        o_ref[...] = (_dot(act.astype(BF16), dn_s[...]) + bd_ref[0]).astype(o_ref.dtype)

    @pl.when(jnp.logical_not(used))
    def _():
        o_ref[...] = jnp.zeros(o_ref.shape, o_ref.dtype)


def _moe_call(plan, n_used, xs, w_gate_up, b_gate_up, w_down, b_down):
    p = xs.shape[0]
    n_tiles = p // MOE_TM
    expert_of = lambda i, plan, nu: (plan[PLAN_EXPERT, i], 0, 0)
    grid_spec = pltpu.PrefetchScalarGridSpec(
        num_scalar_prefetch=2,
        grid=(n_tiles,),
        in_specs=[pl.BlockSpec((MOE_TM, D_MODEL), lambda i, plan, nu: (i, 0)),
                  pl.BlockSpec(memory_space=pl.ANY),
                  pl.BlockSpec((1, 1, 2 * D_FF), expert_of),
                  pl.BlockSpec(memory_space=pl.ANY),
                  pl.BlockSpec((1, 1, D_MODEL), expert_of)],
        out_specs=pl.BlockSpec((MOE_TM, D_MODEL), lambda i, plan, nu: (i, 0)),
        scratch_shapes=[pltpu.VMEM((2, D_MODEL, 2 * D_FF), F32), pltpu.VMEM((2, D_FF, D_MODEL), F32),
                        pltpu.VMEM((D_MODEL, 2 * D_FF), BF16), pltpu.VMEM((D_FF, D_MODEL), BF16),
                        pltpu.SemaphoreType.DMA((2, 2))],
    )
    return pl.pallas_call(
        _moe_kernel,
        grid_spec=grid_spec,
        out_shape=jax.ShapeDtypeStruct((p, D_MODEL), BF16),
        compiler_params=pltpu.CompilerParams(dimension_semantics=("arbitrary",),
                                             vmem_limit_bytes=VMEM_LIMIT),
        name="moe_experts",
    )(plan, n_used, xs, w_gate_up, b_gate_up.reshape(N_EXPERTS, 1, 2 * D_FF),
      w_down, b_down.reshape(N_EXPERTS, 1, D_MODEL))


def _combine_kernel(x1_ref, yg_ref, tw_ref, mod_ref, g_ref, o_ref):
    tw = tw_ref[0]
    acc = tw[:, 0:1] * yg_ref[0, 0].astype(F32)
    for k in range(1, TOP_K):
        acc = acc + tw[:, k:k + 1] * yg_ref[k, 0].astype(F32)
    g2 = mod_ref[0, :, 5 * D_MODEL:6 * D_MODEL]
    x2 = x1_ref[0] + g2 * acc
    ms = jnp.mean(x2 * x2, axis=-1, keepdims=True)
    o_ref[0] = x2 * lax.rsqrt(ms + RMS_EPS) * g_ref[...]


def _combine_call(x1, yg, top_w, mod, norm_g):
    b, l, _ = x1.shape
    tl = 256
    per_batch = mod.shape[0] > 1
    mod_map = (lambda i, j: (i, 0, 0)) if per_batch else (lambda i, j: (0, 0, 0))
    tok = lambda w: pl.BlockSpec((1, tl, w), lambda i, j: (i, j, 0))
    return pl.pallas_call(
        _combine_kernel,
        grid=(b, l // tl),
        in_specs=[tok(D_MODEL),
                  pl.BlockSpec((TOP_K, 1, tl, D_MODEL), lambda i, j: (0, i, j, 0)),
                  tok(TOP_K),
                  pl.BlockSpec((1, 1, 6 * D_MODEL), mod_map),
                  pl.BlockSpec((1, D_MODEL), lambda i, j: (0, 0))],
        out_specs=tok(D_MODEL),
        out_shape=jax.ShapeDtypeStruct((b, l, D_MODEL), F32),
        compiler_params=pltpu.CompilerParams(dimension_semantics=("arbitrary", "arbitrary"),
                                             vmem_limit_bytes=VMEM_LIMIT),
        name="moe_combine_final_norm",
    )(x1, yg, top_w, mod, norm_g.reshape(1, D_MODEL))


def _route(top_i, rank, counts):
    t = top_i.shape[0]
    p = t * TOP_K + N_EXPERTS * MOE_TM
    n_tiles = p // MOE_TM
    padded = ((counts + MOE_TM - 1) // MOE_TM) * MOE_TM
    ends = jnp.cumsum(padded)
    starts = ends - padded
    dest = starts.at[top_i].get(mode="promise_in_bounds") + rank
    tile_start = jnp.arange(n_tiles, dtype=jnp.int32) * MOE_TM
    tile_expert = jnp.minimum(jnp.sum((ends[None, :] <= tile_start[:, None]).astype(jnp.int32), axis=1),
                              N_EXPERTS - 1)
    n_used = (ends[-1] // MOE_TM).reshape(1)
    tile_id = jnp.arange(n_tiles, dtype=jnp.int32)
    prev_expert = jnp.concatenate([jnp.full((1,), -1, tile_expert.dtype), tile_expert[:-1]])
    first = ((tile_expert != prev_expert) & (tile_id < n_used[0])).astype(jnp.int32)
    slot = (jnp.cumsum(first) - 1) % 2
    eid = jnp.arange(N_EXPERTS, dtype=jnp.int32)
    later = jnp.where((eid[None, :] > eid[:, None]) & (counts[None, :] > 0), eid[None, :], N_EXPERTS)
    next_of = jnp.min(later, axis=1)
    next_of = jnp.where(next_of < N_EXPERTS, next_of, -1)
    nxt = jnp.sum(jnp.where(tile_expert[:, None] == eid[None, :], next_of[None, :], 0), axis=1)
    plan = jnp.stack([tile_expert, first, nxt, slot]).astype(jnp.int32)
    pad_lo = jnp.concatenate([starts + counts, ends[-1:]])
    pad_hi = jnp.concatenate([ends, jnp.full((1,), p, ends.dtype)])
    src_tok = _invert_call(dest.reshape(-1), jnp.stack([pad_lo, pad_hi]).astype(jnp.int32), p, t)
    return dest, src_tok, plan, n_used.astype(jnp.int32)


def _invert_kernel(dest_ref, pad_ref, src_ref, *, picks, spread_mask):
    def place(i, carry):
        src_ref[dest_ref[i]] = i // TOP_K
        return carry

    lax.fori_loop(0, picks, place, 0, unroll=8)

    def pad_range(e, carry):
        def pad_row(r, c):
            src_ref[r] = r & spread_mask
            return c

        lax.fori_loop(pad_ref[0, e], pad_ref[1, e], pad_row, 0)
        return carry

    lax.fori_loop(0, pad_ref.shape[1], pad_range, 0)


def _invert_call(dest_flat, pad_ranges, p, t):
    spread_mask = (1 << (t.bit_length() - 1)) - 1
    smem = pl.BlockSpec(memory_space=pltpu.SMEM)
    return pl.pallas_call(
        functools.partial(_invert_kernel, picks=dest_flat.shape[0], spread_mask=spread_mask),
        in_specs=[smem, smem],
        out_specs=smem,
        out_shape=jax.ShapeDtypeStruct((p,), jnp.int32),
        name="route_invert",
    )(dest_flat, pad_ranges)


def _layer_front(x, mod, s_ssd_f, s_ssd_b, s_s5_f, s_s5_b, grid_cols, prm):
    z, xbc, u, dt = _inproj_call(x, mod, prm["norm1_g"], prm["w_in_r"])
    y_ssd, n_ssd_f, n_ssd_b = _ssd_call(z, xbc, dt, s_ssd_f, s_ssd_b, prm["ssd"])
    y_f, n_s5_f = _s5_call(u, _s5_state_in(s_s5_f), prm["s5_f"], grid_cols, False)
    y_s5, n_s5_b = _s5_call(u, _s5_state_in(s_s5_b), prm["s5_b"], grid_cols, True,
                            y_first=y_f, glu=prm["glu"])
    x1, h2, top_i, top_w, rank, cnt = _post_call(x, y_ssd, y_s5, mod, prm["w_out"], prm["norm2_g"],
                                                 prm["w_router"], prm["b_router"])
    states = (n_ssd_f, n_ssd_b, _s5_state_out(n_s5_f), _s5_state_out(n_s5_b))
    return x1, (h2, top_i, rank, cnt), top_w, states


def kernel(x_prompt, x_sample, state_ssd_fwd, state_ssd_bwd, state_s5_fwd, state_s5_bwd, c, c_ctx, w_ada, b_ada, norm1_g, w_in, ssd_conv_w, ssd_conv_b, ssd_dt_bias_fwd, ssd_dt_bias_bwd, ssd_a_log_fwd, ssd_a_log_bwd, ssd_d, ssd_norm_g, s5_a_re_fwd, s5_a_im_fwd, s5_log_dt_fwd, s5_b_re_fwd, s5_b_im_fwd, s5_c_re_fwd, s5_c_im_fwd, s5_a_re_bwd, s5_a_im_bwd, s5_log_dt_bwd, s5_b_re_bwd, s5_b_im_bwd, s5_c_re_bwd, s5_c_im_bwd, s5_d, w_glu, b_glu, w_out, norm2_g, w_router, b_router, w_gate_up, b_gate_up, w_down, b_down, norm_f_g):
    depth = w_ada.shape[0]
    assert depth == 1, "single trunk layer"
    nb_ctx, l_ctx, _ = x_prompt.shape
    nb_lat, l_lat, _ = x_sample.shape

    w_in0 = w_in[0]
    dt_lo = SSD_WIDTH + SSD_CONV_CH
    dt_cols = jnp.pad(w_in0[:, dt_lo:dt_lo + 2 * SSD_HEADS], ((0, 0), (0, DT_PAD - 2 * SSD_HEADS)))
    w_in_r = jnp.concatenate([w_in0[:, :dt_lo], w_in0[:, dt_lo + 2 * SSD_HEADS:], dt_cols], axis=1).astype(BF16)
    pad_dt = lambda f, b: jnp.pad(jnp.concatenate([f, b]).astype(F32), (0, DT_PAD - 2 * SSD_HEADS)).reshape(1, DT_PAD)
    ssd_prm = {
        "conv_w": jnp.pad(ssd_conv_w[0].astype(F32), ((0, 5), (0, 0))),
        "conv_b": ssd_conv_b[0].astype(F32).reshape(1, SSD_CONV_CH),
        "dt_bias": pad_dt(ssd_dt_bias_fwd[0], ssd_dt_bias_bwd[0]),
        "a": pad_dt(-jnp.exp(ssd_a_log_fwd[0].astype(F32)), -jnp.exp(ssd_a_log_bwd[0].astype(F32))),
        "d_skip": jnp.repeat(ssd_d[0].astype(F32), SSD_HEAD_DIM).reshape(1, SSD_WIDTH),
        "norm_g": ssd_norm_g[0].astype(F32).reshape(1, SSD_WIDTH),
        "expand": (jnp.arange(DT_PAD)[None, :, None]
                   == (jnp.arange(SSD_WIDTH) // SSD_HEAD_DIM)[None, None, :]
                   + SSD_HEADS * jnp.arange(2)[:, None, None]).astype(BF16),
    }
    prm = {
        "norm1_g": norm1_g[0], "w_in_r": w_in_r, "ssd": ssd_prm,
        "s5_f": _s5_params(s5_a_re_fwd[0], s5_a_im_fwd[0], s5_log_dt_fwd[0], s5_b_re_fwd[0],
                           s5_b_im_fwd[0], s5_c_re_fwd[0], s5_c_im_fwd[0]),
        "s5_b": _s5_params(s5_a_re_bwd[0], s5_a_im_bwd[0], s5_log_dt_bwd[0], s5_b_re_bwd[0],
                           s5_b_im_bwd[0], s5_c_re_bwd[0], s5_c_im_bwd[0]),
        "glu": {"d": s5_d[0].astype(F32).reshape(1, S5_WIDTH), "w": w_glu[0].astype(BF16),
                "b": b_glu[0].astype(F32).reshape(1, 2 * S5_WIDTH)},
        "w_out": w_out[0].astype(BF16), "norm2_g": norm2_g[0],
        "w_router": w_router[0].astype(F32), "b_router": b_router[0].astype(F32),
    }

    conds = jnp.concatenate([c_ctx[None, :], c], axis=0)
    conds = jnp.pad(conds, ((0, (-conds.shape[0]) % 8), (0, 0)))
    mod = _ada_call(conds, w_ada[0], b_ada[0])
    mod_ctx = mod[0:1].reshape(1, 1, 6 * D_MODEL)
    mod_lat = mod[1:1 + nb_lat].reshape(nb_lat, 1, 6 * D_MODEL)

    zero_ssd = jnp.zeros((nb_ctx, SSD_HEADS, SSD_HEAD_DIM, SSD_STATE), F32)
    zero_s5 = jnp.zeros((nb_ctx, S5_GROUPS, S5_STATE, 2), F32)
    x1_c, routed_c, tw_c, st_c = _layer_front(
        x_prompt, mod_ctx, zero_ssd, zero_ssd, zero_s5, zero_s5, False, prm)
    x1_l, routed_l, tw_l, _ = _layer_front(
        x_sample, mod_lat, state_ssd_fwd[:, 0], state_ssd_bwd[:, 0], state_s5_fwd[:, 0],
        state_s5_bwd[:, 0], True, prm)

    h2_c, ti_c, rk_c, cnt_c = routed_c
    h2_l, ti_l, rk_l, cnt_l = routed_l
    t_ctx = nb_ctx * l_ctx
    n_tok = t_ctx + nb_lat * l_lat
    cnt_c = cnt_c.reshape(N_EXPERTS).astype(jnp.int32)
    cnt_l = cnt_l.reshape(N_EXPERTS).astype(jnp.int32)
    ti_l2 = ti_l.reshape(-1, TOP_K)
    top_i = jnp.concatenate([ti_c.reshape(-1, TOP_K), ti_l2], axis=0)
    rank = jnp.concatenate([rk_c.reshape(-1, TOP_K),
                            rk_l.reshape(-1, TOP_K) + cnt_c.at[ti_l2].get(mode="promise_in_bounds")], axis=0)
    dest, src_tok, plan, n_used = _route(top_i, rank, cnt_c + cnt_l)
    h2 = jnp.concatenate([h2_c.reshape(-1, D_MODEL), h2_l.reshape(-1, D_MODEL),
                          jnp.zeros((max(GATHER_SRC_ROWS - n_tok, 0), D_MODEL), BF16)], axis=0)
    xs = h2.at[src_tok].get(mode="promise_in_bounds")
    ys = _moe_call(plan, n_used, xs, w_gate_up[0], b_gate_up[0], w_down[0], b_down[0])

    def picked_rows(d, nb, l):
        rows = ys.at[d.T.reshape(-1)].get(mode="promise_in_bounds")
        return rows.reshape(TOP_K, nb, l, D_MODEL)

    y_prompt = _combine_call(x1_c, picked_rows(dest[:t_ctx], nb_ctx, l_ctx), tw_c, mod_ctx, norm_f_g)
    y_sample = _combine_call(x1_l, picked_rows(dest[t_ctx:], nb_lat, l_lat), tw_l, mod_lat, norm_f_g)

    ssd_f, ssd_b, s5_f, s5_b = st_c
    return (y_prompt, y_sample, ssd_f[:, None], ssd_b[:, None], s5_f[:, None], s5_b[:, None])
```

```python
import functools
import math

import jax
import jax.numpy as jnp
from jax import lax
from jax.experimental import pallas as pl
from jax.experimental.pallas import tpu as pltpu

F32 = jnp.float32
BF16 = jnp.bfloat16

D_MODEL = 1024
GRID_W = 64
SSD_WIDTH = 512
SSD_HEAD_DIM = 64
SSD_HEADS = 8
SSD_GROUPS = 2
SSD_HEADS_PER_GROUP = SSD_HEADS // SSD_GROUPS
SSD_STATE = 128
SSD_CHUNK = 128
SSD_CONV_CH = SSD_WIDTH + 2 * SSD_GROUPS * SSD_STATE
S5_WIDTH = 512
S5_CH = 16
S5_GROUPS = 32
S5_STATE = 64
N_EXPERTS = 32
TOP_K = 4
D_FF = 1024
SWIGLU_LIMIT = 7.0
SWIGLU_ALPHA = 1.702
RMS_EPS = 1e-5

LANES = 128
DT_PAD = 128
S5_BATCH = 8
S5_COLS = 2 * S5_GROUPS * S5_STATE
S5_BLK = 512
S5_SUB_STEPS = 32
S5_BLOCK_STEPS = 256
MOE_TM = 256
VMEM_LIMIT = 56 * 1024 * 1024


def _sigmoid(x):
    return 1.0 / (1.0 + jnp.exp(-x))


def _split3(x):
    hi = x.astype(BF16)
    r1 = x - hi.astype(F32)
    mid = r1.astype(BF16)
    lo = (r1 - mid.astype(F32)).astype(BF16)
    return hi, mid, lo


def _split2(x):
    hi = x.astype(BF16)
    return hi, (x - hi.astype(F32)).astype(BF16)


def _dot(a, b):
    return jnp.dot(a, b, preferred_element_type=F32)


def _ada_kernel(c_ref, w_ref, b_ref, o_ref):
    c = c_ref[...]
    s = c * _sigmoid(c)
    o_ref[...] = _dot(s.astype(BF16), w_ref[...].astype(BF16)) + b_ref[...]


def _ada_call(conds, w_ada, b_ada):
    n = w_ada.shape[1]
    tn = 1536
    rows = conds.shape[0]
    return pl.pallas_call(
        _ada_kernel,
        grid=(n // tn,),
        in_specs=[pl.BlockSpec((rows, D_MODEL), lambda j: (0, 0)),
                  pl.BlockSpec((D_MODEL, tn), lambda j: (0, j)),
                  pl.BlockSpec((1, tn), lambda j: (0, j))],
        out_specs=pl.BlockSpec((rows, tn), lambda j: (0, j)),
        out_shape=jax.ShapeDtypeStruct((rows, n), F32),
        compiler_params=pltpu.CompilerParams(dimension_semantics=("arbitrary",),
                                             vmem_limit_bytes=VMEM_LIMIT),
        name="ada_mod",
    )(conds, w_ada, b_ada.reshape(1, n))


def _inproj_kernel(x_ref, mod_ref, g_ref, w_ref, z_ref, xbc_ref, u_ref, dt_ref):
    x = x_ref[0]
    ms = jnp.mean(x * x, axis=-1, keepdims=True)
    y = x * lax.rsqrt(ms + RMS_EPS) * g_ref[...]
    sh = mod_ref[0, :, 0:D_MODEL]
    sc = mod_ref[0, :, D_MODEL:2 * D_MODEL]
    h = (y * (1.0 + sc) + sh).astype(BF16)
    z_ref[0] = _dot(h, w_ref[:, 0:512]).astype(BF16)
    xbc_ref[0] = _dot(h, w_ref[:, 512:1536]).astype(BF16)
    u_ref[0] = _dot(h, w_ref[:, 1536:2048])
    dt_ref[0] = _dot(h, w_ref[:, 2048:2048 + DT_PAD])


def _inproj_call(x, mod, norm_g, w_in_r):
    b, l, _ = x.shape
    tl = min(l, 512)
    per_batch = mod.shape[0] > 1
    mod_map = (lambda i, j: (i, 0, 0)) if per_batch else (lambda i, j: (0, 0, 0))
    ncol = w_in_r.shape[1]
    tok = lambda w: pl.BlockSpec((1, tl, w), lambda i, j: (i, j, 0))
    return pl.pallas_call(
        _inproj_kernel,
        grid=(b, l // tl),
        in_specs=[tok(D_MODEL),
                  pl.BlockSpec((1, 1, 6 * D_MODEL), mod_map),
                  pl.BlockSpec((1, D_MODEL), lambda i, j: (0, 0)),
                  pl.BlockSpec((D_MODEL, ncol), lambda i, j: (0, 0))],
        out_specs=[tok(SSD_WIDTH), tok(SSD_CONV_CH), tok(S5_WIDTH), tok(DT_PAD)],
        out_shape=[jax.ShapeDtypeStruct((b, l, SSD_WIDTH), BF16),
                   jax.ShapeDtypeStruct((b, l, SSD_CONV_CH), BF16),
                   jax.ShapeDtypeStruct((b, l, S5_WIDTH), F32),
                   jax.ShapeDtypeStruct((b, l, DT_PAD), F32)],
        compiler_params=pltpu.CompilerParams(dimension_semantics=("arbitrary", "arbitrary"),
                                             vmem_limit_bytes=VMEM_LIMIT),
        name="norm1_inproj",
    )(x, mod, norm_g.reshape(1, D_MODEL), w_in_r)


def _softplus(x):
    return jnp.maximum(x, 0.0) + jnp.log1p(jnp.exp(-jnp.abs(x)))


def _ssd_kernel(z_ref, xbc_ref, dt_ref, s0f_ref, s0b_ref, cw_ref, cb_ref, dtb_ref, a_ref,
                dskip_ref, ng_ref, ex_ref, y_ref, sf_ref, sb_ref,
                xc_s, dts_s, yf_s, yb_s, stf_s, stb_s, *, seq):
    q = SSD_CHUNK
    nc = seq // q
    gw = SSD_HEADS_PER_GROUP * SSD_HEAD_DIM
    row_i = lax.broadcasted_iota(jnp.int32, (q, 1), 0)

    def conv_body(c, carry):
        r0 = pl.multiple_of(c * q, q)
        cur = xbc_ref[0, pl.ds(r0, q), :].astype(F32)
        p0 = pl.multiple_of(jnp.maximum(r0 - 16, 0), 16)
        n0 = pl.multiple_of(jnp.minimum(r0 + q, seq - 16), 16)
        prev_row = xbc_ref[0, pl.ds(p0, 16), :].astype(F32)[15:16, :]
        next_row = xbc_ref[0, pl.ds(n0, 16), :].astype(F32)[0:1, :]
        prev_row = jnp.where(c > 0, prev_row, 0.0)
        next_row = jnp.where(c < nc - 1, next_row, 0.0)
        down = jnp.where(row_i == 0, prev_row, pltpu.roll(cur, 1, axis=0))
        up = jnp.where(row_i == q - 1, next_row, pltpu.roll(cur, q - 1, axis=0))
        v = cw_ref[0:1, :] * down + cw_ref[1:2, :] * cur + cw_ref[2:3, :] * up + cb_ref[...]
        xc_s[pl.ds(r0, q), :] = v * _sigmoid(v)
        dts_s[pl.ds(r0, q), :] = _softplus(dt_ref[0, pl.ds(r0, q), :] + dtb_ref[...])
        return carry

    lax.fori_loop(0, nc, conv_body, 0)

    li = lax.broadcasted_iota(jnp.int32, (q, q), 0)
    si = lax.broadcasted_iota(jnp.int32, (q, q), 1)
    head_of_lane = lax.broadcasted_iota(jnp.int32, (q, gw), 1) // SSD_HEAD_DIM

    def expand(v, e_mat):
        hi, lo = _split2(v)
        return _dot(hi, e_mat) + _dot(lo, e_mat)

    def chunk(c, direction):
        r0 = pl.multiple_of(c * q, q)
        st_ref = stf_s if direction == 0 else stb_s
        y_dst = yf_s if direction == 0 else yb_s
        e_mat = ex_ref[direction]
        causal = (li >= si) if direction == 0 else (li <= si)
        tri = jnp.where(causal, 1.0, 0.0).astype(BF16)
        dts = dts_s[pl.ds(r0, q), :]
        dta = dts * a_ref[...]
        d_hi, d_mid, d_lo = _split3(dta)
        cs = _dot(tri, d_hi) + _dot(tri, d_mid) + _dot(tri, d_lo)
        tot = cs[q - 1:q, :] if direction == 0 else cs[0:1, :]
        cs_t = cs.T
        ecs = jnp.exp(cs)
        tail = jnp.exp(tot - cs)
        dt_x = expand(dts, e_mat)
        dtw_x = expand(dts * tail, e_mat)
        etot_x = expand(jnp.broadcast_to(jnp.exp(tot), (8, DT_PAD)), e_mat)[0:1, :]
        for g in range(SSD_GROUPS):
            b_off = SSD_WIDTH + g * SSD_STATE
            c_off = SSD_WIDTH + SSD_GROUPS * SSD_STATE + g * SSD_STATE
            x_g = xc_s[pl.ds(r0, q), g * gw:(g + 1) * gw]
            bm = xc_s[pl.ds(r0, q), b_off:b_off + SSD_STATE]
            cm = xc_s[pl.ds(r0, q), c_off:c_off + SSD_STATE]
            cb = lax.dot_general(cm.astype(BF16), bm.astype(BF16), (((1,), (1,)), ((), ())),
                                 preferred_element_type=F32)
            st_g = st_ref[g]
            xdt = x_g * dt_x[:, g * gw:(g + 1) * gw]
            lhs, rhs = [], []
            for hh in range(SSD_HEADS_PER_GROUP):
                col = g * SSD_HEADS_PER_GROUP + hh + SSD_HEADS * direction
                seg = jnp.broadcast_to(cs[:, col:col + 1], (q, q)) - cs_t[col:col + 1, :]
                dec = jnp.exp(jnp.where(causal, seg, -1e30))
                lhs.append((cb * dec).astype(BF16))
                lhs.append((cm * jnp.broadcast_to(ecs[:, col:col + 1], (q, q))).astype(BF16))
                own = head_of_lane == hh
                rhs.append(jnp.where(own, xdt, 0.0).astype(BF16))
                rhs.append(jnp.where(own, st_g, 0.0).astype(BF16))
            y_dst[pl.ds(r0, q), g * gw:(g + 1) * gw] = _dot(jnp.concatenate(lhs, axis=1),
                                                             jnp.concatenate(rhs, axis=0))
            xw = (x_g * dtw_x[:, g * gw:(g + 1) * gw]).astype(BF16)
            st_ref[g] = etot_x[:, g * gw:(g + 1) * gw] * st_g + _dot(bm.T.astype(BF16), xw)

    stf_s[...] = s0f_ref[0]
    stb_s[...] = s0b_ref[0]

    def both(i, carry):
        chunk(i, 0)
        chunk(nc - 1 - i, 1)
        return carry

    lax.fori_loop(0, nc, both, 0)
    sf_ref[0] = stf_s[...]
    sb_ref[0] = stb_s[...]

    def finish(c, carry):
        r0 = pl.multiple_of(c * q, q)
        xs = xc_s[pl.ds(r0, q), 0:SSD_WIDTH]
        zz = z_ref[0, pl.ds(r0, q), :].astype(F32)
        y = (yf_s[pl.ds(r0, q), :] + yb_s[pl.ds(r0, q), :] + dskip_ref[...] * xs) * (zz * _sigmoid(zz))
        ms = jnp.mean(y * y, axis=-1, keepdims=True)
        y_ref[0, pl.ds(r0, q), :] = (y * lax.rsqrt(ms + RMS_EPS) * ng_ref[...]).astype(y_ref.dtype)
        return carry

    lax.fori_loop(0, nc, finish, 0)


def _ssd_state_in(s):
    b = s.shape[0]
    s = s.astype(F32).reshape(b, SSD_GROUPS, SSD_HEADS_PER_GROUP, SSD_HEAD_DIM, SSD_STATE)
    return s.transpose(0, 1, 4, 2, 3).reshape(b, SSD_GROUPS, SSD_STATE, SSD_HEADS_PER_GROUP * SSD_HEAD_DIM)


def _ssd_state_out(s):
    b = s.shape[0]
    s = s.reshape(b, SSD_GROUPS, SSD_STATE, SSD_HEADS_PER_GROUP, SSD_HEAD_DIM)
    return s.transpose(0, 1, 3, 4, 2).reshape(b, SSD_HEADS, SSD_HEAD_DIM, SSD_STATE)


def _ssd_call(z, xbc, dt, s0f, s0b, prm):
    b, l, _ = z.shape
    st_shape = (SSD_GROUPS, SSD_STATE, SSD_HEADS_PER_GROUP * SSD_HEAD_DIM)
    tok = lambda w: pl.BlockSpec((1, l, w), lambda i: (i, 0, 0))
    st_spec = pl.BlockSpec((1,) + st_shape, lambda i: (i, 0, 0, 0))
    par = lambda r, w: pl.BlockSpec((r, w), lambda i: (0, 0))
    y, sf, sb = pl.pallas_call(
        functools.partial(_ssd_kernel, seq=l),
        grid=(b,),
        in_specs=[tok(SSD_WIDTH), tok(SSD_CONV_CH), tok(DT_PAD), st_spec, st_spec,
                  par(8, SSD_CONV_CH), par(1, SSD_CONV_CH), par(1, DT_PAD), par(1, DT_PAD),
                  par(1, SSD_WIDTH), par(1, SSD_WIDTH),
                  pl.BlockSpec((2, DT_PAD, SSD_WIDTH), lambda i: (0, 0, 0))],
        out_specs=[tok(SSD_WIDTH), st_spec, st_spec],
        out_shape=[jax.ShapeDtypeStruct((b, l, SSD_WIDTH), BF16),
                   jax.ShapeDtypeStruct((b,) + st_shape, F32),
                   jax.ShapeDtypeStruct((b,) + st_shape, F32)],
        scratch_shapes=[pltpu.VMEM((l, SSD_CONV_CH), F32),
                        pltpu.VMEM((l, DT_PAD), F32),
                        pltpu.VMEM((l, SSD_WIDTH), F32),
                        pltpu.VMEM((l, SSD_WIDTH), F32),
                        pltpu.VMEM(st_shape, F32),
                        pltpu.VMEM(st_shape, F32)],
        compiler_params=pltpu.CompilerParams(dimension_semantics=("arbitrary",),
                                             vmem_limit_bytes=VMEM_LIMIT),
        name="ssd_mixer",
    )(z, xbc, dt, _ssd_state_in(s0f), _ssd_state_in(s0b),
      prm["conv_w"], prm["conv_b"], prm["dt_bias"], prm["a"], prm["d_skip"], prm["norm_g"], prm["expand"])
    return y, _ssd_state_out(sf), _ssd_state_out(sb)


def _gelu_tanh(x):
    return 0.5 * x * (1.0 + jnp.tanh(math.sqrt(2.0 / math.pi) * (x + 0.044715 * (x * x * x))))


def _s5_kernel(*refs, grid_cols, reverse, final):
    if final:
        (u_ref, s0_ref, lr_ref, li_ref, wb_ref, wc_ref, yf_ref, d_ref, wg_ref, bg_ref,
         o_ref, sfin_ref, ur_s, yr_s, st_s) = refs
    else:
        (u_ref, s0_ref, lr_ref, li_ref, wb_ref, wc_ref,
         o_ref, sfin_ref, ur_s, yr_s, st_s) = refs
    nb = S5_BATCH
    steps = S5_BLOCK_STEPS
    rows = steps * nb
    sub_rows = S5_SUB_STEPS * nb
    n_sub = steps // S5_SUB_STEPS
    j = pl.program_id(1)

    @pl.when(j == 0)
    def _():
        st_s[...] = s0_ref[0]

    def to_scan(v):
        v = jnp.transpose(v, (2, 1, 0, 3)) if grid_cols else jnp.swapaxes(v, 0, 1)
        return v.reshape(rows, S5_WIDTH)

    def from_scan(v):
        if grid_cols:
            return jnp.transpose(v.reshape(u_ref.shape[2], u_ref.shape[1], nb, S5_WIDTH), (2, 1, 0, 3))
        return jnp.swapaxes(v.reshape(steps, nb, S5_WIDTH), 0, 1)

    ur_s[...] = to_scan(u_ref[...])

    order = list(range(S5_SUB_STEPS))[::-1] if reverse else list(range(S5_SUB_STEPS))
    blk_per_half = S5_COLS // S5_BLK // 2

    def by_groups(v, perm):
        return jnp.concatenate([v[g * nb:(g + 1) * nb] for g in perm], axis=0)

    def sub_chunk(i, carry):
        sidx = (n_sub - 1 - i) if reverse else i
        r0 = pl.multiple_of(sidx * sub_rows, sub_rows)
        for hf in range(2):
            u_half = ur_s[pl.ds(r0, sub_rows), hf * 256:(hf + 1) * 256]
            if reverse:
                u_half = by_groups(u_half, order)
            uh = u_half.astype(BF16)
            for bq in range(blk_per_half):
                blk = hf * blk_per_half + bq
                c0 = blk * S5_BLK
                lam_r = jnp.broadcast_to(lr_ref[:, c0:c0 + S5_BLK], (nb, S5_BLK))
                lam_i = jnp.broadcast_to(li_ref[:, c0:c0 + S5_BLK], (nb, S5_BLK))
                bu = _dot(uh, wb_ref[hf, :, bq * S5_BLK:(bq + 1) * S5_BLK])
                s = st_s[:, c0:c0 + S5_BLK]
                states = []
                for k in range(S5_SUB_STEPS):
                    sw = jnp.concatenate([s[:, S5_BLK // 2:], s[:, :S5_BLK // 2]], axis=1)
                    s = lam_r * s + lam_i * sw + bu[k * nb:(k + 1) * nb, :]
                    states.append(s)
                st_s[:, c0:c0 + S5_BLK] = s
                y_blk = _dot(jnp.concatenate(states, axis=0).astype(BF16), wc_ref[blk])
                if reverse:
                    y_blk = by_groups(y_blk, order)
                yr_s[pl.ds(r0, sub_rows), blk * 64:(blk + 1) * 64] = y_blk
        return carry

    lax.fori_loop(0, n_sub, sub_chunk, 0)

    @pl.when(j == pl.num_programs(1) - 1)
    def _():
        sfin_ref[0] = st_s[...]

    if not final:
        o_ref[0] = yr_s[...]
        return

    def glu_chunk(i, carry):
        r0 = pl.multiple_of(i * sub_rows, sub_rows)
        y = (yr_s[pl.ds(r0, sub_rows), :] + yf_ref[0, pl.ds(r0, sub_rows), :]
             + d_ref[...] * ur_s[pl.ds(r0, sub_rows), :])
        y = _gelu_tanh(y).astype(BF16)
        val = _dot(y, wg_ref[:, 0:S5_WIDTH]) + bg_ref[:, 0:S5_WIDTH]
        gate = _dot(y, wg_ref[:, S5_WIDTH:2 * S5_WIDTH]) + bg_ref[:, S5_WIDTH:2 * S5_WIDTH]
        yr_s[pl.ds(r0, sub_rows), :] = val * _sigmoid(gate)
        return carry

    lax.fori_loop(0, n_sub, glu_chunk, 0)

    o_ref[...] = from_scan(yr_s[...])


def _s5_call(u, s0, prm, grid_cols, reverse, y_first=None, glu=None):
    b, l, _ = u.shape
    nbg = b // S5_BATCH
    nblk = l // S5_BLOCK_STEPS
    rows = S5_BLOCK_STEPS * S5_BATCH
    final = y_first is not None
    blk_of = (lambda j: nblk - 1 - j) if reverse else (lambda j: j)
    if grid_cols:
        n_rows = l // GRID_W
        u_in = u.reshape(b, n_rows, GRID_W, S5_WIDTH)
        cols_per_blk = S5_BLOCK_STEPS // n_rows
        tok_spec = pl.BlockSpec((S5_BATCH, n_rows, cols_per_blk, S5_WIDTH),
                                lambda i, j: (i, 0, blk_of(j), 0))
        out_nat = jax.ShapeDtypeStruct((b, n_rows, GRID_W, S5_WIDTH), F32)
    else:
        u_in = u
        tok_spec = pl.BlockSpec((S5_BATCH, S5_BLOCK_STEPS, S5_WIDTH), lambda i, j: (i, blk_of(j), 0))
        out_nat = jax.ShapeDtypeStruct((b, l, S5_WIDTH), F32)
    scan_spec = pl.BlockSpec((1, rows, S5_WIDTH), lambda i, j: (i, blk_of(j), 0))
    st_spec = pl.BlockSpec((1, S5_BATCH, S5_COLS), lambda i, j: (i, 0, 0))
    full = lambda shape: pl.BlockSpec(shape, lambda i, j: (0,) * len(shape))
    in_specs = [tok_spec, st_spec, full((1, S5_COLS)), full((1, S5_COLS)),
                full((2, 256, 2048)), full((S5_COLS // S5_BLK, S5_BLK, 64))]
    args = [u_in, s0, prm["lam_r"], prm["lam_i"], prm["wb"], prm["wc"]]
    if final:
        in_specs += [scan_spec, full((1, S5_WIDTH)), full((S5_WIDTH, 2 * S5_WIDTH)), full((1, 2 * S5_WIDTH))]
        args += [y_first, glu["d"], glu["w"], glu["b"]]
        out_specs = [tok_spec, st_spec]
        out_shape = [out_nat, jax.ShapeDtypeStruct((nbg, S5_BATCH, S5_COLS), F32)]
    else:
        out_specs = [scan_spec, st_spec]
        out_shape = [jax.ShapeDtypeStruct((nbg, l * S5_BATCH, S5_WIDTH), F32),
                     jax.ShapeDtypeStruct((nbg, S5_BATCH, S5_COLS), F32)]
    y, sfin = pl.pallas_call(
        functools.partial(_s5_kernel, grid_cols=grid_cols, reverse=reverse, final=final),
        grid=(nbg, nblk),
        in_specs=in_specs,
        out_specs=out_specs,
        out_shape=out_shape,
        scratch_shapes=[pltpu.VMEM((rows, S5_WIDTH), F32),
                        pltpu.VMEM((rows, S5_WIDTH), F32),
                        pltpu.VMEM((S5_BATCH, S5_COLS), F32)],
        compiler_params=pltpu.CompilerParams(dimension_semantics=("arbitrary", "arbitrary"),
                                             vmem_limit_bytes=VMEM_LIMIT),
        name="s5_final" if final else "s5_first",
    )(*args)
    if final:
        y = y.reshape(b, l, S5_WIDTH)
    return y, sfin


def _s5_cols(t):
    lead = t.shape[:-2]
    return t.reshape(lead + (S5_COLS // S5_BLK, S5_BLK // 2))


def _s5_state_in(s0):
    b = s0.shape[0]
    re, im = _s5_cols(s0[..., 0].astype(F32)), _s5_cols(s0[..., 1].astype(F32))
    return jnp.stack([re, im], axis=-2).reshape(b // S5_BATCH, S5_BATCH, S5_COLS)


def _s5_state_out(s):
    b = s.shape[0] * S5_BATCH
    s = s.reshape(b, S5_COLS // S5_BLK, 2, S5_BLK // 2)
    re = s[:, :, 0].reshape(b, S5_GROUPS, S5_STATE)
    im = s[:, :, 1].reshape(b, S5_GROUPS, S5_STATE)
    return jnp.stack([re, im], axis=-1)


def _s5_params(a_re, a_im, log_dt, b_re, b_im, c_re, c_im):
    a_re, a_im = a_re.astype(F32), a_im.astype(F32)
    delta = jnp.exp(log_dt.astype(F32))[:, None]
    mag = jnp.exp(a_re * delta)
    lam_re, lam_im = mag * jnp.cos(a_im * delta), mag * jnp.sin(a_im * delta)
    den = a_re * a_re + a_im * a_im
    nr = lam_re - 1.0
    f_re = (nr * a_re + lam_im * a_im) / den
    f_im = (lam_im * a_re - nr * a_im) / den
    b_re, b_im = b_re.astype(F32), b_im.astype(F32)
    bb_re = f_re[..., None] * b_re - f_im[..., None] * b_im
    bb_im = f_re[..., None] * b_im + f_im[..., None] * b_re
    nblk = S5_COLS // S5_BLK
    lr = _s5_cols(lam_re)
    li = _s5_cols(lam_im)
    lam_r = jnp.stack([lr, lr], axis=-2).reshape(1, S5_COLS)
    lam_i = jnp.stack([-li, li], axis=-2).reshape(1, S5_COLS)
    gq = S5_BLK // 2 // S5_STATE
    eye = jnp.eye(gq, dtype=F32)

    def in_block(bb):
        t = bb.reshape(nblk, gq, S5_STATE, S5_CH)
        return jnp.einsum("gh,kgpc->kgchp", eye, t).reshape(nblk, gq * S5_CH, gq * S5_STATE)

    blk_in = jnp.concatenate([in_block(bb_re), in_block(bb_im)], axis=-1)
    half = nblk // 2
    eye_h = jnp.eye(half, dtype=F32)
    wb = jnp.einsum("jk,hkcn->hjckn", eye_h, blk_in.reshape(2, half, gq * S5_CH, S5_BLK))
    wb = wb.reshape(2, half * gq * S5_CH, half * S5_BLK).astype(BF16)

    def out_block(cc):
        t = cc.astype(F32).reshape(nblk, gq, S5_CH, S5_STATE)
        return jnp.einsum("gh,kgcp->kgphc", eye, t).reshape(nblk, gq * S5_STATE, gq * S5_CH)

    wc = jnp.concatenate([out_block(c_re), -out_block(c_im)], axis=1).astype(BF16)
    return {"lam_r": lam_r, "lam_i": lam_i, "wb": wb, "wc": wc}


def _post_kernel(*refs, tile_base, prev_rows):
    if prev_rows:
        prev_hbm, refs = refs[8], refs[:8] + refs[9:]
    (x_ref, ys_ref, y5_ref, mod_ref, wo_ref, g_ref, wr_ref, br_ref,
     x1_ref, xs_hbm, ti_ref, tw_ref, rk_ref, ct_ref, cnt_ref, cnt_s, xs_v, sem_out, sem_prev) = refs
    tl = x_ref.shape[1]
    blk = tl * TOP_K
    n_steps = pl.num_programs(0) * pl.num_programs(1)
    step = pl.program_id(0) * pl.num_programs(1) + pl.program_id(1)
    buf = step % 2
    first = step == 0

    def block_copy(at_step, at_slot):
        row0 = pl.multiple_of((tile_base + at_step) * blk, blk)
        return pltpu.make_async_copy(xs_v.at[at_slot], xs_hbm.at[pl.ds(row0, blk)], sem_out.at[at_slot])

    def prev_copy():
        return pltpu.make_async_copy(prev_hbm, xs_hbm.at[pl.ds(0, prev_rows)], sem_prev.at[0])

    @pl.when(step >= 2)
    def _():
        block_copy(step - 2, buf).wait()

    if prev_rows:
        @pl.when(first)
        def _():
            prev_copy().start()

    @pl.when(first)
    def _():
        cnt_s[...] = jnp.zeros(cnt_s.shape, cnt_s.dtype)

    mixed = (_dot(ys_ref[0], wo_ref[0:SSD_WIDTH, :])
             + _dot(y5_ref[0].astype(BF16), wo_ref[SSD_WIDTH:SSD_WIDTH + S5_WIDTH, :]))
    g1 = mod_ref[0, :, 2 * D_MODEL:3 * D_MODEL]
    sh2 = mod_ref[0, :, 3 * D_MODEL:4 * D_MODEL]
    sc2 = mod_ref[0, :, 4 * D_MODEL:5 * D_MODEL]
    x1 = x_ref[0] + g1 * mixed
    x1_ref[0] = x1
    ms = jnp.mean(x1 * x1, axis=-1, keepdims=True)
    h = x1 * lax.rsqrt(ms + RMS_EPS) * g_ref[...] * (1.0 + sc2) + sh2

    h_hi, h_mid, _ = _split3(h)
    wr = wr_ref[...]
    w_hi, w_mid, _ = _split3(wr)
    logits = _dot(h_hi, w_hi) + _dot(h_hi, w_mid) + _dot(h_mid, w_hi) + br_ref[...]

    lane = lax.broadcasted_iota(jnp.int32, (tl, N_EXPERTS), 1).astype(F32)
    slot = lax.broadcasted_iota(jnp.int32, (tl, TOP_K), 1)
    work = logits
    onehot = jnp.zeros((tl, N_EXPERTS), F32)
    sels, vals = [], []
    top_i = jnp.zeros((tl, TOP_K), jnp.int32)
    for k in range(TOP_K):
        m = jnp.max(work, axis=-1, keepdims=True)
        idx = jnp.min(jnp.where(work == m, lane, float(N_EXPERTS)), axis=-1, keepdims=True)
        sel = lane == idx
        sels.append(sel)
        vals.append(m)
        top_i = jnp.where(slot == k, idx.astype(jnp.int32), top_i)
        onehot = jnp.where(sel, 1.0, onehot)
        work = jnp.where(sel, -jnp.inf, work)
    es = [jnp.exp(v - vals[0]) for v in vals]
    den = es[0] + es[1] + es[2] + es[3]
    top_w = jnp.zeros((tl, TOP_K), F32)
    for k in range(TOP_K):
        top_w = jnp.where(slot == k, es[k] / den, top_w)

    ri = lax.broadcasted_iota(jnp.int32, (tl, tl), 0)
    ci = lax.broadcasted_iota(jnp.int32, (tl, tl), 1)
    strict = jnp.where(ri > ci, 1.0, 0.0).astype(BF16)
    local = _dot(strict, onehot.astype(BF16))
    before = local + cnt_s[...]
    cnt_tile = jnp.sum(onehot, axis=0, keepdims=True)
    er = lax.broadcasted_iota(jnp.int32, (N_EXPERTS, N_EXPERTS), 0)
    ec = lax.broadcasted_iota(jnp.int32, (N_EXPERTS, N_EXPERTS), 1)
    upper = jnp.where(er < ec, 1.0, 0.0).astype(BF16)
    off = _dot(jnp.broadcast_to(cnt_tile, (8, N_EXPERTS)).astype(BF16), upper)[0:1, :]
    pos = off + local
    lane128 = lax.broadcasted_iota(jnp.int32, (tl, LANES), 1)
    rank = jnp.zeros((tl, TOP_K), jnp.int32)
    pos_cols = jnp.zeros((tl, LANES), F32)
    for k in range(TOP_K):
        rk = jnp.sum(jnp.where(sels[k], before, 0.0), axis=-1, keepdims=True)
        rank = jnp.where(slot == k, rk.astype(jnp.int32), rank)
        pk = jnp.sum(jnp.where(sels[k], pos, 0.0), axis=-1, keepdims=True)
        pos_cols = jnp.where(lane128 == k, pk, pos_cols)
    pos_rows = pos_cols.T
    row_id = lax.broadcasted_iota(jnp.int32, (tl * TOP_K, tl), 0).astype(F32)
    perm = jnp.zeros((tl * TOP_K, tl), F32)
    for k in range(TOP_K):
        perm = jnp.where(row_id == pos_rows[k:k + 1, :], 1.0, perm)
    xs_v[buf] = _dot(perm.astype(BF16), h.astype(BF16)).astype(BF16)
    block_copy(step, buf).start()
    cnt_s[...] = cnt_s[...] + cnt_tile
    ti_ref[0] = top_i
    tw_ref[0] = top_w
    rk_ref[0] = rank
    ct_ref[0] = cnt_tile
    cnt_ref[...] = cnt_s[...]

    @pl.when(step == n_steps - 1)
    def _():
        block_copy(step, buf).wait()

        @pl.when(step >= 1)
        def _():
            block_copy(step - 1, 1 - buf).wait()

        if prev_rows:
            prev_copy().wait()


def _post_call(x, y_ssd, y_s5, mod, w_out, norm_g, w_router, b_router, xs_prev=None):
    b, l, _ = x.shape
    tl = 256
    tiles_per_row = l // tl
    prev_rows = 0 if xs_prev is None else xs_prev.shape[0]
    sorted_rows = prev_rows + b * l * TOP_K
    per_batch = mod.shape[0] > 1
    mod_map = (lambda i, j: (i, 0, 0)) if per_batch else (lambda i, j: (0, 0, 0))
    tok = lambda w: pl.BlockSpec((1, tl, w), lambda i, j: (i, j, 0))
    full = lambda r, w: pl.BlockSpec((r, w), lambda i, j: (0, 0))
    in_specs = [tok(D_MODEL), tok(SSD_WIDTH), tok(S5_WIDTH),
                pl.BlockSpec((1, 1, 6 * D_MODEL), mod_map),
                full(D_MODEL, D_MODEL), full(1, D_MODEL), full(D_MODEL, N_EXPERTS),
                full(1, N_EXPERTS)]
    args = [x, y_ssd, y_s5, mod, w_out, norm_g.reshape(1, D_MODEL), w_router, b_router.reshape(1, N_EXPERTS)]
    if prev_rows:
        in_specs.append(pl.BlockSpec(memory_space=pl.ANY))
        args.append(xs_prev)
    return pl.pallas_call(
        functools.partial(_post_kernel, tile_base=prev_rows // (tl * TOP_K), prev_rows=prev_rows),
        grid=(b, tiles_per_row),
        in_specs=in_specs,
        out_specs=[tok(D_MODEL),
                   pl.BlockSpec(memory_space=pl.ANY),
                   tok(TOP_K), tok(TOP_K), tok(TOP_K),
                   pl.BlockSpec((1, 1, N_EXPERTS), lambda i, j: (i * tiles_per_row + j, 0, 0)),
                   full(1, N_EXPERTS)],
        out_shape=[jax.ShapeDtypeStruct((b, l, D_MODEL), F32),
                   jax.ShapeDtypeStruct((sorted_rows, D_MODEL), BF16),
                   jax.ShapeDtypeStruct((b, l, TOP_K), jnp.int32),
                   jax.ShapeDtypeStruct((b, l, TOP_K), F32),
                   jax.ShapeDtypeStruct((b, l, TOP_K), jnp.int32),
                   jax.ShapeDtypeStruct((b * tiles_per_row, 1, N_EXPERTS), F32),
                   jax.ShapeDtypeStruct((1, N_EXPERTS), F32)],
        scratch_shapes=[pltpu.VMEM((1, N_EXPERTS), F32),
                        pltpu.VMEM((2, tl * TOP_K, D_MODEL), BF16),
                        pltpu.SemaphoreType.DMA((2,)),
                        pltpu.SemaphoreType.DMA((1,))],
        compiler_params=pltpu.CompilerParams(dimension_semantics=("arbitrary", "arbitrary"),
                                             vmem_limit_bytes=VMEM_LIMIT),
        name="outproj_norm2_router",
    )(*args)


PLAN_EXPERT, PLAN_FIRST, PLAN_NEXT, PLAN_SLOT = range(4)


def _moe_kernel(plan_ref, nu_ref, x_ref, wgu_hbm, bgu_ref, wd_hbm, bd_ref, o_ref,
                gu_f32, dn_f32, gu_s, dn_s, sem):
    i = pl.program_id(0)
    used = i < nu_ref[0]

    def weight_copies(expert, slot):
        return (pltpu.make_async_copy(wgu_hbm.at[expert], gu_f32.at[slot], sem.at[0, slot]),
                pltpu.make_async_copy(wd_hbm.at[expert], dn_f32.at[slot], sem.at[1, slot]))

    @pl.when(jnp.logical_and(used, plan_ref[PLAN_FIRST, i] == 1))
    def _():
        slot = plan_ref[PLAN_SLOT, i]
        nxt = plan_ref[PLAN_NEXT, i]

        @pl.when(i == 0)
        def _():
            for cp in weight_copies(plan_ref[PLAN_EXPERT, 0], plan_ref[PLAN_SLOT, 0]):
                cp.start()

        for cp in weight_copies(plan_ref[PLAN_EXPERT, i], slot):
            cp.wait()

        @pl.when(nxt >= 0)
        def _():
            for cp in weight_copies(nxt, 1 - slot):
                cp.start()

        for r in range(D_MODEL // 128):
            gu_s[r * 128:(r + 1) * 128, :] = gu_f32[slot, r * 128:(r + 1) * 128, :].astype(BF16)
            dn_s[r * 128:(r + 1) * 128, :] = dn_f32[slot, r * 128:(r + 1) * 128, :].astype(BF16)

    @pl.when(used)
    def _():
        x = x_ref[...]
        gate = _dot(x, gu_s[:, 0:D_FF]) + bgu_ref[0, :, 0:D_FF]
        up = _dot(x, gu_s[:, D_FF:2 * D_FF]) + bgu_ref[0, :, D_FF:2 * D_FF]
        gate = jnp.minimum(gate, SWIGLU_LIMIT)
        up = jnp.clip(up, -SWIGLU_LIMIT, SWIGLU_LIMIT)
        act = gate * _sigmoid(SWIGLU_ALPHA * gate) * (up + 1.0)
        o_ref[...] = (_dot(act.astype(BF16), dn_s[...]) + bd_ref[0]).astype(o_ref.dtype)

    @pl.when(jnp.logical_not(used))
    def _():
        o_ref[...] = jnp.zeros(o_ref.shape, o_ref.dtype)


def _moe_call(plan, n_used, xs, w_gate_up, b_gate_up, w_down, b_down):
    p = xs.shape[0]
    n_tiles = p // MOE_TM
    expert_of = lambda i, plan, nu: (plan[PLAN_EXPERT, i], 0, 0)
    grid_spec = pltpu.PrefetchScalarGridSpec(
        num_scalar_prefetch=2,
        grid=(n_tiles,),
        in_specs=[pl.BlockSpec((MOE_TM, D_MODEL), lambda i, plan, nu: (i, 0)),
                  pl.BlockSpec(memory_space=pl.ANY),
                  pl.BlockSpec((1, 1, 2 * D_FF), expert_of),
                  pl.BlockSpec(memory_space=pl.ANY),
                  pl.BlockSpec((1, 1, D_MODEL), expert_of)],
        out_specs=pl.BlockSpec((MOE_TM, D_MODEL), lambda i, plan, nu: (i, 0)),
        scratch_shapes=[pltpu.VMEM((2, D_MODEL, 2 * D_FF), F32), pltpu.VMEM((2, D_FF, D_MODEL), F32),
                        pltpu.VMEM((D_MODEL, 2 * D_FF), BF16), pltpu.VMEM((D_FF, D_MODEL), BF16),
                        pltpu.SemaphoreType.DMA((2, 2))],
    )
    return pl.pallas_call(
        _moe_kernel,
        grid_spec=grid_spec,
        out_shape=jax.ShapeDtypeStruct((p, D_MODEL), BF16),
        compiler_params=pltpu.CompilerParams(dimension_semantics=("arbitrary",),
                                             vmem_limit_bytes=VMEM_LIMIT),
        name="moe_experts",
    )(plan, n_used, xs, w_gate_up, b_gate_up.reshape(N_EXPERTS, 1, 2 * D_FF),
      w_down, b_down.reshape(N_EXPERTS, 1, D_MODEL))


def _combine_kernel(x1_ref, yg_ref, tw_ref, mod_ref, g_ref, o_ref):
    tw = tw_ref[0]
    acc = tw[:, 0:1] * yg_ref[0, 0].astype(F32)
    for k in range(1, TOP_K):
        acc = acc + tw[:, k:k + 1] * yg_ref[k, 0].astype(F32)
    g2 = mod_ref[0, :, 5 * D_MODEL:6 * D_MODEL]
    x2 = x1_ref[0] + g2 * acc
    ms = jnp.mean(x2 * x2, axis=-1, keepdims=True)
    o_ref[0] = x2 * lax.rsqrt(ms + RMS_EPS) * g_ref[...]


def _combine_call(x1, yg, top_w, mod, norm_g):
    b, l, _ = x1.shape
    tl = 256
    per_batch = mod.shape[0] > 1
    mod_map = (lambda i, j: (i, 0, 0)) if per_batch else (lambda i, j: (0, 0, 0))
    tok = lambda w: pl.BlockSpec((1, tl, w), lambda i, j: (i, j, 0))
    return pl.pallas_call(
        _combine_kernel,
        grid=(b, l // tl),
        in_specs=[tok(D_MODEL),
                  pl.BlockSpec((TOP_K, 1, tl, D_MODEL), lambda i, j: (0, i, j, 0)),
                  tok(TOP_K),
                  pl.BlockSpec((1, 1, 6 * D_MODEL), mod_map),
                  pl.BlockSpec((1, D_MODEL), lambda i, j: (0, 0))],
        out_specs=tok(D_MODEL),
        out_shape=jax.ShapeDtypeStruct((b, l, D_MODEL), F32),
        compiler_params=pltpu.CompilerParams(dimension_semantics=("arbitrary", "arbitrary"),
                                             vmem_limit_bytes=VMEM_LIMIT),
        name="moe_combine_final_norm",
    )(x1, yg, top_w, mod, norm_g.reshape(1, D_MODEL))


def _route(top_i, rank, counts, cnt_tiles):
    t = top_i.shape[0]
    p = t * TOP_K + N_EXPERTS * MOE_TM
    n_tiles = p // MOE_TM
    padded = ((counts + MOE_TM - 1) // MOE_TM) * MOE_TM
    ends = jnp.cumsum(padded)
    starts = ends - padded
    take = lambda table, idx: table.at[idx].get(mode="promise_in_bounds")
    dest = take(starts, top_i) + rank
    tile_start = jnp.arange(n_tiles, dtype=jnp.int32) * MOE_TM
    tile_expert = jnp.minimum(jnp.sum((ends[None, :] <= tile_start[:, None]).astype(jnp.int32), axis=1),
                              N_EXPERTS - 1)
    n_used = (ends[-1] // MOE_TM).reshape(1)
    tile_id = jnp.arange(n_tiles, dtype=jnp.int32)
    prev_expert = jnp.concatenate([jnp.full((1,), -1, tile_expert.dtype), tile_expert[:-1]])
    first = ((tile_expert != prev_expert) & (tile_id < n_used[0])).astype(jnp.int32)
    slot = (jnp.cumsum(first) - 1) % 2
    eid = jnp.arange(N_EXPERTS, dtype=jnp.int32)
    later = jnp.where((eid[None, :] > eid[:, None]) & (counts[None, :] > 0), eid[None, :], N_EXPERTS)
    next_of = jnp.min(later, axis=1)
    next_of = jnp.where(next_of < N_EXPERTS, next_of, -1)
    nxt = jnp.sum(jnp.where(tile_expert[:, None] == eid[None, :], next_of[None, :], 0), axis=1)
    plan = jnp.stack([tile_expert, first, nxt, slot]).astype(jnp.int32)
    tok_tiles = cnt_tiles.shape[0]
    block = (t * TOP_K) // tok_tiles
    excl = jnp.cumsum(cnt_tiles, axis=0) - cnt_tiles
    off = jnp.cumsum(cnt_tiles, axis=1) - cnt_tiles
    excl_e = take(excl.T, tile_expert)
    off_e = take(off.T, tile_expert)
    r = tile_start[:, None] + jnp.arange(MOE_TM, dtype=jnp.int32)[None, :] - take(starts, tile_expert)[:, None]
    ti = jnp.sum((excl_e[:, None, :] <= r[:, :, None]).astype(jnp.int32), axis=2) - 1
    here = jnp.arange(tok_tiles, dtype=jnp.int32)[None, None, :] == ti[:, :, None]
    excl_i = jnp.sum(jnp.where(here, excl_e[:, None, :], 0), axis=2)
    off_i = jnp.sum(jnp.where(here, off_e[:, None, :], 0), axis=2)
    valid = r < take(counts, tile_expert)[:, None]
    spread = (jnp.arange(p, dtype=jnp.int32) % (t * TOP_K)).reshape(n_tiles, MOE_TM)
    src_row = jnp.where(valid, ti * block + off_i + r - excl_i, spread).reshape(p)
    return dest, src_row, plan, n_used.astype(jnp.int32)


def _layer_front(x, mod, s_ssd_f, s_ssd_b, s_s5_f, s_s5_b, grid_cols, prm, xs_prev):
    z, xbc, u, dt = _inproj_call(x, mod, prm["norm1_g"], prm["w_in_r"])
    y_ssd, n_ssd_f, n_ssd_b = _ssd_call(z, xbc, dt, s_ssd_f, s_ssd_b, prm["ssd"])
    y_f, n_s5_f = _s5_call(u, _s5_state_in(s_s5_f), prm["s5_f"], grid_cols, False)
    y_s5, n_s5_b = _s5_call(u, _s5_state_in(s_s5_b), prm["s5_b"], grid_cols, True,
                            y_first=y_f, glu=prm["glu"])
    x1, xs_buf, top_i, top_w, rank, cnt_tiles, cnt = _post_call(
        x, y_ssd, y_s5, mod, prm["w_out"], prm["norm2_g"], prm["w_router"], prm["b_router"], xs_prev)
    states = (n_ssd_f, n_ssd_b, _s5_state_out(n_s5_f), _s5_state_out(n_s5_b))
    return x1, (xs_buf, top_i, rank, cnt, cnt_tiles), top_w, states


def kernel(x_prompt, x_sample, state_ssd_fwd, state_ssd_bwd, state_s5_fwd, state_s5_bwd, c, c_ctx, w_ada, b_ada, norm1_g, w_in, ssd_conv_w, ssd_conv_b, ssd_dt_bias_fwd, ssd_dt_bias_bwd, ssd_a_log_fwd, ssd_a_log_bwd, ssd_d, ssd_norm_g, s5_a_re_fwd, s5_a_im_fwd, s5_log_dt_fwd, s5_b_re_fwd, s5_b_im_fwd, s5_c_re_fwd, s5_c_im_fwd, s5_a_re_bwd, s5_a_im_bwd, s5_log_dt_bwd, s5_b_re_bwd, s5_b_im_bwd, s5_c_re_bwd, s5_c_im_bwd, s5_d, w_glu, b_glu, w_out, norm2_g, w_router, b_router, w_gate_up, b_gate_up, w_down, b_down, norm_f_g):
    depth = w_ada.shape[0]
    assert depth == 1, "single trunk layer"
    nb_ctx, l_ctx, _ = x_prompt.shape
    nb_lat, l_lat, _ = x_sample.shape

    w_in0 = w_in[0]
    dt_lo = SSD_WIDTH + SSD_CONV_CH
    dt_cols = jnp.pad(w_in0[:, dt_lo:dt_lo + 2 * SSD_HEADS], ((0, 0), (0, DT_PAD - 2 * SSD_HEADS)))
    w_in_r = jnp.concatenate([w_in0[:, :dt_lo], w_in0[:, dt_lo + 2 * SSD_HEADS:], dt_cols], axis=1).astype(BF16)
    pad_dt = lambda f, b: jnp.pad(jnp.concatenate([f, b]).astype(F32), (0, DT_PAD - 2 * SSD_HEADS)).reshape(1, DT_PAD)
    ssd_prm = {
        "conv_w": jnp.pad(ssd_conv_w[0].astype(F32), ((0, 5), (0, 0))),
        "conv_b": ssd_conv_b[0].astype(F32).reshape(1, SSD_CONV_CH),
        "dt_bias": pad_dt(ssd_dt_bias_fwd[0], ssd_dt_bias_bwd[0]),
        "a": pad_dt(-jnp.exp(ssd_a_log_fwd[0].astype(F32)), -jnp.exp(ssd_a_log_bwd[0].astype(F32))),
        "d_skip": jnp.repeat(ssd_d[0].astype(F32), SSD_HEAD_DIM).reshape(1, SSD_WIDTH),
        "norm_g": ssd_norm_g[0].astype(F32).reshape(1, SSD_WIDTH),
        "expand": (jnp.arange(DT_PAD)[None, :, None]
                   == (jnp.arange(SSD_WIDTH) // SSD_HEAD_DIM)[None, None, :]
                   + SSD_HEADS * jnp.arange(2)[:, None, None]).astype(BF16),
    }
    prm = {
        "norm1_g": norm1_g[0], "w_in_r": w_in_r, "ssd": ssd_prm,
        "s5_f": _s5_params(s5_a_re_fwd[0], s5_a_im_fwd[0], s5_log_dt_fwd[0], s5_b_re_fwd[0],
                           s5_b_im_fwd[0], s5_c_re_fwd[0], s5_c_im_fwd[0]),
        "s5_b": _s5_params(s5_a_re_bwd[0], s5_a_im_bwd[0], s5_log_dt_bwd[0], s5_b_re_bwd[0],
                           s5_b_im_bwd[0], s5_c_re_bwd[0], s5_c_im_bwd[0]),
        "glu": {"d": s5_d[0].astype(F32).reshape(1, S5_WIDTH), "w": w_glu[0].astype(BF16),
                "b": b_glu[0].astype(F32).reshape(1, 2 * S5_WIDTH)},
        "w_out": w_out[0].astype(BF16), "norm2_g": norm2_g[0],
        "w_router": w_router[0].astype(F32), "b_router": b_router[0].astype(F32),
    }

    conds = jnp.concatenate([c_ctx[None, :], c], axis=0)
    conds = jnp.pad(conds, ((0, (-conds.shape[0]) % 8), (0, 0)))
    mod = _ada_call(conds, w_ada[0], b_ada[0])
    mod_ctx = mod[0:1].reshape(1, 1, 6 * D_MODEL)
    mod_lat = mod[1:1 + nb_lat].reshape(nb_lat, 1, 6 * D_MODEL)

    zero_ssd = jnp.zeros((nb_ctx, SSD_HEADS, SSD_HEAD_DIM, SSD_STATE), F32)
    zero_s5 = jnp.zeros((nb_ctx, S5_GROUPS, S5_STATE, 2), F32)
    t_ctx = nb_ctx * l_ctx
    x1_c, routed_c, tw_c, st_c = _layer_front(
        x_prompt, mod_ctx, zero_ssd, zero_ssd, zero_s5, zero_s5, False, prm, None)
    x1_l, routed_l, tw_l, _ = _layer_front(
        x_sample, mod_lat, state_ssd_fwd[:, 0], state_ssd_bwd[:, 0], state_s5_fwd[:, 0],
        state_s5_bwd[:, 0], True, prm, routed_c[0])

    _, ti_c, rk_c, cnt_c, ct_c = routed_c
    xs_tiles, ti_l, rk_l, cnt_l, ct_l = routed_l
    cnt_c = cnt_c.reshape(N_EXPERTS).astype(jnp.int32)
    cnt_l = cnt_l.reshape(N_EXPERTS).astype(jnp.int32)
    cnt_tiles = jnp.concatenate([ct_c, ct_l], axis=0).reshape(-1, N_EXPERTS).astype(jnp.int32)
    ti_l2 = ti_l.reshape(-1, TOP_K)
    top_i = jnp.concatenate([ti_c.reshape(-1, TOP_K), ti_l2], axis=0)
    rank = jnp.concatenate([rk_c.reshape(-1, TOP_K),
                            rk_l.reshape(-1, TOP_K) + cnt_c.at[ti_l2].get(mode="promise_in_bounds")], axis=0)
    dest, src_row, plan, n_used = _route(top_i, rank, cnt_c + cnt_l, cnt_tiles)
    xs = xs_tiles.at[src_row].get(mode="promise_in_bounds")
    ys = _moe_call(plan, n_used, xs, w_gate_up[0], b_gate_up[0], w_down[0], b_down[0])

    def picked_rows(d, nb, l):
        rows = ys.at[d.T.reshape(-1)].get(mode="promise_in_bounds")
        return rows.reshape(TOP_K, nb, l, D_MODEL)

    y_prompt = _combine_call(x1_c, picked_rows(dest[:t_ctx], nb_ctx, l_ctx), tw_c, mod_ctx, norm_f_g)
    y_sample = _combine_call(x1_l, picked_rows(dest[t_ctx:], nb_lat, l_lat), tw_l, mod_lat, norm_f_g)

    ssd_f, ssd_b, s5_f, s5_b = st_c
    return (y_prompt, y_sample, ssd_f[:, None], ssd_b[:, None], s5_f[:, None], s5_b[:, None])
```

```python
import functools
import math

import jax
import jax.numpy as jnp
from jax import lax
from jax.experimental import pallas as pl
from jax.experimental.pallas import tpu as pltpu

F32 = jnp.float32
BF16 = jnp.bfloat16

D_MODEL = 1024
GRID_W = 64
SSD_WIDTH = 512
SSD_HEAD_DIM = 64
SSD_HEADS = 8
SSD_GROUPS = 2
SSD_HEADS_PER_GROUP = SSD_HEADS // SSD_GROUPS
SSD_STATE = 128
SSD_CHUNK = 128
SSD_CONV_CH = SSD_WIDTH + 2 * SSD_GROUPS * SSD_STATE
S5_WIDTH = 512
S5_CH = 16
S5_GROUPS = 32
S5_STATE = 64
N_EXPERTS = 32
TOP_K = 4
D_FF = 1024
SWIGLU_LIMIT = 7.0
SWIGLU_ALPHA = 1.702
RMS_EPS = 1e-5

LANES = 128
DT_PAD = 128
S5_BATCH = 8
S5_COLS = 2 * S5_GROUPS * S5_STATE
S5_BLK = 512
S5_SUB_STEPS = 32
S5_BLOCK_STEPS = 256
MOE_TM = 256
VMEM_LIMIT = 56 * 1024 * 1024


def _sigmoid(x):
    return 1.0 / (1.0 + jnp.exp(-x))


def _split3(x):
    hi = x.astype(BF16)
    r1 = x - hi.astype(F32)
    mid = r1.astype(BF16)
    lo = (r1 - mid.astype(F32)).astype(BF16)
    return hi, mid, lo


def _split2(x):
    hi = x.astype(BF16)
    return hi, (x - hi.astype(F32)).astype(BF16)


def _dot(a, b):
    return jnp.dot(a, b, preferred_element_type=F32)


def _ada_kernel(c_ref, w_ref, b_ref, o_ref):
    c = c_ref[...]
    s = c * _sigmoid(c)
    o_ref[...] = _dot(s.astype(BF16), w_ref[...].astype(BF16)) + b_ref[...]


def _ada_call(conds, w_ada, b_ada):
    n = w_ada.shape[1]
    tn = 1536
    rows = conds.shape[0]
    return pl.pallas_call(
        _ada_kernel,
        grid=(n // tn,),
        in_specs=[pl.BlockSpec((rows, D_MODEL), lambda j: (0, 0)),
                  pl.BlockSpec((D_MODEL, tn), lambda j: (0, j)),
                  pl.BlockSpec((1, tn), lambda j: (0, j))],
        out_specs=pl.BlockSpec((rows, tn), lambda j: (0, j)),
        out_shape=jax.ShapeDtypeStruct((rows, n), F32),
        compiler_params=pltpu.CompilerParams(dimension_semantics=("arbitrary",),
                                             vmem_limit_bytes=VMEM_LIMIT),
        name="ada_mod",
    )(conds, w_ada, b_ada.reshape(1, n))


def _inproj_kernel(x_ref, mod_ref, g_ref, w_ref, z_ref, xbc_ref, u_ref, dt_ref):
    x = x_ref[0]
    ms = jnp.mean(x * x, axis=-1, keepdims=True)
    y = x * lax.rsqrt(ms + RMS_EPS) * g_ref[...]
    sh = mod_ref[0, :, 0:D_MODEL]
    sc = mod_ref[0, :, D_MODEL:2 * D_MODEL]
    h = (y * (1.0 + sc) + sh).astype(BF16)
    z_ref[0] = _dot(h, w_ref[:, 0:512]).astype(BF16)
    xbc_ref[0] = _dot(h, w_ref[:, 512:1536]).astype(BF16)
    u_ref[0] = _dot(h, w_ref[:, 1536:2048])
    dt_ref[0] = _dot(h, w_ref[:, 2048:2048 + DT_PAD])


def _inproj_call(x, mod, norm_g, w_in_r):
    b, l, _ = x.shape
    tl = min(l, 512)
    per_batch = mod.shape[0] > 1
    mod_map = (lambda i, j: (i, 0, 0)) if per_batch else (lambda i, j: (0, 0, 0))
    ncol = w_in_r.shape[1]
    tok = lambda w: pl.BlockSpec((1, tl, w), lambda i, j: (i, j, 0))
    return pl.pallas_call(
        _inproj_kernel,
        grid=(b, l // tl),
        in_specs=[tok(D_MODEL),
                  pl.BlockSpec((1, 1, 6 * D_MODEL), mod_map),
                  pl.BlockSpec((1, D_MODEL), lambda i, j: (0, 0)),
                  pl.BlockSpec((D_MODEL, ncol), lambda i, j: (0, 0))],
        out_specs=[tok(SSD_WIDTH), tok(SSD_CONV_CH), tok(S5_WIDTH), tok(DT_PAD)],
        out_shape=[jax.ShapeDtypeStruct((b, l, SSD_WIDTH), BF16),
                   jax.ShapeDtypeStruct((b, l, SSD_CONV_CH), BF16),
                   jax.ShapeDtypeStruct((b, l, S5_WIDTH), F32),
                   jax.ShapeDtypeStruct((b, l, DT_PAD), F32)],
        compiler_params=pltpu.CompilerParams(dimension_semantics=("arbitrary", "arbitrary"),
                                             vmem_limit_bytes=VMEM_LIMIT),
        name="norm1_inproj",
    )(x, mod, norm_g.reshape(1, D_MODEL), w_in_r)


def _softplus(x):
    return jnp.maximum(x, 0.0) + jnp.log1p(jnp.exp(-jnp.abs(x)))


def _ssd_kernel(z_ref, xbc_ref, dt_ref, s0f_ref, s0b_ref, cw_ref, cb_ref, dtb_ref, a_ref,
                dskip_ref, ng_ref, ex_ref, y_ref, sf_ref, sb_ref,
                xc_s, dts_s, yf_s, yb_s, stf_s, stb_s, *, seq):
    q = SSD_CHUNK
    nc = seq // q
    gw = SSD_HEADS_PER_GROUP * SSD_HEAD_DIM
    row_i = lax.broadcasted_iota(jnp.int32, (q, 1), 0)

    def conv_body(c, carry):
        r0 = pl.multiple_of(c * q, q)
        cur = xbc_ref[0, pl.ds(r0, q), :].astype(F32)
        p0 = pl.multiple_of(jnp.maximum(r0 - 16, 0), 16)
        n0 = pl.multiple_of(jnp.minimum(r0 + q, seq - 16), 16)
        prev_row = xbc_ref[0, pl.ds(p0, 16), :].astype(F32)[15:16, :]
        next_row = xbc_ref[0, pl.ds(n0, 16), :].astype(F32)[0:1, :]
        prev_row = jnp.where(c > 0, prev_row, 0.0)
        next_row = jnp.where(c < nc - 1, next_row, 0.0)
        down = jnp.where(row_i == 0, prev_row, pltpu.roll(cur, 1, axis=0))
        up = jnp.where(row_i == q - 1, next_row, pltpu.roll(cur, q - 1, axis=0))
        v = cw_ref[0:1, :] * down + cw_ref[1:2, :] * cur + cw_ref[2:3, :] * up + cb_ref[...]
        xc_s[pl.ds(r0, q), :] = v * _sigmoid(v)
        dts_s[pl.ds(r0, q), :] = _softplus(dt_ref[0, pl.ds(r0, q), :] + dtb_ref[...])
        return carry

    lax.fori_loop(0, nc, conv_body, 0)

    li = lax.broadcasted_iota(jnp.int32, (q, q), 0)
    si = lax.broadcasted_iota(jnp.int32, (q, q), 1)
    head_of_lane = lax.broadcasted_iota(jnp.int32, (q, gw), 1) // SSD_HEAD_DIM

    def expand(v, e_mat):
        hi, lo = _split2(v)
        return _dot(hi, e_mat) + _dot(lo, e_mat)

    def chunk(c, direction):
        r0 = pl.multiple_of(c * q, q)
        st_ref = stf_s if direction == 0 else stb_s
        y_dst = yf_s if direction == 0 else yb_s
        e_mat = ex_ref[direction]
        causal = (li >= si) if direction == 0 else (li <= si)
        tri = jnp.where(causal, 1.0, 0.0).astype(BF16)
        dts = dts_s[pl.ds(r0, q), :]
        dta = dts * a_ref[...]
        d_hi, d_mid, d_lo = _split3(dta)
        cs = _dot(tri, d_hi) + _dot(tri, d_mid) + _dot(tri, d_lo)
        tot = cs[q - 1:q, :] if direction == 0 else cs[0:1, :]
        cs_t = cs.T
        ecs = jnp.exp(cs)
        tail = jnp.exp(tot - cs)
        dt_x = expand(dts, e_mat)
        dtw_x = expand(dts * tail, e_mat)
        etot_x = expand(jnp.broadcast_to(jnp.exp(tot), (8, DT_PAD)), e_mat)[0:1, :]
        for g in range(SSD_GROUPS):
            b_off = SSD_WIDTH + g * SSD_STATE
            c_off = SSD_WIDTH + SSD_GROUPS * SSD_STATE + g * SSD_STATE
            x_g = xc_s[pl.ds(r0, q), g * gw:(g + 1) * gw]
            bm = xc_s[pl.ds(r0, q), b_off:b_off + SSD_STATE]
            cm = xc_s[pl.ds(r0, q), c_off:c_off + SSD_STATE]
            cb = lax.dot_general(cm.astype(BF16), bm.astype(BF16), (((1,), (1,)), ((), ())),
                                 preferred_element_type=F32)
            st_g = st_ref[g]
            xdt = x_g * dt_x[:, g * gw:(g + 1) * gw]
            lhs, rhs = [], []
            for hh in range(SSD_HEADS_PER_GROUP):
                col = g * SSD_HEADS_PER_GROUP + hh + SSD_HEADS * direction
                seg = jnp.broadcast_to(cs[:, col:col + 1], (q, q)) - cs_t[col:col + 1, :]
                dec = jnp.exp(jnp.where(causal, seg, -1e30))
                lhs.append((cb * dec).astype(BF16))
                lhs.append((cm * jnp.broadcast_to(ecs[:, col:col + 1], (q, q))).astype(BF16))
                own = head_of_lane == hh
                rhs.append(jnp.where(own, xdt, 0.0).astype(BF16))
                rhs.append(jnp.where(own, st_g, 0.0).astype(BF16))
            y_dst[pl.ds(r0, q), g * gw:(g + 1) * gw] = _dot(jnp.concatenate(lhs, axis=1),
                                                             jnp.concatenate(rhs, axis=0))
            xw = (x_g * dtw_x[:, g * gw:(g + 1) * gw]).astype(BF16)
            st_ref[g] = etot_x[:, g * gw:(g + 1) * gw] * st_g + _dot(bm.T.astype(BF16), xw)

    stf_s[...] = s0f_ref[0]
    stb_s[...] = s0b_ref[0]

    def both(i, carry):
        chunk(i, 0)
        chunk(nc - 1 - i, 1)
        return carry

    lax.fori_loop(0, nc, both, 0)
    sf_ref[0] = stf_s[...]
    sb_ref[0] = stb_s[...]

    def finish(c, carry):
        r0 = pl.multiple_of(c * q, q)
        xs = xc_s[pl.ds(r0, q), 0:SSD_WIDTH]
        zz = z_ref[0, pl.ds(r0, q), :].astype(F32)
        y = (yf_s[pl.ds(r0, q), :] + yb_s[pl.ds(r0, q), :] + dskip_ref[...] * xs) * (zz * _sigmoid(zz))
        ms = jnp.mean(y * y, axis=-1, keepdims=True)
        y_ref[0, pl.ds(r0, q), :] = (y * lax.rsqrt(ms + RMS_EPS) * ng_ref[...]).astype(y_ref.dtype)
        return carry

    lax.fori_loop(0, nc, finish, 0)


def _ssd_state_in(s):
    b = s.shape[0]
    s = s.astype(F32).reshape(b, SSD_GROUPS, SSD_HEADS_PER_GROUP, SSD_HEAD_DIM, SSD_STATE)
    return s.transpose(0, 1, 4, 2, 3).reshape(b, SSD_GROUPS, SSD_STATE, SSD_HEADS_PER_GROUP * SSD_HEAD_DIM)


def _ssd_state_out(s):
    b = s.shape[0]
    s = s.reshape(b, SSD_GROUPS, SSD_STATE, SSD_HEADS_PER_GROUP, SSD_HEAD_DIM)
    return s.transpose(0, 1, 3, 4, 2).reshape(b, SSD_HEADS, SSD_HEAD_DIM, SSD_STATE)


def _ssd_call(z, xbc, dt, s0f, s0b, prm):
    b, l, _ = z.shape
    st_shape = (SSD_GROUPS, SSD_STATE, SSD_HEADS_PER_GROUP * SSD_HEAD_DIM)
    tok = lambda w: pl.BlockSpec((1, l, w), lambda i: (i, 0, 0))
    st_spec = pl.BlockSpec((1,) + st_shape, lambda i: (i, 0, 0, 0))
    par = lambda r, w: pl.BlockSpec((r, w), lambda i: (0, 0))
    y, sf, sb = pl.pallas_call(
        functools.partial(_ssd_kernel, seq=l),
        grid=(b,),
        in_specs=[tok(SSD_WIDTH), tok(SSD_CONV_CH), tok(DT_PAD), st_spec, st_spec,
                  par(8, SSD_CONV_CH), par(1, SSD_CONV_CH), par(1, DT_PAD), par(1, DT_PAD),
                  par(1, SSD_WIDTH), par(1, SSD_WIDTH),
                  pl.BlockSpec((2, DT_PAD, SSD_WIDTH), lambda i: (0, 0, 0))],
        out_specs=[tok(SSD_WIDTH), st_spec, st_spec],
        out_shape=[jax.ShapeDtypeStruct((b, l, SSD_WIDTH), BF16),
                   jax.ShapeDtypeStruct((b,) + st_shape, F32),
                   jax.ShapeDtypeStruct((b,) + st_shape, F32)],
        scratch_shapes=[pltpu.VMEM((l, SSD_CONV_CH), F32),
                        pltpu.VMEM((l, DT_PAD), F32),
                        pltpu.VMEM((l, SSD_WIDTH), F32),
                        pltpu.VMEM((l, SSD_WIDTH), F32),
                        pltpu.VMEM(st_shape, F32),
                        pltpu.VMEM(st_shape, F32)],
        compiler_params=pltpu.CompilerParams(dimension_semantics=("arbitrary",),
                                             vmem_limit_bytes=VMEM_LIMIT),
        name="ssd_mixer",
    )(z, xbc, dt, _ssd_state_in(s0f), _ssd_state_in(s0b),
      prm["conv_w"], prm["conv_b"], prm["dt_bias"], prm["a"], prm["d_skip"], prm["norm_g"], prm["expand"])
    return y, _ssd_state_out(sf), _ssd_state_out(sb)


def _gelu_tanh(x):
    return 0.5 * x * (1.0 + jnp.tanh(math.sqrt(2.0 / math.pi) * (x + 0.044715 * (x * x * x))))


def _s5_kernel(*refs, grid_cols, reverse, final):
    if final:
        (u_ref, s0_ref, lr_ref, li_ref, wb_ref, wc_ref, yf_ref, d_ref, wg_ref, bg_ref,
         o_ref, sfin_ref, ur_s, yr_s, st_s) = refs
    else:
        (u_ref, s0_ref, lr_ref, li_ref, wb_ref, wc_ref,
         o_ref, sfin_ref, ur_s, yr_s, st_s) = refs
    nb = S5_BATCH
    steps = S5_BLOCK_STEPS
    rows = steps * nb
    sub_rows = S5_SUB_STEPS * nb
    n_sub = steps // S5_SUB_STEPS
    j = pl.program_id(1)

    @pl.when(j == 0)
    def _():
        st_s[...] = s0_ref[0]

    def to_scan(v):
        v = jnp.transpose(v, (2, 1, 0, 3)) if grid_cols else jnp.swapaxes(v, 0, 1)
        return v.reshape(rows, S5_WIDTH)

    def from_scan(v):
        if grid_cols:
            return jnp.transpose(v.reshape(u_ref.shape[2], u_ref.shape[1], nb, S5_WIDTH), (2, 1, 0, 3))
        return jnp.swapaxes(v.reshape(steps, nb, S5_WIDTH), 0, 1)

    ur_s[...] = to_scan(u_ref[...])

    order = list(range(S5_SUB_STEPS))[::-1] if reverse else list(range(S5_SUB_STEPS))
    blk_per_half = S5_COLS // S5_BLK // 2

    def by_groups(v, perm):
        return jnp.concatenate([v[g * nb:(g + 1) * nb] for g in perm], axis=0)

    def sub_chunk(i, carry):
        sidx = (n_sub - 1 - i) if reverse else i
        r0 = pl.multiple_of(sidx * sub_rows, sub_rows)
        for hf in range(2):
            u_half = ur_s[pl.ds(r0, sub_rows), hf * 256:(hf + 1) * 256]
            if reverse:
                u_half = by_groups(u_half, order)
            uh = u_half.astype(BF16)
            for bq in range(blk_per_half):
                blk = hf * blk_per_half + bq
                c0 = blk * S5_BLK
                lam_r = jnp.broadcast_to(lr_ref[:, c0:c0 + S5_BLK], (nb, S5_BLK))
                lam_i = jnp.broadcast_to(li_ref[:, c0:c0 + S5_BLK], (nb, S5_BLK))
                bu = _dot(uh, wb_ref[hf, :, bq * S5_BLK:(bq + 1) * S5_BLK])
                s = st_s[:, c0:c0 + S5_BLK]
                states = []
                for k in range(S5_SUB_STEPS):
                    sw = jnp.concatenate([s[:, S5_BLK // 2:], s[:, :S5_BLK // 2]], axis=1)
                    s = lam_r * s + lam_i * sw + bu[k * nb:(k + 1) * nb, :]
                    states.append(s)
                st_s[:, c0:c0 + S5_BLK] = s
                y_blk = _dot(jnp.concatenate(states, axis=0).astype(BF16), wc_ref[blk])
                if reverse:
                    y_blk = by_groups(y_blk, order)
                yr_s[pl.ds(r0, sub_rows), blk * 64:(blk + 1) * 64] = y_blk
        return carry

    lax.fori_loop(0, n_sub, sub_chunk, 0)

    @pl.when(j == pl.num_programs(1) - 1)
    def _():
        sfin_ref[0] = st_s[...]

    if not final:
        o_ref[0] = yr_s[...]
        return

    def glu_chunk(i, carry):
        r0 = pl.multiple_of(i * sub_rows, sub_rows)
        y = (yr_s[pl.ds(r0, sub_rows), :] + yf_ref[0, pl.ds(r0, sub_rows), :]
             + d_ref[...] * ur_s[pl.ds(r0, sub_rows), :])
        y = _gelu_tanh(y).astype(BF16)
        val = _dot(y, wg_ref[:, 0:S5_WIDTH]) + bg_ref[:, 0:S5_WIDTH]
        gate = _dot(y, wg_ref[:, S5_WIDTH:2 * S5_WIDTH]) + bg_ref[:, S5_WIDTH:2 * S5_WIDTH]
        yr_s[pl.ds(r0, sub_rows), :] = val * _sigmoid(gate)
        return carry

    lax.fori_loop(0, n_sub, glu_chunk, 0)

    o_ref[...] = from_scan(yr_s[...])


def _s5_call(u, s0, prm, grid_cols, reverse, y_first=None, glu=None):
    b, l, _ = u.shape
    nbg = b // S5_BATCH
    nblk = l // S5_BLOCK_STEPS
    rows = S5_BLOCK_STEPS * S5_BATCH
    final = y_first is not None
    blk_of = (lambda j: nblk - 1 - j) if reverse else (lambda j: j)
    if grid_cols:
        n_rows = l // GRID_W
        u_in = u.reshape(b, n_rows, GRID_W, S5_WIDTH)
        cols_per_blk = S5_BLOCK_STEPS // n_rows
        tok_spec = pl.BlockSpec((S5_BATCH, n_rows, cols_per_blk, S5_WIDTH),
                                lambda i, j: (i, 0, blk_of(j), 0))
        out_nat = jax.ShapeDtypeStruct((b, n_rows, GRID_W, S5_WIDTH), F32)
    else:
        u_in = u
        tok_spec = pl.BlockSpec((S5_BATCH, S5_BLOCK_STEPS, S5_WIDTH), lambda i, j: (i, blk_of(j), 0))
        out_nat = jax.ShapeDtypeStruct((b, l, S5_WIDTH), F32)
    scan_spec = pl.BlockSpec((1, rows, S5_WIDTH), lambda i, j: (i, blk_of(j), 0))
    st_spec = pl.BlockSpec((1, S5_BATCH, S5_COLS), lambda i, j: (i, 0, 0))
    full = lambda shape: pl.BlockSpec(shape, lambda i, j: (0,) * len(shape))
    in_specs = [tok_spec, st_spec, full((1, S5_COLS)), full((1, S5_COLS)),
                full((2, 256, 2048)), full((S5_COLS // S5_BLK, S5_BLK, 64))]
    args = [u_in, s0, prm["lam_r"], prm["lam_i"], prm["wb"], prm["wc"]]
    if final:
        in_specs += [scan_spec, full((1, S5_WIDTH)), full((S5_WIDTH, 2 * S5_WIDTH)), full((1, 2 * S5_WIDTH))]
        args += [y_first, glu["d"], glu["w"], glu["b"]]
        out_specs = [tok_spec, st_spec]
        out_shape = [out_nat, jax.ShapeDtypeStruct((nbg, S5_BATCH, S5_COLS), F32)]
    else:
        out_specs = [scan_spec, st_spec]
        out_shape = [jax.ShapeDtypeStruct((nbg, l * S5_BATCH, S5_WIDTH), F32),
                     jax.ShapeDtypeStruct((nbg, S5_BATCH, S5_COLS), F32)]
    y, sfin = pl.pallas_call(
        functools.partial(_s5_kernel, grid_cols=grid_cols, reverse=reverse, final=final),
        grid=(nbg, nblk),
        in_specs=in_specs,
        out_specs=out_specs,
        out_shape=out_shape,
        scratch_shapes=[pltpu.VMEM((rows, S5_WIDTH), F32),
                        pltpu.VMEM((rows, S5_WIDTH), F32),
                        pltpu.VMEM((S5_BATCH, S5_COLS), F32)],
        compiler_params=pltpu.CompilerParams(dimension_semantics=("arbitrary", "arbitrary"),
                                             vmem_limit_bytes=VMEM_LIMIT),
        name="s5_final" if final else "s5_first",
    )(*args)
    if final:
        y = y.reshape(b, l, S5_WIDTH)
    return y, sfin


def _s5_cols(t):
    lead = t.shape[:-2]
    return t.reshape(lead + (S5_COLS // S5_BLK, S5_BLK // 2))


def _s5_state_in(s0):
    b = s0.shape[0]
    re, im = _s5_cols(s0[..., 0].astype(F32)), _s5_cols(s0[..., 1].astype(F32))
    return jnp.stack([re, im], axis=-2).reshape(b // S5_BATCH, S5_BATCH, S5_COLS)


def _s5_state_out(s):
    b = s.shape[0] * S5_BATCH
    s = s.reshape(b, S5_COLS // S5_BLK, 2, S5_BLK // 2)
    re = s[:, :, 0].reshape(b, S5_GROUPS, S5_STATE)
    im = s[:, :, 1].reshape(b, S5_GROUPS, S5_STATE)
    return jnp.stack([re, im], axis=-1)


def _s5_params(a_re, a_im, log_dt, b_re, b_im, c_re, c_im):
    a_re, a_im = a_re.astype(F32), a_im.astype(F32)
    delta = jnp.exp(log_dt.astype(F32))[:, None]
    mag = jnp.exp(a_re * delta)
    lam_re, lam_im = mag * jnp.cos(a_im * delta), mag * jnp.sin(a_im * delta)
    den = a_re * a_re + a_im * a_im
    nr = lam_re - 1.0
    f_re = (nr * a_re + lam_im * a_im) / den
    f_im = (lam_im * a_re - nr * a_im) / den
    b_re, b_im = b_re.astype(F32), b_im.astype(F32)
    bb_re = f_re[..., None] * b_re - f_im[..., None] * b_im
    bb_im = f_re[..., None] * b_im + f_im[..., None] * b_re
    nblk = S5_COLS // S5_BLK
    lr = _s5_cols(lam_re)
    li = _s5_cols(lam_im)
    lam_r = jnp.stack([lr, lr], axis=-2).reshape(1, S5_COLS)
    lam_i = jnp.stack([-li, li], axis=-2).reshape(1, S5_COLS)
    gq = S5_BLK // 2 // S5_STATE
    eye = jnp.eye(gq, dtype=F32)

    def in_block(bb):
        t = bb.reshape(nblk, gq, S5_STATE, S5_CH)
        return jnp.einsum("gh,kgpc->kgchp", eye, t).reshape(nblk, gq * S5_CH, gq * S5_STATE)

    blk_in = jnp.concatenate([in_block(bb_re), in_block(bb_im)], axis=-1)
    half = nblk // 2
    eye_h = jnp.eye(half, dtype=F32)
    wb = jnp.einsum("jk,hkcn->hjckn", eye_h, blk_in.reshape(2, half, gq * S5_CH, S5_BLK))
    wb = wb.reshape(2, half * gq * S5_CH, half * S5_BLK).astype(BF16)

    def out_block(cc):
        t = cc.astype(F32).reshape(nblk, gq, S5_CH, S5_STATE)
        return jnp.einsum("gh,kgcp->kgphc", eye, t).reshape(nblk, gq * S5_STATE, gq * S5_CH)

    wc = jnp.concatenate([out_block(c_re), -out_block(c_im)], axis=1).astype(BF16)
    return {"lam_r": lam_r, "lam_i": lam_i, "wb": wb, "wc": wc}


def _post_kernel(*refs, tile_base, prev_rows):
    if prev_rows:
        prev_hbm, refs = refs[8], refs[:8] + refs[9:]
    (x_ref, ys_ref, y5_ref, mod_ref, wo_ref, g_ref, wr_ref, br_ref,
     x1_ref, xs_hbm, ti_ref, tw_ref, rk_ref, ct_ref, cnt_ref, cnt_s, xs_v, pv_v, sem_out, sem_prev) = refs
    tl = x_ref.shape[1]
    blk = tl * TOP_K
    n_steps = pl.num_programs(0) * pl.num_programs(1)
    step = pl.program_id(0) * pl.num_programs(1) + pl.program_id(1)
    buf = step % 2
    first = step == 0

    def block_copy(at_step, at_slot):
        row0 = pl.multiple_of((tile_base + at_step) * blk, blk)
        return pltpu.make_async_copy(xs_v.at[at_slot], xs_hbm.at[pl.ds(row0, blk)], sem_out.at[at_slot])

    @pl.when(step >= 2)
    def _():
        block_copy(step - 2, buf).wait()

    prev_blocks = prev_rows // blk

    def prev_in(at_step):
        row0 = pl.multiple_of(at_step * blk, blk)
        return pltpu.make_async_copy(prev_hbm.at[pl.ds(row0, blk)], pv_v.at[at_step % 2], sem_prev.at[0, at_step % 2])

    def prev_out(at_step):
        row0 = pl.multiple_of(at_step * blk, blk)
        return pltpu.make_async_copy(pv_v.at[at_step % 2], xs_hbm.at[pl.ds(row0, blk)], sem_prev.at[1, at_step % 2])

    if prev_rows:
        @pl.when(jnp.logical_and(step >= 2, step - 2 < prev_blocks))
        def _():
            prev_out(step - 2).wait()

        @pl.when(step < prev_blocks)
        def _():
            prev_in(step).start()

        @pl.when(jnp.logical_and(step >= 1, step - 1 < prev_blocks))
        def _():
            prev_in(step - 1).wait()
            prev_out(step - 1).start()

    @pl.when(first)
    def _():
        cnt_s[...] = jnp.zeros(cnt_s.shape, cnt_s.dtype)

    mixed = (_dot(ys_ref[0], wo_ref[0:SSD_WIDTH, :])
             + _dot(y5_ref[0].astype(BF16), wo_ref[SSD_WIDTH:SSD_WIDTH + S5_WIDTH, :]))
    g1 = mod_ref[0, :, 2 * D_MODEL:3 * D_MODEL]
    sh2 = mod_ref[0, :, 3 * D_MODEL:4 * D_MODEL]
    sc2 = mod_ref[0, :, 4 * D_MODEL:5 * D_MODEL]
    x1 = x_ref[0] + g1 * mixed
    x1_ref[0] = x1
    ms = jnp.mean(x1 * x1, axis=-1, keepdims=True)
    h = x1 * lax.rsqrt(ms + RMS_EPS) * g_ref[...] * (1.0 + sc2) + sh2

    h_hi, h_mid, _ = _split3(h)
    wr = wr_ref[...]
    w_hi, w_mid, _ = _split3(wr)
    logits = _dot(h_hi, w_hi) + _dot(h_hi, w_mid) + _dot(h_mid, w_hi) + br_ref[...]

    lane = lax.broadcasted_iota(jnp.int32, (tl, N_EXPERTS), 1).astype(F32)
    slot = lax.broadcasted_iota(jnp.int32, (tl, TOP_K), 1)
    work = logits
    onehot = jnp.zeros((tl, N_EXPERTS), F32)
    sels, vals = [], []
    top_i = jnp.zeros((tl, TOP_K), jnp.int32)
    for k in range(TOP_K):
        m = jnp.max(work, axis=-1, keepdims=True)
        idx = jnp.min(jnp.where(work == m, lane, float(N_EXPERTS)), axis=-1, keepdims=True)
        sel = lane == idx
        sels.append(sel)
        vals.append(m)
        top_i = jnp.where(slot == k, idx.astype(jnp.int32), top_i)
        onehot = jnp.where(sel, 1.0, onehot)
        work = jnp.where(sel, -jnp.inf, work)
    es = [jnp.exp(v - vals[0]) for v in vals]
    den = es[0] + es[1] + es[2] + es[3]
    top_w = jnp.zeros((tl, TOP_K), F32)
    for k in range(TOP_K):
        top_w = jnp.where(slot == k, es[k] / den, top_w)

    ri = lax.broadcasted_iota(jnp.int32, (tl, tl), 0)
    ci = lax.broadcasted_iota(jnp.int32, (tl, tl), 1)
    strict = jnp.where(ri > ci, 1.0, 0.0).astype(BF16)
    local = _dot(strict, onehot.astype(BF16))
    before = local + cnt_s[...]
    cnt_tile = jnp.sum(onehot, axis=0, keepdims=True)
    er = lax.broadcasted_iota(jnp.int32, (N_EXPERTS, N_EXPERTS), 0)
    ec = lax.broadcasted_iota(jnp.int32, (N_EXPERTS, N_EXPERTS), 1)
    upper = jnp.where(er < ec, 1.0, 0.0).astype(BF16)
    off = _dot(jnp.broadcast_to(cnt_tile, (8, N_EXPERTS)).astype(BF16), upper)[0:1, :]
    pos = off + local
    lane128 = lax.broadcasted_iota(jnp.int32, (tl, LANES), 1)
    rank = jnp.zeros((tl, TOP_K), jnp.int32)
    pos_cols = jnp.zeros((tl, LANES), F32)
    for k in range(TOP_K):
        rk = jnp.sum(jnp.where(sels[k], before, 0.0), axis=-1, keepdims=True)
        rank = jnp.where(slot == k, rk.astype(jnp.int32), rank)
        pk = jnp.sum(jnp.where(sels[k], pos, 0.0), axis=-1, keepdims=True)
        pos_cols = jnp.where(lane128 == k, pk, pos_cols)
    pos_rows = pos_cols.T
    row_id = lax.broadcasted_iota(jnp.int32, (tl * TOP_K, tl), 0).astype(F32)
    perm = jnp.zeros((tl * TOP_K, tl), F32)
    for k in range(TOP_K):
        perm = jnp.where(row_id == pos_rows[k:k + 1, :], 1.0, perm)
    xs_v[buf] = _dot(perm.astype(BF16), h.astype(BF16)).astype(BF16)
    block_copy(step, buf).start()
    cnt_s[...] = cnt_s[...] + cnt_tile
    ti_ref[0] = top_i
    tw_ref[0] = top_w
    rk_ref[0] = rank
    ct_ref[0] = cnt_tile
    cnt_ref[...] = cnt_s[...]

    @pl.when(step == n_steps - 1)
    def _():
        block_copy(step, buf).wait()

        @pl.when(step >= 1)
        def _():
            block_copy(step - 1, 1 - buf).wait()


def _post_call(x, y_ssd, y_s5, mod, w_out, norm_g, w_router, b_router, xs_prev=None):
    b, l, _ = x.shape
    tl = 256
    tiles_per_row = l // tl
    prev_rows = 0 if xs_prev is None else xs_prev.shape[0]
    assert prev_rows // (tl * TOP_K) + 2 <= b * tiles_per_row, "not enough grid steps to pass the earlier rows through"
    sorted_rows = prev_rows + b * l * TOP_K
    per_batch = mod.shape[0] > 1
    mod_map = (lambda i, j: (i, 0, 0)) if per_batch else (lambda i, j: (0, 0, 0))
    tok = lambda w: pl.BlockSpec((1, tl, w), lambda i, j: (i, j, 0))
    full = lambda r, w: pl.BlockSpec((r, w), lambda i, j: (0, 0))
    in_specs = [tok(D_MODEL), tok(SSD_WIDTH), tok(S5_WIDTH),
                pl.BlockSpec((1, 1, 6 * D_MODEL), mod_map),
                full(D_MODEL, D_MODEL), full(1, D_MODEL), full(D_MODEL, N_EXPERTS),
                full(1, N_EXPERTS)]
    args = [x, y_ssd, y_s5, mod, w_out, norm_g.reshape(1, D_MODEL), w_router, b_router.reshape(1, N_EXPERTS)]
    if prev_rows:
        in_specs.append(pl.BlockSpec(memory_space=pl.ANY))
        args.append(xs_prev)
    return pl.pallas_call(
        functools.partial(_post_kernel, tile_base=prev_rows // (tl * TOP_K), prev_rows=prev_rows),
        grid=(b, tiles_per_row),
        in_specs=in_specs,
        out_specs=[tok(D_MODEL),
                   pl.BlockSpec(memory_space=pl.ANY),
                   tok(TOP_K), tok(TOP_K), tok(TOP_K),
                   pl.BlockSpec((1, 1, N_EXPERTS), lambda i, j: (i * tiles_per_row + j, 0, 0)),
                   full(1, N_EXPERTS)],
        out_shape=[jax.ShapeDtypeStruct((b, l, D_MODEL), F32),
                   jax.ShapeDtypeStruct((sorted_rows, D_MODEL), BF16),
                   jax.ShapeDtypeStruct((b, l, TOP_K), jnp.int32),
                   jax.ShapeDtypeStruct((b, l, TOP_K), F32),
                   jax.ShapeDtypeStruct((b, l, TOP_K), jnp.int32),
                   jax.ShapeDtypeStruct((b * tiles_per_row, 1, N_EXPERTS), F32),
                   jax.ShapeDtypeStruct((1, N_EXPERTS), F32)],
        scratch_shapes=[pltpu.VMEM((1, N_EXPERTS), F32),
                        pltpu.VMEM((2, tl * TOP_K, D_MODEL), BF16),
                        pltpu.VMEM((2, tl * TOP_K, D_MODEL), BF16),
                        pltpu.SemaphoreType.DMA((2,)),
                        pltpu.SemaphoreType.DMA((2, 2))],
        compiler_params=pltpu.CompilerParams(dimension_semantics=("arbitrary", "arbitrary"),
                                             vmem_limit_bytes=VMEM_LIMIT),
        name="outproj_norm2_router",
    )(*args)


PLAN_EXPERT, PLAN_FIRST, PLAN_NEXT, PLAN_SLOT = range(4)


def _moe_kernel(plan_ref, nu_ref, x_ref, wgu_hbm, bgu_ref, wd_hbm, bd_ref, o_ref,
                gu_f32, dn_f32, gu_s, dn_s, sem):
    i = pl.program_id(0)
    used = i < nu_ref[0]

    def weight_copies(expert, slot):
        return (pltpu.make_async_copy(wgu_hbm.at[expert], gu_f32.at[slot], sem.at[0, slot]),
                pltpu.make_async_copy(wd_hbm.at[expert], dn_f32.at[slot], sem.at[1, slot]))

    @pl.when(jnp.logical_and(used, plan_ref[PLAN_FIRST, i] == 1))
    def _():
        slot = plan_ref[PLAN_SLOT, i]
        nxt = plan_ref[PLAN_NEXT, i]

        @pl.when(i == 0)
        def _():
            for cp in weight_copies(plan_ref[PLAN_EXPERT, 0], plan_ref[PLAN_SLOT, 0]):
                cp.start()

        for cp in weight_copies(plan_ref[PLAN_EXPERT, i], slot):
            cp.wait()

        @pl.when(nxt >= 0)
        def _():
            for cp in weight_copies(nxt, 1 - slot):
                cp.start()

        for r in range(D_MODEL // 128):
            gu_s[r * 128:(r + 1) * 128, :] = gu_f32[slot, r * 128:(r + 1) * 128, :].astype(BF16)
            dn_s[r * 128:(r + 1) * 128, :] = dn_f32[slot, r * 128:(r + 1) * 128, :].astype(BF16)

    @pl.when(used)
    def _():
        x = x_ref[...]
        gate = _dot(x, gu_s[:, 0:D_FF]) + bgu_ref[0, :, 0:D_FF]
        up = _dot(x, gu_s[:, D_FF:2 * D_FF]) + bgu_ref[0, :, D_FF:2 * D_FF]
        gate = jnp.minimum(gate, SWIGLU_LIMIT)
        up = jnp.clip(up, -SWIGLU_LIMIT, SWIGLU_LIMIT)
        act = gate * _sigmoid(SWIGLU_ALPHA * gate) * (up + 1.0)
        o_ref[...] = (_dot(act.astype(BF16), dn_s[...]) + bd_ref[0]).astype(o_ref.dtype)

    @pl.when(jnp.logical_not(used))
    def _():
        o_ref[...] = jnp.zeros(o_ref.shape, o_ref.dtype)


def _moe_call(plan, n_used, xs, w_gate_up, b_gate_up, w_down, b_down):
    p = xs.shape[0]
    n_tiles = p // MOE_TM
    expert_of = lambda i, plan, nu: (plan[PLAN_EXPERT, i], 0, 0)
    grid_spec = pltpu.PrefetchScalarGridSpec(
        num_scalar_prefetch=2,
        grid=(n_tiles,),
        in_specs=[pl.BlockSpec((MOE_TM, D_MODEL), lambda i, plan, nu: (i, 0)),
                  pl.BlockSpec(memory_space=pl.ANY),
                  pl.BlockSpec((1, 1, 2 * D_FF), expert_of),
                  pl.BlockSpec(memory_space=pl.ANY),
                  pl.BlockSpec((1, 1, D_MODEL), expert_of)],
        out_specs=pl.BlockSpec((MOE_TM, D_MODEL), lambda i, plan, nu: (i, 0)),
        scratch_shapes=[pltpu.VMEM((2, D_MODEL, 2 * D_FF), F32), pltpu.VMEM((2, D_FF, D_MODEL), F32),
                        pltpu.VMEM((D_MODEL, 2 * D_FF), BF16), pltpu.VMEM((D_FF, D_MODEL), BF16),
                        pltpu.SemaphoreType.DMA((2, 2))],
    )
    return pl.pallas_call(
        _moe_kernel,
        grid_spec=grid_spec,
        out_shape=jax.ShapeDtypeStruct((p, D_MODEL), BF16),
        compiler_params=pltpu.CompilerParams(dimension_semantics=("arbitrary",),
                                             vmem_limit_bytes=VMEM_LIMIT),
        name="moe_experts",
    )(plan, n_used, xs, w_gate_up, b_gate_up.reshape(N_EXPERTS, 1, 2 * D_FF),
      w_down, b_down.reshape(N_EXPERTS, 1, D_MODEL))


def _combine_kernel(x1_ref, yg_ref, tw_ref, mod_ref, g_ref, o_ref):
    tw = tw_ref[0]
    acc = tw[:, 0:1] * yg_ref[0, 0].astype(F32)
    for k in range(1, TOP_K):
        acc = acc + tw[:, k:k + 1] * yg_ref[k, 0].astype(F32)
    g2 = mod_ref[0, :, 5 * D_MODEL:6 * D_MODEL]
    x2 = x1_ref[0] + g2 * acc
    ms = jnp.mean(x2 * x2, axis=-1, keepdims=True)
    o_ref[0] = x2 * lax.rsqrt(ms + RMS_EPS) * g_ref[...]


def _combine_call(x1, yg, top_w, mod, norm_g):
    b, l, _ = x1.shape
    tl = 256
    per_batch = mod.shape[0] > 1
    mod_map = (lambda i, j: (i, 0, 0)) if per_batch else (lambda i, j: (0, 0, 0))
    tok = lambda w: pl.BlockSpec((1, tl, w), lambda i, j: (i, j, 0))
    return pl.pallas_call(
        _combine_kernel,
        grid=(b, l // tl),
        in_specs=[tok(D_MODEL),
                  pl.BlockSpec((TOP_K, 1, tl, D_MODEL), lambda i, j: (0, i, j, 0)),
                  tok(TOP_K),
                  pl.BlockSpec((1, 1, 6 * D_MODEL), mod_map),
                  pl.BlockSpec((1, D_MODEL), lambda i, j: (0, 0))],
        out_specs=tok(D_MODEL),
        out_shape=jax.ShapeDtypeStruct((b, l, D_MODEL), F32),
        compiler_params=pltpu.CompilerParams(dimension_semantics=("arbitrary", "arbitrary"),
                                             vmem_limit_bytes=VMEM_LIMIT),
        name="moe_combine_final_norm",
    )(x1, yg, top_w, mod, norm_g.reshape(1, D_MODEL))


def _route(top_i, rank, counts, cnt_tiles):
    t = top_i.shape[0]
    p = t * TOP_K + N_EXPERTS * MOE_TM
    n_tiles = p // MOE_TM
    padded = ((counts + MOE_TM - 1) // MOE_TM) * MOE_TM
    ends = jnp.cumsum(padded)
    starts = ends - padded
    take = lambda table, idx: table.at[idx].get(mode="promise_in_bounds")
    dest = take(starts, top_i) + rank
    tile_start = jnp.arange(n_tiles, dtype=jnp.int32) * MOE_TM
    tile_expert = jnp.minimum(jnp.sum((ends[None, :] <= tile_start[:, None]).astype(jnp.int32), axis=1),
                              N_EXPERTS - 1)
    n_used = (ends[-1] // MOE_TM).reshape(1)
    tile_id = jnp.arange(n_tiles, dtype=jnp.int32)
    prev_expert = jnp.concatenate([jnp.full((1,), -1, tile_expert.dtype), tile_expert[:-1]])
    first = ((tile_expert != prev_expert) & (tile_id < n_used[0])).astype(jnp.int32)
    slot = (jnp.cumsum(first) - 1) % 2
    eid = jnp.arange(N_EXPERTS, dtype=jnp.int32)
    later = jnp.where((eid[None, :] > eid[:, None]) & (counts[None, :] > 0), eid[None, :], N_EXPERTS)
    next_of = jnp.min(later, axis=1)
    next_of = jnp.where(next_of < N_EXPERTS, next_of, -1)
    nxt = jnp.sum(jnp.where(tile_expert[:, None] == eid[None, :], next_of[None, :], 0), axis=1)
    plan = jnp.stack([tile_expert, first, nxt, slot]).astype(jnp.int32)
    tok_tiles = cnt_tiles.shape[0]
    block = (t * TOP_K) // tok_tiles
    excl = jnp.cumsum(cnt_tiles, axis=0) - cnt_tiles
    off = jnp.cumsum(cnt_tiles, axis=1) - cnt_tiles
    excl_e = take(excl.T, tile_expert)
    off_e = take(off.T, tile_expert)
    r = tile_start[:, None] + jnp.arange(MOE_TM, dtype=jnp.int32)[None, :] - take(starts, tile_expert)[:, None]
    ti = jnp.sum((excl_e[:, None, :] <= r[:, :, None]).astype(jnp.int32), axis=2) - 1
    here = jnp.arange(tok_tiles, dtype=jnp.int32)[None, None, :] == ti[:, :, None]
    excl_i = jnp.sum(jnp.where(here, excl_e[:, None, :], 0), axis=2)
    off_i = jnp.sum(jnp.where(here, off_e[:, None, :], 0), axis=2)
    valid = r < take(counts, tile_expert)[:, None]
    spread = (jnp.arange(p, dtype=jnp.int32) % (t * TOP_K)).reshape(n_tiles, MOE_TM)
    src_row = jnp.where(valid, ti * block + off_i + r - excl_i, spread).reshape(p)
    return dest, src_row, plan, n_used.astype(jnp.int32)


def _layer_front(x, mod, s_ssd_f, s_ssd_b, s_s5_f, s_s5_b, grid_cols, prm, xs_prev):
    z, xbc, u, dt = _inproj_call(x, mod, prm["norm1_g"], prm["w_in_r"])
    y_ssd, n_ssd_f, n_ssd_b = _ssd_call(z, xbc, dt, s_ssd_f, s_ssd_b, prm["ssd"])
    y_f, n_s5_f = _s5_call(u, _s5_state_in(s_s5_f), prm["s5_f"], grid_cols, False)
    y_s5, n_s5_b = _s5_call(u, _s5_state_in(s_s5_b), prm["s5_b"], grid_cols, True,
                            y_first=y_f, glu=prm["glu"])
    x1, xs_buf, top_i, top_w, rank, cnt_tiles, cnt = _post_call(
        x, y_ssd, y_s5, mod, prm["w_out"], prm["norm2_g"], prm["w_router"], prm["b_router"], xs_prev)
    states = (n_ssd_f, n_ssd_b, _s5_state_out(n_s5_f), _s5_state_out(n_s5_b))
    return x1, (xs_buf, top_i, rank, cnt, cnt_tiles), top_w, states


def kernel(x_prompt, x_sample, state_ssd_fwd, state_ssd_bwd, state_s5_fwd, state_s5_bwd, c, c_ctx, w_ada, b_ada, norm1_g, w_in, ssd_conv_w, ssd_conv_b, ssd_dt_bias_fwd, ssd_dt_bias_bwd, ssd_a_log_fwd, ssd_a_log_bwd, ssd_d, ssd_norm_g, s5_a_re_fwd, s5_a_im_fwd, s5_log_dt_fwd, s5_b_re_fwd, s5_b_im_fwd, s5_c_re_fwd, s5_c_im_fwd, s5_a_re_bwd, s5_a_im_bwd, s5_log_dt_bwd, s5_b_re_bwd, s5_b_im_bwd, s5_c_re_bwd, s5_c_im_bwd, s5_d, w_glu, b_glu, w_out, norm2_g, w_router, b_router, w_gate_up, b_gate_up, w_down, b_down, norm_f_g):
    depth = w_ada.shape[0]
    assert depth == 1, "single trunk layer"
    nb_ctx, l_ctx, _ = x_prompt.shape
    nb_lat, l_lat, _ = x_sample.shape

    w_in0 = w_in[0]
    dt_lo = SSD_WIDTH + SSD_CONV_CH
    dt_cols = jnp.pad(w_in0[:, dt_lo:dt_lo + 2 * SSD_HEADS], ((0, 0), (0, DT_PAD - 2 * SSD_HEADS)))
    w_in_r = jnp.concatenate([w_in0[:, :dt_lo], w_in0[:, dt_lo + 2 * SSD_HEADS:], dt_cols], axis=1).astype(BF16)
    pad_dt = lambda f, b: jnp.pad(jnp.concatenate([f, b]).astype(F32), (0, DT_PAD - 2 * SSD_HEADS)).reshape(1, DT_PAD)
    ssd_prm = {
        "conv_w": jnp.pad(ssd_conv_w[0].astype(F32), ((0, 5), (0, 0))),
        "conv_b": ssd_conv_b[0].astype(F32).reshape(1, SSD_CONV_CH),
        "dt_bias": pad_dt(ssd_dt_bias_fwd[0], ssd_dt_bias_bwd[0]),
        "a": pad_dt(-jnp.exp(ssd_a_log_fwd[0].astype(F32)), -jnp.exp(ssd_a_log_bwd[0].astype(F32))),
        "d_skip": jnp.repeat(ssd_d[0].astype(F32), SSD_HEAD_DIM).reshape(1, SSD_WIDTH),
        "norm_g": ssd_norm_g[0].astype(F32).reshape(1, SSD_WIDTH),
        "expand": (jnp.arange(DT_PAD)[None, :, None]
                   == (jnp.arange(SSD_WIDTH) // SSD_HEAD_DIM)[None, None, :]
                   + SSD_HEADS * jnp.arange(2)[:, None, None]).astype(BF16),
    }
    prm = {
        "norm1_g": norm1_g[0], "w_in_r": w_in_r, "ssd": ssd_prm,
        "s5_f": _s5_params(s5_a_re_fwd[0], s5_a_im_fwd[0], s5_log_dt_fwd[0], s5_b_re_fwd[0],
                           s5_b_im_fwd[0], s5_c_re_fwd[0], s5_c_im_fwd[0]),
        "s5_b": _s5_params(s5_a_re_bwd[0], s5_a_im_bwd[0], s5_log_dt_bwd[0], s5_b_re_bwd[0],
                           s5_b_im_bwd[0], s5_c_re_bwd[0], s5_c_im_bwd[0]),
        "glu": {"d": s5_d[0].astype(F32).reshape(1, S5_WIDTH), "w": w_glu[0].astype(BF16),
                "b": b_glu[0].astype(F32).reshape(1, 2 * S5_WIDTH)},
        "w_out": w_out[0].astype(BF16), "norm2_g": norm2_g[0],
        "w_router": w_router[0].astype(F32), "b_router": b_router[0].astype(F32),
    }

    conds = jnp.concatenate([c_ctx[None, :], c], axis=0)
    conds = jnp.pad(conds, ((0, (-conds.shape[0]) % 8), (0, 0)))
    mod = _ada_call(conds, w_ada[0], b_ada[0])
    mod_ctx = mod[0:1].reshape(1, 1, 6 * D_MODEL)
    mod_lat = mod[1:1 + nb_lat].reshape(nb_lat, 1, 6 * D_MODEL)

    zero_ssd = jnp.zeros((nb_ctx, SSD_HEADS, SSD_HEAD_DIM, SSD_STATE), F32)
    zero_s5 = jnp.zeros((nb_ctx, S5_GROUPS, S5_STATE, 2), F32)
    t_ctx = nb_ctx * l_ctx
    x1_c, routed_c, tw_c, st_c = _layer_front(
        x_prompt, mod_ctx, zero_ssd, zero_ssd, zero_s5, zero_s5, False, prm, None)
    x1_l, routed_l, tw_l, _ = _layer_front(
        x_sample, mod_lat, state_ssd_fwd[:, 0], state_ssd_bwd[:, 0], state_s5_fwd[:, 0],
        state_s5_bwd[:, 0], True, prm, routed_c[0])

    _, ti_c, rk_c, cnt_c, ct_c = routed_c
    xs_tiles, ti_l, rk_l, cnt_l, ct_l = routed_l
    cnt_c = cnt_c.reshape(N_EXPERTS).astype(jnp.int32)
    cnt_l = cnt_l.reshape(N_EXPERTS).astype(jnp.int32)
    cnt_tiles = jnp.concatenate([ct_c, ct_l], axis=0).reshape(-1, N_EXPERTS).astype(jnp.int32)
    ti_l2 = ti_l.reshape(-1, TOP_K)
    top_i = jnp.concatenate([ti_c.reshape(-1, TOP_K), ti_l2], axis=0)
    rank = jnp.concatenate([rk_c.reshape(-1, TOP_K),
                            rk_l.reshape(-1, TOP_K) + cnt_c.at[ti_l2].get(mode="promise_in_bounds")], axis=0)
    dest, src_row, plan, n_used = _route(top_i, rank, cnt_c + cnt_l, cnt_tiles)
    xs = xs_tiles.at[src_row].get(mode="promise_in_bounds")
    ys = _moe_call(plan, n_used, xs, w_gate_up[0], b_gate_up[0], w_down[0], b_down[0])

    def picked_rows(d, nb, l):
        rows = ys.at[d.T.reshape(-1)].get(mode="promise_in_bounds")
        return rows.reshape(TOP_K, nb, l, D_MODEL)

    y_prompt = _combine_call(x1_c, picked_rows(dest[:t_ctx], nb_ctx, l_ctx), tw_c, mod_ctx, norm_f_g)
    y_sample = _combine_call(x1_l, picked_rows(dest[t_ctx:], nb_lat, l_lat), tw_l, mod_lat, norm_f_g)

    ssd_f, ssd_b, s5_f, s5_b = st_c
    return (y_prompt, y_sample, ssd_f[:, None], ssd_b[:, None], s5_f[:, None], s5_b[:, None])
```

```python
import functools
import math

import jax
import jax.numpy as jnp
from jax import lax
from jax.experimental import pallas as pl
from jax.experimental.pallas import tpu as pltpu

F32 = jnp.float32
BF16 = jnp.bfloat16

D_MODEL = 1024
GRID_W = 64
SSD_WIDTH = 512
SSD_HEAD_DIM = 64
SSD_HEADS = 8
SSD_GROUPS = 2
SSD_HEADS_PER_GROUP = SSD_HEADS // SSD_GROUPS
SSD_STATE = 128
SSD_CHUNK = 128
SSD_CONV_CH = SSD_WIDTH + 2 * SSD_GROUPS * SSD_STATE
S5_WIDTH = 512
S5_CH = 16
S5_GROUPS = 32
S5_STATE = 64
N_EXPERTS = 32
TOP_K = 4
D_FF = 1024
SWIGLU_LIMIT = 7.0
SWIGLU_ALPHA = 1.702
RMS_EPS = 1e-5

LANES = 128
DT_PAD = 128
S5_BATCH = 8
S5_COLS = 2 * S5_GROUPS * S5_STATE
S5_BLK = 512
S5_SUB_STEPS = 32
S5_BLOCK_STEPS = 256
MOE_TM = 256
VMEM_LIMIT = 56 * 1024 * 1024


def _sigmoid(x):
    return 1.0 / (1.0 + jnp.exp(-x))


def _split3(x):
    hi = x.astype(BF16)
    r1 = x - hi.astype(F32)
    mid = r1.astype(BF16)
    lo = (r1 - mid.astype(F32)).astype(BF16)
    return hi, mid, lo


def _split2(x):
    hi = x.astype(BF16)
    return hi, (x - hi.astype(F32)).astype(BF16)


def _dot(a, b):
    return jnp.dot(a, b, preferred_element_type=F32)


def _ada_kernel(c_ref, w_ref, b_ref, o_ref):
    c = c_ref[...]
    s = c * _sigmoid(c)
    o_ref[...] = _dot(s.astype(BF16), w_ref[...].astype(BF16)) + b_ref[...]


def _ada_call(conds, w_ada, b_ada):
    n = w_ada.shape[1]
    tn = 1536
    rows = conds.shape[0]
    return pl.pallas_call(
        _ada_kernel,
        grid=(n // tn,),
        in_specs=[pl.BlockSpec((rows, D_MODEL), lambda j: (0, 0)),
                  pl.BlockSpec((D_MODEL, tn), lambda j: (0, j)),
                  pl.BlockSpec((1, tn), lambda j: (0, j))],
        out_specs=pl.BlockSpec((rows, tn), lambda j: (0, j)),
        out_shape=jax.ShapeDtypeStruct((rows, n), F32),
        compiler_params=pltpu.CompilerParams(dimension_semantics=("arbitrary",),
                                             vmem_limit_bytes=VMEM_LIMIT),
        name="ada_mod",
    )(conds, w_ada, b_ada.reshape(1, n))


def _inproj_kernel(x_ref, mod_ref, g_ref, w_ref, z_ref, xbc_ref, u_ref, dt_ref):
    x = x_ref[0]
    ms = jnp.mean(x * x, axis=-1, keepdims=True)
    y = x * lax.rsqrt(ms + RMS_EPS) * g_ref[...]
    sh = mod_ref[0, :, 0:D_MODEL]
    sc = mod_ref[0, :, D_MODEL:2 * D_MODEL]
    h = (y * (1.0 + sc) + sh).astype(BF16)
    z_ref[0] = _dot(h, w_ref[:, 0:512]).astype(BF16)
    xbc_ref[0] = _dot(h, w_ref[:, 512:1536]).astype(BF16)
    u_ref[0] = _dot(h, w_ref[:, 1536:2048])
    dt_ref[0] = _dot(h, w_ref[:, 2048:2048 + DT_PAD])


def _inproj_call(x, mod, norm_g, w_in_r):
    b, l, _ = x.shape
    tl = min(l, 512)
    per_batch = mod.shape[0] > 1
    mod_map = (lambda i, j: (i, 0, 0)) if per_batch else (lambda i, j: (0, 0, 0))
    ncol = w_in_r.shape[1]
    tok = lambda w: pl.BlockSpec((1, tl, w), lambda i, j: (i, j, 0))
    return pl.pallas_call(
        _inproj_kernel,
        grid=(b, l // tl),
        in_specs=[tok(D_MODEL),
                  pl.BlockSpec((1, 1, 6 * D_MODEL), mod_map),
                  pl.BlockSpec((1, D_MODEL), lambda i, j: (0, 0)),
                  pl.BlockSpec((D_MODEL, ncol), lambda i, j: (0, 0))],
        out_specs=[tok(SSD_WIDTH), tok(SSD_CONV_CH), tok(S5_WIDTH), tok(DT_PAD)],
        out_shape=[jax.ShapeDtypeStruct((b, l, SSD_WIDTH), BF16),
                   jax.ShapeDtypeStruct((b, l, SSD_CONV_CH), BF16),
                   jax.ShapeDtypeStruct((b, l, S5_WIDTH), F32),
                   jax.ShapeDtypeStruct((b, l, DT_PAD), F32)],
        compiler_params=pltpu.CompilerParams(dimension_semantics=("arbitrary", "arbitrary"),
                                             vmem_limit_bytes=VMEM_LIMIT),
        name="norm1_inproj",
    )(x, mod, norm_g.reshape(1, D_MODEL), w_in_r)


def _softplus(x):
    return jnp.maximum(x, 0.0) + jnp.log1p(jnp.exp(-jnp.abs(x)))


def _ssd_kernel(z_ref, xbc_ref, dt_ref, s0f_ref, s0b_ref, cw_ref, cb_ref, dtb_ref, a_ref,
                dskip_ref, ng_ref, ex_ref, y_ref, sf_ref, sb_ref,
                xc_s, dts_s, yf_s, yb_s, stf_s, stb_s, *, seq):
    q = SSD_CHUNK
    nc = seq // q
    gw = SSD_HEADS_PER_GROUP * SSD_HEAD_DIM
    row_i = lax.broadcasted_iota(jnp.int32, (q, 1), 0)

    def conv_body(c, carry):
        r0 = pl.multiple_of(c * q, q)
        cur = xbc_ref[0, pl.ds(r0, q), :].astype(F32)
        p0 = pl.multiple_of(jnp.maximum(r0 - 16, 0), 16)
        n0 = pl.multiple_of(jnp.minimum(r0 + q, seq - 16), 16)
        prev_row = xbc_ref[0, pl.ds(p0, 16), :].astype(F32)[15:16, :]
        next_row = xbc_ref[0, pl.ds(n0, 16), :].astype(F32)[0:1, :]
        prev_row = jnp.where(c > 0, prev_row, 0.0)
        next_row = jnp.where(c < nc - 1, next_row, 0.0)
        down = jnp.where(row_i == 0, prev_row, pltpu.roll(cur, 1, axis=0))
        up = jnp.where(row_i == q - 1, next_row, pltpu.roll(cur, q - 1, axis=0))
        v = cw_ref[0:1, :] * down + cw_ref[1:2, :] * cur + cw_ref[2:3, :] * up + cb_ref[...]
        xc_s[pl.ds(r0, q), :] = v * _sigmoid(v)
        dts_s[pl.ds(r0, q), :] = _softplus(dt_ref[0, pl.ds(r0, q), :] + dtb_ref[...])
        return carry

    lax.fori_loop(0, nc, conv_body, 0)

    li = lax.broadcasted_iota(jnp.int32, (q, q), 0)
    si = lax.broadcasted_iota(jnp.int32, (q, q), 1)
    head_of_lane = lax.broadcasted_iota(jnp.int32, (q, gw), 1) // SSD_HEAD_DIM

    def expand(v, e_mat):
        hi, lo = _split2(v)
        return _dot(hi, e_mat) + _dot(lo, e_mat)

    def chunk(c, direction):
        r0 = pl.multiple_of(c * q, q)
        st_ref = stf_s if direction == 0 else stb_s
        y_dst = yf_s if direction == 0 else yb_s
        e_mat = ex_ref[direction]
        causal = (li >= si) if direction == 0 else (li <= si)
        tri = jnp.where(causal, 1.0, 0.0).astype(BF16)
        dts = dts_s[pl.ds(r0, q), :]
        dta = dts * a_ref[...]
        d_hi, d_mid, d_lo = _split3(dta)
        cs = _dot(tri, d_hi) + _dot(tri, d_mid) + _dot(tri, d_lo)
        tot = cs[q - 1:q, :] if direction == 0 else cs[0:1, :]
        cs_t = cs.T
        ecs = jnp.exp(cs)
        tail = jnp.exp(tot - cs)
        dt_x = expand(dts, e_mat)
        dtw_x = expand(dts * tail, e_mat)
        etot_x = expand(jnp.broadcast_to(jnp.exp(tot), (8, DT_PAD)), e_mat)[0:1, :]
        for g in range(SSD_GROUPS):
            b_off = SSD_WIDTH + g * SSD_STATE
            c_off = SSD_WIDTH + SSD_GROUPS * SSD_STATE + g * SSD_STATE
            x_g = xc_s[pl.ds(r0, q), g * gw:(g + 1) * gw]
            bm = xc_s[pl.ds(r0, q), b_off:b_off + SSD_STATE]
            cm = xc_s[pl.ds(r0, q), c_off:c_off + SSD_STATE]
            cb = lax.dot_general(cm.astype(BF16), bm.astype(BF16), (((1,), (1,)), ((), ())),
                                 preferred_element_type=F32)
            st_g = st_ref[g]
            xdt = x_g * dt_x[:, g * gw:(g + 1) * gw]
            lhs, rhs = [], []
            for hh in range(SSD_HEADS_PER_GROUP):
                col = g * SSD_HEADS_PER_GROUP + hh + SSD_HEADS * direction
                seg = jnp.broadcast_to(cs[:, col:col + 1], (q, q)) - cs_t[col:col + 1, :]
                dec = jnp.exp(jnp.where(causal, seg, -1e30))
                lhs.append((cb * dec).astype(BF16))
                lhs.append((cm * jnp.broadcast_to(ecs[:, col:col + 1], (q, q))).astype(BF16))
                own = head_of_lane == hh
                rhs.append(jnp.where(own, xdt, 0.0).astype(BF16))
                rhs.append(jnp.where(own, st_g, 0.0).astype(BF16))
            y_dst[pl.ds(r0, q), g * gw:(g + 1) * gw] = _dot(jnp.concatenate(lhs, axis=1),
                                                             jnp.concatenate(rhs, axis=0))
            xw = (x_g * dtw_x[:, g * gw:(g + 1) * gw]).astype(BF16)
            st_ref[g] = etot_x[:, g * gw:(g + 1) * gw] * st_g + _dot(bm.T.astype(BF16), xw)

    stf_s[...] = s0f_ref[0]
    stb_s[...] = s0b_ref[0]

    def both(i, carry):
        chunk(i, 0)
        chunk(nc - 1 - i, 1)
        return carry

    lax.fori_loop(0, nc, both, 0)
    sf_ref[0] = stf_s[...]
    sb_ref[0] = stb_s[...]

    def finish(c, carry):
        r0 = pl.multiple_of(c * q, q)
        xs = xc_s[pl.ds(r0, q), 0:SSD_WIDTH]
        zz = z_ref[0, pl.ds(r0, q), :].astype(F32)
        y = (yf_s[pl.ds(r0, q), :] + yb_s[pl.ds(r0, q), :] + dskip_ref[...] * xs) * (zz * _sigmoid(zz))
        ms = jnp.mean(y * y, axis=-1, keepdims=True)
        y_ref[0, pl.ds(r0, q), :] = (y * lax.rsqrt(ms + RMS_EPS) * ng_ref[...]).astype(y_ref.dtype)
        return carry

    lax.fori_loop(0, nc, finish, 0)


def _ssd_state_in(s):
    b = s.shape[0]
    s = s.astype(F32).reshape(b, SSD_GROUPS, SSD_HEADS_PER_GROUP, SSD_HEAD_DIM, SSD_STATE)
    return s.transpose(0, 1, 4, 2, 3).reshape(b, SSD_GROUPS, SSD_STATE, SSD_HEADS_PER_GROUP * SSD_HEAD_DIM)


def _ssd_state_out(s):
    b = s.shape[0]
    s = s.reshape(b, SSD_GROUPS, SSD_STATE, SSD_HEADS_PER_GROUP, SSD_HEAD_DIM)
    return s.transpose(0, 1, 3, 4, 2).reshape(b, SSD_HEADS, SSD_HEAD_DIM, SSD_STATE)


def _ssd_call(z, xbc, dt, s0f, s0b, prm):
    b, l, _ = z.shape
    st_shape = (SSD_GROUPS, SSD_STATE, SSD_HEADS_PER_GROUP * SSD_HEAD_DIM)
    tok = lambda w: pl.BlockSpec((1, l, w), lambda i: (i, 0, 0))
    st_spec = pl.BlockSpec((1,) + st_shape, lambda i: (i, 0, 0, 0))
    par = lambda r, w: pl.BlockSpec((r, w), lambda i: (0, 0))
    y, sf, sb = pl.pallas_call(
        functools.partial(_ssd_kernel, seq=l),
        grid=(b,),
        in_specs=[tok(SSD_WIDTH), tok(SSD_CONV_CH), tok(DT_PAD), st_spec, st_spec,
                  par(8, SSD_CONV_CH), par(1, SSD_CONV_CH), par(1, DT_PAD), par(1, DT_PAD),
                  par(1, SSD_WIDTH), par(1, SSD_WIDTH),
                  pl.BlockSpec((2, DT_PAD, SSD_WIDTH), lambda i: (0, 0, 0))],
        out_specs=[tok(SSD_WIDTH), st_spec, st_spec],
        out_shape=[jax.ShapeDtypeStruct((b, l, SSD_WIDTH), BF16),
                   jax.ShapeDtypeStruct((b,) + st_shape, F32),
                   jax.ShapeDtypeStruct((b,) + st_shape, F32)],
        scratch_shapes=[pltpu.VMEM((l, SSD_CONV_CH), F32),
                        pltpu.VMEM((l, DT_PAD), F32),
                        pltpu.VMEM((l, SSD_WIDTH), F32),
                        pltpu.VMEM((l, SSD_WIDTH), F32),
                        pltpu.VMEM(st_shape, F32),
                        pltpu.VMEM(st_shape, F32)],
        compiler_params=pltpu.CompilerParams(dimension_semantics=("arbitrary",),
                                             vmem_limit_bytes=VMEM_LIMIT),
        name="ssd_mixer",
    )(z, xbc, dt, _ssd_state_in(s0f), _ssd_state_in(s0b),
      prm["conv_w"], prm["conv_b"], prm["dt_bias"], prm["a"], prm["d_skip"], prm["norm_g"], prm["expand"])
    return y, _ssd_state_out(sf), _ssd_state_out(sb)


def _gelu_tanh(x):
    return 0.5 * x * (1.0 + jnp.tanh(math.sqrt(2.0 / math.pi) * (x + 0.044715 * (x * x * x))))


def _s5_kernel(*refs, grid_cols, reverse, final):
    if final:
        (u_ref, s0_ref, lr_ref, li_ref, wb_ref, wc_ref, yf_ref, d_ref, wg_ref, bg_ref,
         o_ref, sfin_ref, ur_s, yr_s, st_s) = refs
    else:
        (u_ref, s0_ref, lr_ref, li_ref, wb_ref, wc_ref,
         o_ref, sfin_ref, ur_s, yr_s, st_s) = refs
    nb = S5_BATCH
    steps = S5_BLOCK_STEPS
    rows = steps * nb
    sub_rows = S5_SUB_STEPS * nb
    n_sub = steps // S5_SUB_STEPS
    j = pl.program_id(1)

    @pl.when(j == 0)
    def _():
        st_s[...] = s0_ref[0]

    def to_scan(v):
        v = jnp.transpose(v, (2, 1, 0, 3)) if grid_cols else jnp.swapaxes(v, 0, 1)
        return v.reshape(rows, S5_WIDTH)

    def from_scan(v):
        if grid_cols:
            return jnp.transpose(v.reshape(u_ref.shape[2], u_ref.shape[1], nb, S5_WIDTH), (2, 1, 0, 3))
        return jnp.swapaxes(v.reshape(steps, nb, S5_WIDTH), 0, 1)

    ur_s[...] = to_scan(u_ref[...])

    order = list(range(S5_SUB_STEPS))[::-1] if reverse else list(range(S5_SUB_STEPS))
    blk_per_half = S5_COLS // S5_BLK // 2

    def by_groups(v, perm):
        return jnp.concatenate([v[g * nb:(g + 1) * nb] for g in perm], axis=0)

    def sub_chunk(i, carry):
        sidx = (n_sub - 1 - i) if reverse else i
        r0 = pl.multiple_of(sidx * sub_rows, sub_rows)
        for hf in range(2):
            u_half = ur_s[pl.ds(r0, sub_rows), hf * 256:(hf + 1) * 256]
            if reverse:
                u_half = by_groups(u_half, order)
            uh = u_half.astype(BF16)
            for bq in range(blk_per_half):
                blk = hf * blk_per_half + bq
                c0 = blk * S5_BLK
                lam_r = jnp.broadcast_to(lr_ref[:, c0:c0 + S5_BLK], (nb, S5_BLK))
                lam_i = jnp.broadcast_to(li_ref[:, c0:c0 + S5_BLK], (nb, S5_BLK))
                bu = _dot(uh, wb_ref[hf, :, bq * S5_BLK:(bq + 1) * S5_BLK])
                s = st_s[:, c0:c0 + S5_BLK]
                states = []
                for k in range(S5_SUB_STEPS):
                    sw = jnp.concatenate([s[:, S5_BLK // 2:], s[:, :S5_BLK // 2]], axis=1)
                    s = lam_r * s + lam_i * sw + bu[k * nb:(k + 1) * nb, :]
                    states.append(s)
                st_s[:, c0:c0 + S5_BLK] = s
                y_blk = _dot(jnp.concatenate(states, axis=0).astype(BF16), wc_ref[blk])
                if reverse:
                    y_blk = by_groups(y_blk, order)
                yr_s[pl.ds(r0, sub_rows), blk * 64:(blk + 1) * 64] = y_blk
        return carry

    lax.fori_loop(0, n_sub, sub_chunk, 0)

    @pl.when(j == pl.num_programs(1) - 1)
    def _():
        sfin_ref[0] = st_s[...]

    if not final:
        o_ref[0] = yr_s[...]
        return

    def glu_chunk(i, carry):
        r0 = pl.multiple_of(i * sub_rows, sub_rows)
        y = (yr_s[pl.ds(r0, sub_rows), :] + yf_ref[0, pl.ds(r0, sub_rows), :]
             + d_ref[...] * ur_s[pl.ds(r0, sub_rows), :])
        y = _gelu_tanh(y).astype(BF16)
        val = _dot(y, wg_ref[:, 0:S5_WIDTH]) + bg_ref[:, 0:S5_WIDTH]
        gate = _dot(y, wg_ref[:, S5_WIDTH:2 * S5_WIDTH]) + bg_ref[:, S5_WIDTH:2 * S5_WIDTH]
        yr_s[pl.ds(r0, sub_rows), :] = val * _sigmoid(gate)
        return carry

    lax.fori_loop(0, n_sub, glu_chunk, 0)

    o_ref[...] = from_scan(yr_s[...])


def _s5_call(u, s0, prm, grid_cols, reverse, y_first=None, glu=None):
    b, l, _ = u.shape
    nbg = b // S5_BATCH
    nblk = l // S5_BLOCK_STEPS
    rows = S5_BLOCK_STEPS * S5_BATCH
    final = y_first is not None
    blk_of = (lambda j: nblk - 1 - j) if reverse else (lambda j: j)
    if grid_cols:
        n_rows = l // GRID_W
        u_in = u.reshape(b, n_rows, GRID_W, S5_WIDTH)
        cols_per_blk = S5_BLOCK_STEPS // n_rows
        tok_spec = pl.BlockSpec((S5_BATCH, n_rows, cols_per_blk, S5_WIDTH),
                                lambda i, j: (i, 0, blk_of(j), 0))
        out_nat = jax.ShapeDtypeStruct((b, n_rows, GRID_W, S5_WIDTH), F32)
    else:
        u_in = u
        tok_spec = pl.BlockSpec((S5_BATCH, S5_BLOCK_STEPS, S5_WIDTH), lambda i, j: (i, blk_of(j), 0))
        out_nat = jax.ShapeDtypeStruct((b, l, S5_WIDTH), F32)
    scan_spec = pl.BlockSpec((1, rows, S5_WIDTH), lambda i, j: (i, blk_of(j), 0))
    st_spec = pl.BlockSpec((1, S5_BATCH, S5_COLS), lambda i, j: (i, 0, 0))
    full = lambda shape: pl.BlockSpec(shape, lambda i, j: (0,) * len(shape))
    in_specs = [tok_spec, st_spec, full((1, S5_COLS)), full((1, S5_COLS)),
                full((2, 256, 2048)), full((S5_COLS // S5_BLK, S5_BLK, 64))]
    args = [u_in, s0, prm["lam_r"], prm["lam_i"], prm["wb"], prm["wc"]]
    if final:
        in_specs += [scan_spec, full((1, S5_WIDTH)), full((S5_WIDTH, 2 * S5_WIDTH)), full((1, 2 * S5_WIDTH))]
        args += [y_first, glu["d"], glu["w"], glu["b"]]
        out_specs = [tok_spec, st_spec]
        out_shape = [out_nat, jax.ShapeDtypeStruct((nbg, S5_BATCH, S5_COLS), F32)]
    else:
        out_specs = [scan_spec, st_spec]
        out_shape = [jax.ShapeDtypeStruct((nbg, l * S5_BATCH, S5_WIDTH), F32),
                     jax.ShapeDtypeStruct((nbg, S5_BATCH, S5_COLS), F32)]
    y, sfin = pl.pallas_call(
        functools.partial(_s5_kernel, grid_cols=grid_cols, reverse=reverse, final=final),
        grid=(nbg, nblk),
        in_specs=in_specs,
        out_specs=out_specs,
        out_shape=out_shape,
        scratch_shapes=[pltpu.VMEM((rows, S5_WIDTH), F32),
                        pltpu.VMEM((rows, S5_WIDTH), F32),
                        pltpu.VMEM((S5_BATCH, S5_COLS), F32)],
        compiler_params=pltpu.CompilerParams(dimension_semantics=("arbitrary", "arbitrary"),
                                             vmem_limit_bytes=VMEM_LIMIT),
        name="s5_final" if final else "s5_first",
    )(*args)
    if final:
        y = y.reshape(b, l, S5_WIDTH)
    return y, sfin


def _s5_cols(t):
    lead = t.shape[:-2]
    return t.reshape(lead + (S5_COLS // S5_BLK, S5_BLK // 2))


def _s5_state_in(s0):
    b = s0.shape[0]
    re, im = _s5_cols(s0[..., 0].astype(F32)), _s5_cols(s0[..., 1].astype(F32))
    return jnp.stack([re, im], axis=-2).reshape(b // S5_BATCH, S5_BATCH, S5_COLS)


def _s5_state_out(s):
    b = s.shape[0] * S5_BATCH
    s = s.reshape(b, S5_COLS // S5_BLK, 2, S5_BLK // 2)
    re = s[:, :, 0].reshape(b, S5_GROUPS, S5_STATE)
    im = s[:, :, 1].reshape(b, S5_GROUPS, S5_STATE)
    return jnp.stack([re, im], axis=-1)


def _s5_params(a_re, a_im, log_dt, b_re, b_im, c_re, c_im):
    a_re, a_im = a_re.astype(F32), a_im.astype(F32)
    delta = jnp.exp(log_dt.astype(F32))[:, None]
    mag = jnp.exp(a_re * delta)
    lam_re, lam_im = mag * jnp.cos(a_im * delta), mag * jnp.sin(a_im * delta)
    den = a_re * a_re + a_im * a_im
    nr = lam_re - 1.0
    f_re = (nr * a_re + lam_im * a_im) / den
    f_im = (lam_im * a_re - nr * a_im) / den
    b_re, b_im = b_re.astype(F32), b_im.astype(F32)
    bb_re = f_re[..., None] * b_re - f_im[..., None] * b_im
    bb_im = f_re[..., None] * b_im + f_im[..., None] * b_re
    nblk = S5_COLS // S5_BLK
    lr = _s5_cols(lam_re)
    li = _s5_cols(lam_im)
    lam_r = jnp.stack([lr, lr], axis=-2).reshape(1, S5_COLS)
    lam_i = jnp.stack([-li, li], axis=-2).reshape(1, S5_COLS)
    gq = S5_BLK // 2 // S5_STATE
    eye = jnp.eye(gq, dtype=F32)

    def in_block(bb):
        t = bb.reshape(nblk, gq, S5_STATE, S5_CH)
        return jnp.einsum("gh,kgpc->kgchp", eye, t).reshape(nblk, gq * S5_CH, gq * S5_STATE)

    blk_in = jnp.concatenate([in_block(bb_re), in_block(bb_im)], axis=-1)
    half = nblk // 2
    eye_h = jnp.eye(half, dtype=F32)
    wb = jnp.einsum("jk,hkcn->hjckn", eye_h, blk_in.reshape(2, half, gq * S5_CH, S5_BLK))
    wb = wb.reshape(2, half * gq * S5_CH, half * S5_BLK).astype(BF16)

    def out_block(cc):
        t = cc.astype(F32).reshape(nblk, gq, S5_CH, S5_STATE)
        return jnp.einsum("gh,kgcp->kgphc", eye, t).reshape(nblk, gq * S5_STATE, gq * S5_CH)

    wc = jnp.concatenate([out_block(c_re), -out_block(c_im)], axis=1).astype(BF16)
    return {"lam_r": lam_r, "lam_i": lam_i, "wb": wb, "wc": wc}


def _post_kernel(*refs, tile_base, prev_rows):
    if prev_rows:
        prev_hbm, refs = refs[8], refs[:8] + refs[9:]
    (x_ref, ys_ref, y5_ref, mod_ref, wo_ref, g_ref, wr_ref, br_ref,
     x1_ref, xs_hbm, ti_ref, tw_ref, rk_ref, ct_ref, cnt_ref, cnt_s, xs_v, pv_v, sem_out, sem_prev) = refs
    tl = x_ref.shape[1]
    blk = tl * TOP_K
    n_steps = pl.num_programs(0) * pl.num_programs(1)
    step = pl.program_id(0) * pl.num_programs(1) + pl.program_id(1)
    buf = step % 2
    first = step == 0

    def block_copy(at_step, at_slot):
        row0 = pl.multiple_of((tile_base + at_step) * blk, blk)
        return pltpu.make_async_copy(xs_v.at[at_slot], xs_hbm.at[pl.ds(row0, blk)], sem_out.at[at_slot])

    @pl.when(step >= 2)
    def _():
        block_copy(step - 2, buf).wait()

    prev_blocks = prev_rows // blk

    def prev_in(at_step):
        row0 = pl.multiple_of(at_step * blk, blk)
        return pltpu.make_async_copy(prev_hbm.at[pl.ds(row0, blk)], pv_v.at[at_step % 2], sem_prev.at[0, at_step % 2])

    def prev_out(at_step):
        row0 = pl.multiple_of(at_step * blk, blk)
        return pltpu.make_async_copy(pv_v.at[at_step % 2], xs_hbm.at[pl.ds(row0, blk)], sem_prev.at[1, at_step % 2])

    if prev_rows:
        @pl.when(jnp.logical_and(step >= 2, step - 2 < prev_blocks))
        def _():
            prev_out(step - 2).wait()

        @pl.when(step < prev_blocks)
        def _():
            prev_in(step).start()

        @pl.when(jnp.logical_and(step >= 1, step - 1 < prev_blocks))
        def _():
            prev_in(step - 1).wait()
            prev_out(step - 1).start()

    @pl.when(first)
    def _():
        cnt_s[...] = jnp.zeros(cnt_s.shape, cnt_s.dtype)

    mixed = (_dot(ys_ref[0], wo_ref[0:SSD_WIDTH, :])
             + _dot(y5_ref[0].astype(BF16), wo_ref[SSD_WIDTH:SSD_WIDTH + S5_WIDTH, :]))
    g1 = mod_ref[0, :, 2 * D_MODEL:3 * D_MODEL]
    sh2 = mod_ref[0, :, 3 * D_MODEL:4 * D_MODEL]
    sc2 = mod_ref[0, :, 4 * D_MODEL:5 * D_MODEL]
    x1 = x_ref[0] + g1 * mixed
    x1_ref[0] = x1
    ms = jnp.mean(x1 * x1, axis=-1, keepdims=True)
    h = x1 * lax.rsqrt(ms + RMS_EPS) * g_ref[...] * (1.0 + sc2) + sh2

    h_hi, h_mid, _ = _split3(h)
    wr = wr_ref[...]
    w_hi, w_mid, _ = _split3(wr)
    logits = _dot(h_hi, w_hi) + _dot(h_hi, w_mid) + _dot(h_mid, w_hi) + br_ref[...]

    work = logits.T[0:N_EXPERTS, :]
    eid = lax.broadcasted_iota(jnp.int32, (N_EXPERTS, tl), 0).astype(F32)
    row8 = lax.broadcasted_iota(jnp.int32, (8, tl), 0)
    onehot = jnp.zeros((N_EXPERTS, tl), F32)
    sels, vals = [], []
    top_i = jnp.zeros((8, tl), jnp.int32)
    for k in range(TOP_K):
        m = jnp.max(work, axis=0, keepdims=True)
        idx = jnp.min(jnp.where(work == m, eid, float(N_EXPERTS)), axis=0, keepdims=True)
        sel = eid == idx
        sels.append(sel)
        vals.append(m)
        top_i = jnp.where(row8 == k, idx.astype(jnp.int32), top_i)
        onehot = jnp.where(sel, 1.0, onehot)
        work = jnp.where(sel, -jnp.inf, work)
    es = [jnp.exp(v - vals[0]) for v in vals]
    den = es[0] + es[1] + es[2] + es[3]
    top_w = jnp.zeros((8, tl), F32)
    for k in range(TOP_K):
        top_w = jnp.where(row8 == k, es[k] / den, top_w)

    ri = lax.broadcasted_iota(jnp.int32, (tl, tl), 0)
    ci = lax.broadcasted_iota(jnp.int32, (tl, tl), 1)
    earlier = jnp.where(ri < ci, 1.0, 0.0).astype(BF16)
    local = _dot(onehot.astype(BF16), earlier)
    before = local + cnt_s[...]
    cnt_tile = jnp.sum(onehot, axis=1, keepdims=True)
    er = lax.broadcasted_iota(jnp.int32, (N_EXPERTS, N_EXPERTS), 0)
    ec = lax.broadcasted_iota(jnp.int32, (N_EXPERTS, N_EXPERTS), 1)
    lower = jnp.where(ec < er, 1.0, 0.0).astype(BF16)
    off = _dot(lower, jnp.broadcast_to(cnt_tile, (N_EXPERTS, LANES)).astype(BF16))[:, 0:1]
    pos = off + local
    rank = jnp.zeros((8, tl), jnp.int32)
    row_id = lax.broadcasted_iota(jnp.int32, (tl * TOP_K, tl), 0).astype(F32)
    perm = jnp.zeros((tl * TOP_K, tl), F32)
    for k in range(TOP_K):
        rk = jnp.sum(jnp.where(sels[k], before, 0.0), axis=0, keepdims=True)
        rank = jnp.where(row8 == k, rk.astype(jnp.int32), rank)
        pk = jnp.sum(jnp.where(sels[k], pos, 0.0), axis=0, keepdims=True)
        perm = jnp.where(row_id == pk, 1.0, perm)
    xs_v[buf] = _dot(perm.astype(BF16), h.astype(BF16)).astype(BF16)
    block_copy(step, buf).start()
    cnt_s[...] = cnt_s[...] + cnt_tile
    ti_ref[0] = top_i
    tw_ref[0] = top_w
    rk_ref[0] = rank
    ct_ref[0] = cnt_tile
    cnt_ref[...] = cnt_s[...]

    @pl.when(step == n_steps - 1)
    def _():
        block_copy(step, buf).wait()

        @pl.when(step >= 1)
        def _():
            block_copy(step - 1, 1 - buf).wait()


def _post_call(x, y_ssd, y_s5, mod, w_out, norm_g, w_router, b_router, xs_prev=None):
    b, l, _ = x.shape
    tl = 256
    tiles_per_row = l // tl
    prev_rows = 0 if xs_prev is None else xs_prev.shape[0]
    assert prev_rows // (tl * TOP_K) + 2 <= b * tiles_per_row, "not enough grid steps to pass the earlier rows through"
    sorted_rows = prev_rows + b * l * TOP_K
    per_batch = mod.shape[0] > 1
    mod_map = (lambda i, j: (i, 0, 0)) if per_batch else (lambda i, j: (0, 0, 0))
    tok = lambda w: pl.BlockSpec((1, tl, w), lambda i, j: (i, j, 0))
    full = lambda r, w: pl.BlockSpec((r, w), lambda i, j: (0, 0))
    slots = pl.BlockSpec((1, 8, tl), lambda i, j: (i, 0, j))
    in_specs = [tok(D_MODEL), tok(SSD_WIDTH), tok(S5_WIDTH),
                pl.BlockSpec((1, 1, 6 * D_MODEL), mod_map),
                full(D_MODEL, D_MODEL), full(1, D_MODEL), full(D_MODEL, LANES),
                full(1, LANES)]
    w_router = jnp.pad(w_router, ((0, 0), (0, LANES - N_EXPERTS)))
    b_router = jnp.pad(b_router, (0, LANES - N_EXPERTS), constant_values=-1e30)
    args = [x, y_ssd, y_s5, mod, w_out, norm_g.reshape(1, D_MODEL), w_router, b_router.reshape(1, LANES)]
    if prev_rows:
        in_specs.append(pl.BlockSpec(memory_space=pl.ANY))
        args.append(xs_prev)
    return pl.pallas_call(
        functools.partial(_post_kernel, tile_base=prev_rows // (tl * TOP_K), prev_rows=prev_rows),
        grid=(b, tiles_per_row),
        in_specs=in_specs,
        out_specs=[tok(D_MODEL),
                   pl.BlockSpec(memory_space=pl.ANY),
                   slots, slots, slots,
                   pl.BlockSpec((1, N_EXPERTS, 1), lambda i, j: (i * tiles_per_row + j, 0, 0)),
                   full(N_EXPERTS, 1)],
        out_shape=[jax.ShapeDtypeStruct((b, l, D_MODEL), F32),
                   jax.ShapeDtypeStruct((sorted_rows, D_MODEL), BF16),
                   jax.ShapeDtypeStruct((b, 8, l), jnp.int32),
                   jax.ShapeDtypeStruct((b, 8, l), F32),
                   jax.ShapeDtypeStruct((b, 8, l), jnp.int32),
                   jax.ShapeDtypeStruct((b * tiles_per_row, N_EXPERTS, 1), F32),
                   jax.ShapeDtypeStruct((N_EXPERTS, 1), F32)],
        scratch_shapes=[pltpu.VMEM((N_EXPERTS, 1), F32),
                        pltpu.VMEM((2, tl * TOP_K, D_MODEL), BF16),
                        pltpu.VMEM((2, tl * TOP_K, D_MODEL), BF16),
                        pltpu.SemaphoreType.DMA((2,)),
                        pltpu.SemaphoreType.DMA((2, 2))],
        compiler_params=pltpu.CompilerParams(dimension_semantics=("arbitrary", "arbitrary"),
                                             vmem_limit_bytes=VMEM_LIMIT),
        name="outproj_norm2_router",
    )(*args)


PLAN_EXPERT, PLAN_FIRST, PLAN_NEXT, PLAN_SLOT = range(4)


def _moe_kernel(plan_ref, nu_ref, x_ref, wgu_hbm, bgu_ref, wd_hbm, bd_ref, o_ref,
                gu_f32, dn_f32, gu_s, dn_s, sem):
    i = pl.program_id(0)
    used = i < nu_ref[0]

    def weight_copies(expert, slot):
        return (pltpu.make_async_copy(wgu_hbm.at[expert], gu_f32.at[slot], sem.at[0, slot]),
                pltpu.make_async_copy(wd_hbm.at[expert], dn_f32.at[slot], sem.at[1, slot]))

    @pl.when(jnp.logical_and(used, plan_ref[PLAN_FIRST, i] == 1))
    def _():
        slot = plan_ref[PLAN_SLOT, i]
        nxt = plan_ref[PLAN_NEXT, i]

        @pl.when(i == 0)
        def _():
            for cp in weight_copies(plan_ref[PLAN_EXPERT, 0], plan_ref[PLAN_SLOT, 0]):
                cp.start()

        for cp in weight_copies(plan_ref[PLAN_EXPERT, i], slot):
            cp.wait()

        @pl.when(nxt >= 0)
        def _():
            for cp in weight_copies(nxt, 1 - slot):
                cp.start()

        for r in range(D_MODEL // 128):
            gu_s[r * 128:(r + 1) * 128, :] = gu_f32[slot, r * 128:(r + 1) * 128, :].astype(BF16)
            dn_s[r * 128:(r + 1) * 128, :] = dn_f32[slot, r * 128:(r + 1) * 128, :].astype(BF16)

    @pl.when(used)
    def _():
        x = x_ref[...]
        gate = _dot(x, gu_s[:, 0:D_FF]) + bgu_ref[0, :, 0:D_FF]
        up = _dot(x, gu_s[:, D_FF:2 * D_FF]) + bgu_ref[0, :, D_FF:2 * D_FF]
        gate = jnp.minimum(gate, SWIGLU_LIMIT)
        up = jnp.clip(up, -SWIGLU_LIMIT, SWIGLU_LIMIT)
        act = gate * _sigmoid(SWIGLU_ALPHA * gate) * (up + 1.0)
        o_ref[...] = (_dot(act.astype(BF16), dn_s[...]) + bd_ref[0]).astype(o_ref.dtype)

    @pl.when(jnp.logical_not(used))
    def _():
        o_ref[...] = jnp.zeros(o_ref.shape, o_ref.dtype)


def _moe_call(plan, n_used, xs, w_gate_up, b_gate_up, w_down, b_down):
    p = xs.shape[0]
    n_tiles = p // MOE_TM
    expert_of = lambda i, plan, nu: (plan[PLAN_EXPERT, i], 0, 0)
    grid_spec = pltpu.PrefetchScalarGridSpec(
        num_scalar_prefetch=2,
        grid=(n_tiles,),
        in_specs=[pl.BlockSpec((MOE_TM, D_MODEL), lambda i, plan, nu: (i, 0)),
                  pl.BlockSpec(memory_space=pl.ANY),
                  pl.BlockSpec((1, 1, 2 * D_FF), expert_of),
                  pl.BlockSpec(memory_space=pl.ANY),
                  pl.BlockSpec((1, 1, D_MODEL), expert_of)],
        out_specs=pl.BlockSpec((MOE_TM, D_MODEL), lambda i, plan, nu: (i, 0)),
        scratch_shapes=[pltpu.VMEM((2, D_MODEL, 2 * D_FF), F32), pltpu.VMEM((2, D_FF, D_MODEL), F32),
                        pltpu.VMEM((D_MODEL, 2 * D_FF), BF16), pltpu.VMEM((D_FF, D_MODEL), BF16),
                        pltpu.SemaphoreType.DMA((2, 2))],
    )
    return pl.pallas_call(
        _moe_kernel,
        grid_spec=grid_spec,
        out_shape=jax.ShapeDtypeStruct((p, D_MODEL), BF16),
        compiler_params=pltpu.CompilerParams(dimension_semantics=("arbitrary",),
                                             vmem_limit_bytes=VMEM_LIMIT),
        name="moe_experts",
    )(plan, n_used, xs, w_gate_up, b_gate_up.reshape(N_EXPERTS, 1, 2 * D_FF),
      w_down, b_down.reshape(N_EXPERTS, 1, D_MODEL))


def _combine_kernel(x1_ref, yg_ref, tw_ref, mod_ref, g_ref, o_ref):
    tw = tw_ref[0]
    acc = tw[:, 0:1] * yg_ref[0, 0].astype(F32)
    for k in range(1, TOP_K):
        acc = acc + tw[:, k:k + 1] * yg_ref[k, 0].astype(F32)
    g2 = mod_ref[0, :, 5 * D_MODEL:6 * D_MODEL]
    x2 = x1_ref[0] + g2 * acc
    ms = jnp.mean(x2 * x2, axis=-1, keepdims=True)
    o_ref[0] = x2 * lax.rsqrt(ms + RMS_EPS) * g_ref[...]


def _combine_call(x1, yg, top_w, mod, norm_g):
    b, l, _ = x1.shape
    tl = 256
    per_batch = mod.shape[0] > 1
    mod_map = (lambda i, j: (i, 0, 0)) if per_batch else (lambda i, j: (0, 0, 0))
    tok = lambda w: pl.BlockSpec((1, tl, w), lambda i, j: (i, j, 0))
    return pl.pallas_call(
        _combine_kernel,
        grid=(b, l // tl),
        in_specs=[tok(D_MODEL),
                  pl.BlockSpec((TOP_K, 1, tl, D_MODEL), lambda i, j: (0, i, j, 0)),
                  tok(TOP_K),
                  pl.BlockSpec((1, 1, 6 * D_MODEL), mod_map),
                  pl.BlockSpec((1, D_MODEL), lambda i, j: (0, 0))],
        out_specs=tok(D_MODEL),
        out_shape=jax.ShapeDtypeStruct((b, l, D_MODEL), F32),
        compiler_params=pltpu.CompilerParams(dimension_semantics=("arbitrary", "arbitrary"),
                                             vmem_limit_bytes=VMEM_LIMIT),
        name="moe_combine_final_norm",
    )(x1, yg, top_w, mod, norm_g.reshape(1, D_MODEL))


def _route(top_i, rank, counts, cnt_tiles):
    t = top_i.shape[0]
    p = t * TOP_K + N_EXPERTS * MOE_TM
    n_tiles = p // MOE_TM
    padded = ((counts + MOE_TM - 1) // MOE_TM) * MOE_TM
    ends = jnp.cumsum(padded)
    starts = ends - padded
    take = lambda table, idx: table.at[idx].get(mode="promise_in_bounds")
    dest = take(starts, top_i) + rank
    tile_start = jnp.arange(n_tiles, dtype=jnp.int32) * MOE_TM
    tile_expert = jnp.minimum(jnp.sum((ends[None, :] <= tile_start[:, None]).astype(jnp.int32), axis=1),
                              N_EXPERTS - 1)
    n_used = (ends[-1] // MOE_TM).reshape(1)
    tile_id = jnp.arange(n_tiles, dtype=jnp.int32)
    prev_expert = jnp.concatenate([jnp.full((1,), -1, tile_expert.dtype), tile_expert[:-1]])
    first = ((tile_expert != prev_expert) & (tile_id < n_used[0])).astype(jnp.int32)
    slot = (jnp.cumsum(first) - 1) % 2
    eid = jnp.arange(N_EXPERTS, dtype=jnp.int32)
    later = jnp.where((eid[None, :] > eid[:, None]) & (counts[None, :] > 0), eid[None, :], N_EXPERTS)
    next_of = jnp.min(later, axis=1)
    next_of = jnp.where(next_of < N_EXPERTS, next_of, -1)
    nxt = jnp.sum(jnp.where(tile_expert[:, None] == eid[None, :], next_of[None, :], 0), axis=1)
    plan = jnp.stack([tile_expert, first, nxt, slot]).astype(jnp.int32)
    tok_tiles = cnt_tiles.shape[0]
    block = (t * TOP_K) // tok_tiles
    excl = jnp.cumsum(cnt_tiles, axis=0) - cnt_tiles
    off = jnp.cumsum(cnt_tiles, axis=1) - cnt_tiles
    excl_e = take(excl.T, tile_expert)
    shift_e = take((off - excl).T, tile_expert)
    r = tile_start[:, None] + jnp.arange(MOE_TM, dtype=jnp.int32)[None, :] - take(starts, tile_expert)[:, None]
    ti = jnp.sum((excl_e[:, None, :] <= r[:, :, None]).astype(jnp.int32), axis=2) - 1
    here = jnp.arange(tok_tiles, dtype=jnp.int32)[None, None, :] == ti[:, :, None]
    shift_i = jnp.sum(jnp.where(here, shift_e[:, None, :], 0), axis=2)
    valid = r < take(counts, tile_expert)[:, None]
    spread = (jnp.arange(p, dtype=jnp.int32) % (t * TOP_K)).reshape(n_tiles, MOE_TM)
    src_row = jnp.where(valid, ti * block + r + shift_i, spread).reshape(p)
    return dest, src_row, plan, n_used.astype(jnp.int32)


def _layer_front(x, mod, s_ssd_f, s_ssd_b, s_s5_f, s_s5_b, grid_cols, prm, xs_prev):
    z, xbc, u, dt = _inproj_call(x, mod, prm["norm1_g"], prm["w_in_r"])
    y_ssd, n_ssd_f, n_ssd_b = _ssd_call(z, xbc, dt, s_ssd_f, s_ssd_b, prm["ssd"])
    y_f, n_s5_f = _s5_call(u, _s5_state_in(s_s5_f), prm["s5_f"], grid_cols, False)
    y_s5, n_s5_b = _s5_call(u, _s5_state_in(s_s5_b), prm["s5_b"], grid_cols, True,
                            y_first=y_f, glu=prm["glu"])
    x1, xs_buf, top_i, top_w, rank, cnt_tiles, cnt = _post_call(
        x, y_ssd, y_s5, mod, prm["w_out"], prm["norm2_g"], prm["w_router"], prm["b_router"], xs_prev)
    per_token = lambda a: jnp.swapaxes(a[:, :TOP_K, :], 1, 2)
    top_i, top_w, rank = per_token(top_i), per_token(top_w), per_token(rank)
    states = (n_ssd_f, n_ssd_b, _s5_state_out(n_s5_f), _s5_state_out(n_s5_b))
    return x1, (xs_buf, top_i, rank, cnt, cnt_tiles), top_w, states


def kernel(x_prompt, x_sample, state_ssd_fwd, state_ssd_bwd, state_s5_fwd, state_s5_bwd, c, c_ctx, w_ada, b_ada, norm1_g, w_in, ssd_conv_w, ssd_conv_b, ssd_dt_bias_fwd, ssd_dt_bias_bwd, ssd_a_log_fwd, ssd_a_log_bwd, ssd_d, ssd_norm_g, s5_a_re_fwd, s5_a_im_fwd, s5_log_dt_fwd, s5_b_re_fwd, s5_b_im_fwd, s5_c_re_fwd, s5_c_im_fwd, s5_a_re_bwd, s5_a_im_bwd, s5_log_dt_bwd, s5_b_re_bwd, s5_b_im_bwd, s5_c_re_bwd, s5_c_im_bwd, s5_d, w_glu, b_glu, w_out, norm2_g, w_router, b_router, w_gate_up, b_gate_up, w_down, b_down, norm_f_g):
    depth = w_ada.shape[0]
    assert depth == 1, "single trunk layer"
    nb_ctx, l_ctx, _ = x_prompt.shape
    nb_lat, l_lat, _ = x_sample.shape

    w_in0 = w_in[0]
    dt_lo = SSD_WIDTH + SSD_CONV_CH
    dt_cols = jnp.pad(w_in0[:, dt_lo:dt_lo + 2 * SSD_HEADS], ((0, 0), (0, DT_PAD - 2 * SSD_HEADS)))
    w_in_r = jnp.concatenate([w_in0[:, :dt_lo], w_in0[:, dt_lo + 2 * SSD_HEADS:], dt_cols], axis=1).astype(BF16)
    pad_dt = lambda f, b: jnp.pad(jnp.concatenate([f, b]).astype(F32), (0, DT_PAD - 2 * SSD_HEADS)).reshape(1, DT_PAD)
    ssd_prm = {
        "conv_w": jnp.pad(ssd_conv_w[0].astype(F32), ((0, 5), (0, 0))),
        "conv_b": ssd_conv_b[0].astype(F32).reshape(1, SSD_CONV_CH),
        "dt_bias": pad_dt(ssd_dt_bias_fwd[0], ssd_dt_bias_bwd[0]),
        "a": pad_dt(-jnp.exp(ssd_a_log_fwd[0].astype(F32)), -jnp.exp(ssd_a_log_bwd[0].astype(F32))),
        "d_skip": jnp.repeat(ssd_d[0].astype(F32), SSD_HEAD_DIM).reshape(1, SSD_WIDTH),
        "norm_g": ssd_norm_g[0].astype(F32).reshape(1, SSD_WIDTH),
        "expand": (jnp.arange(DT_PAD)[None, :, None]
                   == (jnp.arange(SSD_WIDTH) // SSD_HEAD_DIM)[None, None, :]
                   + SSD_HEADS * jnp.arange(2)[:, None, None]).astype(BF16),
    }
    prm = {
        "norm1_g": norm1_g[0], "w_in_r": w_in_r, "ssd": ssd_prm,
        "s5_f": _s5_params(s5_a_re_fwd[0], s5_a_im_fwd[0], s5_log_dt_fwd[0], s5_b_re_fwd[0],
                           s5_b_im_fwd[0], s5_c_re_fwd[0], s5_c_im_fwd[0]),
        "s5_b": _s5_params(s5_a_re_bwd[0], s5_a_im_bwd[0], s5_log_dt_bwd[0], s5_b_re_bwd[0],
                           s5_b_im_bwd[0], s5_c_re_bwd[0], s5_c_im_bwd[0]),
        "glu": {"d": s5_d[0].astype(F32).reshape(1, S5_WIDTH), "w": w_glu[0].astype(BF16),
                "b": b_glu[0].astype(F32).reshape(1, 2 * S5_WIDTH)},
        "w_out": w_out[0].astype(BF16), "norm2_g": norm2_g[0],
        "w_router": w_router[0].astype(F32), "b_router": b_router[0].astype(F32),
    }

    conds = jnp.concatenate([c_ctx[None, :], c], axis=0)
    conds = jnp.pad(conds, ((0, (-conds.shape[0]) % 8), (0, 0)))
    mod = _ada_call(conds, w_ada[0], b_ada[0])
    mod_ctx = mod[0:1].reshape(1, 1, 6 * D_MODEL)
    mod_lat = mod[1:1 + nb_lat].reshape(nb_lat, 1, 6 * D_MODEL)

    zero_ssd = jnp.zeros((nb_ctx, SSD_HEADS, SSD_HEAD_DIM, SSD_STATE), F32)
    zero_s5 = jnp.zeros((nb_ctx, S5_GROUPS, S5_STATE, 2), F32)
    t_ctx = nb_ctx * l_ctx
    x1_c, routed_c, tw_c, st_c = _layer_front(
        x_prompt, mod_ctx, zero_ssd, zero_ssd, zero_s5, zero_s5, False, prm, None)
    x1_l, routed_l, tw_l, _ = _layer_front(
        x_sample, mod_lat, state_ssd_fwd[:, 0], state_ssd_bwd[:, 0], state_s5_fwd[:, 0],
        state_s5_bwd[:, 0], True, prm, routed_c[0])

    _, ti_c, rk_c, cnt_c, ct_c = routed_c
    xs_tiles, ti_l, rk_l, cnt_l, ct_l = routed_l
    cnt_c = cnt_c.reshape(N_EXPERTS).astype(jnp.int32)
    cnt_l = cnt_l.reshape(N_EXPERTS).astype(jnp.int32)
    cnt_tiles = jnp.concatenate([ct_c, ct_l], axis=0).reshape(-1, N_EXPERTS).astype(jnp.int32)
    ti_l2 = ti_l.reshape(-1, TOP_K)
    top_i = jnp.concatenate([ti_c.reshape(-1, TOP_K), ti_l2], axis=0)
    rank = jnp.concatenate([rk_c.reshape(-1, TOP_K),
                            rk_l.reshape(-1, TOP_K) + cnt_c.at[ti_l2].get(mode="promise_in_bounds")], axis=0)
    dest, src_row, plan, n_used = _route(top_i, rank, cnt_c + cnt_l, cnt_tiles)
    xs = xs_tiles.at[src_row].get(mode="promise_in_bounds")
    ys = _moe_call(plan, n_used, xs, w_gate_up[0], b_gate_up[0], w_down[0], b_down[0])

    def picked_rows(d, nb, l):
        rows = ys.at[d.T.reshape(-1)].get(mode="promise_in_bounds")
        return rows.reshape(TOP_K, nb, l, D_MODEL)

    y_prompt = _combine_call(x1_c, picked_rows(dest[:t_ctx], nb_ctx, l_ctx), tw_c, mod_ctx, norm_f_g)
    y_sample = _combine_call(x1_l, picked_rows(dest[t_ctx:], nb_lat, l_lat), tw_l, mod_lat, norm_f_g)

    ssd_f, ssd_b, s5_f, s5_b = st_c
    return (y_prompt, y_sample, ssd_f[:, None], ssd_b[:, None], s5_f[:, None], s5_b[:, None])
```

```python
import functools
import math

import jax
import jax.numpy as jnp
from jax import lax
from jax.experimental import pallas as pl
from jax.experimental.pallas import tpu as pltpu

F32 = jnp.float32
BF16 = jnp.bfloat16

D_MODEL = 1024
GRID_W = 64
SSD_WIDTH = 512
SSD_HEAD_DIM = 64
SSD_HEADS = 8
SSD_GROUPS = 2
SSD_HEADS_PER_GROUP = SSD_HEADS // SSD_GROUPS
SSD_STATE = 128
SSD_CHUNK = 128
SSD_CONV_CH = SSD_WIDTH + 2 * SSD_GROUPS * SSD_STATE
S5_WIDTH = 512
S5_CH = 16
S5_GROUPS = 32
S5_STATE = 64
N_EXPERTS = 32
TOP_K = 4
D_FF = 1024
SWIGLU_LIMIT = 7.0
SWIGLU_ALPHA = 1.702
RMS_EPS = 1e-5

LANES = 128
DT_PAD = 128
S5_BATCH = 8
S5_COLS = 2 * S5_GROUPS * S5_STATE
S5_BLK = 512
S5_SUB_STEPS = 32
S5_BLOCK_STEPS = 256
MOE_TM = 256
VMEM_LIMIT = 56 * 1024 * 1024


def _sigmoid(x):
    return 1.0 / (1.0 + jnp.exp(-x))


def _split3(x):
    hi = x.astype(BF16)
    r1 = x - hi.astype(F32)
    mid = r1.astype(BF16)
    lo = (r1 - mid.astype(F32)).astype(BF16)
    return hi, mid, lo


def _split2(x):
    hi = x.astype(BF16)
    return hi, (x - hi.astype(F32)).astype(BF16)


def _dot(a, b):
    return jnp.dot(a, b, preferred_element_type=F32)


def _ada_kernel(c_ref, w_ref, b_ref, o_ref):
    c = c_ref[...]
    s = c * _sigmoid(c)
    o_ref[...] = _dot(s.astype(BF16), w_ref[...].astype(BF16)) + b_ref[...]


def _ada_call(conds, w_ada, b_ada):
    n = w_ada.shape[1]
    tn = 1536
    rows = conds.shape[0]
    return pl.pallas_call(
        _ada_kernel,
        grid=(n // tn,),
        in_specs=[pl.BlockSpec((rows, D_MODEL), lambda j: (0, 0)),
                  pl.BlockSpec((D_MODEL, tn), lambda j: (0, j)),
                  pl.BlockSpec((1, tn), lambda j: (0, j))],
        out_specs=pl.BlockSpec((rows, tn), lambda j: (0, j)),
        out_shape=jax.ShapeDtypeStruct((rows, n), F32),
        compiler_params=pltpu.CompilerParams(dimension_semantics=("arbitrary",),
                                             vmem_limit_bytes=VMEM_LIMIT),
        name="ada_mod",
    )(conds, w_ada, b_ada.reshape(1, n))


def _inproj_kernel(x_ref, mod_ref, g_ref, w_ref, z_ref, xbc_ref, u_ref, dt_ref):
    x = x_ref[0]
    ms = jnp.mean(x * x, axis=-1, keepdims=True)
    y = x * lax.rsqrt(ms + RMS_EPS) * g_ref[...]
    sh = mod_ref[0, :, 0:D_MODEL]
    sc = mod_ref[0, :, D_MODEL:2 * D_MODEL]
    h = (y * (1.0 + sc) + sh).astype(BF16)
    z_ref[0] = _dot(h, w_ref[:, 0:512]).astype(BF16)
    xbc_ref[0] = _dot(h, w_ref[:, 512:1536]).astype(BF16)
    u_ref[0] = _dot(h, w_ref[:, 1536:2048])
    dt_ref[0] = _dot(h, w_ref[:, 2048:2048 + DT_PAD])


def _inproj_call(x, mod, norm_g, w_in_r):
    b, l, _ = x.shape
    tl = min(l, 512)
    per_batch = mod.shape[0] > 1
    mod_map = (lambda i, j: (i, 0, 0)) if per_batch else (lambda i, j: (0, 0, 0))
    ncol = w_in_r.shape[1]
    tok = lambda w: pl.BlockSpec((1, tl, w), lambda i, j: (i, j, 0))
    return pl.pallas_call(
        _inproj_kernel,
        grid=(b, l // tl),
        in_specs=[tok(D_MODEL),
                  pl.BlockSpec((1, 1, 6 * D_MODEL), mod_map),
                  pl.BlockSpec((1, D_MODEL), lambda i, j: (0, 0)),
                  pl.BlockSpec((D_MODEL, ncol), lambda i, j: (0, 0))],
        out_specs=[tok(SSD_WIDTH), tok(SSD_CONV_CH), tok(S5_WIDTH), tok(DT_PAD)],
        out_shape=[jax.ShapeDtypeStruct((b, l, SSD_WIDTH), BF16),
                   jax.ShapeDtypeStruct((b, l, SSD_CONV_CH), BF16),
                   jax.ShapeDtypeStruct((b, l, S5_WIDTH), F32),
                   jax.ShapeDtypeStruct((b, l, DT_PAD), F32)],
        compiler_params=pltpu.CompilerParams(dimension_semantics=("arbitrary", "arbitrary"),
                                             vmem_limit_bytes=VMEM_LIMIT),
        name="norm1_inproj",
    )(x, mod, norm_g.reshape(1, D_MODEL), w_in_r)


def _softplus(x):
    return jnp.maximum(x, 0.0) + jnp.log1p(jnp.exp(-jnp.abs(x)))


def _ssd_kernel(z_ref, xbc_ref, dt_ref, s0f_ref, s0b_ref, cw_ref, cb_ref, dtb_ref, a_ref,
                dskip_ref, ng_ref, ex_ref, y_ref, sf_ref, sb_ref,
                xc_s, dts_s, yf_s, yb_s, stf_s, stb_s, *, seq):
    q = SSD_CHUNK
    nc = seq // q
    gw = SSD_HEADS_PER_GROUP * SSD_HEAD_DIM
    row_i = lax.broadcasted_iota(jnp.int32, (q, 1), 0)

    def conv_body(c, carry):
        r0 = pl.multiple_of(c * q, q)
        cur = xbc_ref[0, pl.ds(r0, q), :].astype(F32)
        p0 = pl.multiple_of(jnp.maximum(r0 - 16, 0), 16)
        n0 = pl.multiple_of(jnp.minimum(r0 + q, seq - 16), 16)
        prev_row = xbc_ref[0, pl.ds(p0, 16), :].astype(F32)[15:16, :]
        next_row = xbc_ref[0, pl.ds(n0, 16), :].astype(F32)[0:1, :]
        prev_row = jnp.where(c > 0, prev_row, 0.0)
        next_row = jnp.where(c < nc - 1, next_row, 0.0)
        down = jnp.where(row_i == 0, prev_row, pltpu.roll(cur, 1, axis=0))
        up = jnp.where(row_i == q - 1, next_row, pltpu.roll(cur, q - 1, axis=0))
        v = cw_ref[0:1, :] * down + cw_ref[1:2, :] * cur + cw_ref[2:3, :] * up + cb_ref[...]
        xc_s[pl.ds(r0, q), :] = v * _sigmoid(v)
        dts_s[pl.ds(r0, q), :] = _softplus(dt_ref[0, pl.ds(r0, q), :] + dtb_ref[...])
        return carry

    lax.fori_loop(0, nc, conv_body, 0)

    li = lax.broadcasted_iota(jnp.int32, (q, q), 0)
    si = lax.broadcasted_iota(jnp.int32, (q, q), 1)
    head_of_lane = lax.broadcasted_iota(jnp.int32, (q, gw), 1) // SSD_HEAD_DIM

    def expand(v, e_mat):
        hi, lo = _split2(v)
        return _dot(hi, e_mat) + _dot(lo, e_mat)

    def chunk(c, direction):
        r0 = pl.multiple_of(c * q, q)
        st_ref = stf_s if direction == 0 else stb_s
        y_dst = yf_s if direction == 0 else yb_s
        e_mat = ex_ref[direction]
        causal = (li >= si) if direction == 0 else (li <= si)
        tri = jnp.where(causal, 1.0, 0.0).astype(BF16)
        dts = dts_s[pl.ds(r0, q), :]
        dta = dts * a_ref[...]
        d_hi, d_mid, d_lo = _split3(dta)
        cs = _dot(tri, d_hi) + _dot(tri, d_mid) + _dot(tri, d_lo)
        tot = cs[q - 1:q, :] if direction == 0 else cs[0:1, :]
        cs_t = cs.T
        ecs = jnp.exp(cs)
        tail = jnp.exp(tot - cs)
        dt_x = expand(dts, e_mat)
        dtw_x = expand(dts * tail, e_mat)
        etot_x = expand(jnp.broadcast_to(jnp.exp(tot), (8, DT_PAD)), e_mat)[0:1, :]
        for g in range(SSD_GROUPS):
            b_off = SSD_WIDTH + g * SSD_STATE
            c_off = SSD_WIDTH + SSD_GROUPS * SSD_STATE + g * SSD_STATE
            x_g = xc_s[pl.ds(r0, q), g * gw:(g + 1) * gw]
            bm = xc_s[pl.ds(r0, q), b_off:b_off + SSD_STATE]
            cm = xc_s[pl.ds(r0, q), c_off:c_off + SSD_STATE]
            cb = lax.dot_general(cm.astype(BF16), bm.astype(BF16), (((1,), (1,)), ((), ())),
                                 preferred_element_type=F32)
            st_g = st_ref[g]
            xdt = x_g * dt_x[:, g * gw:(g + 1) * gw]
            lhs, rhs = [], []
            for hh in range(SSD_HEADS_PER_GROUP):
                col = g * SSD_HEADS_PER_GROUP + hh + SSD_HEADS * direction
                seg = jnp.broadcast_to(cs[:, col:col + 1], (q, q)) - cs_t[col:col + 1, :]
                dec = jnp.exp(jnp.where(causal, seg, -1e30))
                lhs.append((cb * dec).astype(BF16))
                lhs.append((cm * jnp.broadcast_to(ecs[:, col:col + 1], (q, q))).astype(BF16))
                own = head_of_lane == hh
                rhs.append(jnp.where(own, xdt, 0.0).astype(BF16))
                rhs.append(jnp.where(own, st_g, 0.0).astype(BF16))
            y_dst[pl.ds(r0, q), g * gw:(g + 1) * gw] = _dot(jnp.concatenate(lhs, axis=1),
                                                             jnp.concatenate(rhs, axis=0))
            xw = (x_g * dtw_x[:, g * gw:(g + 1) * gw]).astype(BF16)
            st_ref[g] = etot_x[:, g * gw:(g + 1) * gw] * st_g + _dot(bm.T.astype(BF16), xw)

    stf_s[...] = s0f_ref[0]
    stb_s[...] = s0b_ref[0]

    def both(i, carry):
        chunk(i, 0)
        chunk(nc - 1 - i, 1)
        return carry

    lax.fori_loop(0, nc, both, 0)
    sf_ref[0] = stf_s[...]
    sb_ref[0] = stb_s[...]

    def finish(c, carry):
        r0 = pl.multiple_of(c * q, q)
        xs = xc_s[pl.ds(r0, q), 0:SSD_WIDTH]
        zz = z_ref[0, pl.ds(r0, q), :].astype(F32)
        y = (yf_s[pl.ds(r0, q), :] + yb_s[pl.ds(r0, q), :] + dskip_ref[...] * xs) * (zz * _sigmoid(zz))
        ms = jnp.mean(y * y, axis=-1, keepdims=True)
        y_ref[0, pl.ds(r0, q), :] = (y * lax.rsqrt(ms + RMS_EPS) * ng_ref[...]).astype(y_ref.dtype)
        return carry

    lax.fori_loop(0, nc, finish, 0)


def _ssd_state_in(s):
    b = s.shape[0]
    s = s.astype(F32).reshape(b, SSD_GROUPS, SSD_HEADS_PER_GROUP, SSD_HEAD_DIM, SSD_STATE)
    return s.transpose(0, 1, 4, 2, 3).reshape(b, SSD_GROUPS, SSD_STATE, SSD_HEADS_PER_GROUP * SSD_HEAD_DIM)


def _ssd_state_out(s):
    b = s.shape[0]
    s = s.reshape(b, SSD_GROUPS, SSD_STATE, SSD_HEADS_PER_GROUP, SSD_HEAD_DIM)
    return s.transpose(0, 1, 3, 4, 2).reshape(b, SSD_HEADS, SSD_HEAD_DIM, SSD_STATE)


def _ssd_call(z, xbc, dt, s0f, s0b, prm):
    b, l, _ = z.shape
    st_shape = (SSD_GROUPS, SSD_STATE, SSD_HEADS_PER_GROUP * SSD_HEAD_DIM)
    tok = lambda w: pl.BlockSpec((1, l, w), lambda i: (i, 0, 0))
    st_spec = pl.BlockSpec((1,) + st_shape, lambda i: (i, 0, 0, 0))
    par = lambda r, w: pl.BlockSpec((r, w), lambda i: (0, 0))
    y, sf, sb = pl.pallas_call(
        functools.partial(_ssd_kernel, seq=l),
        grid=(b,),
        in_specs=[tok(SSD_WIDTH), tok(SSD_CONV_CH), tok(DT_PAD), st_spec, st_spec,
                  par(8, SSD_CONV_CH), par(1, SSD_CONV_CH), par(1, DT_PAD), par(1, DT_PAD),
                  par(1, SSD_WIDTH), par(1, SSD_WIDTH),
                  pl.BlockSpec((2, DT_PAD, SSD_WIDTH), lambda i: (0, 0, 0))],
        out_specs=[tok(SSD_WIDTH), st_spec, st_spec],
        out_shape=[jax.ShapeDtypeStruct((b, l, SSD_WIDTH), BF16),
                   jax.ShapeDtypeStruct((b,) + st_shape, F32),
                   jax.ShapeDtypeStruct((b,) + st_shape, F32)],
        scratch_shapes=[pltpu.VMEM((l, SSD_CONV_CH), F32),
                        pltpu.VMEM((l, DT_PAD), F32),
                        pltpu.VMEM((l, SSD_WIDTH), F32),
                        pltpu.VMEM((l, SSD_WIDTH), F32),
                        pltpu.VMEM(st_shape, F32),
                        pltpu.VMEM(st_shape, F32)],
        compiler_params=pltpu.CompilerParams(dimension_semantics=("arbitrary",),
                                             vmem_limit_bytes=VMEM_LIMIT),
        name="ssd_mixer",
    )(z, xbc, dt, _ssd_state_in(s0f), _ssd_state_in(s0b),
      prm["conv_w"], prm["conv_b"], prm["dt_bias"], prm["a"], prm["d_skip"], prm["norm_g"], prm["expand"])
    return y, _ssd_state_out(sf), _ssd_state_out(sb)


def _gelu_tanh(x):
    return 0.5 * x * (1.0 + jnp.tanh(math.sqrt(2.0 / math.pi) * (x + 0.044715 * (x * x * x))))


def _s5_kernel(*refs, grid_cols, reverse, final):
    if final:
        (u_ref, s0_ref, lr_ref, li_ref, wb_ref, wc_ref, yf_ref, d_ref, wg_ref, bg_ref,
         o_ref, sfin_ref, ur_s, yr_s, st_s) = refs
    else:
        (u_ref, s0_ref, lr_ref, li_ref, wb_ref, wc_ref,
         o_ref, sfin_ref, ur_s, yr_s, st_s) = refs
    nb = S5_BATCH
    steps = S5_BLOCK_STEPS
    rows = steps * nb
    sub_rows = S5_SUB_STEPS * nb
    n_sub = steps // S5_SUB_STEPS
    j = pl.program_id(1)

    @pl.when(j == 0)
    def _():
        st_s[...] = s0_ref[0]

    def to_scan(v):
        v = jnp.transpose(v, (2, 1, 0, 3)) if grid_cols else jnp.swapaxes(v, 0, 1)
        return v.reshape(rows, S5_WIDTH)

    def from_scan(v):
        if grid_cols:
            return jnp.transpose(v.reshape(u_ref.shape[2], u_ref.shape[1], nb, S5_WIDTH), (2, 1, 0, 3))
        return jnp.swapaxes(v.reshape(steps, nb, S5_WIDTH), 0, 1)

    ur_s[...] = to_scan(u_ref[...])

    order = list(range(S5_SUB_STEPS))[::-1] if reverse else list(range(S5_SUB_STEPS))
    blk_per_half = S5_COLS // S5_BLK // 2

    def by_groups(v, perm):
        return jnp.concatenate([v[g * nb:(g + 1) * nb] for g in perm], axis=0)

    def sub_chunk(i, carry):
        sidx = (n_sub - 1 - i) if reverse else i
        r0 = pl.multiple_of(sidx * sub_rows, sub_rows)
        for hf in range(2):
            u_half = ur_s[pl.ds(r0, sub_rows), hf * 256:(hf + 1) * 256]
            if reverse:
                u_half = by_groups(u_half, order)
            uh = u_half.astype(BF16)
            for bp in range(blk_per_half // 2):
                blk = hf * blk_per_half + 2 * bp
                c0 = blk * S5_BLK
                wide = 2 * S5_BLK
                lam_r = jnp.broadcast_to(lr_ref[:, c0:c0 + wide], (nb, wide))
                lam_i = jnp.broadcast_to(li_ref[:, c0:c0 + wide], (nb, wide))
                bu = _dot(uh, wb_ref[hf, :, 2 * bp * S5_BLK:2 * (bp + 1) * S5_BLK])
                s = st_s[:, c0:c0 + wide]
                states = []
                for k in range(S5_SUB_STEPS):
                    q = S5_BLK // 2
                    sw = jnp.concatenate([s[:, q:2 * q], s[:, 0:q], s[:, 3 * q:4 * q], s[:, 2 * q:3 * q]], axis=1)
                    s = lam_r * s + lam_i * sw + bu[k * nb:(k + 1) * nb, :]
                    states.append(s)
                st_s[:, c0:c0 + wide] = s
                traj = jnp.concatenate(states, axis=0).astype(BF16)
                for half in range(2):
                    y_blk = _dot(traj[:, half * S5_BLK:(half + 1) * S5_BLK], wc_ref[blk + half])
                    if reverse:
                        y_blk = by_groups(y_blk, order)
                    yr_s[pl.ds(r0, sub_rows), (blk + half) * 64:(blk + half + 1) * 64] = y_blk
        return carry

    lax.fori_loop(0, n_sub, sub_chunk, 0)

    @pl.when(j == pl.num_programs(1) - 1)
    def _():
        sfin_ref[0] = st_s[...]

    if not final:
        o_ref[0] = yr_s[...]
        return

    def glu_chunk(i, carry):
        r0 = pl.multiple_of(i * sub_rows, sub_rows)
        y = (yr_s[pl.ds(r0, sub_rows), :] + yf_ref[0, pl.ds(r0, sub_rows), :]
             + d_ref[...] * ur_s[pl.ds(r0, sub_rows), :])
        y = _gelu_tanh(y).astype(BF16)
        val = _dot(y, wg_ref[:, 0:S5_WIDTH]) + bg_ref[:, 0:S5_WIDTH]
        gate = _dot(y, wg_ref[:, S5_WIDTH:2 * S5_WIDTH]) + bg_ref[:, S5_WIDTH:2 * S5_WIDTH]
        yr_s[pl.ds(r0, sub_rows), :] = val * _sigmoid(gate)
        return carry

    lax.fori_loop(0, n_sub, glu_chunk, 0)

    o_ref[...] = from_scan(yr_s[...])


def _s5_call(u, s0, prm, grid_cols, reverse, y_first=None, glu=None):
    b, l, _ = u.shape
    nbg = b // S5_BATCH
    nblk = l // S5_BLOCK_STEPS
    rows = S5_BLOCK_STEPS * S5_BATCH
    final = y_first is not None
    blk_of = (lambda j: nblk - 1 - j) if reverse else (lambda j: j)
    if grid_cols:
        n_rows = l // GRID_W
        u_in = u.reshape(b, n_rows, GRID_W, S5_WIDTH)
        cols_per_blk = S5_BLOCK_STEPS // n_rows
        tok_spec = pl.BlockSpec((S5_BATCH, n_rows, cols_per_blk, S5_WIDTH),
                                lambda i, j: (i, 0, blk_of(j), 0))
        out_nat = jax.ShapeDtypeStruct((b, n_rows, GRID_W, S5_WIDTH), F32)
    else:
        u_in = u
        tok_spec = pl.BlockSpec((S5_BATCH, S5_BLOCK_STEPS, S5_WIDTH), lambda i, j: (i, blk_of(j), 0))
        out_nat = jax.ShapeDtypeStruct((b, l, S5_WIDTH), F32)
    scan_spec = pl.BlockSpec((1, rows, S5_WIDTH), lambda i, j: (i, blk_of(j), 0))
    st_spec = pl.BlockSpec((1, S5_BATCH, S5_COLS), lambda i, j: (i, 0, 0))
    full = lambda shape: pl.BlockSpec(shape, lambda i, j: (0,) * len(shape))
    in_specs = [tok_spec, st_spec, full((1, S5_COLS)), full((1, S5_COLS)),
                full((2, 256, 2048)), full((S5_COLS // S5_BLK, S5_BLK, 64))]
    args = [u_in, s0, prm["lam_r"], prm["lam_i"], prm["wb"], prm["wc"]]
    if final:
        in_specs += [scan_spec, full((1, S5_WIDTH)), full((S5_WIDTH, 2 * S5_WIDTH)), full((1, 2 * S5_WIDTH))]
        args += [y_first, glu["d"], glu["w"], glu["b"]]
        out_specs = [tok_spec, st_spec]
        out_shape = [out_nat, jax.ShapeDtypeStruct((nbg, S5_BATCH, S5_COLS), F32)]
    else:
        out_specs = [scan_spec, st_spec]
        out_shape = [jax.ShapeDtypeStruct((nbg, l * S5_BATCH, S5_WIDTH), F32),
                     jax.ShapeDtypeStruct((nbg, S5_BATCH, S5_COLS), F32)]
    y, sfin = pl.pallas_call(
        functools.partial(_s5_kernel, grid_cols=grid_cols, reverse=reverse, final=final),
        grid=(nbg, nblk),
        in_specs=in_specs,
        out_specs=out_specs,
        out_shape=out_shape,
        scratch_shapes=[pltpu.VMEM((rows, S5_WIDTH), F32),
                        pltpu.VMEM((rows, S5_WIDTH), F32),
                        pltpu.VMEM((S5_BATCH, S5_COLS), F32)],
        compiler_params=pltpu.CompilerParams(dimension_semantics=("arbitrary", "arbitrary"),
                                             vmem_limit_bytes=VMEM_LIMIT),
        name="s5_final" if final else "s5_first",
    )(*args)
    if final:
        y = y.reshape(b, l, S5_WIDTH)
    return y, sfin


def _s5_cols(t):
    lead = t.shape[:-2]
    return t.reshape(lead + (S5_COLS // S5_BLK, S5_BLK // 2))


def _s5_state_in(s0):
    b = s0.shape[0]
    re, im = _s5_cols(s0[..., 0].astype(F32)), _s5_cols(s0[..., 1].astype(F32))
    return jnp.stack([re, im], axis=-2).reshape(b // S5_BATCH, S5_BATCH, S5_COLS)


def _s5_state_out(s):
    b = s.shape[0] * S5_BATCH
    s = s.reshape(b, S5_COLS // S5_BLK, 2, S5_BLK // 2)
    re = s[:, :, 0].reshape(b, S5_GROUPS, S5_STATE)
    im = s[:, :, 1].reshape(b, S5_GROUPS, S5_STATE)
    return jnp.stack([re, im], axis=-1)


def _s5_params(a_re, a_im, log_dt, b_re, b_im, c_re, c_im):
    a_re, a_im = a_re.astype(F32), a_im.astype(F32)
    delta = jnp.exp(log_dt.astype(F32))[:, None]
    mag = jnp.exp(a_re * delta)
    lam_re, lam_im = mag * jnp.cos(a_im * delta), mag * jnp.sin(a_im * delta)
    den = a_re * a_re + a_im * a_im
    nr = lam_re - 1.0
    f_re = (nr * a_re + lam_im * a_im) / den
    f_im = (lam_im * a_re - nr * a_im) / den
    b_re, b_im = b_re.astype(F32), b_im.astype(F32)
    bb_re = f_re[..., None] * b_re - f_im[..., None] * b_im
    bb_im = f_re[..., None] * b_im + f_im[..., None] * b_re
    nblk = S5_COLS // S5_BLK
    lr = _s5_cols(lam_re)
    li = _s5_cols(lam_im)
    lam_r = jnp.stack([lr, lr], axis=-2).reshape(1, S5_COLS)
    lam_i = jnp.stack([-li, li], axis=-2).reshape(1, S5_COLS)
    gq = S5_BLK // 2 // S5_STATE
    eye = jnp.eye(gq, dtype=F32)

    def in_block(bb):
        t = bb.reshape(nblk, gq, S5_STATE, S5_CH)
        return jnp.einsum("gh,kgpc->kgchp", eye, t).reshape(nblk, gq * S5_CH, gq * S5_STATE)

    blk_in = jnp.concatenate([in_block(bb_re), in_block(bb_im)], axis=-1)
    half = nblk // 2
    eye_h = jnp.eye(half, dtype=F32)
    wb = jnp.einsum("jk,hkcn->hjckn", eye_h, blk_in.reshape(2, half, gq * S5_CH, S5_BLK))
    wb = wb.reshape(2, half * gq * S5_CH, half * S5_BLK).astype(BF16)

    def out_block(cc):
        t = cc.astype(F32).reshape(nblk, gq, S5_CH, S5_STATE)
        return jnp.einsum("gh,kgcp->kgphc", eye, t).reshape(nblk, gq * S5_STATE, gq * S5_CH)

    wc = jnp.concatenate([out_block(c_re), -out_block(c_im)], axis=1).astype(BF16)
    return {"lam_r": lam_r, "lam_i": lam_i, "wb": wb, "wc": wc}


def _post_kernel(*refs, tile_base, prev_rows):
    if prev_rows:
        prev_hbm, refs = refs[8], refs[:8] + refs[9:]
    (x_ref, ys_ref, y5_ref, mod_ref, wo_ref, g_ref, wr_ref, br_ref,
     x1_ref, xs_hbm, ti_ref, tw_ref, rk_ref, ct_ref, cnt_ref, cnt_s, xs_v, pv_v, sem_out, sem_prev) = refs
    tl = x_ref.shape[1]
    blk = tl * TOP_K
    n_steps = pl.num_programs(0) * pl.num_programs(1)
    step = pl.program_id(0) * pl.num_programs(1) + pl.program_id(1)
    buf = step % 2
    first = step == 0

    def block_copy(at_step, at_slot):
        row0 = pl.multiple_of((tile_base + at_step) * blk, blk)
        return pltpu.make_async_copy(xs_v.at[at_slot], xs_hbm.at[pl.ds(row0, blk)], sem_out.at[at_slot])

    @pl.when(step >= 2)
    def _():
        block_copy(step - 2, buf).wait()

    prev_blocks = prev_rows // blk

    def prev_in(at_step):
        row0 = pl.multiple_of(at_step * blk, blk)
        return pltpu.make_async_copy(prev_hbm.at[pl.ds(row0, blk)], pv_v.at[at_step % 2], sem_prev.at[0, at_step % 2])

    def prev_out(at_step):
        row0 = pl.multiple_of(at_step * blk, blk)
        return pltpu.make_async_copy(pv_v.at[at_step % 2], xs_hbm.at[pl.ds(row0, blk)], sem_prev.at[1, at_step % 2])

    if prev_rows:
        @pl.when(jnp.logical_and(step >= 2, step - 2 < prev_blocks))
        def _():
            prev_out(step - 2).wait()

        @pl.when(step < prev_blocks)
        def _():
            prev_in(step).start()

        @pl.when(jnp.logical_and(step >= 1, step - 1 < prev_blocks))
        def _():
            prev_in(step - 1).wait()
            prev_out(step - 1).start()

    @pl.when(first)
    def _():
        cnt_s[...] = jnp.zeros(cnt_s.shape, cnt_s.dtype)

    mixed = (_dot(ys_ref[0], wo_ref[0:SSD_WIDTH, :])
             + _dot(y5_ref[0].astype(BF16), wo_ref[SSD_WIDTH:SSD_WIDTH + S5_WIDTH, :]))
    g1 = mod_ref[0, :, 2 * D_MODEL:3 * D_MODEL]
    sh2 = mod_ref[0, :, 3 * D_MODEL:4 * D_MODEL]
    sc2 = mod_ref[0, :, 4 * D_MODEL:5 * D_MODEL]
    x1 = x_ref[0] + g1 * mixed
    x1_ref[0] = x1
    ms = jnp.mean(x1 * x1, axis=-1, keepdims=True)
    h = x1 * lax.rsqrt(ms + RMS_EPS) * g_ref[...] * (1.0 + sc2) + sh2

    h_hi, h_mid, _ = _split3(h)
    wr = wr_ref[...]
    w_hi, w_mid, _ = _split3(wr)
    logits = _dot(h_hi, w_hi) + _dot(h_hi, w_mid) + _dot(h_mid, w_hi) + br_ref[...]

    work = logits.T[0:N_EXPERTS, :]
    eid = lax.broadcasted_iota(jnp.int32, (N_EXPERTS, tl), 0).astype(F32)
    row8 = lax.broadcasted_iota(jnp.int32, (8, tl), 0)
    onehot = jnp.zeros((N_EXPERTS, tl), F32)
    sels, vals = [], []
    top_i = jnp.zeros((8, tl), jnp.int32)
    for k in range(TOP_K):
        m = jnp.max(work, axis=0, keepdims=True)
        idx = jnp.min(jnp.where(work == m, eid, float(N_EXPERTS)), axis=0, keepdims=True)
        sel = eid == idx
        sels.append(sel)
        vals.append(m)
        top_i = jnp.where(row8 == k, idx.astype(jnp.int32), top_i)
        onehot = jnp.where(sel, 1.0, onehot)
        work = jnp.where(sel, -jnp.inf, work)
    es = [jnp.exp(v - vals[0]) for v in vals]
    den = es[0] + es[1] + es[2] + es[3]
    top_w = jnp.zeros((8, tl), F32)
    for k in range(TOP_K):
        top_w = jnp.where(row8 == k, es[k] / den, top_w)

    ri = lax.broadcasted_iota(jnp.int32, (tl, tl), 0)
    ci = lax.broadcasted_iota(jnp.int32, (tl, tl), 1)
    earlier = jnp.where(ri < ci, 1.0, 0.0).astype(BF16)
    local = _dot(onehot.astype(BF16), earlier)
    before = local + cnt_s[...]
    cnt_tile = jnp.sum(onehot, axis=1, keepdims=True)
    er = lax.broadcasted_iota(jnp.int32, (N_EXPERTS, N_EXPERTS), 0)
    ec = lax.broadcasted_iota(jnp.int32, (N_EXPERTS, N_EXPERTS), 1)
    lower = jnp.where(ec < er, 1.0, 0.0).astype(BF16)
    off = _dot(lower, jnp.broadcast_to(cnt_tile, (N_EXPERTS, LANES)).astype(BF16))[:, 0:1]
    pos = off + local
    rank = jnp.zeros((8, tl), jnp.int32)
    row_id = lax.broadcasted_iota(jnp.int32, (tl * TOP_K, tl), 0).astype(F32)
    perm = jnp.zeros((tl * TOP_K, tl), F32)
    for k in range(TOP_K):
        rk = jnp.sum(jnp.where(sels[k], before, 0.0), axis=0, keepdims=True)
        rank = jnp.where(row8 == k, rk.astype(jnp.int32), rank)
        pk = jnp.sum(jnp.where(sels[k], pos, 0.0), axis=0, keepdims=True)
        perm = jnp.where(row_id == pk, 1.0, perm)
    xs_v[buf] = _dot(perm.astype(BF16), h.astype(BF16)).astype(BF16)
    block_copy(step, buf).start()
    cnt_s[...] = cnt_s[...] + cnt_tile
    ti_ref[0] = top_i
    tw_ref[0] = top_w
    rk_ref[0] = rank
    ct_ref[0] = cnt_tile
    cnt_ref[...] = cnt_s[...]

    @pl.when(step == n_steps - 1)
    def _():
        block_copy(step, buf).wait()

        @pl.when(step >= 1)
        def _():
            block_copy(step - 1, 1 - buf).wait()


def _post_call(x, y_ssd, y_s5, mod, w_out, norm_g, w_router, b_router, xs_prev=None):
    b, l, _ = x.shape
    tl = 256
    tiles_per_row = l // tl
    prev_rows = 0 if xs_prev is None else xs_prev.shape[0]
    assert prev_rows // (tl * TOP_K) + 2 <= b * tiles_per_row, "not enough grid steps to pass the earlier rows through"
    sorted_rows = prev_rows + b * l * TOP_K
    per_batch = mod.shape[0] > 1
    mod_map = (lambda i, j: (i, 0, 0)) if per_batch else (lambda i, j: (0, 0, 0))
    tok = lambda w: pl.BlockSpec((1, tl, w), lambda i, j: (i, j, 0))
    full = lambda r, w: pl.BlockSpec((r, w), lambda i, j: (0, 0))
    slots = pl.BlockSpec((1, 8, tl), lambda i, j: (i, 0, j))
    in_specs = [tok(D_MODEL), tok(SSD_WIDTH), tok(S5_WIDTH),
                pl.BlockSpec((1, 1, 6 * D_MODEL), mod_map),
                full(D_MODEL, D_MODEL), full(1, D_MODEL), full(D_MODEL, LANES),
                full(1, LANES)]
    w_router = jnp.pad(w_router, ((0, 0), (0, LANES - N_EXPERTS)))
    b_router = jnp.pad(b_router, (0, LANES - N_EXPERTS), constant_values=-1e30)
    args = [x, y_ssd, y_s5, mod, w_out, norm_g.reshape(1, D_MODEL), w_router, b_router.reshape(1, LANES)]
    if prev_rows:
        in_specs.append(pl.BlockSpec(memory_space=pl.ANY))
        args.append(xs_prev)
    return pl.pallas_call(
        functools.partial(_post_kernel, tile_base=prev_rows // (tl * TOP_K), prev_rows=prev_rows),
        grid=(b, tiles_per_row),
        in_specs=in_specs,
        out_specs=[tok(D_MODEL),
                   pl.BlockSpec(memory_space=pl.ANY),
                   slots, slots, slots,
                   pl.BlockSpec((1, N_EXPERTS, 1), lambda i, j: (i * tiles_per_row + j, 0, 0)),
                   full(N_EXPERTS, 1)],
        out_shape=[jax.ShapeDtypeStruct((b, l, D_MODEL), F32),
                   jax.ShapeDtypeStruct((sorted_rows, D_MODEL), BF16),
                   jax.ShapeDtypeStruct((b, 8, l), jnp.int32),
                   jax.ShapeDtypeStruct((b, 8, l), F32),
                   jax.ShapeDtypeStruct((b, 8, l), jnp.int32),
                   jax.ShapeDtypeStruct((b * tiles_per_row, N_EXPERTS, 1), F32),
                   jax.ShapeDtypeStruct((N_EXPERTS, 1), F32)],
        scratch_shapes=[pltpu.VMEM((N_EXPERTS, 1), F32),
                        pltpu.VMEM((2, tl * TOP_K, D_MODEL), BF16),
                        pltpu.VMEM((2, tl * TOP_K, D_MODEL), BF16),
                        pltpu.SemaphoreType.DMA((2,)),
                        pltpu.SemaphoreType.DMA((2, 2))],
        compiler_params=pltpu.CompilerParams(dimension_semantics=("arbitrary", "arbitrary"),
                                             vmem_limit_bytes=VMEM_LIMIT),
        name="outproj_norm2_router",
    )(*args)


PLAN_EXPERT, PLAN_FIRST, PLAN_NEXT, PLAN_SLOT = range(4)


def _moe_kernel(plan_ref, nu_ref, x_ref, wgu_hbm, bgu_ref, wd_hbm, bd_ref, o_ref,
                gu_f32, dn_f32, gu_s, dn_s, sem):
    i = pl.program_id(0)
    used = i < nu_ref[0]

    def weight_copies(expert, slot):
        return (pltpu.make_async_copy(wgu_hbm.at[expert], gu_f32.at[slot], sem.at[0, slot]),
                pltpu.make_async_copy(wd_hbm.at[expert], dn_f32.at[slot], sem.at[1, slot]))

    @pl.when(jnp.logical_and(used, plan_ref[PLAN_FIRST, i] == 1))
    def _():
        slot = plan_ref[PLAN_SLOT, i]
        nxt = plan_ref[PLAN_NEXT, i]

        @pl.when(i == 0)
        def _():
            for cp in weight_copies(plan_ref[PLAN_EXPERT, 0], plan_ref[PLAN_SLOT, 0]):
                cp.start()

        for cp in weight_copies(plan_ref[PLAN_EXPERT, i], slot):
            cp.wait()

        @pl.when(nxt >= 0)
        def _():
            for cp in weight_copies(nxt, 1 - slot):
                cp.start()

        for r in range(D_MODEL // 128):
            gu_s[r * 128:(r + 1) * 128, :] = gu_f32[slot, r * 128:(r + 1) * 128, :].astype(BF16)
            dn_s[r * 128:(r + 1) * 128, :] = dn_f32[slot, r * 128:(r + 1) * 128, :].astype(BF16)

    @pl.when(used)
    def _():
        x = x_ref[...]
        gate = _dot(x, gu_s[:, 0:D_FF]) + bgu_ref[0, :, 0:D_FF]
        up = _dot(x, gu_s[:, D_FF:2 * D_FF]) + bgu_ref[0, :, D_FF:2 * D_FF]
        gate = jnp.minimum(gate, SWIGLU_LIMIT)
        up = jnp.clip(up, -SWIGLU_LIMIT, SWIGLU_LIMIT)
        act = gate * _sigmoid(SWIGLU_ALPHA * gate) * (up + 1.0)
        o_ref[...] = (_dot(act.astype(BF16), dn_s[...]) + bd_ref[0]).astype(o_ref.dtype)

    @pl.when(jnp.logical_not(used))
    def _():
        o_ref[...] = jnp.zeros(o_ref.shape, o_ref.dtype)


def _moe_call(plan, n_used, xs, w_gate_up, b_gate_up, w_down, b_down):
    p = xs.shape[0]
    n_tiles = p // MOE_TM
    expert_of = lambda i, plan, nu: (plan[PLAN_EXPERT, i], 0, 0)
    grid_spec = pltpu.PrefetchScalarGridSpec(
        num_scalar_prefetch=2,
        grid=(n_tiles,),
        in_specs=[pl.BlockSpec((MOE_TM, D_MODEL), lambda i, plan, nu: (i, 0)),
                  pl.BlockSpec(memory_space=pl.ANY),
                  pl.BlockSpec((1, 1, 2 * D_FF), expert_of),
                  pl.BlockSpec(memory_space=pl.ANY),
                  pl.BlockSpec((1, 1, D_MODEL), expert_of)],
        out_specs=pl.BlockSpec((MOE_TM, D_MODEL), lambda i, plan, nu: (i, 0)),
        scratch_shapes=[pltpu.VMEM((2, D_MODEL, 2 * D_FF), F32), pltpu.VMEM((2, D_FF, D_MODEL), F32),
                        pltpu.VMEM((D_MODEL, 2 * D_FF), BF16), pltpu.VMEM((D_FF, D_MODEL), BF16),
                        pltpu.SemaphoreType.DMA((2, 2))],
    )
    return pl.pallas_call(
        _moe_kernel,
        grid_spec=grid_spec,
        out_shape=jax.ShapeDtypeStruct((p, D_MODEL), BF16),
        compiler_params=pltpu.CompilerParams(dimension_semantics=("arbitrary",),
                                             vmem_limit_bytes=VMEM_LIMIT),
        name="moe_experts",
    )(plan, n_used, xs, w_gate_up, b_gate_up.reshape(N_EXPERTS, 1, 2 * D_FF),
      w_down, b_down.reshape(N_EXPERTS, 1, D_MODEL))


def _combine_kernel(x1_ref, yg_ref, tw_ref, mod_ref, g_ref, o_ref):
    tw = tw_ref[0]
    acc = tw[:, 0:1] * yg_ref[0, 0].astype(F32)
    for k in range(1, TOP_K):
        acc = acc + tw[:, k:k + 1] * yg_ref[k, 0].astype(F32)
    g2 = mod_ref[0, :, 5 * D_MODEL:6 * D_MODEL]
    x2 = x1_ref[0] + g2 * acc
    ms = jnp.mean(x2 * x2, axis=-1, keepdims=True)
    o_ref[0] = x2 * lax.rsqrt(ms + RMS_EPS) * g_ref[...]


def _combine_call(x1, yg, top_w, mod, norm_g):
    b, l, _ = x1.shape
    tl = 256
    per_batch = mod.shape[0] > 1
    mod_map = (lambda i, j: (i, 0, 0)) if per_batch else (lambda i, j: (0, 0, 0))
    tok = lambda w: pl.BlockSpec((1, tl, w), lambda i, j: (i, j, 0))
    return pl.pallas_call(
        _combine_kernel,
        grid=(b, l // tl),
        in_specs=[tok(D_MODEL),
                  pl.BlockSpec((TOP_K, 1, tl, D_MODEL), lambda i, j: (0, i, j, 0)),
                  tok(TOP_K),
                  pl.BlockSpec((1, 1, 6 * D_MODEL), mod_map),
                  pl.BlockSpec((1, D_MODEL), lambda i, j: (0, 0))],
        out_specs=tok(D_MODEL),
        out_shape=jax.ShapeDtypeStruct((b, l, D_MODEL), F32),
        compiler_params=pltpu.CompilerParams(dimension_semantics=("arbitrary", "arbitrary"),
                                             vmem_limit_bytes=VMEM_LIMIT),
        name="moe_combine_final_norm",
    )(x1, yg, top_w, mod, norm_g.reshape(1, D_MODEL))


def _route(top_i, rank, counts, cnt_tiles):
    t = top_i.shape[0]
    p = t * TOP_K + N_EXPERTS * MOE_TM
    n_tiles = p // MOE_TM
    padded = ((counts + MOE_TM - 1) // MOE_TM) * MOE_TM
    ends = jnp.cumsum(padded)
    starts = ends - padded
    take = lambda table, idx: table.at[idx].get(mode="promise_in_bounds")
    dest = take(starts, top_i) + rank
    tile_start = jnp.arange(n_tiles, dtype=jnp.int32) * MOE_TM
    tile_expert = jnp.minimum(jnp.sum((ends[None, :] <= tile_start[:, None]).astype(jnp.int32), axis=1),
                              N_EXPERTS - 1)
    n_used = (ends[-1] // MOE_TM).reshape(1)
    tile_id = jnp.arange(n_tiles, dtype=jnp.int32)
    prev_expert = jnp.concatenate([jnp.full((1,), -1, tile_expert.dtype), tile_expert[:-1]])
    first = ((tile_expert != prev_expert) & (tile_id < n_used[0])).astype(jnp.int32)
    slot = (jnp.cumsum(first) - 1) % 2
    eid = jnp.arange(N_EXPERTS, dtype=jnp.int32)
    later = jnp.where((eid[None, :] > eid[:, None]) & (counts[None, :] > 0), eid[None, :], N_EXPERTS)
    next_of = jnp.min(later, axis=1)
    next_of = jnp.where(next_of < N_EXPERTS, next_of, -1)
    nxt = jnp.sum(jnp.where(tile_expert[:, None] == eid[None, :], next_of[None, :], 0), axis=1)
    plan = jnp.stack([tile_expert, first, nxt, slot]).astype(jnp.int32)
    tok_tiles = cnt_tiles.shape[0]
    block = (t * TOP_K) // tok_tiles
    excl = jnp.cumsum(cnt_tiles, axis=0) - cnt_tiles
    off = jnp.cumsum(cnt_tiles, axis=1) - cnt_tiles
    excl_e = take(excl.T, tile_expert)
    shift_e = take((off - excl).T, tile_expert)
    r = tile_start[:, None] + jnp.arange(MOE_TM, dtype=jnp.int32)[None, :] - take(starts, tile_expert)[:, None]
    ti = jnp.sum((excl_e[:, None, :] <= r[:, :, None]).astype(jnp.int32), axis=2) - 1
    here = jnp.arange(tok_tiles, dtype=jnp.int32)[None, None, :] == ti[:, :, None]
    shift_i = jnp.sum(jnp.where(here, shift_e[:, None, :], 0), axis=2)
    valid = r < take(counts, tile_expert)[:, None]
    spread = (jnp.arange(p, dtype=jnp.int32) % (t * TOP_K)).reshape(n_tiles, MOE_TM)
    src_row = jnp.where(valid, ti * block + r + shift_i, spread).reshape(p)
    return dest, src_row, plan, n_used.astype(jnp.int32)


def _layer_front(x, mod, s_ssd_f, s_ssd_b, s_s5_f, s_s5_b, grid_cols, prm, xs_prev):
    z, xbc, u, dt = _inproj_call(x, mod, prm["norm1_g"], prm["w_in_r"])
    y_ssd, n_ssd_f, n_ssd_b = _ssd_call(z, xbc, dt, s_ssd_f, s_ssd_b, prm["ssd"])
    y_f, n_s5_f = _s5_call(u, _s5_state_in(s_s5_f), prm["s5_f"], grid_cols, False)
    y_s5, n_s5_b = _s5_call(u, _s5_state_in(s_s5_b), prm["s5_b"], grid_cols, True,
                            y_first=y_f, glu=prm["glu"])
    x1, xs_buf, top_i, top_w, rank, cnt_tiles, cnt = _post_call(
        x, y_ssd, y_s5, mod, prm["w_out"], prm["norm2_g"], prm["w_router"], prm["b_router"], xs_prev)
    per_token = lambda a: jnp.swapaxes(a[:, :TOP_K, :], 1, 2)
    top_i, top_w, rank = per_token(top_i), per_token(top_w), per_token(rank)
    states = (n_ssd_f, n_ssd_b, _s5_state_out(n_s5_f), _s5_state_out(n_s5_b))
    return x1, (xs_buf, top_i, rank, cnt, cnt_tiles), top_w, states


def kernel(x_prompt, x_sample, state_ssd_fwd, state_ssd_bwd, state_s5_fwd, state_s5_bwd, c, c_ctx, w_ada, b_ada, norm1_g, w_in, ssd_conv_w, ssd_conv_b, ssd_dt_bias_fwd, ssd_dt_bias_bwd, ssd_a_log_fwd, ssd_a_log_bwd, ssd_d, ssd_norm_g, s5_a_re_fwd, s5_a_im_fwd, s5_log_dt_fwd, s5_b_re_fwd, s5_b_im_fwd, s5_c_re_fwd, s5_c_im_fwd, s5_a_re_bwd, s5_a_im_bwd, s5_log_dt_bwd, s5_b_re_bwd, s5_b_im_bwd, s5_c_re_bwd, s5_c_im_bwd, s5_d, w_glu, b_glu, w_out, norm2_g, w_router, b_router, w_gate_up, b_gate_up, w_down, b_down, norm_f_g):
    depth = w_ada.shape[0]
    assert depth == 1, "single trunk layer"
    nb_ctx, l_ctx, _ = x_prompt.shape
    nb_lat, l_lat, _ = x_sample.shape

    w_in0 = w_in[0]
    dt_lo = SSD_WIDTH + SSD_CONV_CH
    dt_cols = jnp.pad(w_in0[:, dt_lo:dt_lo + 2 * SSD_HEADS], ((0, 0), (0, DT_PAD - 2 * SSD_HEADS)))
    w_in_r = jnp.concatenate([w_in0[:, :dt_lo], w_in0[:, dt_lo + 2 * SSD_HEADS:], dt_cols], axis=1).astype(BF16)
    pad_dt = lambda f, b: jnp.pad(jnp.concatenate([f, b]).astype(F32), (0, DT_PAD - 2 * SSD_HEADS)).reshape(1, DT_PAD)
    ssd_prm = {
        "conv_w": jnp.pad(ssd_conv_w[0].astype(F32), ((0, 5), (0, 0))),
        "conv_b": ssd_conv_b[0].astype(F32).reshape(1, SSD_CONV_CH),
        "dt_bias": pad_dt(ssd_dt_bias_fwd[0], ssd_dt_bias_bwd[0]),
        "a": pad_dt(-jnp.exp(ssd_a_log_fwd[0].astype(F32)), -jnp.exp(ssd_a_log_bwd[0].astype(F32))),
        "d_skip": jnp.repeat(ssd_d[0].astype(F32), SSD_HEAD_DIM).reshape(1, SSD_WIDTH),
        "norm_g": ssd_norm_g[0].astype(F32).reshape(1, SSD_WIDTH),
        "expand": (jnp.arange(DT_PAD)[None, :, None]
                   == (jnp.arange(SSD_WIDTH) // SSD_HEAD_DIM)[None, None, :]
                   + SSD_HEADS * jnp.arange(2)[:, None, None]).astype(BF16),
    }
    prm = {
        "norm1_g": norm1_g[0], "w_in_r": w_in_r, "ssd": ssd_prm,
        "s5_f": _s5_params(s5_a_re_fwd[0], s5_a_im_fwd[0], s5_log_dt_fwd[0], s5_b_re_fwd[0],
                           s5_b_im_fwd[0], s5_c_re_fwd[0], s5_c_im_fwd[0]),
        "s5_b": _s5_params(s5_a_re_bwd[0], s5_a_im_bwd[0], s5_log_dt_bwd[0], s5_b_re_bwd[0],
                           s5_b_im_bwd[0], s5_c_re_bwd[0], s5_c_im_bwd[0]),
        "glu": {"d": s5_d[0].astype(F32).reshape(1, S5_WIDTH), "w": w_glu[0].astype(BF16),
                "b": b_glu[0].astype(F32).reshape(1, 2 * S5_WIDTH)},
        "w_out": w_out[0].astype(BF16), "norm2_g": norm2_g[0],
        "w_router": w_router[0].astype(F32), "b_router": b_router[0].astype(F32),
    }

    conds = jnp.concatenate([c_ctx[None, :], c], axis=0)
    conds = jnp.pad(conds, ((0, (-conds.shape[0]) % 8), (0, 0)))
    mod = _ada_call(conds, w_ada[0], b_ada[0])
    mod_ctx = mod[0:1].reshape(1, 1, 6 * D_MODEL)
    mod_lat = mod[1:1 + nb_lat].reshape(nb_lat, 1, 6 * D_MODEL)

    zero_ssd = jnp.zeros((nb_ctx, SSD_HEADS, SSD_HEAD_DIM, SSD_STATE), F32)
    zero_s5 = jnp.zeros((nb_ctx, S5_GROUPS, S5_STATE, 2), F32)
    t_ctx = nb_ctx * l_ctx
    x1_c, routed_c, tw_c, st_c = _layer_front(
        x_prompt, mod_ctx, zero_ssd, zero_ssd, zero_s5, zero_s5, False, prm, None)
    x1_l, routed_l, tw_l, _ = _layer_front(
        x_sample, mod_lat, state_ssd_fwd[:, 0], state_ssd_bwd[:, 0], state_s5_fwd[:, 0],
        state_s5_bwd[:, 0], True, prm, routed_c[0])

    _, ti_c, rk_c, cnt_c, ct_c = routed_c
    xs_tiles, ti_l, rk_l, cnt_l, ct_l = routed_l
    cnt_c = cnt_c.reshape(N_EXPERTS).astype(jnp.int32)
    cnt_l = cnt_l.reshape(N_EXPERTS).astype(jnp.int32)
    cnt_tiles = jnp.concatenate([ct_c, ct_l], axis=0).reshape(-1, N_EXPERTS).astype(jnp.int32)
    ti_l2 = ti_l.reshape(-1, TOP_K)
    top_i = jnp.concatenate([ti_c.reshape(-1, TOP_K), ti_l2], axis=0)
    rank = jnp.concatenate([rk_c.reshape(-1, TOP_K),
                            rk_l.reshape(-1, TOP_K) + cnt_c.at[ti_l2].get(mode="promise_in_bounds")], axis=0)
    dest, src_row, plan, n_used = _route(top_i, rank, cnt_c + cnt_l, cnt_tiles)
    xs = xs_tiles.at[src_row].get(mode="promise_in_bounds")
    ys = _moe_call(plan, n_used, xs, w_gate_up[0], b_gate_up[0], w_down[0], b_down[0])

    def picked_rows(d, nb, l):
        rows = ys.at[d.T.reshape(-1)].get(mode="promise_in_bounds")
        return rows.reshape(TOP_K, nb, l, D_MODEL)

    y_prompt = _combine_call(x1_c, picked_rows(dest[:t_ctx], nb_ctx, l_ctx), tw_c, mod_ctx, norm_f_g)
    y_sample = _combine_call(x1_l, picked_rows(dest[t_ctx:], nb_lat, l_lat), tw_l, mod_lat, norm_f_g)

    ssd_f, ssd_b, s5_f, s5_b = st_c
    return (y_prompt, y_sample, ssd_f[:, None], ssd_b[:, None], s5_f[:, None], s5_b[:, None])
```

```python
import functools
import math

import jax
import jax.numpy as jnp
from jax import lax
from jax.experimental import pallas as pl
from jax.experimental.pallas import tpu as pltpu

F32 = jnp.float32
BF16 = jnp.bfloat16

D_MODEL = 1024
GRID_W = 64
SSD_WIDTH = 512
SSD_HEAD_DIM = 64
SSD_HEADS = 8
SSD_GROUPS = 2
SSD_HEADS_PER_GROUP = SSD_HEADS // SSD_GROUPS
SSD_STATE = 128
SSD_CHUNK = 128
SSD_CONV_CH = SSD_WIDTH + 2 * SSD_GROUPS * SSD_STATE
S5_WIDTH = 512
S5_CH = 16
S5_GROUPS = 32
S5_STATE = 64
N_EXPERTS = 32
TOP_K = 4
D_FF = 1024
SWIGLU_LIMIT = 7.0
SWIGLU_ALPHA = 1.702
RMS_EPS = 1e-5

LANES = 128
DT_PAD = 128
S5_BATCH = 8
S5_COLS = 2 * S5_GROUPS * S5_STATE
S5_BLK = 512
S5_SUB_STEPS = 32
S5_SCAN_GROUP = 4
S5_BLOCK_STEPS = 256
MOE_TM = 256
VMEM_LIMIT = 56 * 1024 * 1024


def _sigmoid(x):
    return 1.0 / (1.0 + jnp.exp(-x))


def _split3(x):
    hi = x.astype(BF16)
    r1 = x - hi.astype(F32)
    mid = r1.astype(BF16)
    lo = (r1 - mid.astype(F32)).astype(BF16)
    return hi, mid, lo


def _split2(x):
    hi = x.astype(BF16)
    return hi, (x - hi.astype(F32)).astype(BF16)


def _dot(a, b):
    return jnp.dot(a, b, preferred_element_type=F32)


def _ada_kernel(c_ref, w_ref, b_ref, o_ref):
    c = c_ref[...]
    s = c * _sigmoid(c)
    o_ref[...] = _dot(s.astype(BF16), w_ref[...].astype(BF16)) + b_ref[...]


def _ada_call(conds, w_ada, b_ada):
    n = w_ada.shape[1]
    tn = 1536
    rows = conds.shape[0]
    return pl.pallas_call(
        _ada_kernel,
        grid=(n // tn,),
        in_specs=[pl.BlockSpec((rows, D_MODEL), lambda j: (0, 0)),
                  pl.BlockSpec((D_MODEL, tn), lambda j: (0, j)),
                  pl.BlockSpec((1, tn), lambda j: (0, j))],
        out_specs=pl.BlockSpec((rows, tn), lambda j: (0, j)),
        out_shape=jax.ShapeDtypeStruct((rows, n), F32),
        compiler_params=pltpu.CompilerParams(dimension_semantics=("arbitrary",),
                                             vmem_limit_bytes=VMEM_LIMIT),
        name="ada_mod",
    )(conds, w_ada, b_ada.reshape(1, n))


def _inproj_kernel(x_ref, mod_ref, g_ref, w_ref, z_ref, xbc_ref, u_ref, dt_ref):
    x = x_ref[0]
    ms = jnp.mean(x * x, axis=-1, keepdims=True)
    y = x * lax.rsqrt(ms + RMS_EPS) * g_ref[...]
    sh = mod_ref[0, :, 0:D_MODEL]
    sc = mod_ref[0, :, D_MODEL:2 * D_MODEL]
    h = (y * (1.0 + sc) + sh).astype(BF16)
    z_ref[0] = _dot(h, w_ref[:, 0:512]).astype(BF16)
    xbc_ref[0] = _dot(h, w_ref[:, 512:1536]).astype(BF16)
    u_ref[0] = _dot(h, w_ref[:, 1536:2048])
    dt_ref[0] = _dot(h, w_ref[:, 2048:2048 + DT_PAD])


def _inproj_call(x, mod, norm_g, w_in_r):
    b, l, _ = x.shape
    tl = min(l, 512)
    per_batch = mod.shape[0] > 1
    mod_map = (lambda i, j: (i, 0, 0)) if per_batch else (lambda i, j: (0, 0, 0))
    ncol = w_in_r.shape[1]
    tok = lambda w: pl.BlockSpec((1, tl, w), lambda i, j: (i, j, 0))
    return pl.pallas_call(
        _inproj_kernel,
        grid=(b, l // tl),
        in_specs=[tok(D_MODEL),
                  pl.BlockSpec((1, 1, 6 * D_MODEL), mod_map),
                  pl.BlockSpec((1, D_MODEL), lambda i, j: (0, 0)),
                  pl.BlockSpec((D_MODEL, ncol), lambda i, j: (0, 0))],
        out_specs=[tok(SSD_WIDTH), tok(SSD_CONV_CH), tok(S5_WIDTH), tok(DT_PAD)],
        out_shape=[jax.ShapeDtypeStruct((b, l, SSD_WIDTH), BF16),
                   jax.ShapeDtypeStruct((b, l, SSD_CONV_CH), BF16),
                   jax.ShapeDtypeStruct((b, l, S5_WIDTH), F32),
                   jax.ShapeDtypeStruct((b, l, DT_PAD), F32)],
        compiler_params=pltpu.CompilerParams(dimension_semantics=("arbitrary", "arbitrary"),
                                             vmem_limit_bytes=VMEM_LIMIT),
        name="norm1_inproj",
    )(x, mod, norm_g.reshape(1, D_MODEL), w_in_r)


def _softplus(x):
    return jnp.maximum(x, 0.0) + jnp.log1p(jnp.exp(-jnp.abs(x)))


def _ssd_kernel(z_ref, xbc_ref, dt_ref, s0f_ref, s0b_ref, cw_ref, cb_ref, dtb_ref, a_ref,
                dskip_ref, ng_ref, ex_ref, y_ref, sf_ref, sb_ref,
                xc_s, dts_s, yf_s, yb_s, stf_s, stb_s, *, seq):
    q = SSD_CHUNK
    nc = seq // q
    gw = SSD_HEADS_PER_GROUP * SSD_HEAD_DIM
    row_i = lax.broadcasted_iota(jnp.int32, (q, 1), 0)

    def conv_body(c, carry):
        r0 = pl.multiple_of(c * q, q)
        cur = xbc_ref[0, pl.ds(r0, q), :].astype(F32)
        p0 = pl.multiple_of(jnp.maximum(r0 - 16, 0), 16)
        n0 = pl.multiple_of(jnp.minimum(r0 + q, seq - 16), 16)
        prev_row = xbc_ref[0, pl.ds(p0, 16), :].astype(F32)[15:16, :]
        next_row = xbc_ref[0, pl.ds(n0, 16), :].astype(F32)[0:1, :]
        prev_row = jnp.where(c > 0, prev_row, 0.0)
        next_row = jnp.where(c < nc - 1, next_row, 0.0)
        down = jnp.where(row_i == 0, prev_row, pltpu.roll(cur, 1, axis=0))
        up = jnp.where(row_i == q - 1, next_row, pltpu.roll(cur, q - 1, axis=0))
        v = cw_ref[0:1, :] * down + cw_ref[1:2, :] * cur + cw_ref[2:3, :] * up + cb_ref[...]
        xc_s[pl.ds(r0, q), :] = v * _sigmoid(v)
        dts_s[pl.ds(r0, q), :] = _softplus(dt_ref[0, pl.ds(r0, q), :] + dtb_ref[...])
        return carry

    lax.fori_loop(0, nc, conv_body, 0)

    li = lax.broadcasted_iota(jnp.int32, (q, q), 0)
    si = lax.broadcasted_iota(jnp.int32, (q, q), 1)
    head_of_lane = lax.broadcasted_iota(jnp.int32, (q, gw), 1) // SSD_HEAD_DIM

    def expand(v, e_mat):
        hi, lo = _split2(v)
        return _dot(hi, e_mat) + _dot(lo, e_mat)

    def chunk(c, direction):
        r0 = pl.multiple_of(c * q, q)
        st_ref = stf_s if direction == 0 else stb_s
        y_dst = yf_s if direction == 0 else yb_s
        e_mat = ex_ref[direction]
        causal = (li >= si) if direction == 0 else (li <= si)
        tri = jnp.where(causal, 1.0, 0.0).astype(BF16)
        dts = dts_s[pl.ds(r0, q), :]
        dta = dts * a_ref[...]
        d_hi, d_mid, d_lo = _split3(dta)
        cs = _dot(tri, d_hi) + _dot(tri, d_mid) + _dot(tri, d_lo)
        tot = cs[q - 1:q, :] if direction == 0 else cs[0:1, :]
        cs_t = cs.T
        ecs = jnp.exp(cs)
        tail = jnp.exp(tot - cs)
        dt_x = expand(dts, e_mat)
        dtw_x = expand(dts * tail, e_mat)
        etot_x = expand(jnp.broadcast_to(jnp.exp(tot), (8, DT_PAD)), e_mat)[0:1, :]
        for g in range(SSD_GROUPS):
            b_off = SSD_WIDTH + g * SSD_STATE
            c_off = SSD_WIDTH + SSD_GROUPS * SSD_STATE + g * SSD_STATE
            x_g = xc_s[pl.ds(r0, q), g * gw:(g + 1) * gw]
            bm = xc_s[pl.ds(r0, q), b_off:b_off + SSD_STATE]
            cm = xc_s[pl.ds(r0, q), c_off:c_off + SSD_STATE]
            cb = lax.dot_general(cm.astype(BF16), bm.astype(BF16), (((1,), (1,)), ((), ())),
                                 preferred_element_type=F32)
            st_g = st_ref[g]
            xdt = x_g * dt_x[:, g * gw:(g + 1) * gw]
            lhs, rhs = [], []
            for hh in range(SSD_HEADS_PER_GROUP):
                col = g * SSD_HEADS_PER_GROUP + hh + SSD_HEADS * direction
                seg = jnp.broadcast_to(cs[:, col:col + 1], (q, q)) - cs_t[col:col + 1, :]
                dec = jnp.exp(jnp.where(causal, seg, -1e30))
                lhs.append((cb * dec).astype(BF16))
                lhs.append((cm * jnp.broadcast_to(ecs[:, col:col + 1], (q, q))).astype(BF16))
                own = head_of_lane == hh
                rhs.append(jnp.where(own, xdt, 0.0).astype(BF16))
                rhs.append(jnp.where(own, st_g, 0.0).astype(BF16))
            y_dst[pl.ds(r0, q), g * gw:(g + 1) * gw] = _dot(jnp.concatenate(lhs, axis=1),
                                                             jnp.concatenate(rhs, axis=0))
            xw = (x_g * dtw_x[:, g * gw:(g + 1) * gw]).astype(BF16)
            st_ref[g] = etot_x[:, g * gw:(g + 1) * gw] * st_g + _dot(bm.T.astype(BF16), xw)

    stf_s[...] = s0f_ref[0]
    stb_s[...] = s0b_ref[0]

    def both(i, carry):
        chunk(i, 0)
        chunk(nc - 1 - i, 1)
        return carry

    lax.fori_loop(0, nc, both, 0)
    sf_ref[0] = stf_s[...]
    sb_ref[0] = stb_s[...]

    def finish(c, carry):
        r0 = pl.multiple_of(c * q, q)
        xs = xc_s[pl.ds(r0, q), 0:SSD_WIDTH]
        zz = z_ref[0, pl.ds(r0, q), :].astype(F32)
        y = (yf_s[pl.ds(r0, q), :] + yb_s[pl.ds(r0, q), :] + dskip_ref[...] * xs) * (zz * _sigmoid(zz))
        ms = jnp.mean(y * y, axis=-1, keepdims=True)
        y_ref[0, pl.ds(r0, q), :] = (y * lax.rsqrt(ms + RMS_EPS) * ng_ref[...]).astype(y_ref.dtype)
        return carry

    lax.fori_loop(0, nc, finish, 0)


def _ssd_state_in(s):
    b = s.shape[0]
    s = s.astype(F32).reshape(b, SSD_GROUPS, SSD_HEADS_PER_GROUP, SSD_HEAD_DIM, SSD_STATE)
    return s.transpose(0, 1, 4, 2, 3).reshape(b, SSD_GROUPS, SSD_STATE, SSD_HEADS_PER_GROUP * SSD_HEAD_DIM)


def _ssd_state_out(s):
    b = s.shape[0]
    s = s.reshape(b, SSD_GROUPS, SSD_STATE, SSD_HEADS_PER_GROUP, SSD_HEAD_DIM)
    return s.transpose(0, 1, 3, 4, 2).reshape(b, SSD_HEADS, SSD_HEAD_DIM, SSD_STATE)


def _ssd_call(z, xbc, dt, s0f, s0b, prm):
    b, l, _ = z.shape
    st_shape = (SSD_GROUPS, SSD_STATE, SSD_HEADS_PER_GROUP * SSD_HEAD_DIM)
    tok = lambda w: pl.BlockSpec((1, l, w), lambda i: (i, 0, 0))
    st_spec = pl.BlockSpec((1,) + st_shape, lambda i: (i, 0, 0, 0))
    par = lambda r, w: pl.BlockSpec((r, w), lambda i: (0, 0))
    y, sf, sb = pl.pallas_call(
        functools.partial(_ssd_kernel, seq=l),
        grid=(b,),
        in_specs=[tok(SSD_WIDTH), tok(SSD_CONV_CH), tok(DT_PAD), st_spec, st_spec,
                  par(8, SSD_CONV_CH), par(1, SSD_CONV_CH), par(1, DT_PAD), par(1, DT_PAD),
                  par(1, SSD_WIDTH), par(1, SSD_WIDTH),
                  pl.BlockSpec((2, DT_PAD, SSD_WIDTH), lambda i: (0, 0, 0))],
        out_specs=[tok(SSD_WIDTH), st_spec, st_spec],
        out_shape=[jax.ShapeDtypeStruct((b, l, SSD_WIDTH), BF16),
                   jax.ShapeDtypeStruct((b,) + st_shape, F32),
                   jax.ShapeDtypeStruct((b,) + st_shape, F32)],
        scratch_shapes=[pltpu.VMEM((l, SSD_CONV_CH), F32),
                        pltpu.VMEM((l, DT_PAD), F32),
                        pltpu.VMEM((l, SSD_WIDTH), F32),
                        pltpu.VMEM((l, SSD_WIDTH), F32),
                        pltpu.VMEM(st_shape, F32),
                        pltpu.VMEM(st_shape, F32)],
        compiler_params=pltpu.CompilerParams(dimension_semantics=("arbitrary",),
                                             vmem_limit_bytes=VMEM_LIMIT),
        name="ssd_mixer",
    )(z, xbc, dt, _ssd_state_in(s0f), _ssd_state_in(s0b),
      prm["conv_w"], prm["conv_b"], prm["dt_bias"], prm["a"], prm["d_skip"], prm["norm_g"], prm["expand"])
    return y, _ssd_state_out(sf), _ssd_state_out(sb)


def _gelu_tanh(x):
    return 0.5 * x * (1.0 + jnp.tanh(math.sqrt(2.0 / math.pi) * (x + 0.044715 * (x * x * x))))


def _s5_kernel(*refs, grid_cols, reverse, final):
    if final:
        (u_ref, s0_ref, lr_ref, li_ref, wb_ref, wc_ref, yf_ref, d_ref, wg_ref, bg_ref,
         o_ref, sfin_ref, ur_s, yr_s, st_s) = refs
    else:
        (u_ref, s0_ref, lr_ref, li_ref, wb_ref, wc_ref,
         o_ref, sfin_ref, ur_s, yr_s, st_s) = refs
    nb = S5_BATCH
    steps = S5_BLOCK_STEPS
    rows = steps * nb
    sub_rows = S5_SUB_STEPS * nb
    n_sub = steps // S5_SUB_STEPS
    j = pl.program_id(1)

    @pl.when(j == 0)
    def _():
        st_s[...] = s0_ref[0]

    def to_scan(v):
        v = jnp.transpose(v, (2, 1, 0, 3)) if grid_cols else jnp.swapaxes(v, 0, 1)
        return v.reshape(rows, S5_WIDTH)

    def from_scan(v):
        if grid_cols:
            return jnp.transpose(v.reshape(u_ref.shape[2], u_ref.shape[1], nb, S5_WIDTH), (2, 1, 0, 3))
        return jnp.swapaxes(v.reshape(steps, nb, S5_WIDTH), 0, 1)

    ur_s[...] = to_scan(u_ref[...])

    order = list(range(S5_SUB_STEPS))[::-1] if reverse else list(range(S5_SUB_STEPS))
    blk_per_half = S5_COLS // S5_BLK // 2

    def by_groups(v, perm):
        return jnp.concatenate([v[g * nb:(g + 1) * nb] for g in perm], axis=0)

    def sub_chunk(i, carry):
        sidx = (n_sub - 1 - i) if reverse else i
        r0 = pl.multiple_of(sidx * sub_rows, sub_rows)
        for hf in range(2):
            u_half = ur_s[pl.ds(r0, sub_rows), hf * 256:(hf + 1) * 256]
            if reverse:
                u_half = by_groups(u_half, order)
            uh = u_half.astype(BF16)
            grp = S5_SCAN_GROUP
            for bp in range(blk_per_half // grp):
                blk = hf * blk_per_half + grp * bp
                c0 = blk * S5_BLK
                wide = grp * S5_BLK
                q = S5_BLK // 2
                lam_r = jnp.broadcast_to(lr_ref[:, c0:c0 + wide], (nb, wide))
                lam_i = jnp.broadcast_to(li_ref[:, c0:c0 + wide], (nb, wide))
                bu = _dot(uh, wb_ref[hf, :, grp * bp * S5_BLK:grp * (bp + 1) * S5_BLK])
                s = st_s[:, c0:c0 + wide]
                states = []
                for k in range(S5_SUB_STEPS):
                    sw = jnp.concatenate([s[:, (2 * g + 1 - h) * q:(2 * g + 2 - h) * q]
                                          for g in range(grp) for h in range(2)], axis=1)
                    s = lam_r * s + lam_i * sw + bu[k * nb:(k + 1) * nb, :]
                    states.append(s)
                st_s[:, c0:c0 + wide] = s
                traj = jnp.concatenate(states, axis=0).astype(BF16)
                for g in range(grp):
                    y_blk = _dot(traj[:, g * S5_BLK:(g + 1) * S5_BLK], wc_ref[blk + g])
                    if reverse:
                        y_blk = by_groups(y_blk, order)
                    yr_s[pl.ds(r0, sub_rows), (blk + g) * 64:(blk + g + 1) * 64] = y_blk
        return carry

    lax.fori_loop(0, n_sub, sub_chunk, 0)

    @pl.when(j == pl.num_programs(1) - 1)
    def _():
        sfin_ref[0] = st_s[...]

    if not final:
        o_ref[0] = yr_s[...]
        return

    def glu_chunk(i, carry):
        r0 = pl.multiple_of(i * sub_rows, sub_rows)
        y = (yr_s[pl.ds(r0, sub_rows), :] + yf_ref[0, pl.ds(r0, sub_rows), :]
             + d_ref[...] * ur_s[pl.ds(r0, sub_rows), :])
        y = _gelu_tanh(y).astype(BF16)
        val = _dot(y, wg_ref[:, 0:S5_WIDTH]) + bg_ref[:, 0:S5_WIDTH]
        gate = _dot(y, wg_ref[:, S5_WIDTH:2 * S5_WIDTH]) + bg_ref[:, S5_WIDTH:2 * S5_WIDTH]
        yr_s[pl.ds(r0, sub_rows), :] = val * _sigmoid(gate)
        return carry

    lax.fori_loop(0, n_sub, glu_chunk, 0)

    o_ref[...] = from_scan(yr_s[...])


def _s5_call(u, s0, prm, grid_cols, reverse, y_first=None, glu=None):
    b, l, _ = u.shape
    nbg = b // S5_BATCH
    nblk = l // S5_BLOCK_STEPS
    rows = S5_BLOCK_STEPS * S5_BATCH
    final = y_first is not None
    blk_of = (lambda j: nblk - 1 - j) if reverse else (lambda j: j)
    if grid_cols:
        n_rows = l // GRID_W
        u_in = u.reshape(b, n_rows, GRID_W, S5_WIDTH)
        cols_per_blk = S5_BLOCK_STEPS // n_rows
        tok_spec = pl.BlockSpec((S5_BATCH, n_rows, cols_per_blk, S5_WIDTH),
                                lambda i, j: (i, 0, blk_of(j), 0))
        out_nat = jax.ShapeDtypeStruct((b, n_rows, GRID_W, S5_WIDTH), F32)
    else:
        u_in = u
        tok_spec = pl.BlockSpec((S5_BATCH, S5_BLOCK_STEPS, S5_WIDTH), lambda i, j: (i, blk_of(j), 0))
        out_nat = jax.ShapeDtypeStruct((b, l, S5_WIDTH), F32)
    scan_spec = pl.BlockSpec((1, rows, S5_WIDTH), lambda i, j: (i, blk_of(j), 0))
    st_spec = pl.BlockSpec((1, S5_BATCH, S5_COLS), lambda i, j: (i, 0, 0))
    full = lambda shape: pl.BlockSpec(shape, lambda i, j: (0,) * len(shape))
    in_specs = [tok_spec, st_spec, full((1, S5_COLS)), full((1, S5_COLS)),
                full((2, 256, 2048)), full((S5_COLS // S5_BLK, S5_BLK, 64))]
    args = [u_in, s0, prm["lam_r"], prm["lam_i"], prm["wb"], prm["wc"]]
    if final:
        in_specs += [scan_spec, full((1, S5_WIDTH)), full((S5_WIDTH, 2 * S5_WIDTH)), full((1, 2 * S5_WIDTH))]
        args += [y_first, glu["d"], glu["w"], glu["b"]]
        out_specs = [tok_spec, st_spec]
        out_shape = [out_nat, jax.ShapeDtypeStruct((nbg, S5_BATCH, S5_COLS), F32)]
    else:
        out_specs = [scan_spec, st_spec]
        out_shape = [jax.ShapeDtypeStruct((nbg, l * S5_BATCH, S5_WIDTH), F32),
                     jax.ShapeDtypeStruct((nbg, S5_BATCH, S5_COLS), F32)]
    y, sfin = pl.pallas_call(
        functools.partial(_s5_kernel, grid_cols=grid_cols, reverse=reverse, final=final),
        grid=(nbg, nblk),
        in_specs=in_specs,
        out_specs=out_specs,
        out_shape=out_shape,
        scratch_shapes=[pltpu.VMEM((rows, S5_WIDTH), F32),
                        pltpu.VMEM((rows, S5_WIDTH), F32),
                        pltpu.VMEM((S5_BATCH, S5_COLS), F32)],
        compiler_params=pltpu.CompilerParams(dimension_semantics=("arbitrary", "arbitrary"),
                                             vmem_limit_bytes=VMEM_LIMIT),
        name="s5_final" if final else "s5_first",
    )(*args)
    if final:
        y = y.reshape(b, l, S5_WIDTH)
    return y, sfin


def _s5_cols(t):
    lead = t.shape[:-2]
    return t.reshape(lead + (S5_COLS // S5_BLK, S5_BLK // 2))


def _s5_state_in(s0):
    b = s0.shape[0]
    re, im = _s5_cols(s0[..., 0].astype(F32)), _s5_cols(s0[..., 1].astype(F32))
    return jnp.stack([re, im], axis=-2).reshape(b // S5_BATCH, S5_BATCH, S5_COLS)


def _s5_state_out(s):
    b = s.shape[0] * S5_BATCH
    s = s.reshape(b, S5_COLS // S5_BLK, 2, S5_BLK // 2)
    re = s[:, :, 0].reshape(b, S5_GROUPS, S5_STATE)
    im = s[:, :, 1].reshape(b, S5_GROUPS, S5_STATE)
    return jnp.stack([re, im], axis=-1)


def _s5_params(a_re, a_im, log_dt, b_re, b_im, c_re, c_im):
    a_re, a_im = a_re.astype(F32), a_im.astype(F32)
    delta = jnp.exp(log_dt.astype(F32))[:, None]
    mag = jnp.exp(a_re * delta)
    lam_re, lam_im = mag * jnp.cos(a_im * delta), mag * jnp.sin(a_im * delta)
    den = a_re * a_re + a_im * a_im
    nr = lam_re - 1.0
    f_re = (nr * a_re + lam_im * a_im) / den
    f_im = (lam_im * a_re - nr * a_im) / den
    b_re, b_im = b_re.astype(F32), b_im.astype(F32)
    bb_re = f_re[..., None] * b_re - f_im[..., None] * b_im
    bb_im = f_re[..., None] * b_im + f_im[..., None] * b_re
    nblk = S5_COLS // S5_BLK
    lr = _s5_cols(lam_re)
    li = _s5_cols(lam_im)
    lam_r = jnp.stack([lr, lr], axis=-2).reshape(1, S5_COLS)
    lam_i = jnp.stack([-li, li], axis=-2).reshape(1, S5_COLS)
    gq = S5_BLK // 2 // S5_STATE
    eye = jnp.eye(gq, dtype=F32)

    def in_block(bb):
        t = bb.reshape(nblk, gq, S5_STATE, S5_CH)
        return jnp.einsum("gh,kgpc->kgchp", eye, t).reshape(nblk, gq * S5_CH, gq * S5_STATE)

    blk_in = jnp.concatenate([in_block(bb_re), in_block(bb_im)], axis=-1)
    half = nblk // 2
    eye_h = jnp.eye(half, dtype=F32)
    wb = jnp.einsum("jk,hkcn->hjckn", eye_h, blk_in.reshape(2, half, gq * S5_CH, S5_BLK))
    wb = wb.reshape(2, half * gq * S5_CH, half * S5_BLK).astype(BF16)

    def out_block(cc):
        t = cc.astype(F32).reshape(nblk, gq, S5_CH, S5_STATE)
        return jnp.einsum("gh,kgcp->kgphc", eye, t).reshape(nblk, gq * S5_STATE, gq * S5_CH)

    wc = jnp.concatenate([out_block(c_re), -out_block(c_im)], axis=1).astype(BF16)
    return {"lam_r": lam_r, "lam_i": lam_i, "wb": wb, "wc": wc}


def _post_kernel(*refs, tile_base, prev_rows):
    if prev_rows:
        prev_hbm, refs = refs[8], refs[:8] + refs[9:]
    (x_ref, ys_ref, y5_ref, mod_ref, wo_ref, g_ref, wr_ref, br_ref,
     x1_ref, xs_hbm, ti_ref, tw_ref, rk_ref, ct_ref, cnt_ref, cnt_s, xs_v, pv_v, sem_out, sem_prev) = refs
    tl = x_ref.shape[1]
    blk = tl * TOP_K
    n_steps = pl.num_programs(0) * pl.num_programs(1)
    step = pl.program_id(0) * pl.num_programs(1) + pl.program_id(1)
    buf = step % 2
    first = step == 0

    def block_copy(at_step, at_slot):
        row0 = pl.multiple_of((tile_base + at_step) * blk, blk)
        return pltpu.make_async_copy(xs_v.at[at_slot], xs_hbm.at[pl.ds(row0, blk)], sem_out.at[at_slot])

    @pl.when(step >= 2)
    def _():
        block_copy(step - 2, buf).wait()

    prev_blocks = prev_rows // blk

    def prev_in(at_step):
        row0 = pl.multiple_of(at_step * blk, blk)
        return pltpu.make_async_copy(prev_hbm.at[pl.ds(row0, blk)], pv_v.at[at_step % 2], sem_prev.at[0, at_step % 2])

    def prev_out(at_step):
        row0 = pl.multiple_of(at_step * blk, blk)
        return pltpu.make_async_copy(pv_v.at[at_step % 2], xs_hbm.at[pl.ds(row0, blk)], sem_prev.at[1, at_step % 2])

    if prev_rows:
        @pl.when(jnp.logical_and(step >= 2, step - 2 < prev_blocks))
        def _():
            prev_out(step - 2).wait()

        @pl.when(step < prev_blocks)
        def _():
            prev_in(step).start()

        @pl.when(jnp.logical_and(step >= 1, step - 1 < prev_blocks))
        def _():
            prev_in(step - 1).wait()
            prev_out(step - 1).start()

    @pl.when(first)
    def _():
        cnt_s[...] = jnp.zeros(cnt_s.shape, cnt_s.dtype)

    mixed = (_dot(ys_ref[0], wo_ref[0:SSD_WIDTH, :])
             + _dot(y5_ref[0].astype(BF16), wo_ref[SSD_WIDTH:SSD_WIDTH + S5_WIDTH, :]))
    g1 = mod_ref[0, :, 2 * D_MODEL:3 * D_MODEL]
    sh2 = mod_ref[0, :, 3 * D_MODEL:4 * D_MODEL]
    sc2 = mod_ref[0, :, 4 * D_MODEL:5 * D_MODEL]
    x1 = x_ref[0] + g1 * mixed
    x1_ref[0] = x1
    ms = jnp.mean(x1 * x1, axis=-1, keepdims=True)
    h = x1 * lax.rsqrt(ms + RMS_EPS) * g_ref[...] * (1.0 + sc2) + sh2

    h_hi, h_mid, _ = _split3(h)
    wr = wr_ref[...]
    w_hi, w_mid, _ = _split3(wr)
    logits = _dot(h_hi, w_hi) + _dot(h_hi, w_mid) + _dot(h_mid, w_hi) + br_ref[...]

    work = logits.T[0:N_EXPERTS, :]
    eid = lax.broadcasted_iota(jnp.int32, (N_EXPERTS, tl), 0).astype(F32)
    row8 = lax.broadcasted_iota(jnp.int32, (8, tl), 0)
    onehot = jnp.zeros((N_EXPERTS, tl), F32)
    sels, vals = [], []
    top_i = jnp.zeros((8, tl), jnp.int32)
    for k in range(TOP_K):
        m = jnp.max(work, axis=0, keepdims=True)
        idx = jnp.min(jnp.where(work == m, eid, float(N_EXPERTS)), axis=0, keepdims=True)
        sel = eid == idx
        sels.append(sel)
        vals.append(m)
        top_i = jnp.where(row8 == k, idx.astype(jnp.int32), top_i)
        onehot = jnp.where(sel, 1.0, onehot)
        work = jnp.where(sel, -jnp.inf, work)
    es = [jnp.exp(v - vals[0]) for v in vals]
    den = es[0] + es[1] + es[2] + es[3]
    top_w = jnp.zeros((8, tl), F32)
    for k in range(TOP_K):
        top_w = jnp.where(row8 == k, es[k] / den, top_w)

    ri = lax.broadcasted_iota(jnp.int32, (tl, tl), 0)
    ci = lax.broadcasted_iota(jnp.int32, (tl, tl), 1)
    earlier = jnp.where(ri < ci, 1.0, 0.0).astype(BF16)
    local = _dot(onehot.astype(BF16), earlier)
    before = local + cnt_s[...]
    cnt_tile = jnp.sum(onehot, axis=1, keepdims=True)
    er = lax.broadcasted_iota(jnp.int32, (N_EXPERTS, N_EXPERTS), 0)
    ec = lax.broadcasted_iota(jnp.int32, (N_EXPERTS, N_EXPERTS), 1)
    lower = jnp.where(ec < er, 1.0, 0.0).astype(BF16)
    off = _dot(lower, jnp.broadcast_to(cnt_tile, (N_EXPERTS, LANES)).astype(BF16))[:, 0:1]
    pos = off + local
    rank = jnp.zeros((8, tl), jnp.int32)
    row_id = lax.broadcasted_iota(jnp.int32, (tl * TOP_K, tl), 0).astype(F32)
    perm = jnp.zeros((tl * TOP_K, tl), F32)
    for k in range(TOP_K):
        rk = jnp.sum(jnp.where(sels[k], before, 0.0), axis=0, keepdims=True)
        rank = jnp.where(row8 == k, rk.astype(jnp.int32), rank)
        pk = jnp.sum(jnp.where(sels[k], pos, 0.0), axis=0, keepdims=True)
        perm = jnp.where(row_id == pk, 1.0, perm)
    xs_v[buf] = _dot(perm.astype(BF16), h.astype(BF16)).astype(BF16)
    block_copy(step, buf).start()
    cnt_s[...] = cnt_s[...] + cnt_tile
    ti_ref[0] = top_i
    tw_ref[0] = top_w
    rk_ref[0] = rank
    ct_ref[0] = cnt_tile
    cnt_ref[...] = cnt_s[...]

    @pl.when(step == n_steps - 1)
    def _():
        block_copy(step, buf).wait()

        @pl.when(step >= 1)
        def _():
            block_copy(step - 1, 1 - buf).wait()


def _post_call(x, y_ssd, y_s5, mod, w_out, norm_g, w_router, b_router, xs_prev=None):
    b, l, _ = x.shape
    tl = 256
    tiles_per_row = l // tl
    prev_rows = 0 if xs_prev is None else xs_prev.shape[0]
    assert prev_rows // (tl * TOP_K) + 2 <= b * tiles_per_row, "not enough grid steps to pass the earlier rows through"
    sorted_rows = prev_rows + b * l * TOP_K
    per_batch = mod.shape[0] > 1
    mod_map = (lambda i, j: (i, 0, 0)) if per_batch else (lambda i, j: (0, 0, 0))
    tok = lambda w: pl.BlockSpec((1, tl, w), lambda i, j: (i, j, 0))
    full = lambda r, w: pl.BlockSpec((r, w), lambda i, j: (0, 0))
    slots = pl.BlockSpec((1, 8, tl), lambda i, j: (i, 0, j))
    in_specs = [tok(D_MODEL), tok(SSD_WIDTH), tok(S5_WIDTH),
                pl.BlockSpec((1, 1, 6 * D_MODEL), mod_map),
                full(D_MODEL, D_MODEL), full(1, D_MODEL), full(D_MODEL, LANES),
                full(1, LANES)]
    w_router = jnp.pad(w_router, ((0, 0), (0, LANES - N_EXPERTS)))
    b_router = jnp.pad(b_router, (0, LANES - N_EXPERTS), constant_values=-1e30)
    args = [x, y_ssd, y_s5, mod, w_out, norm_g.reshape(1, D_MODEL), w_router, b_router.reshape(1, LANES)]
    if prev_rows:
        in_specs.append(pl.BlockSpec(memory_space=pl.ANY))
        args.append(xs_prev)
    return pl.pallas_call(
        functools.partial(_post_kernel, tile_base=prev_rows // (tl * TOP_K), prev_rows=prev_rows),
        grid=(b, tiles_per_row),
        in_specs=in_specs,
        out_specs=[tok(D_MODEL),
                   pl.BlockSpec(memory_space=pl.ANY),
                   slots, slots, slots,
                   pl.BlockSpec((1, N_EXPERTS, 1), lambda i, j: (i * tiles_per_row + j, 0, 0)),
                   full(N_EXPERTS, 1)],
        out_shape=[jax.ShapeDtypeStruct((b, l, D_MODEL), F32),
                   jax.ShapeDtypeStruct((sorted_rows, D_MODEL), BF16),
                   jax.ShapeDtypeStruct((b, 8, l), jnp.int32),
                   jax.ShapeDtypeStruct((b, 8, l), F32),
                   jax.ShapeDtypeStruct((b, 8, l), jnp.int32),
                   jax.ShapeDtypeStruct((b * tiles_per_row, N_EXPERTS, 1), F32),
                   jax.ShapeDtypeStruct((N_EXPERTS, 1), F32)],
        scratch_shapes=[pltpu.VMEM((N_EXPERTS, 1), F32),
                        pltpu.VMEM((2, tl * TOP_K, D_MODEL), BF16),
                        pltpu.VMEM((2, tl * TOP_K, D_MODEL), BF16),
                        pltpu.SemaphoreType.DMA((2,)),
                        pltpu.SemaphoreType.DMA((2, 2))],
        compiler_params=pltpu.CompilerParams(dimension_semantics=("arbitrary", "arbitrary"),
                                             vmem_limit_bytes=VMEM_LIMIT),
        name="outproj_norm2_router",
    )(*args)


PLAN_EXPERT, PLAN_FIRST, PLAN_NEXT, PLAN_SLOT = range(4)


def _moe_kernel(plan_ref, nu_ref, x_ref, wgu_hbm, bgu_ref, wd_hbm, bd_ref, o_ref,
                gu_f32, dn_f32, gu_s, dn_s, sem):
    i = pl.program_id(0)
    used = i < nu_ref[0]

    def weight_copies(expert, slot):
        return (pltpu.make_async_copy(wgu_hbm.at[expert], gu_f32.at[slot], sem.at[0, slot]),
                pltpu.make_async_copy(wd_hbm.at[expert], dn_f32.at[slot], sem.at[1, slot]))

    @pl.when(jnp.logical_and(used, plan_ref[PLAN_FIRST, i] == 1))
    def _():
        slot = plan_ref[PLAN_SLOT, i]
        nxt = plan_ref[PLAN_NEXT, i]

        @pl.when(i == 0)
        def _():
            for cp in weight_copies(plan_ref[PLAN_EXPERT, 0], plan_ref[PLAN_SLOT, 0]):
                cp.start()

        for cp in weight_copies(plan_ref[PLAN_EXPERT, i], slot):
            cp.wait()

        @pl.when(nxt >= 0)
        def _():
            for cp in weight_copies(nxt, 1 - slot):
                cp.start()

        for r in range(D_MODEL // 128):
            gu_s[r * 128:(r + 1) * 128, :] = gu_f32[slot, r * 128:(r + 1) * 128, :].astype(BF16)
            dn_s[r * 128:(r + 1) * 128, :] = dn_f32[slot, r * 128:(r + 1) * 128, :].astype(BF16)

    @pl.when(used)
    def _():
        x = x_ref[...]
        gate = _dot(x, gu_s[:, 0:D_FF]) + bgu_ref[0, :, 0:D_FF]
        up = _dot(x, gu_s[:, D_FF:2 * D_FF]) + bgu_ref[0, :, D_FF:2 * D_FF]
        gate = jnp.minimum(gate, SWIGLU_LIMIT)
        up = jnp.clip(up, -SWIGLU_LIMIT, SWIGLU_LIMIT)
        act = gate * _sigmoid(SWIGLU_ALPHA * gate) * (up + 1.0)
        o_ref[...] = (_dot(act.astype(BF16), dn_s[...]) + bd_ref[0]).astype(o_ref.dtype)

    @pl.when(jnp.logical_not(used))
    def _():
        o_ref[...] = jnp.zeros(o_ref.shape, o_ref.dtype)


def _moe_call(plan, n_used, xs, w_gate_up, b_gate_up, w_down, b_down):
    p = xs.shape[0]
    n_tiles = p // MOE_TM
    expert_of = lambda i, plan, nu: (plan[PLAN_EXPERT, i], 0, 0)
    grid_spec = pltpu.PrefetchScalarGridSpec(
        num_scalar_prefetch=2,
        grid=(n_tiles,),
        in_specs=[pl.BlockSpec((MOE_TM, D_MODEL), lambda i, plan, nu: (i, 0)),
                  pl.BlockSpec(memory_space=pl.ANY),
                  pl.BlockSpec((1, 1, 2 * D_FF), expert_of),
                  pl.BlockSpec(memory_space=pl.ANY),
                  pl.BlockSpec((1, 1, D_MODEL), expert_of)],
        out_specs=pl.BlockSpec((MOE_TM, D_MODEL), lambda i, plan, nu: (i, 0)),
        scratch_shapes=[pltpu.VMEM((2, D_MODEL, 2 * D_FF), F32), pltpu.VMEM((2, D_FF, D_MODEL), F32),
                        pltpu.VMEM((D_MODEL, 2 * D_FF), BF16), pltpu.VMEM((D_FF, D_MODEL), BF16),
                        pltpu.SemaphoreType.DMA((2, 2))],
    )
    return pl.pallas_call(
        _moe_kernel,
        grid_spec=grid_spec,
        out_shape=jax.ShapeDtypeStruct((p, D_MODEL), BF16),
        compiler_params=pltpu.CompilerParams(dimension_semantics=("arbitrary",),
                                             vmem_limit_bytes=VMEM_LIMIT),
        name="moe_experts",
    )(plan, n_used, xs, w_gate_up, b_gate_up.reshape(N_EXPERTS, 1, 2 * D_FF),
      w_down, b_down.reshape(N_EXPERTS, 1, D_MODEL))


def _combine_kernel(x1_ref, yg_ref, tw_ref, mod_ref, g_ref, o_ref):
    tw = tw_ref[0]
    acc = tw[:, 0:1] * yg_ref[0, 0].astype(F32)
    for k in range(1, TOP_K):
        acc = acc + tw[:, k:k + 1] * yg_ref[k, 0].astype(F32)
    g2 = mod_ref[0, :, 5 * D_MODEL:6 * D_MODEL]
    x2 = x1_ref[0] + g2 * acc
    ms = jnp.mean(x2 * x2, axis=-1, keepdims=True)
    o_ref[0] = x2 * lax.rsqrt(ms + RMS_EPS) * g_ref[...]


def _combine_call(x1, yg, top_w, mod, norm_g):
    b, l, _ = x1.shape
    tl = 256
    per_batch = mod.shape[0] > 1
    mod_map = (lambda i, j: (i, 0, 0)) if per_batch else (lambda i, j: (0, 0, 0))
    tok = lambda w: pl.BlockSpec((1, tl, w), lambda i, j: (i, j, 0))
    return pl.pallas_call(
        _combine_kernel,
        grid=(b, l // tl),
        in_specs=[tok(D_MODEL),
                  pl.BlockSpec((TOP_K, 1, tl, D_MODEL), lambda i, j: (0, i, j, 0)),
                  tok(TOP_K),
                  pl.BlockSpec((1, 1, 6 * D_MODEL), mod_map),
                  pl.BlockSpec((1, D_MODEL), lambda i, j: (0, 0))],
        out_specs=tok(D_MODEL),
        out_shape=jax.ShapeDtypeStruct((b, l, D_MODEL), F32),
        compiler_params=pltpu.CompilerParams(dimension_semantics=("arbitrary", "arbitrary"),
                                             vmem_limit_bytes=VMEM_LIMIT),
        name="moe_combine_final_norm",
    )(x1, yg, top_w, mod, norm_g.reshape(1, D_MODEL))


def _route(top_i, rank, counts, cnt_tiles):
    t = top_i.shape[0]
    p = t * TOP_K + N_EXPERTS * MOE_TM
    n_tiles = p // MOE_TM
    padded = ((counts + MOE_TM - 1) // MOE_TM) * MOE_TM
    ends = jnp.cumsum(padded)
    starts = ends - padded
    take = lambda table, idx: table.at[idx].get(mode="promise_in_bounds")
    dest = take(starts, top_i) + rank
    tile_start = jnp.arange(n_tiles, dtype=jnp.int32) * MOE_TM
    tile_expert = jnp.minimum(jnp.sum((ends[None, :] <= tile_start[:, None]).astype(jnp.int32), axis=1),
                              N_EXPERTS - 1)
    n_used = (ends[-1] // MOE_TM).reshape(1)
    tile_id = jnp.arange(n_tiles, dtype=jnp.int32)
    prev_expert = jnp.concatenate([jnp.full((1,), -1, tile_expert.dtype), tile_expert[:-1]])
    first = ((tile_expert != prev_expert) & (tile_id < n_used[0])).astype(jnp.int32)
    slot = (jnp.cumsum(first) - 1) % 2
    eid = jnp.arange(N_EXPERTS, dtype=jnp.int32)
    later = jnp.where((eid[None, :] > eid[:, None]) & (counts[None, :] > 0), eid[None, :], N_EXPERTS)
    next_of = jnp.min(later, axis=1)
    next_of = jnp.where(next_of < N_EXPERTS, next_of, -1)
    nxt = jnp.sum(jnp.where(tile_expert[:, None] == eid[None, :], next_of[None, :], 0), axis=1)
    plan = jnp.stack([tile_expert, first, nxt, slot]).astype(jnp.int32)
    tok_tiles = cnt_tiles.shape[0]
    block = (t * TOP_K) // tok_tiles
    excl = jnp.cumsum(cnt_tiles, axis=0) - cnt_tiles
    off = jnp.cumsum(cnt_tiles, axis=1) - cnt_tiles
    excl_e = take(excl.T, tile_expert)
    shift_e = take((off - excl).T, tile_expert)
    r = tile_start[:, None] + jnp.arange(MOE_TM, dtype=jnp.int32)[None, :] - take(starts, tile_expert)[:, None]
    ti = jnp.sum((excl_e[:, None, :] <= r[:, :, None]).astype(jnp.int32), axis=2) - 1
    here = jnp.arange(tok_tiles, dtype=jnp.int32)[None, None, :] == ti[:, :, None]
    shift_i = jnp.sum(jnp.where(here, shift_e[:, None, :], 0), axis=2)
    valid = r < take(counts, tile_expert)[:, None]
    spread = (jnp.arange(p, dtype=jnp.int32) % (t * TOP_K)).reshape(n_tiles, MOE_TM)
    src_row = jnp.where(valid, ti * block + r + shift_i, spread).reshape(p)
    return dest, src_row, plan, n_used.astype(jnp.int32)


def _layer_front(x, mod, s_ssd_f, s_ssd_b, s_s5_f, s_s5_b, grid_cols, prm, xs_prev):
    z, xbc, u, dt = _inproj_call(x, mod, prm["norm1_g"], prm["w_in_r"])
    y_ssd, n_ssd_f, n_ssd_b = _ssd_call(z, xbc, dt, s_ssd_f, s_ssd_b, prm["ssd"])
    y_f, n_s5_f = _s5_call(u, _s5_state_in(s_s5_f), prm["s5_f"], grid_cols, False)
    y_s5, n_s5_b = _s5_call(u, _s5_state_in(s_s5_b), prm["s5_b"], grid_cols, True,
                            y_first=y_f, glu=prm["glu"])
    x1, xs_buf, top_i, top_w, rank, cnt_tiles, cnt = _post_call(
        x, y_ssd, y_s5, mod, prm["w_out"], prm["norm2_g"], prm["w_router"], prm["b_router"], xs_prev)
    per_token = lambda a: jnp.swapaxes(a[:, :TOP_K, :], 1, 2)
    top_i, top_w, rank = per_token(top_i), per_token(top_w), per_token(rank)
    states = (n_ssd_f, n_ssd_b, _s5_state_out(n_s5_f), _s5_state_out(n_s5_b))
    return x1, (xs_buf, top_i, rank, cnt, cnt_tiles), top_w, states


def kernel(x_prompt, x_sample, state_ssd_fwd, state_ssd_bwd, state_s5_fwd, state_s5_bwd, c, c_ctx, w_ada, b_ada, norm1_g, w_in, ssd_conv_w, ssd_conv_b, ssd_dt_bias_fwd, ssd_dt_bias_bwd, ssd_a_log_fwd, ssd_a_log_bwd, ssd_d, ssd_norm_g, s5_a_re_fwd, s5_a_im_fwd, s5_log_dt_fwd, s5_b_re_fwd, s5_b_im_fwd, s5_c_re_fwd, s5_c_im_fwd, s5_a_re_bwd, s5_a_im_bwd, s5_log_dt_bwd, s5_b_re_bwd, s5_b_im_bwd, s5_c_re_bwd, s5_c_im_bwd, s5_d, w_glu, b_glu, w_out, norm2_g, w_router, b_router, w_gate_up, b_gate_up, w_down, b_down, norm_f_g):
    depth = w_ada.shape[0]
    assert depth == 1, "single trunk layer"
    nb_ctx, l_ctx, _ = x_prompt.shape
    nb_lat, l_lat, _ = x_sample.shape

    w_in0 = w_in[0]
    dt_lo = SSD_WIDTH + SSD_CONV_CH
    dt_cols = jnp.pad(w_in0[:, dt_lo:dt_lo + 2 * SSD_HEADS], ((0, 0), (0, DT_PAD - 2 * SSD_HEADS)))
    w_in_r = jnp.concatenate([w_in0[:, :dt_lo], w_in0[:, dt_lo + 2 * SSD_HEADS:], dt_cols], axis=1).astype(BF16)
    pad_dt = lambda f, b: jnp.pad(jnp.concatenate([f, b]).astype(F32), (0, DT_PAD - 2 * SSD_HEADS)).reshape(1, DT_PAD)
    ssd_prm = {
        "conv_w": jnp.pad(ssd_conv_w[0].astype(F32), ((0, 5), (0, 0))),
        "conv_b": ssd_conv_b[0].astype(F32).reshape(1, SSD_CONV_CH),
        "dt_bias": pad_dt(ssd_dt_bias_fwd[0], ssd_dt_bias_bwd[0]),
        "a": pad_dt(-jnp.exp(ssd_a_log_fwd[0].astype(F32)), -jnp.exp(ssd_a_log_bwd[0].astype(F32))),
        "d_skip": jnp.repeat(ssd_d[0].astype(F32), SSD_HEAD_DIM).reshape(1, SSD_WIDTH),
        "norm_g": ssd_norm_g[0].astype(F32).reshape(1, SSD_WIDTH),
        "expand": (jnp.arange(DT_PAD)[None, :, None]
                   == (jnp.arange(SSD_WIDTH) // SSD_HEAD_DIM)[None, None, :]
                   + SSD_HEADS * jnp.arange(2)[:, None, None]).astype(BF16),
    }
    prm = {
        "norm1_g": norm1_g[0], "w_in_r": w_in_r, "ssd": ssd_prm,
        "s5_f": _s5_params(s5_a_re_fwd[0], s5_a_im_fwd[0], s5_log_dt_fwd[0], s5_b_re_fwd[0],
                           s5_b_im_fwd[0], s5_c_re_fwd[0], s5_c_im_fwd[0]),
        "s5_b": _s5_params(s5_a_re_bwd[0], s5_a_im_bwd[0], s5_log_dt_bwd[0], s5_b_re_bwd[0],
                           s5_b_im_bwd[0], s5_c_re_bwd[0], s5_c_im_bwd[0]),
        "glu": {"d": s5_d[0].astype(F32).reshape(1, S5_WIDTH), "w": w_glu[0].astype(BF16),
                "b": b_glu[0].astype(F32).reshape(1, 2 * S5_WIDTH)},
        "w_out": w_out[0].astype(BF16), "norm2_g": norm2_g[0],
        "w_router": w_router[0].astype(F32), "b_router": b_router[0].astype(F32),
    }

    conds = jnp.concatenate([c_ctx[None, :], c], axis=0)
    conds = jnp.pad(conds, ((0, (-conds.shape[0]) % 8), (0, 0)))
    mod = _ada_call(conds, w_ada[0], b_ada[0])
    mod_ctx = mod[0:1].reshape(1, 1, 6 * D_MODEL)
    mod_lat = mod[1:1 + nb_lat].reshape(nb_lat, 1, 6 * D_MODEL)

    zero_ssd = jnp.zeros((nb_ctx, SSD_HEADS, SSD_HEAD_DIM, SSD_STATE), F32)
    zero_s5 = jnp.zeros((nb_ctx, S5_GROUPS, S5_STATE, 2), F32)
    t_ctx = nb_ctx * l_ctx
    x1_c, routed_c, tw_c, st_c = _layer_front(
        x_prompt, mod_ctx, zero_ssd, zero_ssd, zero_s5, zero_s5, False, prm, None)
    x1_l, routed_l, tw_l, _ = _layer_front(
        x_sample, mod_lat, state_ssd_fwd[:, 0], state_ssd_bwd[:, 0], state_s5_fwd[:, 0],
        state_s5_bwd[:, 0], True, prm, routed_c[0])

    _, ti_c, rk_c, cnt_c, ct_c = routed_c
    xs_tiles, ti_l, rk_l, cnt_l, ct_l = routed_l
    cnt_c = cnt_c.reshape(N_EXPERTS).astype(jnp.int32)
    cnt_l = cnt_l.reshape(N_EXPERTS).astype(jnp.int32)
    cnt_tiles = jnp.concatenate([ct_c, ct_l], axis=0).reshape(-1, N_EXPERTS).astype(jnp.int32)
    ti_l2 = ti_l.reshape(-1, TOP_K)
    top_i = jnp.concatenate([ti_c.reshape(-1, TOP_K), ti_l2], axis=0)
    rank = jnp.concatenate([rk_c.reshape(-1, TOP_K),
                            rk_l.reshape(-1, TOP_K) + cnt_c.at[ti_l2].get(mode="promise_in_bounds")], axis=0)
    dest, src_row, plan, n_used = _route(top_i, rank, cnt_c + cnt_l, cnt_tiles)
    xs = xs_tiles.at[src_row].get(mode="promise_in_bounds")
    ys = _moe_call(plan, n_used, xs, w_gate_up[0], b_gate_up[0], w_down[0], b_down[0])

    def picked_rows(d, nb, l):
        rows = ys.at[d.T.reshape(-1)].get(mode="promise_in_bounds")
        return rows.reshape(TOP_K, nb, l, D_MODEL)

    y_prompt = _combine_call(x1_c, picked_rows(dest[:t_ctx], nb_ctx, l_ctx), tw_c, mod_ctx, norm_f_g)
    y_sample = _combine_call(x1_l, picked_rows(dest[t_ctx:], nb_lat, l_lat), tw_l, mod_lat, norm_f_g)

    ssd_f, ssd_b, s5_f, s5_b = st_c
    return (y_prompt, y_sample, ssd_f[:, None], ssd_b[:, None], s5_f[:, None], s5_b[:, None])
```

```python
import functools
import math

import jax
import jax.numpy as jnp
from jax import lax
from jax.experimental import pallas as pl
from jax.experimental.pallas import tpu as pltpu

F32 = jnp.float32
BF16 = jnp.bfloat16

D_MODEL = 1024
GRID_W = 64
SSD_WIDTH = 512
SSD_HEAD_DIM = 64
SSD_HEADS = 8
SSD_GROUPS = 2
SSD_HEADS_PER_GROUP = SSD_HEADS // SSD_GROUPS
SSD_STATE = 128
SSD_CHUNK = 128
SSD_CONV_CH = SSD_WIDTH + 2 * SSD_GROUPS * SSD_STATE
S5_WIDTH = 512
S5_CH = 16
S5_GROUPS = 32
S5_STATE = 64
N_EXPERTS = 32
TOP_K = 4
D_FF = 1024
SWIGLU_LIMIT = 7.0
SWIGLU_ALPHA = 1.702
RMS_EPS = 1e-5

LANES = 128
DT_PAD = 128
S5_BATCH = 8
S5_COLS = 2 * S5_GROUPS * S5_STATE
S5_BLK = 512
S5_SUB_STEPS = 32
S5_SCAN_GROUP = 4
S5_BLOCK_STEPS = 256
MOE_TM = 256
VMEM_LIMIT = 56 * 1024 * 1024


def _sigmoid(x):
    return 1.0 / (1.0 + jnp.exp(-x))


def _split3(x):
    hi = x.astype(BF16)
    r1 = x - hi.astype(F32)
    mid = r1.astype(BF16)
    lo = (r1 - mid.astype(F32)).astype(BF16)
    return hi, mid, lo


def _split2(x):
    hi = x.astype(BF16)
    return hi, (x - hi.astype(F32)).astype(BF16)


def _dot(a, b):
    return jnp.dot(a, b, preferred_element_type=F32)


def _ada_kernel(c_ref, w_ref, b_ref, o_ref):
    c = c_ref[...]
    s = c * _sigmoid(c)
    o_ref[...] = _dot(s.astype(BF16), w_ref[...].astype(BF16)) + b_ref[...]


def _ada_call(conds, w_ada, b_ada):
    n = w_ada.shape[1]
    tn = 1536
    rows = conds.shape[0]
    return pl.pallas_call(
        _ada_kernel,
        grid=(n // tn,),
        in_specs=[pl.BlockSpec((rows, D_MODEL), lambda j: (0, 0)),
                  pl.BlockSpec((D_MODEL, tn), lambda j: (0, j)),
                  pl.BlockSpec((1, tn), lambda j: (0, j))],
        out_specs=pl.BlockSpec((rows, tn), lambda j: (0, j)),
        out_shape=jax.ShapeDtypeStruct((rows, n), F32),
        compiler_params=pltpu.CompilerParams(dimension_semantics=("arbitrary",),
                                             vmem_limit_bytes=VMEM_LIMIT),
        name="ada_mod",
    )(conds, w_ada, b_ada.reshape(1, n))


def _inproj_kernel(x_ref, mod_ref, g_ref, w_ref, z_ref, xbc_ref, u_ref, dt_ref):
    x = x_ref[0]
    ms = jnp.mean(x * x, axis=-1, keepdims=True)
    y = x * lax.rsqrt(ms + RMS_EPS) * g_ref[...]
    sh = mod_ref[0, :, 0:D_MODEL]
    sc = mod_ref[0, :, D_MODEL:2 * D_MODEL]
    h = (y * (1.0 + sc) + sh).astype(BF16)
    z_ref[0] = _dot(h, w_ref[:, 0:512]).astype(BF16)
    xbc_ref[0] = _dot(h, w_ref[:, 512:1536]).astype(BF16)
    u_ref[0] = _dot(h, w_ref[:, 1536:2048])
    dt_ref[0] = _dot(h, w_ref[:, 2048:2048 + DT_PAD])


def _inproj_call(x, mod, norm_g, w_in_r):
    b, l, _ = x.shape
    tl = min(l, 512)
    per_batch = mod.shape[0] > 1
    mod_map = (lambda i, j: (i, 0, 0)) if per_batch else (lambda i, j: (0, 0, 0))
    ncol = w_in_r.shape[1]
    tok = lambda w: pl.BlockSpec((1, tl, w), lambda i, j: (i, j, 0))
    return pl.pallas_call(
        _inproj_kernel,
        grid=(b, l // tl),
        in_specs=[tok(D_MODEL),
                  pl.BlockSpec((1, 1, 6 * D_MODEL), mod_map),
                  pl.BlockSpec((1, D_MODEL), lambda i, j: (0, 0)),
                  pl.BlockSpec((D_MODEL, ncol), lambda i, j: (0, 0))],
        out_specs=[tok(SSD_WIDTH), tok(SSD_CONV_CH), tok(S5_WIDTH), tok(DT_PAD)],
        out_shape=[jax.ShapeDtypeStruct((b, l, SSD_WIDTH), BF16),
                   jax.ShapeDtypeStruct((b, l, SSD_CONV_CH), BF16),
                   jax.ShapeDtypeStruct((b, l, S5_WIDTH), F32),
                   jax.ShapeDtypeStruct((b, l, DT_PAD), F32)],
        compiler_params=pltpu.CompilerParams(dimension_semantics=("arbitrary", "arbitrary"),
                                             vmem_limit_bytes=VMEM_LIMIT),
        name="norm1_inproj",
    )(x, mod, norm_g.reshape(1, D_MODEL), w_in_r)


def _softplus(x):
    return jnp.maximum(x, 0.0) + jnp.log1p(jnp.exp(-jnp.abs(x)))


def _ssd_kernel(z_ref, xbc_ref, dt_ref, s0f_ref, s0b_ref, cw_ref, cb_ref, dtb_ref, a_ref,
                dskip_ref, ng_ref, ex_ref, y_ref, sf_ref, sb_ref,
                xc_s, dts_s, yf_s, yb_s, stf_s, stb_s, *, seq):
    q = SSD_CHUNK
    nc = seq // q
    gw = SSD_HEADS_PER_GROUP * SSD_HEAD_DIM
    row_i = lax.broadcasted_iota(jnp.int32, (q, 1), 0)

    def conv_body(c, carry):
        r0 = pl.multiple_of(c * q, q)
        cur = xbc_ref[0, pl.ds(r0, q), :].astype(F32)
        p0 = pl.multiple_of(jnp.maximum(r0 - 16, 0), 16)
        n0 = pl.multiple_of(jnp.minimum(r0 + q, seq - 16), 16)
        prev_row = xbc_ref[0, pl.ds(p0, 16), :].astype(F32)[15:16, :]
        next_row = xbc_ref[0, pl.ds(n0, 16), :].astype(F32)[0:1, :]
        prev_row = jnp.where(c > 0, prev_row, 0.0)
        next_row = jnp.where(c < nc - 1, next_row, 0.0)
        down = jnp.where(row_i == 0, prev_row, pltpu.roll(cur, 1, axis=0))
        up = jnp.where(row_i == q - 1, next_row, pltpu.roll(cur, q - 1, axis=0))
        v = cw_ref[0:1, :] * down + cw_ref[1:2, :] * cur + cw_ref[2:3, :] * up + cb_ref[...]
        xc_s[pl.ds(r0, q), :] = v * _sigmoid(v)
        dts_s[pl.ds(r0, q), :] = _softplus(dt_ref[0, pl.ds(r0, q), :] + dtb_ref[...])
        return carry

    lax.fori_loop(0, nc, conv_body, 0)

    li = lax.broadcasted_iota(jnp.int32, (q, q), 0)
    si = lax.broadcasted_iota(jnp.int32, (q, q), 1)
    head_of_lane = lax.broadcasted_iota(jnp.int32, (q, gw), 1) // SSD_HEAD_DIM

    def expand(v, e_mat):
        hi, lo = _split2(v)
        return _dot(hi, e_mat) + _dot(lo, e_mat)

    def chunk(c, direction):
        r0 = pl.multiple_of(c * q, q)
        st_ref = stf_s if direction == 0 else stb_s
        y_dst = yf_s if direction == 0 else yb_s
        e_mat = ex_ref[direction]
        causal = (li >= si) if direction == 0 else (li <= si)
        tri = jnp.where(causal, 1.0, 0.0).astype(BF16)
        dts = dts_s[pl.ds(r0, q), :]
        dta = dts * a_ref[...]
        d_hi, d_mid, d_lo = _split3(dta)
        cs = _dot(tri, d_hi) + _dot(tri, d_mid) + _dot(tri, d_lo)
        tot = cs[q - 1:q, :] if direction == 0 else cs[0:1, :]
        cs_t = cs.T
        ecs = jnp.exp(cs)
        tail = jnp.exp(tot - cs)
        dt_x = expand(dts, e_mat)
        dtw_x = expand(dts * tail, e_mat)
        etot_x = expand(jnp.broadcast_to(jnp.exp(tot), (8, DT_PAD)), e_mat)[0:1, :]
        for g in range(SSD_GROUPS):
            b_off = SSD_WIDTH + g * SSD_STATE
            c_off = SSD_WIDTH + SSD_GROUPS * SSD_STATE + g * SSD_STATE
            x_g = xc_s[pl.ds(r0, q), g * gw:(g + 1) * gw]
            bm = xc_s[pl.ds(r0, q), b_off:b_off + SSD_STATE]
            cm = xc_s[pl.ds(r0, q), c_off:c_off + SSD_STATE]
            cb = lax.dot_general(cm.astype(BF16), bm.astype(BF16), (((1,), (1,)), ((), ())),
                                 preferred_element_type=F32)
            st_g = st_ref[g]
            xdt = x_g * dt_x[:, g * gw:(g + 1) * gw]
            lhs, rhs = [], []
            for hh in range(SSD_HEADS_PER_GROUP):
                col = g * SSD_HEADS_PER_GROUP + hh + SSD_HEADS * direction
                seg = jnp.broadcast_to(cs[:, col:col + 1], (q, q)) - cs_t[col:col + 1, :]
                dec = jnp.exp(jnp.where(causal, seg, -1e30))
                lhs.append((cb * dec).astype(BF16))
                lhs.append((cm * jnp.broadcast_to(ecs[:, col:col + 1], (q, q))).astype(BF16))
                own = head_of_lane == hh
                rhs.append(jnp.where(own, xdt, 0.0).astype(BF16))
                rhs.append(jnp.where(own, st_g, 0.0).astype(BF16))
            y_dst[pl.ds(r0, q), g * gw:(g + 1) * gw] = _dot(jnp.concatenate(lhs, axis=1),
                                                             jnp.concatenate(rhs, axis=0))
            xw = (x_g * dtw_x[:, g * gw:(g + 1) * gw]).astype(BF16)
            st_ref[g] = etot_x[:, g * gw:(g + 1) * gw] * st_g + _dot(bm.T.astype(BF16), xw)

    stf_s[...] = s0f_ref[0]
    stb_s[...] = s0b_ref[0]

    def both(i, carry):
        chunk(i, 0)
        chunk(nc - 1 - i, 1)
        return carry

    lax.fori_loop(0, nc, both, 0)
    sf_ref[0] = stf_s[...]
    sb_ref[0] = stb_s[...]

    def finish(c, carry):
        r0 = pl.multiple_of(c * q, q)
        xs = xc_s[pl.ds(r0, q), 0:SSD_WIDTH]
        zz = z_ref[0, pl.ds(r0, q), :].astype(F32)
        y = (yf_s[pl.ds(r0, q), :] + yb_s[pl.ds(r0, q), :] + dskip_ref[...] * xs) * (zz * _sigmoid(zz))
        ms = jnp.mean(y * y, axis=-1, keepdims=True)
        y_ref[0, pl.ds(r0, q), :] = (y * lax.rsqrt(ms + RMS_EPS) * ng_ref[...]).astype(y_ref.dtype)
        return carry

    lax.fori_loop(0, nc, finish, 0)


def _ssd_state_in(s):
    b = s.shape[0]
    s = s.astype(F32).reshape(b, SSD_GROUPS, SSD_HEADS_PER_GROUP, SSD_HEAD_DIM, SSD_STATE)
    return s.transpose(0, 1, 4, 2, 3).reshape(b, SSD_GROUPS, SSD_STATE, SSD_HEADS_PER_GROUP * SSD_HEAD_DIM)


def _ssd_state_out(s):
    b = s.shape[0]
    s = s.reshape(b, SSD_GROUPS, SSD_STATE, SSD_HEADS_PER_GROUP, SSD_HEAD_DIM)
    return s.transpose(0, 1, 3, 4, 2).reshape(b, SSD_HEADS, SSD_HEAD_DIM, SSD_STATE)


def _ssd_call(z, xbc, dt, s0f, s0b, prm):
    b, l, _ = z.shape
    st_shape = (SSD_GROUPS, SSD_STATE, SSD_HEADS_PER_GROUP * SSD_HEAD_DIM)
    tok = lambda w: pl.BlockSpec((1, l, w), lambda i: (i, 0, 0))
    st_spec = pl.BlockSpec((1,) + st_shape, lambda i: (i, 0, 0, 0))
    par = lambda r, w: pl.BlockSpec((r, w), lambda i: (0, 0))
    y, sf, sb = pl.pallas_call(
        functools.partial(_ssd_kernel, seq=l),
        grid=(b,),
        in_specs=[tok(SSD_WIDTH), tok(SSD_CONV_CH), tok(DT_PAD), st_spec, st_spec,
                  par(8, SSD_CONV_CH), par(1, SSD_CONV_CH), par(1, DT_PAD), par(1, DT_PAD),
                  par(1, SSD_WIDTH), par(1, SSD_WIDTH),
                  pl.BlockSpec((2, DT_PAD, SSD_WIDTH), lambda i: (0, 0, 0))],
        out_specs=[tok(SSD_WIDTH), st_spec, st_spec],
        out_shape=[jax.ShapeDtypeStruct((b, l, SSD_WIDTH), BF16),
                   jax.ShapeDtypeStruct((b,) + st_shape, F32),
                   jax.ShapeDtypeStruct((b,) + st_shape, F32)],
        scratch_shapes=[pltpu.VMEM((l, SSD_CONV_CH), F32),
                        pltpu.VMEM((l, DT_PAD), F32),
                        pltpu.VMEM((l, SSD_WIDTH), F32),
                        pltpu.VMEM((l, SSD_WIDTH), F32),
                        pltpu.VMEM(st_shape, F32),
                        pltpu.VMEM(st_shape, F32)],
        compiler_params=pltpu.CompilerParams(dimension_semantics=("arbitrary",),
                                             vmem_limit_bytes=VMEM_LIMIT),
        name="ssd_mixer",
    )(z, xbc, dt, _ssd_state_in(s0f), _ssd_state_in(s0b),
      prm["conv_w"], prm["conv_b"], prm["dt_bias"], prm["a"], prm["d_skip"], prm["norm_g"], prm["expand"])
    return y, _ssd_state_out(sf), _ssd_state_out(sb)


def _gelu_tanh(x):
    return 0.5 * x * (1.0 + jnp.tanh(math.sqrt(2.0 / math.pi) * (x + 0.044715 * (x * x * x))))


def _s5_kernel(*refs, grid_cols, reverse, final):
    if final:
        (u_ref, s0_ref, lr_ref, li_ref, wb_ref, wc_ref, yf_ref, d_ref, wg_ref, bg_ref,
         o_ref, sfin_ref, ur_s, yr_s, st_s) = refs
    else:
        (u_ref, s0_ref, lr_ref, li_ref, wb_ref, wc_ref,
         o_ref, sfin_ref, ur_s, yr_s, st_s) = refs
    nb = S5_BATCH
    steps = S5_BLOCK_STEPS
    rows = steps * nb
    sub_rows = S5_SUB_STEPS * nb
    n_sub = steps // S5_SUB_STEPS
    j = pl.program_id(1)

    @pl.when(j == 0)
    def _():
        st_s[...] = s0_ref[0]

    def to_scan(v):
        v = jnp.transpose(v, (2, 1, 0, 3)) if grid_cols else jnp.swapaxes(v, 0, 1)
        return v.reshape(rows, S5_WIDTH)

    def from_scan(v):
        if grid_cols:
            return jnp.transpose(v.reshape(u_ref.shape[2], u_ref.shape[1], nb, S5_WIDTH), (2, 1, 0, 3))
        return jnp.swapaxes(v.reshape(steps, nb, S5_WIDTH), 0, 1)

    ur_s[...] = to_scan(u_ref[...])

    order = list(range(S5_SUB_STEPS))[::-1] if reverse else list(range(S5_SUB_STEPS))
    blk_per_half = S5_COLS // S5_BLK // 2

    def by_groups(v, perm):
        return jnp.concatenate([v[g * nb:(g + 1) * nb] for g in perm], axis=0)

    def sub_chunk(i, carry):
        sidx = (n_sub - 1 - i) if reverse else i
        r0 = pl.multiple_of(sidx * sub_rows, sub_rows)
        for hf in range(2):
            u_half = ur_s[pl.ds(r0, sub_rows), hf * 256:(hf + 1) * 256]
            if reverse:
                u_half = by_groups(u_half, order)
            uh = u_half.astype(BF16)
            grp = S5_SCAN_GROUP
            for bp in range(blk_per_half // grp):
                blk = hf * blk_per_half + grp * bp
                c0 = blk * S5_BLK
                wide = grp * S5_BLK
                q = S5_BLK // 2
                lam_r = jnp.broadcast_to(lr_ref[:, c0:c0 + wide], (nb, wide))
                lam_i = jnp.broadcast_to(li_ref[:, c0:c0 + wide], (nb, wide))
                bu = _dot(uh, wb_ref[hf, :, grp * bp * S5_BLK:grp * (bp + 1) * S5_BLK])
                s = st_s[:, c0:c0 + wide]
                states = []
                for k in range(S5_SUB_STEPS):
                    sw = jnp.concatenate([s[:, (2 * g + 1 - h) * q:(2 * g + 2 - h) * q]
                                          for g in range(grp) for h in range(2)], axis=1)
                    s = lam_r * s + lam_i * sw + bu[k * nb:(k + 1) * nb, :]
                    states.append(s)
                st_s[:, c0:c0 + wide] = s
                traj = jnp.concatenate(states, axis=0).astype(BF16)
                for g in range(grp):
                    y_blk = _dot(traj[:, g * S5_BLK:(g + 1) * S5_BLK], wc_ref[blk + g])
                    if reverse:
                        y_blk = by_groups(y_blk, order)
                    yr_s[pl.ds(r0, sub_rows), (blk + g) * 64:(blk + g + 1) * 64] = y_blk
        return carry

    lax.fori_loop(0, n_sub, sub_chunk, 0)

    @pl.when(j == pl.num_programs(1) - 1)
    def _():
        sfin_ref[0] = st_s[...]

    if not final:
        o_ref[0] = yr_s[...]
        return

    def glu_chunk(i, carry):
        r0 = pl.multiple_of(i * sub_rows, sub_rows)
        y = (yr_s[pl.ds(r0, sub_rows), :] + yf_ref[0, pl.ds(r0, sub_rows), :]
             + d_ref[...] * ur_s[pl.ds(r0, sub_rows), :])
        y = _gelu_tanh(y).astype(BF16)
        val = _dot(y, wg_ref[:, 0:S5_WIDTH]) + bg_ref[:, 0:S5_WIDTH]
        gate = _dot(y, wg_ref[:, S5_WIDTH:2 * S5_WIDTH]) + bg_ref[:, S5_WIDTH:2 * S5_WIDTH]
        yr_s[pl.ds(r0, sub_rows), :] = val * _sigmoid(gate)
        return carry

    lax.fori_loop(0, n_sub, glu_chunk, 0)

    o_ref[...] = from_scan(yr_s[...])


def _s5_call(u, s0, prm, grid_cols, reverse, y_first=None, glu=None):
    b, l, _ = u.shape
    nbg = b // S5_BATCH
    nblk = l // S5_BLOCK_STEPS
    rows = S5_BLOCK_STEPS * S5_BATCH
    final = y_first is not None
    blk_of = (lambda j: nblk - 1 - j) if reverse else (lambda j: j)
    if grid_cols:
        n_rows = l // GRID_W
        u_in = u.reshape(b, n_rows, GRID_W, S5_WIDTH)
        cols_per_blk = S5_BLOCK_STEPS // n_rows
        tok_spec = pl.BlockSpec((S5_BATCH, n_rows, cols_per_blk, S5_WIDTH),
                                lambda i, j: (i, 0, blk_of(j), 0))
        out_nat = jax.ShapeDtypeStruct((b, n_rows, GRID_W, S5_WIDTH), F32)
    else:
        u_in = u
        tok_spec = pl.BlockSpec((S5_BATCH, S5_BLOCK_STEPS, S5_WIDTH), lambda i, j: (i, blk_of(j), 0))
        out_nat = jax.ShapeDtypeStruct((b, l, S5_WIDTH), F32)
    scan_spec = pl.BlockSpec((1, rows, S5_WIDTH), lambda i, j: (i, blk_of(j), 0))
    st_spec = pl.BlockSpec((1, S5_BATCH, S5_COLS), lambda i, j: (i, 0, 0))
    full = lambda shape: pl.BlockSpec(shape, lambda i, j: (0,) * len(shape))
    in_specs = [tok_spec, st_spec, full((1, S5_COLS)), full((1, S5_COLS)),
                full((2, 256, 2048)), full((S5_COLS // S5_BLK, S5_BLK, 64))]
    args = [u_in, s0, prm["lam_r"], prm["lam_i"], prm["wb"], prm["wc"]]
    if final:
        in_specs += [scan_spec, full((1, S5_WIDTH)), full((S5_WIDTH, 2 * S5_WIDTH)), full((1, 2 * S5_WIDTH))]
        args += [y_first, glu["d"], glu["w"], glu["b"]]
        out_specs = [tok_spec, st_spec]
        out_shape = [out_nat, jax.ShapeDtypeStruct((nbg, S5_BATCH, S5_COLS), F32)]
    else:
        out_specs = [scan_spec, st_spec]
        out_shape = [jax.ShapeDtypeStruct((nbg, l * S5_BATCH, S5_WIDTH), F32),
                     jax.ShapeDtypeStruct((nbg, S5_BATCH, S5_COLS), F32)]
    y, sfin = pl.pallas_call(
        functools.partial(_s5_kernel, grid_cols=grid_cols, reverse=reverse, final=final),
        grid=(nbg, nblk),
        in_specs=in_specs,
        out_specs=out_specs,
        out_shape=out_shape,
        scratch_shapes=[pltpu.VMEM((rows, S5_WIDTH), F32),
                        pltpu.VMEM((rows, S5_WIDTH), F32),
                        pltpu.VMEM((S5_BATCH, S5_COLS), F32)],
        compiler_params=pltpu.CompilerParams(dimension_semantics=("arbitrary", "arbitrary"),
                                             vmem_limit_bytes=VMEM_LIMIT),
        name="s5_final" if final else "s5_first",
    )(*args)
    if final:
        y = y.reshape(b, l, S5_WIDTH)
    return y, sfin


def _s5_cols(t):
    lead = t.shape[:-2]
    return t.reshape(lead + (S5_COLS // S5_BLK, S5_BLK // 2))


def _s5_state_in(s0):
    b = s0.shape[0]
    re, im = _s5_cols(s0[..., 0].astype(F32)), _s5_cols(s0[..., 1].astype(F32))
    return jnp.stack([re, im], axis=-2).reshape(b // S5_BATCH, S5_BATCH, S5_COLS)


def _s5_state_out(s):
    b = s.shape[0] * S5_BATCH
    s = s.reshape(b, S5_COLS // S5_BLK, 2, S5_BLK // 2)
    re = s[:, :, 0].reshape(b, S5_GROUPS, S5_STATE)
    im = s[:, :, 1].reshape(b, S5_GROUPS, S5_STATE)
    return jnp.stack([re, im], axis=-1)


def _s5_params(a_re, a_im, log_dt, b_re, b_im, c_re, c_im):
    a_re, a_im = a_re.astype(F32), a_im.astype(F32)
    delta = jnp.exp(log_dt.astype(F32))[:, None]
    mag = jnp.exp(a_re * delta)
    lam_re, lam_im = mag * jnp.cos(a_im * delta), mag * jnp.sin(a_im * delta)
    den = a_re * a_re + a_im * a_im
    nr = lam_re - 1.0
    f_re = (nr * a_re + lam_im * a_im) / den
    f_im = (lam_im * a_re - nr * a_im) / den
    b_re, b_im = b_re.astype(F32), b_im.astype(F32)
    bb_re = f_re[..., None] * b_re - f_im[..., None] * b_im
    bb_im = f_re[..., None] * b_im + f_im[..., None] * b_re
    nblk = S5_COLS // S5_BLK
    lr = _s5_cols(lam_re)
    li = _s5_cols(lam_im)
    lam_r = jnp.stack([lr, lr], axis=-2).reshape(1, S5_COLS)
    lam_i = jnp.stack([-li, li], axis=-2).reshape(1, S5_COLS)
    gq = S5_BLK // 2 // S5_STATE
    eye = jnp.eye(gq, dtype=F32)

    def in_block(bb):
        t = bb.reshape(nblk, gq, S5_STATE, S5_CH)
        return jnp.einsum("gh,kgpc->kgchp", eye, t).reshape(nblk, gq * S5_CH, gq * S5_STATE)

    blk_in = jnp.concatenate([in_block(bb_re), in_block(bb_im)], axis=-1)
    half = nblk // 2
    eye_h = jnp.eye(half, dtype=F32)
    wb = jnp.einsum("jk,hkcn->hjckn", eye_h, blk_in.reshape(2, half, gq * S5_CH, S5_BLK))
    wb = wb.reshape(2, half * gq * S5_CH, half * S5_BLK).astype(BF16)

    def out_block(cc):
        t = cc.astype(F32).reshape(nblk, gq, S5_CH, S5_STATE)
        return jnp.einsum("gh,kgcp->kgphc", eye, t).reshape(nblk, gq * S5_STATE, gq * S5_CH)

    wc = jnp.concatenate([out_block(c_re), -out_block(c_im)], axis=1).astype(BF16)
    return {"lam_r": lam_r, "lam_i": lam_i, "wb": wb, "wc": wc}


def _post_kernel(*refs, tile_base, prev_rows):
    if prev_rows:
        prev_hbm, refs = refs[8], refs[:8] + refs[9:]
    (x_ref, ys_ref, y5_ref, mod_ref, wo_ref, g_ref, wr_ref, br_ref,
     x1_ref, xs_hbm, ti_ref, tw_ref, rk_ref, ct_ref, cnt_ref, cnt_s, xs_v, pv_v, sem_out, sem_prev) = refs
    tl = x_ref.shape[1]
    blk = tl * TOP_K
    n_steps = pl.num_programs(0) * pl.num_programs(1)
    step = pl.program_id(0) * pl.num_programs(1) + pl.program_id(1)
    buf = step % 2
    first = step == 0

    def block_copy(at_step, at_slot):
        row0 = pl.multiple_of((tile_base + at_step) * blk, blk)
        return pltpu.make_async_copy(xs_v.at[at_slot], xs_hbm.at[pl.ds(row0, blk)], sem_out.at[at_slot])

    @pl.when(step >= 2)
    def _():
        block_copy(step - 2, buf).wait()

    prev_blocks = prev_rows // blk

    def prev_in(at_step):
        row0 = pl.multiple_of(at_step * blk, blk)
        return pltpu.make_async_copy(prev_hbm.at[pl.ds(row0, blk)], pv_v.at[at_step % 2], sem_prev.at[0, at_step % 2])

    def prev_out(at_step):
        row0 = pl.multiple_of(at_step * blk, blk)
        return pltpu.make_async_copy(pv_v.at[at_step % 2], xs_hbm.at[pl.ds(row0, blk)], sem_prev.at[1, at_step % 2])

    if prev_rows:
        @pl.when(jnp.logical_and(step >= 2, step - 2 < prev_blocks))
        def _():
            prev_out(step - 2).wait()

        @pl.when(step < prev_blocks)
        def _():
            prev_in(step).start()

        @pl.when(jnp.logical_and(step >= 1, step - 1 < prev_blocks))
        def _():
            prev_in(step - 1).wait()
            prev_out(step - 1).start()

    @pl.when(first)
    def _():
        cnt_s[...] = jnp.zeros(cnt_s.shape, cnt_s.dtype)

    mixed = (_dot(ys_ref[0], wo_ref[0:SSD_WIDTH, :])
             + _dot(y5_ref[0].astype(BF16), wo_ref[SSD_WIDTH:SSD_WIDTH + S5_WIDTH, :]))
    g1 = mod_ref[0, :, 2 * D_MODEL:3 * D_MODEL]
    sh2 = mod_ref[0, :, 3 * D_MODEL:4 * D_MODEL]
    sc2 = mod_ref[0, :, 4 * D_MODEL:5 * D_MODEL]
    x1 = x_ref[0] + g1 * mixed
    x1_ref[0] = x1
    ms = jnp.mean(x1 * x1, axis=-1, keepdims=True)
    h = x1 * lax.rsqrt(ms + RMS_EPS) * g_ref[...] * (1.0 + sc2) + sh2

    h_hi, h_mid, _ = _split3(h)
    wr = wr_ref[...]
    w_hi, w_mid, _ = _split3(wr)
    logits = _dot(h_hi, w_hi) + _dot(h_hi, w_mid) + _dot(h_mid, w_hi) + br_ref[...]

    work = logits.T[0:N_EXPERTS, :]
    eid = lax.broadcasted_iota(jnp.int32, (N_EXPERTS, tl), 0).astype(F32)
    row8 = lax.broadcasted_iota(jnp.int32, (8, tl), 0)
    onehot = jnp.zeros((N_EXPERTS, tl), F32)
    sels, vals = [], []
    top_i = jnp.zeros((8, tl), jnp.int32)
    for k in range(TOP_K):
        m = jnp.max(work, axis=0, keepdims=True)
        idx = jnp.min(jnp.where(work == m, eid, float(N_EXPERTS)), axis=0, keepdims=True)
        sel = eid == idx
        sels.append(sel)
        vals.append(m)
        top_i = jnp.where(row8 == k, idx.astype(jnp.int32), top_i)
        onehot = jnp.where(sel, 1.0, onehot)
        work = jnp.where(sel, -jnp.inf, work)
    es = [jnp.exp(v - vals[0]) for v in vals]
    den = es[0] + es[1] + es[2] + es[3]
    top_w = jnp.zeros((8, tl), F32)
    for k in range(TOP_K):
        top_w = jnp.where(row8 == k, es[k] / den, top_w)

    ri = lax.broadcasted_iota(jnp.int32, (tl, tl), 0)
    ci = lax.broadcasted_iota(jnp.int32, (tl, tl), 1)
    earlier = jnp.where(ri < ci, 1.0, 0.0).astype(BF16)
    local = _dot(onehot.astype(BF16), earlier)
    before = local + cnt_s[...]
    cnt_tile = jnp.sum(onehot, axis=1, keepdims=True)
    er = lax.broadcasted_iota(jnp.int32, (N_EXPERTS, N_EXPERTS), 0)
    ec = lax.broadcasted_iota(jnp.int32, (N_EXPERTS, N_EXPERTS), 1)
    lower = jnp.where(ec < er, 1.0, 0.0).astype(BF16)
    off = _dot(lower, jnp.broadcast_to(cnt_tile, (N_EXPERTS, LANES)).astype(BF16))[:, 0:1]
    pos = off + local
    rank = jnp.zeros((8, tl), jnp.int32)
    row_id = lax.broadcasted_iota(jnp.int32, (tl * TOP_K, tl), 0).astype(F32)
    perm = jnp.zeros((tl * TOP_K, tl), F32)
    for k in range(TOP_K):
        rk = jnp.sum(jnp.where(sels[k], before, 0.0), axis=0, keepdims=True)
        rank = jnp.where(row8 == k, rk.astype(jnp.int32), rank)
        pk = jnp.sum(jnp.where(sels[k], pos, 0.0), axis=0, keepdims=True)
        perm = jnp.where(row_id == pk, 1.0, perm)
    xs_v[buf] = _dot(perm.astype(BF16), h.astype(BF16)).astype(BF16)
    block_copy(step, buf).start()
    cnt_s[...] = cnt_s[...] + cnt_tile
    ti_ref[0] = top_i
    tw_ref[0] = top_w
    rk_ref[0] = rank
    ct_ref[0] = cnt_tile
    cnt_ref[...] = cnt_s[...]

    @pl.when(step == n_steps - 1)
    def _():
        block_copy(step, buf).wait()

        @pl.when(step >= 1)
        def _():
            block_copy(step - 1, 1 - buf).wait()


def _post_call(x, y_ssd, y_s5, mod, w_out, norm_g, w_router, b_router, xs_prev=None):
    b, l, _ = x.shape
    tl = 256
    tiles_per_row = l // tl
    prev_rows = 0 if xs_prev is None else xs_prev.shape[0]
    assert prev_rows // (tl * TOP_K) + 2 <= b * tiles_per_row, "not enough grid steps to pass the earlier rows through"
    sorted_rows = prev_rows + b * l * TOP_K
    per_batch = mod.shape[0] > 1
    mod_map = (lambda i, j: (i, 0, 0)) if per_batch else (lambda i, j: (0, 0, 0))
    tok = lambda w: pl.BlockSpec((1, tl, w), lambda i, j: (i, j, 0))
    full = lambda r, w: pl.BlockSpec((r, w), lambda i, j: (0, 0))
    slots = pl.BlockSpec((1, 8, tl), lambda i, j: (i, 0, j))
    in_specs = [tok(D_MODEL), tok(SSD_WIDTH), tok(S5_WIDTH),
                pl.BlockSpec((1, 1, 6 * D_MODEL), mod_map),
                full(D_MODEL, D_MODEL), full(1, D_MODEL), full(D_MODEL, LANES),
                full(1, LANES)]
    w_router = jnp.pad(w_router, ((0, 0), (0, LANES - N_EXPERTS)))
    b_router = jnp.pad(b_router, (0, LANES - N_EXPERTS), constant_values=-1e30)
    args = [x, y_ssd, y_s5, mod, w_out, norm_g.reshape(1, D_MODEL), w_router, b_router.reshape(1, LANES)]
    if prev_rows:
        in_specs.append(pl.BlockSpec(memory_space=pl.ANY))
        args.append(xs_prev)
    return pl.pallas_call(
        functools.partial(_post_kernel, tile_base=prev_rows // (tl * TOP_K), prev_rows=prev_rows),
        grid=(b, tiles_per_row),
        in_specs=in_specs,
        out_specs=[tok(D_MODEL),
                   pl.BlockSpec(memory_space=pl.ANY),
                   slots, slots, slots,
                   pl.BlockSpec((1, N_EXPERTS, 1), lambda i, j: (i * tiles_per_row + j, 0, 0)),
                   full(N_EXPERTS, 1)],
        out_shape=[jax.ShapeDtypeStruct((b, l, D_MODEL), F32),
                   jax.ShapeDtypeStruct((sorted_rows, D_MODEL), BF16),
                   jax.ShapeDtypeStruct((b, 8, l), jnp.int32),
                   jax.ShapeDtypeStruct((b, 8, l), F32),
                   jax.ShapeDtypeStruct((b, 8, l), jnp.int32),
                   jax.ShapeDtypeStruct((b * tiles_per_row, N_EXPERTS, 1), F32),
                   jax.ShapeDtypeStruct((N_EXPERTS, 1), F32)],
        scratch_shapes=[pltpu.VMEM((N_EXPERTS, 1), F32),
                        pltpu.VMEM((2, tl * TOP_K, D_MODEL), BF16),
                        pltpu.VMEM((2, tl * TOP_K, D_MODEL), BF16),
                        pltpu.SemaphoreType.DMA((2,)),
                        pltpu.SemaphoreType.DMA((2, 2))],
        compiler_params=pltpu.CompilerParams(dimension_semantics=("arbitrary", "arbitrary"),
                                             vmem_limit_bytes=VMEM_LIMIT),
        name="outproj_norm2_router",
    )(*args)


PLAN_EXPERT, PLAN_FIRST, PLAN_NEXT, PLAN_SLOT = range(4)


def _moe_kernel(plan_ref, nu_ref, x_ref, wgu_hbm, bgu_ref, wd_hbm, bd_ref, o_ref,
                gu_f32, dn_f32, gu_s, dn_s, sem):
    i = pl.program_id(0)
    used = i < nu_ref[0]

    def weight_copies(expert, slot):
        return (pltpu.make_async_copy(wgu_hbm.at[expert], gu_f32.at[slot], sem.at[0, slot]),
                pltpu.make_async_copy(wd_hbm.at[expert], dn_f32.at[slot], sem.at[1, slot]))

    @pl.when(jnp.logical_and(used, plan_ref[PLAN_FIRST, i] == 1))
    def _():
        slot = plan_ref[PLAN_SLOT, i]
        nxt = plan_ref[PLAN_NEXT, i]

        @pl.when(i == 0)
        def _():
            for cp in weight_copies(plan_ref[PLAN_EXPERT, 0], plan_ref[PLAN_SLOT, 0]):
                cp.start()

        for cp in weight_copies(plan_ref[PLAN_EXPERT, i], slot):
            cp.wait()

        @pl.when(nxt >= 0)
        def _():
            for cp in weight_copies(nxt, 1 - slot):
                cp.start()

        for r in range(D_MODEL // 128):
            gu_s[r * 128:(r + 1) * 128, :] = gu_f32[slot, r * 128:(r + 1) * 128, :].astype(BF16)
            dn_s[r * 128:(r + 1) * 128, :] = dn_f32[slot, r * 128:(r + 1) * 128, :].astype(BF16)

    @pl.when(used)
    def _():
        x = x_ref[...]
        gate = _dot(x, gu_s[:, 0:D_FF]) + bgu_ref[0, :, 0:D_FF]
        up = _dot(x, gu_s[:, D_FF:2 * D_FF]) + bgu_ref[0, :, D_FF:2 * D_FF]
        gate = jnp.minimum(gate, SWIGLU_LIMIT)
        up = jnp.clip(up, -SWIGLU_LIMIT, SWIGLU_LIMIT)
        act = gate * _sigmoid(SWIGLU_ALPHA * gate) * (up + 1.0)
        o_ref[...] = (_dot(act.astype(BF16), dn_s[...]) + bd_ref[0]).astype(o_ref.dtype)

    @pl.when(jnp.logical_not(used))
    def _():
        o_ref[...] = jnp.zeros(o_ref.shape, o_ref.dtype)


def _moe_call(plan, n_used, xs, w_gate_up, b_gate_up, w_down, b_down):
    p = xs.shape[0]
    n_tiles = p // MOE_TM
    expert_of = lambda i, plan, nu: (plan[PLAN_EXPERT, i], 0, 0)
    grid_spec = pltpu.PrefetchScalarGridSpec(
        num_scalar_prefetch=2,
        grid=(n_tiles,),
        in_specs=[pl.BlockSpec((MOE_TM, D_MODEL), lambda i, plan, nu: (i, 0)),
                  pl.BlockSpec(memory_space=pl.ANY),
                  pl.BlockSpec((1, 1, 2 * D_FF), expert_of),
                  pl.BlockSpec(memory_space=pl.ANY),
                  pl.BlockSpec((1, 1, D_MODEL), expert_of)],
        out_specs=pl.BlockSpec((MOE_TM, D_MODEL), lambda i, plan, nu: (i, 0)),
        scratch_shapes=[pltpu.VMEM((2, D_MODEL, 2 * D_FF), F32), pltpu.VMEM((2, D_FF, D_MODEL), F32),
                        pltpu.VMEM((D_MODEL, 2 * D_FF), BF16), pltpu.VMEM((D_FF, D_MODEL), BF16),
                        pltpu.SemaphoreType.DMA((2, 2))],
    )
    return pl.pallas_call(
        _moe_kernel,
        grid_spec=grid_spec,
        out_shape=jax.ShapeDtypeStruct((p, D_MODEL), BF16),
        compiler_params=pltpu.CompilerParams(dimension_semantics=("arbitrary",),
                                             vmem_limit_bytes=VMEM_LIMIT),
        name="moe_experts",
    )(plan, n_used, xs, w_gate_up, b_gate_up.reshape(N_EXPERTS, 1, 2 * D_FF),
      w_down, b_down.reshape(N_EXPERTS, 1, D_MODEL))


def _combine_kernel(x1_ref, yg_ref, tw_ref, mod_ref, g_ref, o_ref):
    tw = tw_ref[0]
    acc = tw[:, 0:1] * yg_ref[0, 0].astype(F32)
    for k in range(1, TOP_K):
        acc = acc + tw[:, k:k + 1] * yg_ref[k, 0].astype(F32)
    g2 = mod_ref[0, :, 5 * D_MODEL:6 * D_MODEL]
    x2 = x1_ref[0] + g2 * acc
    ms = jnp.mean(x2 * x2, axis=-1, keepdims=True)
    o_ref[0] = x2 * lax.rsqrt(ms + RMS_EPS) * g_ref[...]


def _combine_call(x1, yg, top_w, mod, norm_g):
    b, l, _ = x1.shape
    tl = 256
    per_batch = mod.shape[0] > 1
    mod_map = (lambda i, j: (i, 0, 0)) if per_batch else (lambda i, j: (0, 0, 0))
    tok = lambda w: pl.BlockSpec((1, tl, w), lambda i, j: (i, j, 0))
    return pl.pallas_call(
        _combine_kernel,
        grid=(b, l // tl),
        in_specs=[tok(D_MODEL),
                  pl.BlockSpec((TOP_K, 1, tl, D_MODEL), lambda i, j: (0, i, j, 0)),
                  tok(TOP_K),
                  pl.BlockSpec((1, 1, 6 * D_MODEL), mod_map),
                  pl.BlockSpec((1, D_MODEL), lambda i, j: (0, 0))],
        out_specs=tok(D_MODEL),
        out_shape=jax.ShapeDtypeStruct((b, l, D_MODEL), F32),
        compiler_params=pltpu.CompilerParams(dimension_semantics=("arbitrary", "arbitrary"),
                                             vmem_limit_bytes=VMEM_LIMIT),
        name="moe_combine_final_norm",
    )(x1, yg, top_w, mod, norm_g.reshape(1, D_MODEL))


def _route(top_i, rank, counts, cnt_tiles):
    t = top_i.shape[1]
    p = t * TOP_K + N_EXPERTS * MOE_TM
    n_tiles = p // MOE_TM
    padded = ((counts + MOE_TM - 1) // MOE_TM) * MOE_TM
    ends = jnp.cumsum(padded)
    starts = ends - padded
    take = lambda table, idx: table.at[idx].get(mode="promise_in_bounds")
    dest = take(starts, top_i) + rank
    tile_start = jnp.arange(n_tiles, dtype=jnp.int32) * MOE_TM
    tile_expert = jnp.minimum(jnp.sum((ends[None, :] <= tile_start[:, None]).astype(jnp.int32), axis=1),
                              N_EXPERTS - 1)
    n_used = (ends[-1] // MOE_TM).reshape(1)
    tile_id = jnp.arange(n_tiles, dtype=jnp.int32)
    prev_expert = jnp.concatenate([jnp.full((1,), -1, tile_expert.dtype), tile_expert[:-1]])
    first = ((tile_expert != prev_expert) & (tile_id < n_used[0])).astype(jnp.int32)
    slot = (jnp.cumsum(first) - 1) % 2
    eid = jnp.arange(N_EXPERTS, dtype=jnp.int32)
    later = jnp.where((eid[None, :] > eid[:, None]) & (counts[None, :] > 0), eid[None, :], N_EXPERTS)
    next_of = jnp.min(later, axis=1)
    next_of = jnp.where(next_of < N_EXPERTS, next_of, -1)
    nxt = jnp.sum(jnp.where(tile_expert[:, None] == eid[None, :], next_of[None, :], 0), axis=1)
    plan = jnp.stack([tile_expert, first, nxt, slot]).astype(jnp.int32)
    tok_tiles = cnt_tiles.shape[0]
    block = (t * TOP_K) // tok_tiles
    excl = jnp.cumsum(cnt_tiles, axis=0) - cnt_tiles
    off = jnp.cumsum(cnt_tiles, axis=1) - cnt_tiles
    excl_e = take(excl.T, tile_expert)
    shift_e = take((off - excl).T, tile_expert)
    r = tile_start[:, None] + jnp.arange(MOE_TM, dtype=jnp.int32)[None, :] - take(starts, tile_expert)[:, None]
    ti = jnp.sum((excl_e[:, None, :] <= r[:, :, None]).astype(jnp.int32), axis=2) - 1
    here = jnp.arange(tok_tiles, dtype=jnp.int32)[None, None, :] == ti[:, :, None]
    shift_i = jnp.sum(jnp.where(here, shift_e[:, None, :], 0), axis=2)
    valid = r < take(counts, tile_expert)[:, None]
    spread = (jnp.arange(p, dtype=jnp.int32) % (t * TOP_K)).reshape(n_tiles, MOE_TM)
    src_row = jnp.where(valid, ti * block + r + shift_i, spread).reshape(p)
    return dest, src_row, plan, n_used.astype(jnp.int32)


def _layer_front(x, mod, s_ssd_f, s_ssd_b, s_s5_f, s_s5_b, grid_cols, prm, xs_prev):
    z, xbc, u, dt = _inproj_call(x, mod, prm["norm1_g"], prm["w_in_r"])
    y_ssd, n_ssd_f, n_ssd_b = _ssd_call(z, xbc, dt, s_ssd_f, s_ssd_b, prm["ssd"])
    y_f, n_s5_f = _s5_call(u, _s5_state_in(s_s5_f), prm["s5_f"], grid_cols, False)
    y_s5, n_s5_b = _s5_call(u, _s5_state_in(s_s5_b), prm["s5_b"], grid_cols, True,
                            y_first=y_f, glu=prm["glu"])
    x1, xs_buf, top_i, top_w, rank, cnt_tiles, cnt = _post_call(
        x, y_ssd, y_s5, mod, prm["w_out"], prm["norm2_g"], prm["w_router"], prm["b_router"], xs_prev)
    top_w = jnp.swapaxes(top_w[:, :TOP_K, :], 1, 2)
    by_slot = lambda a: jnp.swapaxes(a[:, :TOP_K, :], 0, 1).reshape(TOP_K, -1)
    top_i, rank = by_slot(top_i), by_slot(rank)
    states = (n_ssd_f, n_ssd_b, _s5_state_out(n_s5_f), _s5_state_out(n_s5_b))
    return x1, (xs_buf, top_i, rank, cnt, cnt_tiles), top_w, states


def kernel(x_prompt, x_sample, state_ssd_fwd, state_ssd_bwd, state_s5_fwd, state_s5_bwd, c, c_ctx, w_ada, b_ada, norm1_g, w_in, ssd_conv_w, ssd_conv_b, ssd_dt_bias_fwd, ssd_dt_bias_bwd, ssd_a_log_fwd, ssd_a_log_bwd, ssd_d, ssd_norm_g, s5_a_re_fwd, s5_a_im_fwd, s5_log_dt_fwd, s5_b_re_fwd, s5_b_im_fwd, s5_c_re_fwd, s5_c_im_fwd, s5_a_re_bwd, s5_a_im_bwd, s5_log_dt_bwd, s5_b_re_bwd, s5_b_im_bwd, s5_c_re_bwd, s5_c_im_bwd, s5_d, w_glu, b_glu, w_out, norm2_g, w_router, b_router, w_gate_up, b_gate_up, w_down, b_down, norm_f_g):
    depth = w_ada.shape[0]
    assert depth == 1, "single trunk layer"
    nb_ctx, l_ctx, _ = x_prompt.shape
    nb_lat, l_lat, _ = x_sample.shape

    w_in0 = w_in[0]
    dt_lo = SSD_WIDTH + SSD_CONV_CH
    dt_cols = jnp.pad(w_in0[:, dt_lo:dt_lo + 2 * SSD_HEADS], ((0, 0), (0, DT_PAD - 2 * SSD_HEADS)))
    w_in_r = jnp.concatenate([w_in0[:, :dt_lo], w_in0[:, dt_lo + 2 * SSD_HEADS:], dt_cols], axis=1).astype(BF16)
    pad_dt = lambda f, b: jnp.pad(jnp.concatenate([f, b]).astype(F32), (0, DT_PAD - 2 * SSD_HEADS)).reshape(1, DT_PAD)
    ssd_prm = {
        "conv_w": jnp.pad(ssd_conv_w[0].astype(F32), ((0, 5), (0, 0))),
        "conv_b": ssd_conv_b[0].astype(F32).reshape(1, SSD_CONV_CH),
        "dt_bias": pad_dt(ssd_dt_bias_fwd[0], ssd_dt_bias_bwd[0]),
        "a": pad_dt(-jnp.exp(ssd_a_log_fwd[0].astype(F32)), -jnp.exp(ssd_a_log_bwd[0].astype(F32))),
        "d_skip": jnp.repeat(ssd_d[0].astype(F32), SSD_HEAD_DIM).reshape(1, SSD_WIDTH),
        "norm_g": ssd_norm_g[0].astype(F32).reshape(1, SSD_WIDTH),
        "expand": (jnp.arange(DT_PAD)[None, :, None]
                   == (jnp.arange(SSD_WIDTH) // SSD_HEAD_DIM)[None, None, :]
                   + SSD_HEADS * jnp.arange(2)[:, None, None]).astype(BF16),
    }
    prm = {
        "norm1_g": norm1_g[0], "w_in_r": w_in_r, "ssd": ssd_prm,
        "s5_f": _s5_params(s5_a_re_fwd[0], s5_a_im_fwd[0], s5_log_dt_fwd[0], s5_b_re_fwd[0],
                           s5_b_im_fwd[0], s5_c_re_fwd[0], s5_c_im_fwd[0]),
        "s5_b": _s5_params(s5_a_re_bwd[0], s5_a_im_bwd[0], s5_log_dt_bwd[0], s5_b_re_bwd[0],
                           s5_b_im_bwd[0], s5_c_re_bwd[0], s5_c_im_bwd[0]),
        "glu": {"d": s5_d[0].astype(F32).reshape(1, S5_WIDTH), "w": w_glu[0].astype(BF16),
                "b": b_glu[0].astype(F32).reshape(1, 2 * S5_WIDTH)},
        "w_out": w_out[0].astype(BF16), "norm2_g": norm2_g[0],
        "w_router": w_router[0].astype(F32), "b_router": b_router[0].astype(F32),
    }

    conds = jnp.concatenate([c_ctx[None, :], c], axis=0)
    conds = jnp.pad(conds, ((0, (-conds.shape[0]) % 8), (0, 0)))
    mod = _ada_call(conds, w_ada[0], b_ada[0])
    mod_ctx = mod[0:1].reshape(1, 1, 6 * D_MODEL)
    mod_lat = mod[1:1 + nb_lat].reshape(nb_lat, 1, 6 * D_MODEL)

    zero_ssd = jnp.zeros((nb_ctx, SSD_HEADS, SSD_HEAD_DIM, SSD_STATE), F32)
    zero_s5 = jnp.zeros((nb_ctx, S5_GROUPS, S5_STATE, 2), F32)
    t_ctx = nb_ctx * l_ctx
    x1_c, routed_c, tw_c, st_c = _layer_front(
        x_prompt, mod_ctx, zero_ssd, zero_ssd, zero_s5, zero_s5, False, prm, None)
    x1_l, routed_l, tw_l, _ = _layer_front(
        x_sample, mod_lat, state_ssd_fwd[:, 0], state_ssd_bwd[:, 0], state_s5_fwd[:, 0],
        state_s5_bwd[:, 0], True, prm, routed_c[0])

    _, ti_c, rk_c, cnt_c, ct_c = routed_c
    xs_tiles, ti_l, rk_l, cnt_l, ct_l = routed_l
    cnt_c = cnt_c.reshape(N_EXPERTS).astype(jnp.int32)
    cnt_l = cnt_l.reshape(N_EXPERTS).astype(jnp.int32)
    cnt_tiles = jnp.concatenate([ct_c, ct_l], axis=0).reshape(-1, N_EXPERTS).astype(jnp.int32)
    top_i = jnp.concatenate([ti_c, ti_l], axis=1)
    rank = jnp.concatenate([rk_c, rk_l + cnt_c.at[ti_l].get(mode="promise_in_bounds")], axis=1)
    dest, src_row, plan, n_used = _route(top_i, rank, cnt_c + cnt_l, cnt_tiles)
    xs = xs_tiles.at[src_row].get(mode="promise_in_bounds")
    ys = _moe_call(plan, n_used, xs, w_gate_up[0], b_gate_up[0], w_down[0], b_down[0])

    def picked_rows(d, nb, l):
        rows = ys.at[d.reshape(-1)].get(mode="promise_in_bounds")
        return rows.reshape(TOP_K, nb, l, D_MODEL)

    y_prompt = _combine_call(x1_c, picked_rows(dest[:, :t_ctx], nb_ctx, l_ctx), tw_c, mod_ctx, norm_f_g)
    y_sample = _combine_call(x1_l, picked_rows(dest[:, t_ctx:], nb_lat, l_lat), tw_l, mod_lat, norm_f_g)

    ssd_f, ssd_b, s5_f, s5_b = st_c
    return (y_prompt, y_sample, ssd_f[:, None], ssd_b[:, None], s5_f[:, None], s5_b[:, None])
```

```python
import functools
import math

import jax
import jax.numpy as jnp
from jax import lax
from jax.experimental import pallas as pl
from jax.experimental.pallas import tpu as pltpu

F32 = jnp.float32
BF16 = jnp.bfloat16

D_MODEL = 1024
GRID_W = 64
SSD_WIDTH = 512
SSD_HEAD_DIM = 64
SSD_HEADS = 8
SSD_GROUPS = 2
SSD_HEADS_PER_GROUP = SSD_HEADS // SSD_GROUPS
SSD_STATE = 128
SSD_CHUNK = 128
SSD_CONV_CH = SSD_WIDTH + 2 * SSD_GROUPS * SSD_STATE
S5_WIDTH = 512
S5_CH = 16
S5_GROUPS = 32
S5_STATE = 64
N_EXPERTS = 32
TOP_K = 4
D_FF = 1024
SWIGLU_LIMIT = 7.0
SWIGLU_ALPHA = 1.702
RMS_EPS = 1e-5

LANES = 128
DT_PAD = 128
S5_BATCH = 8
S5_COLS = 2 * S5_GROUPS * S5_STATE
S5_BLK = 512
S5_SUB_STEPS = 32
S5_SCAN_GROUP = 4
S5_BLOCK_STEPS = 256
MOE_TM = 256
VMEM_LIMIT = 56 * 1024 * 1024


def _sigmoid(x):
    return 1.0 / (1.0 + jnp.exp(-x))


def _split3(x):
    hi = x.astype(BF16)
    r1 = x - hi.astype(F32)
    mid = r1.astype(BF16)
    lo = (r1 - mid.astype(F32)).astype(BF16)
    return hi, mid, lo


def _split2(x):
    hi = x.astype(BF16)
    return hi, (x - hi.astype(F32)).astype(BF16)


def _dot(a, b):
    return jnp.dot(a, b, preferred_element_type=F32)


def _ada_kernel(c_ref, w_ref, b_ref, o_ref):
    c = c_ref[...]
    s = c * _sigmoid(c)
    o_ref[...] = _dot(s.astype(BF16), w_ref[...].astype(BF16)) + b_ref[...]


def _ada_call(conds, w_ada, b_ada):
    n = w_ada.shape[1]
    tn = 1536
    rows = conds.shape[0]
    return pl.pallas_call(
        _ada_kernel,
        grid=(n // tn,),
        in_specs=[pl.BlockSpec((rows, D_MODEL), lambda j: (0, 0)),
                  pl.BlockSpec((D_MODEL, tn), lambda j: (0, j)),
                  pl.BlockSpec((1, tn), lambda j: (0, j))],
        out_specs=pl.BlockSpec((rows, tn), lambda j: (0, j)),
        out_shape=jax.ShapeDtypeStruct((rows, n), F32),
        compiler_params=pltpu.CompilerParams(dimension_semantics=("arbitrary",),
                                             vmem_limit_bytes=VMEM_LIMIT),
        name="ada_mod",
    )(conds, w_ada, b_ada.reshape(1, n))


def _inproj_kernel(x_ref, mod_ref, g_ref, w_ref, z_ref, xbc_ref, u_ref, dt_ref):
    x = x_ref[0]
    ms = jnp.mean(x * x, axis=-1, keepdims=True)
    y = x * lax.rsqrt(ms + RMS_EPS) * g_ref[...]
    sh = mod_ref[0, :, 0:D_MODEL]
    sc = mod_ref[0, :, D_MODEL:2 * D_MODEL]
    h = (y * (1.0 + sc) + sh).astype(BF16)
    z_ref[0] = _dot(h, w_ref[:, 0:512]).astype(BF16)
    xbc_ref[0] = _dot(h, w_ref[:, 512:1536]).astype(BF16)
    u_ref[0] = _dot(h, w_ref[:, 1536:2048])
    dt_ref[0] = _dot(h, w_ref[:, 2048:2048 + DT_PAD])


def _inproj_call(x, mod, norm_g, w_in_r):
    b, l, _ = x.shape
    tl = min(l, 512)
    per_batch = mod.shape[0] > 1
    mod_map = (lambda i, j: (i, 0, 0)) if per_batch else (lambda i, j: (0, 0, 0))
    ncol = w_in_r.shape[1]
    tok = lambda w: pl.BlockSpec((1, tl, w), lambda i, j: (i, j, 0))
    return pl.pallas_call(
        _inproj_kernel,
        grid=(b, l // tl),
        in_specs=[tok(D_MODEL),
                  pl.BlockSpec((1, 1, 6 * D_MODEL), mod_map),
                  pl.BlockSpec((1, D_MODEL), lambda i, j: (0, 0)),
                  pl.BlockSpec((D_MODEL, ncol), lambda i, j: (0, 0))],
        out_specs=[tok(SSD_WIDTH), tok(SSD_CONV_CH), tok(S5_WIDTH), tok(DT_PAD)],
        out_shape=[jax.ShapeDtypeStruct((b, l, SSD_WIDTH), BF16),
                   jax.ShapeDtypeStruct((b, l, SSD_CONV_CH), BF16),
                   jax.ShapeDtypeStruct((b, l, S5_WIDTH), F32),
                   jax.ShapeDtypeStruct((b, l, DT_PAD), F32)],
        compiler_params=pltpu.CompilerParams(dimension_semantics=("arbitrary", "arbitrary"),
                                             vmem_limit_bytes=VMEM_LIMIT),
        name="norm1_inproj",
    )(x, mod, norm_g.reshape(1, D_MODEL), w_in_r)


def _softplus(x):
    return jnp.maximum(x, 0.0) + jnp.log1p(jnp.exp(-jnp.abs(x)))


def _ssd_kernel(z_ref, xbc_ref, dt_ref, s0f_ref, s0b_ref, cw_ref, cb_ref, dtb_ref, a_ref,
                dskip_ref, ng_ref, ex_ref, y_ref, sf_ref, sb_ref,
                xc_s, dts_s, yf_s, yb_s, stf_s, stb_s, *, seq):
    q = SSD_CHUNK
    nc = seq // q
    gw = SSD_HEADS_PER_GROUP * SSD_HEAD_DIM
    row_i = lax.broadcasted_iota(jnp.int32, (q, 1), 0)

    def conv_body(c, carry):
        r0 = pl.multiple_of(c * q, q)
        cur = xbc_ref[0, pl.ds(r0, q), :].astype(F32)
        p0 = pl.multiple_of(jnp.maximum(r0 - 16, 0), 16)
        n0 = pl.multiple_of(jnp.minimum(r0 + q, seq - 16), 16)
        prev_row = xbc_ref[0, pl.ds(p0, 16), :].astype(F32)[15:16, :]
        next_row = xbc_ref[0, pl.ds(n0, 16), :].astype(F32)[0:1, :]
        prev_row = jnp.where(c > 0, prev_row, 0.0)
        next_row = jnp.where(c < nc - 1, next_row, 0.0)
        down = jnp.where(row_i == 0, prev_row, pltpu.roll(cur, 1, axis=0))
        up = jnp.where(row_i == q - 1, next_row, pltpu.roll(cur, q - 1, axis=0))
        v = cw_ref[0:1, :] * down + cw_ref[1:2, :] * cur + cw_ref[2:3, :] * up + cb_ref[...]
        xc_s[pl.ds(r0, q), :] = v * _sigmoid(v)
        dts_s[pl.ds(r0, q), :] = _softplus(dt_ref[0, pl.ds(r0, q), :] + dtb_ref[...])
        return carry

    lax.fori_loop(0, nc, conv_body, 0)

    li = lax.broadcasted_iota(jnp.int32, (q, q), 0)
    si = lax.broadcasted_iota(jnp.int32, (q, q), 1)
    head_of_lane = lax.broadcasted_iota(jnp.int32, (q, gw), 1) // SSD_HEAD_DIM

    def expand(v, e_mat):
        hi, lo = _split2(v)
        return _dot(hi, e_mat) + _dot(lo, e_mat)

    def chunk(c, direction):
        r0 = pl.multiple_of(c * q, q)
        st_ref = stf_s if direction == 0 else stb_s
        y_dst = yf_s if direction == 0 else yb_s
        e_mat = ex_ref[direction]
        causal = (li >= si) if direction == 0 else (li <= si)
        tri = jnp.where(causal, 1.0, 0.0).astype(BF16)
        dts = dts_s[pl.ds(r0, q), :]
        dta = dts * a_ref[...]
        d_hi, d_mid, d_lo = _split3(dta)
        cs = _dot(tri, d_hi) + _dot(tri, d_mid) + _dot(tri, d_lo)
        tot = cs[q - 1:q, :] if direction == 0 else cs[0:1, :]
        cs_t = cs.T
        ecs = jnp.exp(cs)
        tail = jnp.exp(tot - cs)
        dt_x = _dot(dts.astype(BF16), e_mat)
        dtw_x = _dot((dts * tail).astype(BF16), e_mat)
        etot_x = expand(jnp.broadcast_to(jnp.exp(tot), (8, DT_PAD)), e_mat)[0:1, :]
        for g in range(SSD_GROUPS):
            b_off = SSD_WIDTH + g * SSD_STATE
            c_off = SSD_WIDTH + SSD_GROUPS * SSD_STATE + g * SSD_STATE
            x_g = xc_s[pl.ds(r0, q), g * gw:(g + 1) * gw]
            bm = xc_s[pl.ds(r0, q), b_off:b_off + SSD_STATE]
            cm = xc_s[pl.ds(r0, q), c_off:c_off + SSD_STATE]
            cb = lax.dot_general(cm.astype(BF16), bm.astype(BF16), (((1,), (1,)), ((), ())),
                                 preferred_element_type=F32)
            st_g = st_ref[g]
            xdt = x_g * dt_x[:, g * gw:(g + 1) * gw]
            lhs, rhs = [], []
            for hh in range(SSD_HEADS_PER_GROUP):
                col = g * SSD_HEADS_PER_GROUP + hh + SSD_HEADS * direction
                seg = jnp.broadcast_to(cs[:, col:col + 1], (q, q)) - cs_t[col:col + 1, :]
                dec = jnp.exp(jnp.where(causal, seg, -1e30))
                lhs.append((cb * dec).astype(BF16))
                lhs.append((cm * jnp.broadcast_to(ecs[:, col:col + 1], (q, q))).astype(BF16))
                own = head_of_lane == hh
                rhs.append(jnp.where(own, xdt, 0.0).astype(BF16))
                rhs.append(jnp.where(own, st_g, 0.0).astype(BF16))
            y_dst[pl.ds(r0, q), g * gw:(g + 1) * gw] = _dot(jnp.concatenate(lhs, axis=1),
                                                             jnp.concatenate(rhs, axis=0))
            xw = (x_g * dtw_x[:, g * gw:(g + 1) * gw]).astype(BF16)
            st_ref[g] = etot_x[:, g * gw:(g + 1) * gw] * st_g + _dot(bm.T.astype(BF16), xw)

    stf_s[...] = s0f_ref[0]
    stb_s[...] = s0b_ref[0]

    def both(i, carry):
        chunk(i, 0)
        chunk(nc - 1 - i, 1)
        return carry

    lax.fori_loop(0, nc, both, 0)
    sf_ref[0] = stf_s[...]
    sb_ref[0] = stb_s[...]

    def finish(c, carry):
        r0 = pl.multiple_of(c * q, q)
        xs = xc_s[pl.ds(r0, q), 0:SSD_WIDTH]
        zz = z_ref[0, pl.ds(r0, q), :].astype(F32)
        y = (yf_s[pl.ds(r0, q), :] + yb_s[pl.ds(r0, q), :] + dskip_ref[...] * xs) * (zz * _sigmoid(zz))
        ms = jnp.mean(y * y, axis=-1, keepdims=True)
        y_ref[0, pl.ds(r0, q), :] = (y * lax.rsqrt(ms + RMS_EPS) * ng_ref[...]).astype(y_ref.dtype)
        return carry

    lax.fori_loop(0, nc, finish, 0)


def _ssd_state_in(s):
    b = s.shape[0]
    s = s.astype(F32).reshape(b, SSD_GROUPS, SSD_HEADS_PER_GROUP, SSD_HEAD_DIM, SSD_STATE)
    return s.transpose(0, 1, 4, 2, 3).reshape(b, SSD_GROUPS, SSD_STATE, SSD_HEADS_PER_GROUP * SSD_HEAD_DIM)


def _ssd_state_out(s):
    b = s.shape[0]
    s = s.reshape(b, SSD_GROUPS, SSD_STATE, SSD_HEADS_PER_GROUP, SSD_HEAD_DIM)
    return s.transpose(0, 1, 3, 4, 2).reshape(b, SSD_HEADS, SSD_HEAD_DIM, SSD_STATE)


def _ssd_call(z, xbc, dt, s0f, s0b, prm):
    b, l, _ = z.shape
    st_shape = (SSD_GROUPS, SSD_STATE, SSD_HEADS_PER_GROUP * SSD_HEAD_DIM)
    tok = lambda w: pl.BlockSpec((1, l, w), lambda i: (i, 0, 0))
    st_spec = pl.BlockSpec((1,) + st_shape, lambda i: (i, 0, 0, 0))
    par = lambda r, w: pl.BlockSpec((r, w), lambda i: (0, 0))
    y, sf, sb = pl.pallas_call(
        functools.partial(_ssd_kernel, seq=l),
        grid=(b,),
        in_specs=[tok(SSD_WIDTH), tok(SSD_CONV_CH), tok(DT_PAD), st_spec, st_spec,
                  par(8, SSD_CONV_CH), par(1, SSD_CONV_CH), par(1, DT_PAD), par(1, DT_PAD),
                  par(1, SSD_WIDTH), par(1, SSD_WIDTH),
                  pl.BlockSpec((2, DT_PAD, SSD_WIDTH), lambda i: (0, 0, 0))],
        out_specs=[tok(SSD_WIDTH), st_spec, st_spec],
        out_shape=[jax.ShapeDtypeStruct((b, l, SSD_WIDTH), BF16),
                   jax.ShapeDtypeStruct((b,) + st_shape, F32),
                   jax.ShapeDtypeStruct((b,) + st_shape, F32)],
        scratch_shapes=[pltpu.VMEM((l, SSD_CONV_CH), F32),
                        pltpu.VMEM((l, DT_PAD), F32),
                        pltpu.VMEM((l, SSD_WIDTH), F32),
                        pltpu.VMEM((l, SSD_WIDTH), F32),
                        pltpu.VMEM(st_shape, F32),
                        pltpu.VMEM(st_shape, F32)],
        compiler_params=pltpu.CompilerParams(dimension_semantics=("arbitrary",),
                                             vmem_limit_bytes=VMEM_LIMIT),
        name="ssd_mixer",
    )(z, xbc, dt, _ssd_state_in(s0f), _ssd_state_in(s0b),
      prm["conv_w"], prm["conv_b"], prm["dt_bias"], prm["a"], prm["d_skip"], prm["norm_g"], prm["expand"])
    return y, _ssd_state_out(sf), _ssd_state_out(sb)


def _gelu_tanh(x):
    return 0.5 * x * (1.0 + jnp.tanh(math.sqrt(2.0 / math.pi) * (x + 0.044715 * (x * x * x))))


def _s5_kernel(*refs, grid_cols, reverse, final):
    if final:
        (u_ref, s0_ref, lr_ref, li_ref, wb_ref, wc_ref, yf_ref, d_ref, wg_ref, bg_ref,
         o_ref, sfin_ref, ur_s, yr_s, st_s) = refs
    else:
        (u_ref, s0_ref, lr_ref, li_ref, wb_ref, wc_ref,
         o_ref, sfin_ref, ur_s, yr_s, st_s) = refs
    nb = S5_BATCH
    steps = S5_BLOCK_STEPS
    rows = steps * nb
    sub_rows = S5_SUB_STEPS * nb
    n_sub = steps // S5_SUB_STEPS
    j = pl.program_id(1)

    @pl.when(j == 0)
    def _():
        st_s[...] = s0_ref[0]

    def to_scan(v):
        v = jnp.transpose(v, (2, 1, 0, 3)) if grid_cols else jnp.swapaxes(v, 0, 1)
        return v.reshape(rows, S5_WIDTH)

    def from_scan(v):
        if grid_cols:
            return jnp.transpose(v.reshape(u_ref.shape[2], u_ref.shape[1], nb, S5_WIDTH), (2, 1, 0, 3))
        return jnp.swapaxes(v.reshape(steps, nb, S5_WIDTH), 0, 1)

    ur_s[...] = to_scan(u_ref[...])

    order = list(range(S5_SUB_STEPS))[::-1] if reverse else list(range(S5_SUB_STEPS))
    blk_per_half = S5_COLS // S5_BLK // 2

    def by_groups(v, perm):
        return jnp.concatenate([v[g * nb:(g + 1) * nb] for g in perm], axis=0)

    def sub_chunk(i, carry):
        sidx = (n_sub - 1 - i) if reverse else i
        r0 = pl.multiple_of(sidx * sub_rows, sub_rows)
        for hf in range(2):
            u_half = ur_s[pl.ds(r0, sub_rows), hf * 256:(hf + 1) * 256]
            if reverse:
                u_half = by_groups(u_half, order)
            uh = u_half.astype(BF16)
            grp = S5_SCAN_GROUP
            for bp in range(blk_per_half // grp):
                blk = hf * blk_per_half + grp * bp
                c0 = blk * S5_BLK
                wide = grp * S5_BLK
                q = S5_BLK // 2
                lam_r = jnp.broadcast_to(lr_ref[:, c0:c0 + wide], (nb, wide))
                lam_i = jnp.broadcast_to(li_ref[:, c0:c0 + wide], (nb, wide))
                bu = _dot(uh, wb_ref[hf, :, grp * bp * S5_BLK:grp * (bp + 1) * S5_BLK])
                s = st_s[:, c0:c0 + wide]
                states = []
                for k in range(S5_SUB_STEPS):
                    sw = jnp.concatenate([s[:, (2 * g + 1 - h) * q:(2 * g + 2 - h) * q]
                                          for g in range(grp) for h in range(2)], axis=1)
                    s = lam_r * s + lam_i * sw + bu[k * nb:(k + 1) * nb, :]
                    states.append(s)
                st_s[:, c0:c0 + wide] = s
                traj = jnp.concatenate(states, axis=0).astype(BF16)
                for g in range(grp):
                    y_blk = _dot(traj[:, g * S5_BLK:(g + 1) * S5_BLK], wc_ref[blk + g])
                    if reverse:
                        y_blk = by_groups(y_blk, order)
                    yr_s[pl.ds(r0, sub_rows), (blk + g) * 64:(blk + g + 1) * 64] = y_blk
        return carry

    lax.fori_loop(0, n_sub, sub_chunk, 0)

    @pl.when(j == pl.num_programs(1) - 1)
    def _():
        sfin_ref[0] = st_s[...]

    if not final:
        o_ref[0] = yr_s[...]
        return

    def glu_chunk(i, carry):
        r0 = pl.multiple_of(i * sub_rows, sub_rows)
        y = (yr_s[pl.ds(r0, sub_rows), :] + yf_ref[0, pl.ds(r0, sub_rows), :]
             + d_ref[...] * ur_s[pl.ds(r0, sub_rows), :])
        y = _gelu_tanh(y).astype(BF16)
        val = _dot(y, wg_ref[:, 0:S5_WIDTH]) + bg_ref[:, 0:S5_WIDTH]
        gate = _dot(y, wg_ref[:, S5_WIDTH:2 * S5_WIDTH]) + bg_ref[:, S5_WIDTH:2 * S5_WIDTH]
        yr_s[pl.ds(r0, sub_rows), :] = val * _sigmoid(gate)
        return carry

    lax.fori_loop(0, n_sub, glu_chunk, 0)

    o_ref[...] = from_scan(yr_s[...])


def _s5_call(u, s0, prm, grid_cols, reverse, y_first=None, glu=None):
    b, l, _ = u.shape
    nbg = b // S5_BATCH
    nblk = l // S5_BLOCK_STEPS
    rows = S5_BLOCK_STEPS * S5_BATCH
    final = y_first is not None
    blk_of = (lambda j: nblk - 1 - j) if reverse else (lambda j: j)
    if grid_cols:
        n_rows = l // GRID_W
        u_in = u.reshape(b, n_rows, GRID_W, S5_WIDTH)
        cols_per_blk = S5_BLOCK_STEPS // n_rows
        tok_spec = pl.BlockSpec((S5_BATCH, n_rows, cols_per_blk, S5_WIDTH),
                                lambda i, j: (i, 0, blk_of(j), 0))
        out_nat = jax.ShapeDtypeStruct((b, n_rows, GRID_W, S5_WIDTH), F32)
    else:
        u_in = u
        tok_spec = pl.BlockSpec((S5_BATCH, S5_BLOCK_STEPS, S5_WIDTH), lambda i, j: (i, blk_of(j), 0))
        out_nat = jax.ShapeDtypeStruct((b, l, S5_WIDTH), F32)
    scan_spec = pl.BlockSpec((1, rows, S5_WIDTH), lambda i, j: (i, blk_of(j), 0))
    st_spec = pl.BlockSpec((1, S5_BATCH, S5_COLS), lambda i, j: (i, 0, 0))
    full = lambda shape: pl.BlockSpec(shape, lambda i, j: (0,) * len(shape))
    in_specs = [tok_spec, st_spec, full((1, S5_COLS)), full((1, S5_COLS)),
                full((2, 256, 2048)), full((S5_COLS // S5_BLK, S5_BLK, 64))]
    args = [u_in, s0, prm["lam_r"], prm["lam_i"], prm["wb"], prm["wc"]]
    if final:
        in_specs += [scan_spec, full((1, S5_WIDTH)), full((S5_WIDTH, 2 * S5_WIDTH)), full((1, 2 * S5_WIDTH))]
        args += [y_first, glu["d"], glu["w"], glu["b"]]
        out_specs = [tok_spec, st_spec]
        out_shape = [out_nat, jax.ShapeDtypeStruct((nbg, S5_BATCH, S5_COLS), F32)]
    else:
        out_specs = [scan_spec, st_spec]
        out_shape = [jax.ShapeDtypeStruct((nbg, l * S5_BATCH, S5_WIDTH), F32),
                     jax.ShapeDtypeStruct((nbg, S5_BATCH, S5_COLS), F32)]
    y, sfin = pl.pallas_call(
        functools.partial(_s5_kernel, grid_cols=grid_cols, reverse=reverse, final=final),
        grid=(nbg, nblk),
        in_specs=in_specs,
        out_specs=out_specs,
        out_shape=out_shape,
        scratch_shapes=[pltpu.VMEM((rows, S5_WIDTH), F32),
                        pltpu.VMEM((rows, S5_WIDTH), F32),
                        pltpu.VMEM((S5_BATCH, S5_COLS), F32)],
        compiler_params=pltpu.CompilerParams(dimension_semantics=("arbitrary", "arbitrary"),
                                             vmem_limit_bytes=VMEM_LIMIT),
        name="s5_final" if final else "s5_first",
    )(*args)
    if final:
        y = y.reshape(b, l, S5_WIDTH)
    return y, sfin


def _s5_cols(t):
    lead = t.shape[:-2]
    return t.reshape(lead + (S5_COLS // S5_BLK, S5_BLK // 2))


def _s5_state_in(s0):
    b = s0.shape[0]
    re, im = _s5_cols(s0[..., 0].astype(F32)), _s5_cols(s0[..., 1].astype(F32))
    return jnp.stack([re, im], axis=-2).reshape(b // S5_BATCH, S5_BATCH, S5_COLS)


def _s5_state_out(s):
    b = s.shape[0] * S5_BATCH
    s = s.reshape(b, S5_COLS // S5_BLK, 2, S5_BLK // 2)
    re = s[:, :, 0].reshape(b, S5_GROUPS, S5_STATE)
    im = s[:, :, 1].reshape(b, S5_GROUPS, S5_STATE)
    return jnp.stack([re, im], axis=-1)


def _s5_params(a_re, a_im, log_dt, b_re, b_im, c_re, c_im):
    a_re, a_im = a_re.astype(F32), a_im.astype(F32)
    delta = jnp.exp(log_dt.astype(F32))[:, None]
    mag = jnp.exp(a_re * delta)
    lam_re, lam_im = mag * jnp.cos(a_im * delta), mag * jnp.sin(a_im * delta)
    den = a_re * a_re + a_im * a_im
    nr = lam_re - 1.0
    f_re = (nr * a_re + lam_im * a_im) / den
    f_im = (lam_im * a_re - nr * a_im) / den
    b_re, b_im = b_re.astype(F32), b_im.astype(F32)
    bb_re = f_re[..., None] * b_re - f_im[..., None] * b_im
    bb_im = f_re[..., None] * b_im + f_im[..., None] * b_re
    nblk = S5_COLS // S5_BLK
    lr = _s5_cols(lam_re)
    li = _s5_cols(lam_im)
    lam_r = jnp.stack([lr, lr], axis=-2).reshape(1, S5_COLS)
    lam_i = jnp.stack([-li, li], axis=-2).reshape(1, S5_COLS)
    gq = S5_BLK // 2 // S5_STATE
    eye = jnp.eye(gq, dtype=F32)

    def in_block(bb):
        t = bb.reshape(nblk, gq, S5_STATE, S5_CH)
        return jnp.einsum("gh,kgpc->kgchp", eye, t).reshape(nblk, gq * S5_CH, gq * S5_STATE)

    blk_in = jnp.concatenate([in_block(bb_re), in_block(bb_im)], axis=-1)
    half = nblk // 2
    eye_h = jnp.eye(half, dtype=F32)
    wb = jnp.einsum("jk,hkcn->hjckn", eye_h, blk_in.reshape(2, half, gq * S5_CH, S5_BLK))
    wb = wb.reshape(2, half * gq * S5_CH, half * S5_BLK).astype(BF16)

    def out_block(cc):
        t = cc.astype(F32).reshape(nblk, gq, S5_CH, S5_STATE)
        return jnp.einsum("gh,kgcp->kgphc", eye, t).reshape(nblk, gq * S5_STATE, gq * S5_CH)

    wc = jnp.concatenate([out_block(c_re), -out_block(c_im)], axis=1).astype(BF16)
    return {"lam_r": lam_r, "lam_i": lam_i, "wb": wb, "wc": wc}


def _post_kernel(*refs, tile_base, prev_rows):
    if prev_rows:
        prev_hbm, refs = refs[8], refs[:8] + refs[9:]
    (x_ref, ys_ref, y5_ref, mod_ref, wo_ref, g_ref, wr_ref, br_ref,
     x1_ref, xs_hbm, ti_ref, tw_ref, rk_ref, ct_ref, cnt_ref, cnt_s, xs_v, pv_v, sem_out, sem_prev) = refs
    tl = x_ref.shape[1]
    blk = tl * TOP_K
    n_steps = pl.num_programs(0) * pl.num_programs(1)
    step = pl.program_id(0) * pl.num_programs(1) + pl.program_id(1)
    buf = step % 2
    first = step == 0

    def block_copy(at_step, at_slot):
        row0 = pl.multiple_of((tile_base + at_step) * blk, blk)
        return pltpu.make_async_copy(xs_v.at[at_slot], xs_hbm.at[pl.ds(row0, blk)], sem_out.at[at_slot])

    @pl.when(step >= 2)
    def _():
        block_copy(step - 2, buf).wait()

    prev_blocks = prev_rows // blk

    def prev_in(at_step):
        row0 = pl.multiple_of(at_step * blk, blk)
        return pltpu.make_async_copy(prev_hbm.at[pl.ds(row0, blk)], pv_v.at[at_step % 2], sem_prev.at[0, at_step % 2])

    def prev_out(at_step):
        row0 = pl.multiple_of(at_step * blk, blk)
        return pltpu.make_async_copy(pv_v.at[at_step % 2], xs_hbm.at[pl.ds(row0, blk)], sem_prev.at[1, at_step % 2])

    if prev_rows:
        @pl.when(jnp.logical_and(step >= 2, step - 2 < prev_blocks))
        def _():
            prev_out(step - 2).wait()

        @pl.when(step < prev_blocks)
        def _():
            prev_in(step).start()

        @pl.when(jnp.logical_and(step >= 1, step - 1 < prev_blocks))
        def _():
            prev_in(step - 1).wait()
            prev_out(step - 1).start()

    @pl.when(first)
    def _():
        cnt_s[...] = jnp.zeros(cnt_s.shape, cnt_s.dtype)

    mixed = (_dot(ys_ref[0], wo_ref[0:SSD_WIDTH, :])
             + _dot(y5_ref[0].astype(BF16), wo_ref[SSD_WIDTH:SSD_WIDTH + S5_WIDTH, :]))
    g1 = mod_ref[0, :, 2 * D_MODEL:3 * D_MODEL]
    sh2 = mod_ref[0, :, 3 * D_MODEL:4 * D_MODEL]
    sc2 = mod_ref[0, :, 4 * D_MODEL:5 * D_MODEL]
    x1 = x_ref[0] + g1 * mixed
    x1_ref[0] = x1
    ms = jnp.mean(x1 * x1, axis=-1, keepdims=True)
    h = x1 * lax.rsqrt(ms + RMS_EPS) * g_ref[...] * (1.0 + sc2) + sh2

    h_hi, h_mid, _ = _split3(h)
    wr = wr_ref[...]
    w_hi, w_mid, _ = _split3(wr)
    logits = _dot(h_hi, w_hi) + _dot(h_hi, w_mid) + _dot(h_mid, w_hi) + br_ref[...]

    work = logits.T[0:N_EXPERTS, :]
    eid = lax.broadcasted_iota(jnp.int32, (N_EXPERTS, tl), 0).astype(F32)
    row8 = lax.broadcasted_iota(jnp.int32, (8, tl), 0)
    onehot = jnp.zeros((N_EXPERTS, tl), F32)
    sels, vals = [], []
    top_i = jnp.zeros((8, tl), jnp.int32)
    for k in range(TOP_K):
        m = jnp.max(work, axis=0, keepdims=True)
        idx = jnp.min(jnp.where(work == m, eid, float(N_EXPERTS)), axis=0, keepdims=True)
        sel = eid == idx
        sels.append(sel)
        vals.append(m)
        top_i = jnp.where(row8 == k, idx.astype(jnp.int32), top_i)
        onehot = jnp.where(sel, 1.0, onehot)
        work = jnp.where(sel, -jnp.inf, work)
    es = [jnp.exp(v - vals[0]) for v in vals]
    den = es[0] + es[1] + es[2] + es[3]
    top_w = jnp.zeros((8, tl), F32)
    for k in range(TOP_K):
        top_w = jnp.where(row8 == k, es[k] / den, top_w)

    ri = lax.broadcasted_iota(jnp.int32, (tl, tl), 0)
    ci = lax.broadcasted_iota(jnp.int32, (tl, tl), 1)
    earlier = jnp.where(ri < ci, 1.0, 0.0).astype(BF16)
    local = _dot(onehot.astype(BF16), earlier)
    before = local + cnt_s[...]
    cnt_tile = jnp.sum(onehot, axis=1, keepdims=True)
    er = lax.broadcasted_iota(jnp.int32, (N_EXPERTS, N_EXPERTS), 0)
    ec = lax.broadcasted_iota(jnp.int32, (N_EXPERTS, N_EXPERTS), 1)
    lower = jnp.where(ec < er, 1.0, 0.0).astype(BF16)
    off = _dot(lower, jnp.broadcast_to(cnt_tile, (N_EXPERTS, LANES)).astype(BF16))[:, 0:1]
    pos = off + local
    rank = jnp.zeros((8, tl), jnp.int32)
    row_id = lax.broadcasted_iota(jnp.int32, (tl * TOP_K, tl), 0).astype(F32)
    perm = jnp.zeros((tl * TOP_K, tl), F32)
    for k in range(TOP_K):
        rk = jnp.sum(jnp.where(sels[k], before, 0.0), axis=0, keepdims=True)
        rank = jnp.where(row8 == k, rk.astype(jnp.int32), rank)
        pk = jnp.sum(jnp.where(sels[k], pos, 0.0), axis=0, keepdims=True)
        perm = jnp.where(row_id == pk, 1.0, perm)
    xs_v[buf] = _dot(perm.astype(BF16), h.astype(BF16)).astype(BF16)
    block_copy(step, buf).start()
    cnt_s[...] = cnt_s[...] + cnt_tile
    ti_ref[0] = top_i
    tw_ref[0] = top_w
    rk_ref[0] = rank
    ct_ref[0] = cnt_tile
    cnt_ref[...] = cnt_s[...]

    @pl.when(step == n_steps - 1)
    def _():
        block_copy(step, buf).wait()

        @pl.when(step >= 1)
        def _():
            block_copy(step - 1, 1 - buf).wait()


def _post_call(x, y_ssd, y_s5, mod, w_out, norm_g, w_router, b_router, xs_prev=None):
    b, l, _ = x.shape
    tl = 256
    tiles_per_row = l // tl
    prev_rows = 0 if xs_prev is None else xs_prev.shape[0]
    assert prev_rows // (tl * TOP_K) + 2 <= b * tiles_per_row, "not enough grid steps to pass the earlier rows through"
    sorted_rows = prev_rows + b * l * TOP_K
    per_batch = mod.shape[0] > 1
    mod_map = (lambda i, j: (i, 0, 0)) if per_batch else (lambda i, j: (0, 0, 0))
    tok = lambda w: pl.BlockSpec((1, tl, w), lambda i, j: (i, j, 0))
    full = lambda r, w: pl.BlockSpec((r, w), lambda i, j: (0, 0))
    slots = pl.BlockSpec((1, 8, tl), lambda i, j: (i, 0, j))
    in_specs = [tok(D_MODEL), tok(SSD_WIDTH), tok(S5_WIDTH),
                pl.BlockSpec((1, 1, 6 * D_MODEL), mod_map),
                full(D_MODEL, D_MODEL), full(1, D_MODEL), full(D_MODEL, LANES),
                full(1, LANES)]
    w_router = jnp.pad(w_router, ((0, 0), (0, LANES - N_EXPERTS)))
    b_router = jnp.pad(b_router, (0, LANES - N_EXPERTS), constant_values=-1e30)
    args = [x, y_ssd, y_s5, mod, w_out, norm_g.reshape(1, D_MODEL), w_router, b_router.reshape(1, LANES)]
    if prev_rows:
        in_specs.append(pl.BlockSpec(memory_space=pl.ANY))
        args.append(xs_prev)
    return pl.pallas_call(
        functools.partial(_post_kernel, tile_base=prev_rows // (tl * TOP_K), prev_rows=prev_rows),
        grid=(b, tiles_per_row),
        in_specs=in_specs,
        out_specs=[tok(D_MODEL),
                   pl.BlockSpec(memory_space=pl.ANY),
                   slots, slots, slots,
                   pl.BlockSpec((1, N_EXPERTS, 1), lambda i, j: (i * tiles_per_row + j, 0, 0)),
                   full(N_EXPERTS, 1)],
        out_shape=[jax.ShapeDtypeStruct((b, l, D_MODEL), F32),
                   jax.ShapeDtypeStruct((sorted_rows, D_MODEL), BF16),
                   jax.ShapeDtypeStruct((b, 8, l), jnp.int32),
                   jax.ShapeDtypeStruct((b, 8, l), F32),
                   jax.ShapeDtypeStruct((b, 8, l), jnp.int32),
                   jax.ShapeDtypeStruct((b * tiles_per_row, N_EXPERTS, 1), F32),
                   jax.ShapeDtypeStruct((N_EXPERTS, 1), F32)],
        scratch_shapes=[pltpu.VMEM((N_EXPERTS, 1), F32),
                        pltpu.VMEM((2, tl * TOP_K, D_MODEL), BF16),
                        pltpu.VMEM((2, tl * TOP_K, D_MODEL), BF16),
                        pltpu.SemaphoreType.DMA((2,)),
                        pltpu.SemaphoreType.DMA((2, 2))],
        compiler_params=pltpu.CompilerParams(dimension_semantics=("arbitrary", "arbitrary"),
                                             vmem_limit_bytes=VMEM_LIMIT),
        name="outproj_norm2_router",
    )(*args)


PLAN_EXPERT, PLAN_FIRST, PLAN_NEXT, PLAN_SLOT = range(4)


def _moe_kernel(plan_ref, nu_ref, x_ref, wgu_hbm, bgu_ref, wd_hbm, bd_ref, o_ref,
                gu_f32, dn_f32, gu_s, dn_s, sem):
    i = pl.program_id(0)
    used = i < nu_ref[0]

    def weight_copies(expert, slot):
        return (pltpu.make_async_copy(wgu_hbm.at[expert], gu_f32.at[slot], sem.at[0, slot]),
                pltpu.make_async_copy(wd_hbm.at[expert], dn_f32.at[slot], sem.at[1, slot]))

    @pl.when(jnp.logical_and(used, plan_ref[PLAN_FIRST, i] == 1))
    def _():
        slot = plan_ref[PLAN_SLOT, i]
        nxt = plan_ref[PLAN_NEXT, i]

        @pl.when(i == 0)
        def _():
            for cp in weight_copies(plan_ref[PLAN_EXPERT, 0], plan_ref[PLAN_SLOT, 0]):
                cp.start()

        for cp in weight_copies(plan_ref[PLAN_EXPERT, i], slot):
            cp.wait()

        @pl.when(nxt >= 0)
        def _():
            for cp in weight_copies(nxt, 1 - slot):
                cp.start()

        for r in range(D_MODEL // 128):
            gu_s[r * 128:(r + 1) * 128, :] = gu_f32[slot, r * 128:(r + 1) * 128, :].astype(BF16)
            dn_s[r * 128:(r + 1) * 128, :] = dn_f32[slot, r * 128:(r + 1) * 128, :].astype(BF16)

    @pl.when(used)
    def _():
        x = x_ref[...]
        gate = _dot(x, gu_s[:, 0:D_FF]) + bgu_ref[0, :, 0:D_FF]
        up = _dot(x, gu_s[:, D_FF:2 * D_FF]) + bgu_ref[0, :, D_FF:2 * D_FF]
        gate = jnp.minimum(gate, SWIGLU_LIMIT)
        up = jnp.clip(up, -SWIGLU_LIMIT, SWIGLU_LIMIT)
        act = gate * _sigmoid(SWIGLU_ALPHA * gate) * (up + 1.0)
        o_ref[...] = (_dot(act.astype(BF16), dn_s[...]) + bd_ref[0]).astype(o_ref.dtype)

    @pl.when(jnp.logical_not(used))
    def _():
        o_ref[...] = jnp.zeros(o_ref.shape, o_ref.dtype)


def _moe_call(plan, n_used, xs, w_gate_up, b_gate_up, w_down, b_down):
    p = xs.shape[0]
    n_tiles = p // MOE_TM
    expert_of = lambda i, plan, nu: (plan[PLAN_EXPERT, i], 0, 0)
    grid_spec = pltpu.PrefetchScalarGridSpec(
        num_scalar_prefetch=2,
        grid=(n_tiles,),
        in_specs=[pl.BlockSpec((MOE_TM, D_MODEL), lambda i, plan, nu: (i, 0)),
                  pl.BlockSpec(memory_space=pl.ANY),
                  pl.BlockSpec((1, 1, 2 * D_FF), expert_of),
                  pl.BlockSpec(memory_space=pl.ANY),
                  pl.BlockSpec((1, 1, D_MODEL), expert_of)],
        out_specs=pl.BlockSpec((MOE_TM, D_MODEL), lambda i, plan, nu: (i, 0)),
        scratch_shapes=[pltpu.VMEM((2, D_MODEL, 2 * D_FF), F32), pltpu.VMEM((2, D_FF, D_MODEL), F32),
                        pltpu.VMEM((D_MODEL, 2 * D_FF), BF16), pltpu.VMEM((D_FF, D_MODEL), BF16),
                        pltpu.SemaphoreType.DMA((2, 2))],
    )
    return pl.pallas_call(
        _moe_kernel,
        grid_spec=grid_spec,
        out_shape=jax.ShapeDtypeStruct((p, D_MODEL), BF16),
        compiler_params=pltpu.CompilerParams(dimension_semantics=("arbitrary",),
                                             vmem_limit_bytes=VMEM_LIMIT),
        name="moe_experts",
    )(plan, n_used, xs, w_gate_up, b_gate_up.reshape(N_EXPERTS, 1, 2 * D_FF),
      w_down, b_down.reshape(N_EXPERTS, 1, D_MODEL))


def _combine_kernel(x1_ref, yg_ref, tw_ref, mod_ref, g_ref, o_ref):
    tw = tw_ref[0]
    acc = tw[:, 0:1] * yg_ref[0, 0].astype(F32)
    for k in range(1, TOP_K):
        acc = acc + tw[:, k:k + 1] * yg_ref[k, 0].astype(F32)
    g2 = mod_ref[0, :, 5 * D_MODEL:6 * D_MODEL]
    x2 = x1_ref[0] + g2 * acc
    ms = jnp.mean(x2 * x2, axis=-1, keepdims=True)
    o_ref[0] = x2 * lax.rsqrt(ms + RMS_EPS) * g_ref[...]


def _combine_call(x1, yg, top_w, mod, norm_g):
    b, l, _ = x1.shape
    tl = 256
    per_batch = mod.shape[0] > 1
    mod_map = (lambda i, j: (i, 0, 0)) if per_batch else (lambda i, j: (0, 0, 0))
    tok = lambda w: pl.BlockSpec((1, tl, w), lambda i, j: (i, j, 0))
    return pl.pallas_call(
        _combine_kernel,
        grid=(b, l // tl),
        in_specs=[tok(D_MODEL),
                  pl.BlockSpec((TOP_K, 1, tl, D_MODEL), lambda i, j: (0, i, j, 0)),
                  tok(TOP_K),
                  pl.BlockSpec((1, 1, 6 * D_MODEL), mod_map),
                  pl.BlockSpec((1, D_MODEL), lambda i, j: (0, 0))],
        out_specs=tok(D_MODEL),
        out_shape=jax.ShapeDtypeStruct((b, l, D_MODEL), F32),
        compiler_params=pltpu.CompilerParams(dimension_semantics=("arbitrary", "arbitrary"),
                                             vmem_limit_bytes=VMEM_LIMIT),
        name="moe_combine_final_norm",
    )(x1, yg, top_w, mod, norm_g.reshape(1, D_MODEL))


def _route(top_i, rank, counts, cnt_tiles):
    t = top_i.shape[0]
    p = t * TOP_K + N_EXPERTS * MOE_TM
    n_tiles = p // MOE_TM
    padded = ((counts + MOE_TM - 1) // MOE_TM) * MOE_TM
    ends = jnp.cumsum(padded)
    starts = ends - padded
    take = lambda table, idx: table.at[idx].get(mode="promise_in_bounds")
    dest = take(starts, top_i) + rank
    tile_start = jnp.arange(n_tiles, dtype=jnp.int32) * MOE_TM
    tile_expert = jnp.minimum(jnp.sum((ends[None, :] <= tile_start[:, None]).astype(jnp.int32), axis=1),
                              N_EXPERTS - 1)
    n_used = (ends[-1] // MOE_TM).reshape(1)
    tile_id = jnp.arange(n_tiles, dtype=jnp.int32)
    prev_expert = jnp.concatenate([jnp.full((1,), -1, tile_expert.dtype), tile_expert[:-1]])
    first = ((tile_expert != prev_expert) & (tile_id < n_used[0])).astype(jnp.int32)
    slot = (jnp.cumsum(first) - 1) % 2
    eid = jnp.arange(N_EXPERTS, dtype=jnp.int32)
    later = jnp.where((eid[None, :] > eid[:, None]) & (counts[None, :] > 0), eid[None, :], N_EXPERTS)
    next_of = jnp.min(later, axis=1)
    next_of = jnp.where(next_of < N_EXPERTS, next_of, -1)
    nxt = jnp.sum(jnp.where(tile_expert[:, None] == eid[None, :], next_of[None, :], 0), axis=1)
    plan = jnp.stack([tile_expert, first, nxt, slot]).astype(jnp.int32)
    tok_tiles = cnt_tiles.shape[0]
    block = (t * TOP_K) // tok_tiles
    excl = jnp.cumsum(cnt_tiles, axis=0) - cnt_tiles
    off = jnp.cumsum(cnt_tiles, axis=1) - cnt_tiles
    excl_e = take(excl.T, tile_expert)
    shift_e = take((off - excl).T, tile_expert)
    r = tile_start[:, None] + jnp.arange(MOE_TM, dtype=jnp.int32)[None, :] - take(starts, tile_expert)[:, None]
    ti = jnp.sum((excl_e[:, None, :] <= r[:, :, None]).astype(jnp.int32), axis=2) - 1
    here = jnp.arange(tok_tiles, dtype=jnp.int32)[None, None, :] == ti[:, :, None]
    shift_i = jnp.sum(jnp.where(here, shift_e[:, None, :], 0), axis=2)
    valid = r < take(counts, tile_expert)[:, None]
    spread = (jnp.arange(p, dtype=jnp.int32) % (t * TOP_K)).reshape(n_tiles, MOE_TM)
    src_row = jnp.where(valid, ti * block + r + shift_i, spread).reshape(p)
    return dest, src_row, plan, n_used.astype(jnp.int32)


def _layer_front(x, mod, s_ssd_f, s_ssd_b, s_s5_f, s_s5_b, grid_cols, prm, xs_prev):
    z, xbc, u, dt = _inproj_call(x, mod, prm["norm1_g"], prm["w_in_r"])
    y_ssd, n_ssd_f, n_ssd_b = _ssd_call(z, xbc, dt, s_ssd_f, s_ssd_b, prm["ssd"])
    y_f, n_s5_f = _s5_call(u, _s5_state_in(s_s5_f), prm["s5_f"], grid_cols, False)
    y_s5, n_s5_b = _s5_call(u, _s5_state_in(s_s5_b), prm["s5_b"], grid_cols, True,
                            y_first=y_f, glu=prm["glu"])
    x1, xs_buf, top_i, top_w, rank, cnt_tiles, cnt = _post_call(
        x, y_ssd, y_s5, mod, prm["w_out"], prm["norm2_g"], prm["w_router"], prm["b_router"], xs_prev)
    per_token = lambda a: jnp.swapaxes(a[:, :TOP_K, :], 1, 2)
    top_i, top_w, rank = per_token(top_i), per_token(top_w), per_token(rank)
    states = (n_ssd_f, n_ssd_b, _s5_state_out(n_s5_f), _s5_state_out(n_s5_b))
    return x1, (xs_buf, top_i, rank, cnt, cnt_tiles), top_w, states


def kernel(x_prompt, x_sample, state_ssd_fwd, state_ssd_bwd, state_s5_fwd, state_s5_bwd, c, c_ctx, w_ada, b_ada, norm1_g, w_in, ssd_conv_w, ssd_conv_b, ssd_dt_bias_fwd, ssd_dt_bias_bwd, ssd_a_log_fwd, ssd_a_log_bwd, ssd_d, ssd_norm_g, s5_a_re_fwd, s5_a_im_fwd, s5_log_dt_fwd, s5_b_re_fwd, s5_b_im_fwd, s5_c_re_fwd, s5_c_im_fwd, s5_a_re_bwd, s5_a_im_bwd, s5_log_dt_bwd, s5_b_re_bwd, s5_b_im_bwd, s5_c_re_bwd, s5_c_im_bwd, s5_d, w_glu, b_glu, w_out, norm2_g, w_router, b_router, w_gate_up, b_gate_up, w_down, b_down, norm_f_g):
    depth = w_ada.shape[0]
    assert depth == 1, "single trunk layer"
    nb_ctx, l_ctx, _ = x_prompt.shape
    nb_lat, l_lat, _ = x_sample.shape

    w_in0 = w_in[0]
    dt_lo = SSD_WIDTH + SSD_CONV_CH
    dt_cols = jnp.pad(w_in0[:, dt_lo:dt_lo + 2 * SSD_HEADS], ((0, 0), (0, DT_PAD - 2 * SSD_HEADS)))
    w_in_r = jnp.concatenate([w_in0[:, :dt_lo], w_in0[:, dt_lo + 2 * SSD_HEADS:], dt_cols], axis=1).astype(BF16)
    pad_dt = lambda f, b: jnp.pad(jnp.concatenate([f, b]).astype(F32), (0, DT_PAD - 2 * SSD_HEADS)).reshape(1, DT_PAD)
    ssd_prm = {
        "conv_w": jnp.pad(ssd_conv_w[0].astype(F32), ((0, 5), (0, 0))),
        "conv_b": ssd_conv_b[0].astype(F32).reshape(1, SSD_CONV_CH),
        "dt_bias": pad_dt(ssd_dt_bias_fwd[0], ssd_dt_bias_bwd[0]),
        "a": pad_dt(-jnp.exp(ssd_a_log_fwd[0].astype(F32)), -jnp.exp(ssd_a_log_bwd[0].astype(F32))),
        "d_skip": jnp.repeat(ssd_d[0].astype(F32), SSD_HEAD_DIM).reshape(1, SSD_WIDTH),
        "norm_g": ssd_norm_g[0].astype(F32).reshape(1, SSD_WIDTH),
        "expand": (jnp.arange(DT_PAD)[None, :, None]
                   == (jnp.arange(SSD_WIDTH) // SSD_HEAD_DIM)[None, None, :]
                   + SSD_HEADS * jnp.arange(2)[:, None, None]).astype(BF16),
    }
    prm = {
        "norm1_g": norm1_g[0], "w_in_r": w_in_r, "ssd": ssd_prm,
        "s5_f": _s5_params(s5_a_re_fwd[0], s5_a_im_fwd[0], s5_log_dt_fwd[0], s5_b_re_fwd[0],
                           s5_b_im_fwd[0], s5_c_re_fwd[0], s5_c_im_fwd[0]),
        "s5_b": _s5_params(s5_a_re_bwd[0], s5_a_im_bwd[0], s5_log_dt_bwd[0], s5_b_re_bwd[0],
                           s5_b_im_bwd[0], s5_c_re_bwd[0], s5_c_im_bwd[0]),
        "glu": {"d": s5_d[0].astype(F32).reshape(1, S5_WIDTH), "w": w_glu[0].astype(BF16),
                "b": b_glu[0].astype(F32).reshape(1, 2 * S5_WIDTH)},
        "w_out": w_out[0].astype(BF16), "norm2_g": norm2_g[0],
        "w_router": w_router[0].astype(F32), "b_router": b_router[0].astype(F32),
    }

    conds = jnp.concatenate([c_ctx[None, :], c], axis=0)
    conds = jnp.pad(conds, ((0, (-conds.shape[0]) % 8), (0, 0)))
    mod = _ada_call(conds, w_ada[0], b_ada[0])
    mod_ctx = mod[0:1].reshape(1, 1, 6 * D_MODEL)
    mod_lat = mod[1:1 + nb_lat].reshape(nb_lat, 1, 6 * D_MODEL)

    zero_ssd = jnp.zeros((nb_ctx, SSD_HEADS, SSD_HEAD_DIM, SSD_STATE), F32)
    zero_s5 = jnp.zeros((nb_ctx, S5_GROUPS, S5_STATE, 2), F32)
    t_ctx = nb_ctx * l_ctx
    x1_c, routed_c, tw_c, st_c = _layer_front(
        x_prompt, mod_ctx, zero_ssd, zero_ssd, zero_s5, zero_s5, False, prm, None)
    x1_l, routed_l, tw_l, _ = _layer_front(
        x_sample, mod_lat, state_ssd_fwd[:, 0], state_ssd_bwd[:, 0], state_s5_fwd[:, 0],
        state_s5_bwd[:, 0], True, prm, routed_c[0])

    _, ti_c, rk_c, cnt_c, ct_c = routed_c
    xs_tiles, ti_l, rk_l, cnt_l, ct_l = routed_l
    cnt_c = cnt_c.reshape(N_EXPERTS).astype(jnp.int32)
    cnt_l = cnt_l.reshape(N_EXPERTS).astype(jnp.int32)
    cnt_tiles = jnp.concatenate([ct_c, ct_l], axis=0).reshape(-1, N_EXPERTS).astype(jnp.int32)
    ti_l2 = ti_l.reshape(-1, TOP_K)
    top_i = jnp.concatenate([ti_c.reshape(-1, TOP_K), ti_l2], axis=0)
    rank = jnp.concatenate([rk_c.reshape(-1, TOP_K),
                            rk_l.reshape(-1, TOP_K) + cnt_c.at[ti_l2].get(mode="promise_in_bounds")], axis=0)
    dest, src_row, plan, n_used = _route(top_i, rank, cnt_c + cnt_l, cnt_tiles)
    xs = xs_tiles.at[src_row].get(mode="promise_in_bounds")
    ys = _moe_call(plan, n_used, xs, w_gate_up[0], b_gate_up[0], w_down[0], b_down[0])

    def picked_rows(d, nb, l):
        rows = ys.at[d.T.reshape(-1)].get(mode="promise_in_bounds")
        return rows.reshape(TOP_K, nb, l, D_MODEL)

    y_prompt = _combine_call(x1_c, picked_rows(dest[:t_ctx], nb_ctx, l_ctx), tw_c, mod_ctx, norm_f_g)
    y_sample = _combine_call(x1_l, picked_rows(dest[t_ctx:], nb_lat, l_lat), tw_l, mod_lat, norm_f_g)

    ssd_f, ssd_b, s5_f, s5_b = st_c
    return (y_prompt, y_sample, ssd_f[:, None], ssd_b[:, None], s5_f[:, None], s5_b[:, None])
```
